```python
import math
import jax, jax.numpy as jnp
from jax import lax
import numpy as np

D_MODEL = 1024
BATCH = 8
SEQ = 4096
DEPTH = 4

GDN_HEADS = 4
GDN_DK = 128
GDN_DV = 128
GDN_CONV = 4
GDN_CHUNK = 64
SWA_HQ = 8
SWA_HKV = 2
SWA_DH = 64
WINDOW = 128
POOL_WINDOWS = (2, 4, 8, 16)
POOL_GROUPS = 4
POOL_GDIM = 128
N_BRANCH = 3
N_EXPERTS = 32
TOP_K = 4
D_FF = 1024
ROUTE_BLOCK = 256
SWIGLU_ALPHA = 1.702
SWIGLU_LIMIT = 7.0
LN_EPS = 1e-5
RMS_EPS = 1e-6
DEEPNORM_ALPHA = (2 * DEPTH) ** 0.25
DEEPNORM_BETA = (8 * DEPTH) ** -0.25

GDN_QK = GDN_HEADS * GDN_DK
GDN_V = GDN_HEADS * GDN_DV
SWA_Q = SWA_HQ * SWA_DH
SWA_KV = SWA_HKV * SWA_DH
POOL_DIM = POOL_GROUPS * POOL_GDIM
IN_SIZES = (GDN_QK, GDN_QK, GDN_V, GDN_V, GDN_HEADS, GDN_HEADS, SWA_Q, SWA_KV, SWA_KV, POOL_DIM, N_BRANCH * D_MODEL)
IN_WIDTH = sum(IN_SIZES)

kernel_name = "hybrid_gdn_swa_pool_moe_deepnorm"


def layer_norm(x):
    xf = x.astype(jnp.float32)
    mu = jnp.mean(xf, -1, keepdims=True)
    var = jnp.mean(jnp.square(xf - mu), -1, keepdims=True)
    return ((xf - mu) * lax.rsqrt(var + LN_EPS)).astype(x.dtype)


def layer_norm_affine(x, g, b):
    xf = x.astype(jnp.float32)
    mu = jnp.mean(xf, -1, keepdims=True)
    var = jnp.mean(jnp.square(xf - mu), -1, keepdims=True)
    y = (xf - mu) * lax.rsqrt(var + LN_EPS) * g.astype(jnp.float32) + b.astype(jnp.float32)
    return y.astype(x.dtype)


def causal_depthwise_conv(x, w):
    k_width, chans = w.shape
    return lax.conv_general_dilated(x, w[:, None, :], window_strides=(1,), padding=((k_width - 1, 0),),
                                    dimension_numbers=('NWC', 'WIO', 'NWC'), feature_group_count=chans)


def l2norm(t):
    return t * lax.rsqrt(jnp.sum(t * t, -1, keepdims=True) + 1e-6)


def gated_delta_rule(q, k, v, g, beta):
    B, S, H, DK = q.shape
    DV = v.shape[-1]
    C = GDN_CHUNK
    N = S // C
    f32 = jnp.float32

    def chunks(t):
        t = t.astype(f32).reshape((B, N, C, H) + t.shape[3:])
        return jnp.moveaxis(t, 3, 1)

    q, k, v, g, beta = chunks(q), chunks(k), chunks(v), chunks(g), chunks(beta)
    q = l2norm(q) * (DK ** -0.5)
    k = l2norm(k)
    gc = jnp.cumsum(g, axis=-1)
    causal = jnp.tril(jnp.ones((C, C), bool))
    strict = jnp.tril(jnp.ones((C, C), bool), -1)
    diff = gc[..., :, None] - gc[..., None, :]
    decay = jnp.where(causal, jnp.exp(jnp.where(causal, diff, 0.0)), 0.0)
    k_beta = k * beta[..., None]
    lower = jnp.where(strict, jnp.einsum('bhnck,bhnsk->bhncs', k_beta, k) * decay, 0.0)
    a_mat = lower + jnp.eye(C, dtype=f32)
    rhs = jnp.concatenate([v * beta[..., None], k_beta * jnp.exp(gc)[..., None]], -1)
    sol = lax.linalg.triangular_solve(a_mat, rhs, left_side=True, lower=True, unit_diagonal=True)
    u, w = sol[..., :DV], sol[..., DV:]
    attn = jnp.einsum('bhnck,bhnsk->bhncs', q, k) * decay
    q_g = q * jnp.exp(gc)[..., None]
    k_g = k * jnp.exp(gc[..., -1:] - gc)[..., None]
    g_last = jnp.exp(gc[..., -1])

    def step(state, inp):
        qg_i, kg_i, u_i, w_i, attn_i, gl_i = inp
        v_new = u_i - jnp.einsum('bhck,bhkv->bhcv', w_i, state)
        out = jnp.einsum('bhck,bhkv->bhcv', qg_i, state) + jnp.einsum('bhcs,bhsv->bhcv', attn_i, v_new)
        state = state * gl_i[..., None, None] + jnp.einsum('bhck,bhcv->bhkv', kg_i, v_new)
        return state, out

    xs = tuple(jnp.moveaxis(t, 2, 0) for t in (q_g, k_g, u, w, attn, g_last))
    s0 = jnp.zeros((B, H, DK, DV), f32)
    _, outs = lax.scan(step, s0, xs)
    return jnp.transpose(outs, (1, 0, 3, 2, 4)).reshape(B, S, H, DV)


def sliding_window_attention(q, k, v, sinks):
    B, S, _ = q.shape
    G = SWA_HQ // SWA_HKV
    NB = S // WINDOW
    f32 = jnp.float32
    qb = q.reshape(B, NB, WINDOW, SWA_HKV, G, SWA_DH)
    kb = k.reshape(B, NB, WINDOW, SWA_HKV, SWA_DH)
    vb = v.reshape(B, NB, WINDOW, SWA_HKV, SWA_DH)

    def with_prev(t):
        prev = jnp.pad(t, ((0, 0), (1, 0), (0, 0), (0, 0), (0, 0)))[:, :-1]
        return jnp.concatenate([prev, t], axis=2)

    kk, vv = with_prev(kb), with_prev(vb)
    s = jnp.einsum('bnqhgd,bnkhd->bnhgqk', qb, kk).astype(f32) * (SWA_DH ** -0.5)
    qi = jnp.arange(WINDOW)[:, None]
    ki = jnp.arange(2 * WINDOW)[None, :]
    dist = qi + WINDOW - ki
    blk = jnp.arange(NB)[:, None, None]
    valid = (dist >= 0) & (dist < WINDOW) & (blk * WINDOW + ki - WINDOW >= 0)
    slopes = 2.0 ** (-8.0 * jnp.arange(1, SWA_HQ + 1, dtype=f32) / SWA_HQ)
    alibi = -slopes.reshape(SWA_HKV, G, 1, 1) * dist.astype(f32)
    s = jnp.where(valid[None, :, None, None], s + alibi, -jnp.inf)
    sink = jnp.broadcast_to(sinks.astype(f32).reshape(1, 1, SWA_HKV, G, 1, 1), s.shape[:-1] + (1,))
    p = jax.nn.softmax(jnp.concatenate([s, sink], -1), axis=-1)[..., :-1]
    o = jnp.einsum('bnhgqk,bnkhd->bnqhgd', p.astype(v.dtype), vv)
    return o.reshape(B, S, SWA_Q)


def multiscale_pool(u, pool_w, pool_scale):
    B, S, _ = u.shape
    f32 = jnp.float32
    uf = u.astype(f32).reshape(B, S, POOL_GROUPS, POOL_GDIM)
    cs = jnp.pad(jnp.cumsum(uf, axis=1), ((0, 0), (1, 0), (0, 0), (0, 0)))
    t = jnp.arange(S)
    outs = []
    for gi, win in enumerate(POOL_WINDOWS):
        hi = cs[:, 1:, gi]
        lo = jnp.concatenate([jnp.zeros((B, win - 1, POOL_GDIM), f32), cs[:, :S - win + 1, gi]], axis=1)
        cnt = jnp.minimum(t + 1, win).astype(f32)[None, :, None]
        outs.append((hi - lo) / cnt - uf[:, :, gi])
    d = jnp.stack(outs, axis=2)
    y = jnp.einsum('bsgc,gcd->bsgd', d, pool_w.astype(f32)).reshape(B, S, POOL_DIM)
    return (y * pool_scale.astype(f32)).astype(u.dtype)


def hybrid_mixer(h, w_in, conv_w, a_log, dt_bias, gdn_norm_w, sinks, pool_w, pool_scale, w_pa, w_pb, w_pc, w_o):
    B, S, _ = h.shape
    f32 = jnp.float32
    points = [int(p) for p in np.cumsum(IN_SIZES)[:-1]]
    proj = h @ w_in
    q_a, k_a, v_a, z_a, a_a, b_a, q_b, k_b, v_b, u_c, gate_logits = jnp.split(proj, points, axis=-1)
    qkv = jax.nn.silu(causal_depthwise_conv(jnp.concatenate([q_a, k_a, v_a], -1), conv_w))
    q_a, k_a, v_a = jnp.split(qkv, [GDN_QK, 2 * GDN_QK], axis=-1)
    beta = jax.nn.sigmoid(b_a.astype(f32))
    g = -jnp.exp(a_log.astype(f32)) * jax.nn.softplus(a_a.astype(f32) + dt_bias.astype(f32))
    o_a = gated_delta_rule(q_a.reshape(B, S, GDN_HEADS, GDN_DK), k_a.reshape(B, S, GDN_HEADS, GDN_DK),
                           v_a.reshape(B, S, GDN_HEADS, GDN_DV), g, beta)
    o_a = (o_a * lax.rsqrt(jnp.mean(o_a * o_a, -1, keepdims=True) + RMS_EPS) * gdn_norm_w.astype(f32)
           * jax.nn.silu(z_a.astype(f32).reshape(B, S, GDN_HEADS, GDN_DV)))
    y_a = o_a.reshape(B, S, GDN_V).astype(h.dtype)
    y_b = sliding_window_attention(q_b, k_b, v_b, sinks)
    y_c = multiscale_pool(u_c, pool_w, pool_scale)
    g_a, g_b, g_c = jnp.split(jax.nn.sigmoid(gate_logits), N_BRANCH, axis=-1)
    merged = g_a * (y_a @ w_pa) + g_b * (y_b @ w_pb) + g_c * (y_c @ w_pc)
    return merged @ w_o


def moe_ffn(h, router_w, router_b, exp_w1, exp_b1, exp_w2, exp_b2):
    B, S, D = h.shape
    T = B * S
    TK = T * TOP_K
    xt = h.reshape(T, D)
    logits = (xt @ router_w + router_b).astype(jnp.float32)
    top_v, top_i = lax.top_k(logits, TOP_K)
    gate = jax.nn.softmax(top_v, axis=-1)
    flat_e = top_i.reshape(-1)
    order = jnp.argsort(flat_e)
    e_sorted = flat_e[order]
    tok_sorted = (order // TOP_K).astype(jnp.int32)
    gate_sorted = gate.reshape(-1)[order]
    counts = jnp.bincount(flat_e, length=N_EXPERTS)
    padded = (counts + ROUTE_BLOCK - 1) // ROUTE_BLOCK * ROUTE_BLOCK
    start = jnp.cumsum(counts) - counts
    ends_p = jnp.cumsum(padded)
    pstart = ends_p - padded
    dest = pstart[e_sorted] + (jnp.arange(TK) - start[e_sorted])
    n_blocks = (TK + ROUTE_BLOCK - 1) // ROUTE_BLOCK + N_EXPERTS
    rows = n_blocks * ROUTE_BLOCK
    buf_tok = jnp.zeros((rows,), jnp.int32).at[dest].set(tok_sorted)
    buf_gate = jnp.zeros((rows,), jnp.float32).at[dest].set(gate_sorted)
    block_e = jnp.minimum(jnp.searchsorted(ends_p, jnp.arange(n_blocks) * ROUTE_BLOCK, side='right'), N_EXPERTS - 1)
    xs = xt[buf_tok].reshape(n_blocks, ROUTE_BLOCK, D)

    def expert_block(args):
        xb, e = args
        gu = xb @ exp_w1[e] + exp_b1[e]
        glu, lin = gu[:, :D_FF], gu[:, D_FF:]
        glu = jnp.minimum(glu, SWIGLU_LIMIT)
        lin = jnp.clip(lin, -SWIGLU_LIMIT, SWIGLU_LIMIT)
        act = glu * jax.nn.sigmoid(SWIGLU_ALPHA * glu) * (lin + 1.0)
        return act @ exp_w2[e] + exp_b2[e]

    ys = lax.map(expert_block, (xs, block_e)).reshape(rows, D)
    out = jax.ops.segment_sum(ys * buf_gate[:, None], buf_tok, num_segments=T)
    return out.reshape(B, S, D).astype(h.dtype)


def setup_inputs(seed: int = 0) -> dict:
    key = jax.random.key(seed)
    ks = jax.random.split(key, 26)
    f32 = jnp.float32

    def nrm(k, shape, scale):
        return jax.random.normal(k, shape, f32) * scale

    L, D, E = DEPTH, D_MODEL, N_EXPERTS
    col_scale = jnp.concatenate([jnp.full((n,), DEEPNORM_BETA if i in (2, 8) else 1.0, f32)
                                 for i, n in enumerate(IN_SIZES)])
    dt = jnp.exp(jax.random.uniform(ks[7], (L, GDN_HEADS), f32, math.log(1e-3), math.log(1e-1)))
    return {
        'x': nrm(ks[0], (BATCH, SEQ, D), 1.0),
        'c': nrm(ks[1], (BATCH, D), 1.0),
        'ada_w': nrm(ks[2], (L, D, 6 * D), 0.2 * D ** -0.5),
        'ada_b': nrm(ks[3], (L, 6 * D), 0.02),
        'w_in': nrm(ks[4], (L, D, IN_WIDTH), D ** -0.5) * col_scale,
        'conv_w': nrm(ks[5], (L, GDN_CONV, 2 * GDN_QK + GDN_V), GDN_CONV ** -0.5),
        'a_log': jnp.log(jax.random.uniform(ks[6], (L, GDN_HEADS), f32, 1.0, 16.0)),
        'dt_bias': dt + jnp.log(-jnp.expm1(-dt)),
        'gdn_norm_w': 1.0 + nrm(ks[8], (L, GDN_DV), 0.1),
        'sinks': nrm(ks[9], (L, SWA_HQ), 0.5),
        'pool_w': nrm(ks[10], (L, POOL_GROUPS, POOL_GDIM, POOL_GDIM), POOL_GDIM ** -0.5),
        'pool_scale': 1.0 + nrm(ks[11], (L, POOL_DIM), 0.1),
        'w_pa': nrm(ks[12], (L, GDN_V, D), GDN_V ** -0.5),
        'w_pb': nrm(ks[13], (L, SWA_Q, D), SWA_Q ** -0.5),
        'w_pc': nrm(ks[14], (L, POOL_DIM, D), POOL_DIM ** -0.5),
        'w_o': nrm(ks[15], (L, D, D), DEEPNORM_BETA * D ** -0.5),
        'ln1_g': 1.0 + nrm(ks[16], (L, D), 0.1),
        'ln1_b': nrm(ks[17], (L, D), 0.02),
        'ln2_g': 1.0 + nrm(ks[18], (L, D), 0.1),
        'ln2_b': nrm(ks[19], (L, D), 0.02),
        'router_w': nrm(ks[20], (L, D, E), D ** -0.5),
        'router_b': nrm(ks[21], (L, E), 0.01),
        'exp_w1': nrm(ks[22], (L, E, D, 2 * D_FF), D ** -0.5),
        'exp_b1': nrm(ks[23], (L, E, 2 * D_FF), 0.02),
        'exp_w2': nrm(ks[24], (L, E, D_FF, D), DEEPNORM_BETA * D_FF ** -0.5),
        'exp_b2': nrm(ks[25], (L, E, D), 0.02),
    }


def reference(x, c, ada_w, ada_b, w_in, conv_w, a_log, dt_bias, gdn_norm_w, sinks, pool_w, pool_scale,
              w_pa, w_pb, w_pc, w_o, ln1_g, ln1_b, ln2_g, ln2_b, router_w, router_b,
              exp_w1, exp_b1, exp_w2, exp_b2):
    cond = jax.nn.silu(c)
    for l in range(DEPTH):
        mod = cond @ ada_w[l] + ada_b[l]
        sh1, sc1, gt1, sh2, sc2, gt2 = jnp.split(mod[:, None, :], 6, axis=-1)
        h = layer_norm(x) * (1.0 + sc1) + sh1
        y = hybrid_mixer(h, w_in[l], conv_w[l], a_log[l], dt_bias[l], gdn_norm_w[l], sinks[l], pool_w[l],
                         pool_scale[l], w_pa[l], w_pb[l], w_pc[l], w_o[l])
        x = layer_norm_affine(DEEPNORM_ALPHA * x + (1.0 + gt1) * y, ln1_g[l], ln1_b[l])
        h = layer_norm(x) * (1.0 + sc2) + sh2
        y = moe_ffn(h, router_w[l], router_b[l], exp_w1[l], exp_b1[l], exp_w2[l], exp_b2[l])
        x = layer_norm_affine(DEEPNORM_ALPHA * x + (1.0 + gt2) * y, ln2_g[l], ln2_b[l])
    return x
```

```python
import functools

import jax
import jax.numpy as jnp
from jax import lax
from jax.experimental import pallas as pl
from jax.experimental.pallas import tpu as pltpu

D_MODEL = 1024
DEPTH = 4
GDN_HEADS = 4
GDN_DK = 128
GDN_DV = 128
GDN_CONV = 4
GDN_CHUNK = 64
SWA_HQ = 8
SWA_HKV = 2
SWA_DH = 64
WINDOW = 128
POOL_WINDOWS = (2, 4, 8, 16)
POOL_GROUPS = 4
POOL_GDIM = 128
N_BRANCH = 3
N_EXPERTS = 32
TOP_K = 4
D_FF = 1024
ROUTE_BLOCK = 256
SWIGLU_ALPHA = 1.702
SWIGLU_LIMIT = 7.0
LN_EPS = 1e-5
RMS_EPS = 1e-6
DEEPNORM_ALPHA = (2 * DEPTH) ** 0.25

GDN_QK = GDN_HEADS * GDN_DK
GDN_V = GDN_HEADS * GDN_DV
SWA_Q = SWA_HQ * SWA_DH
SWA_KV = SWA_HKV * SWA_DH
POOL_DIM = POOL_GROUPS * POOL_GDIM
GQA_GROUP = SWA_HQ // SWA_HKV
MAX_POOL = max(POOL_WINDOWS)

QKV_W = 2 * GDN_QK + GDN_V
AB_W = 128
SEG_QKV = 0
SEG_Z = SEG_QKV + QKV_W
SEG_AB = SEG_Z + GDN_V
SEG_QB = SEG_AB + AB_W
SEG_KVB = SEG_QB + SWA_Q
SEG_UC = SEG_KVB + 2 * SWA_KV
SEG_GATE = SEG_UC + POOL_DIM
PROJ_W = SEG_GATE + N_BRANCH * D_MODEL

TOKEN_TILE = 512
NEG_BIG = -1e30
VMEM_LIMIT = 56 * 1024 * 1024

F32 = jnp.float32
BF16 = jnp.bfloat16


def _params(*sem):
    return pltpu.CompilerParams(dimension_semantics=sem, vmem_limit_bytes=VMEM_LIMIT)


def _sigmoid(x):
    return 1.0 / (1.0 + jnp.exp(-x))


def _layer_norm(x):
    mu = jnp.mean(x, -1, keepdims=True)
    xc = x - mu
    var = jnp.mean(xc * xc, -1, keepdims=True)
    return xc * lax.rsqrt(var + LN_EPS)


def _dot(a, b):
    return jnp.dot(a, b, preferred_element_type=F32)


def _dot_nt(a, b):
    return lax.dot_general(a, b, (((1,), (1,)), ((), ())), preferred_element_type=F32)


def _dot_tn(a, b):
    return lax.dot_general(a, b, (((0,), (0,)), ((), ())), preferred_element_type=F32)


def _bdot(a, b):
    return lax.dot_general(a, b, (((2,), (1,)), ((0,), (0,))), preferred_element_type=F32)


def _bdot_nt(a, b):
    return lax.dot_general(a, b, (((2,), (2,)), ((0,), (0,))), preferred_element_type=F32)


def _ada_kernel(c_ref, w_ref, b_ref, o_ref):
    c = c_ref[...]
    cond = c * _sigmoid(c)
    o_ref[0] = _dot(cond.astype(BF16), w_ref[0].astype(BF16)) + b_ref[0]


def _ada_mod(c, ada_w, ada_b):
    depth, d, n = ada_w.shape
    b = c.shape[0]
    tn = 1024
    return pl.pallas_call(
        _ada_kernel,
        grid=(depth, n // tn),
        in_specs=[
            pl.BlockSpec((b, d), lambda l, j: (0, 0)),
            pl.BlockSpec((1, d, tn), lambda l, j: (l, 0, j)),
            pl.BlockSpec((1, 1, tn), lambda l, j: (l, 0, j)),
        ],
        out_specs=pl.BlockSpec((1, b, tn), lambda l, j: (l, 0, j)),
        out_shape=jax.ShapeDtypeStruct((depth, b, n), F32),
        compiler_params=_params("parallel", "parallel"),
        name="ada_mod",
    )(c, ada_w, ada_b.reshape(depth, 1, n))


def _proj_kernel(x_ref, sh_ref, sc_ref, w_ref, qkv_ref, z_ref, ab_ref, qb_ref, kvb_ref, uc_ref, gate_ref):
    h = (_layer_norm(x_ref[...]) * (1.0 + sc_ref[0]) + sh_ref[0]).astype(BF16)

    def seg(start, width):
        return _dot(h, w_ref[:, start:start + width])

    for j in range(QKV_W // 512):
        qkv_ref[:, j * 512:(j + 1) * 512] = seg(SEG_QKV + j * 512, 512).astype(BF16)
    z_ref[...] = seg(SEG_Z, GDN_V).astype(BF16)
    ab_ref[...] = seg(SEG_AB, AB_W)
    qb_ref[...] = seg(SEG_QB, SWA_Q).astype(BF16)
    kvb_ref[...] = seg(SEG_KVB, 2 * SWA_KV).astype(BF16)
    uc_ref[...] = seg(SEG_UC, POOL_DIM).astype(BF16)
    for j in range(N_BRANCH * D_MODEL // 512):
        gate_ref[:, j * 512:(j + 1) * 512] = _sigmoid(seg(SEG_GATE + j * 512, 512)).astype(BF16)


def _in_proj(x, mod3, w, seq):
    t, d = x.shape
    tm = TOKEN_TILE
    per_b = seq // tm
    widths = (QKV_W, GDN_V, AB_W, SWA_Q, 2 * SWA_KV, POOL_DIM, N_BRANCH * D_MODEL)
    dtypes = (BF16, BF16, F32, BF16, BF16, BF16, BF16)
    return pl.pallas_call(
        _proj_kernel,
        grid=(t // tm,),
        in_specs=[
            pl.BlockSpec((tm, d), lambda i: (i, 0)),
            pl.BlockSpec((1, 1, d), lambda i: (i // per_b, 0, 0)),
            pl.BlockSpec((1, 1, d), lambda i: (i // per_b, 0, 1)),
            pl.BlockSpec((d, PROJ_W), lambda i: (0, 0)),
        ],
        out_specs=[pl.BlockSpec((tm, n), lambda i: (i, 0)) for n in widths],
        out_shape=[jax.ShapeDtypeStruct((t, n), dt) for n, dt in zip(widths, dtypes)],
        compiler_params=_params("parallel"),
        name="in_proj",
    )(x, mod3, mod3, w)


def _softplus(x):
    return jnp.maximum(x, 0.0) + jnp.log(1.0 + jnp.exp(-jnp.abs(x)))


def _unit_lower_inverse(lm, row, col):
    n = lm.shape[-1]
    eye = (row == col).astype(F32)
    t = None
    s = 1
    while s < n:
        join = ((row // (2 * s)) == (col // (2 * s))) & ((row % (2 * s)) >= s) & ((col % (2 * s)) < s)
        cm = jnp.where(join, lm, 0.0)
        if t is None:
            t = eye - cm
        else:
            p = _bdot(t, cm)
            t = t - _bdot(p, t)
        s *= 2
    return t


def _gdn_kernel(qkv_ref, ab_ref, z_ref, convw_ref, gpar_ref, nw_ref, o_ref, carry_ref, state_ref, *, tb):
    c_len = GDN_CHUNK
    nc = tb // c_len

    @pl.when(pl.program_id(1) == 0)
    def _():
        carry_ref[...] = jnp.zeros_like(carry_ref)
        state_ref[...] = jnp.zeros_like(state_ref)

    xin = qkv_ref[...].astype(F32)
    xe = jnp.concatenate([carry_ref[...], xin], axis=0)
    carry_ref[...] = xin[tb - 8:, :]
    cw = convw_ref[...]
    conv = xin * cw[3:4]
    for j in range(GDN_CONV - 1):
        conv = conv + xe[5 + j:5 + j + tb] * cw[j:j + 1]
    act = conv * _sigmoid(conv)

    ab = ab_ref[...]
    gpar = gpar_ref[...]
    g_all = gpar[0:1] * _softplus(ab + gpar[1:2])
    beta_all = _sigmoid(ab)
    pos = lax.broadcasted_iota(jnp.int32, (tb, AB_W), 0) % c_len
    gc_all = g_all
    s = 1
    while s < c_len:
        gc_all = gc_all + jnp.where(pos >= s, pltpu.roll(gc_all, s, axis=0), 0.0)
        s *= 2

    row = lax.broadcasted_iota(jnp.int32, (nc, c_len, c_len), 1)
    col = lax.broadcasted_iota(jnp.int32, (nc, c_len, c_len), 2)
    causal = row >= col
    strict = row > col
    diag = row == col

    pre = []
    for h in range(GDN_HEADS):
        q = act[:, h * GDN_DK:(h + 1) * GDN_DK]
        k = act[:, GDN_QK + h * GDN_DK:GDN_QK + (h + 1) * GDN_DK]
        v = act[:, 2 * GDN_QK + h * GDN_DV:2 * GDN_QK + (h + 1) * GDN_DV]
        q = q * lax.rsqrt(jnp.sum(q * q, -1, keepdims=True) + 1e-6) * (GDN_DK ** -0.5)
        k = k * lax.rsqrt(jnp.sum(k * k, -1, keepdims=True) + 1e-6)
        gc = jnp.broadcast_to(gc_all[:, h:h + 1], (tb, GDN_DK)).reshape(nc, c_len, GDN_DK)
        beta = jnp.broadcast_to(beta_all[:, GDN_HEADS + h:GDN_HEADS + h + 1], (tb, GDN_DK)).reshape(nc, c_len, GDN_DK)
        q = q.reshape(nc, c_len, GDN_DK)
        k = k.reshape(nc, c_len, GDN_DK)
        v = v.reshape(nc, c_len, GDN_DV)

        gc_i = gc[:, :, :c_len]
        gc_j = jnp.sum(jnp.where(diag, gc_i, 0.0), axis=1, keepdims=True)
        decay = jnp.where(causal, jnp.exp(jnp.where(causal, gc_i - gc_j, 0.0)), 0.0)
        eg = jnp.exp(gc)
        gc_last = gc[:, c_len - 1:c_len, :]
        k_beta = k * beta
        kb16 = k.astype(BF16)
        lower = jnp.where(strict, _bdot_nt(k_beta.astype(BF16), kb16) * decay, 0.0)
        attn = jnp.where(causal, _bdot_nt(q.astype(BF16), kb16) * decay, 0.0)
        tinv = _unit_lower_inverse(lower, row, col)
        u = _bdot(tinv, v * beta)
        w = _bdot(tinv, k_beta * eg)
        q_g = (q * eg).astype(BF16)
        k_g = (k * jnp.exp(gc_last - gc)).astype(BF16)
        pre.append((q_g, k_g, u, w.astype(BF16), attn.astype(BF16), jnp.exp(gc_last)))

    outs = [[None] * nc for _ in range(GDN_HEADS)]
    states = [state_ref[h] for h in range(GDN_HEADS)]
    for c in range(nc):
        for h in range(GDN_HEADS):
            q_g, k_g, u, w, attn, g_last = pre[h]
            st = states[h]
            st16 = st.astype(BF16)
            v_new = u[c] - _dot(w[c], st16)
            outs[h][c] = _dot(q_g[c], st16) + _dot(attn[c], v_new.astype(BF16))
            states[h] = st * g_last[c] + _dot_tn(k_g[c], v_new.astype(BF16))
    nw = nw_ref[...]
    for h in range(GDN_HEADS):
        state_ref[h] = states[h]
        o = jnp.concatenate(outs[h], axis=0)
        zh = z_ref[:, h * GDN_DV:(h + 1) * GDN_DV].astype(F32)
        o = o * lax.rsqrt(jnp.mean(o * o, -1, keepdims=True) + RMS_EPS) * nw * (zh * _sigmoid(zh))
        o_ref[:, h * GDN_DV:(h + 1) * GDN_DV] = o.astype(BF16)


def _gdn(qkv, ab, z, conv_w, gpar, norm_w, batch, seq):
    t = qkv.shape[0]
    tb = TOKEN_TILE
    per_b = seq // tb
    return pl.pallas_call(
        functools.partial(_gdn_kernel, tb=tb),
        grid=(batch, per_b),
        in_specs=[
            pl.BlockSpec((tb, QKV_W), lambda b, j: (b * per_b + j, 0)),
            pl.BlockSpec((tb, AB_W), lambda b, j: (b * per_b + j, 0)),
            pl.BlockSpec((tb, GDN_V), lambda b, j: (b * per_b + j, 0)),
            pl.BlockSpec((GDN_CONV, QKV_W), lambda b, j: (0, 0)),
            pl.BlockSpec((2, AB_W), lambda b, j: (0, 0)),
            pl.BlockSpec((1, GDN_DV), lambda b, j: (0, 0)),
        ],
        out_specs=pl.BlockSpec((tb, GDN_V), lambda b, j: (b * per_b + j, 0)),
        out_shape=jax.ShapeDtypeStruct((t, GDN_V), BF16),
        scratch_shapes=[pltpu.VMEM((8, QKV_W), F32), pltpu.VMEM((GDN_HEADS, GDN_DK, GDN_DV), F32)],
        compiler_params=_params("parallel", "arbitrary"),
        name="gdn",
    )(qkv, ab, z, conv_w, gpar, norm_w)


def _swa_kernel(sink_ref, q_ref, kv_ref, kvp_ref, bias_ref, o_ref, *, tq):
    first = pl.program_id(1) == 0
    kv = jnp.concatenate([kvp_ref[...], kv_ref[...]], axis=0)
    key_in_prev = lax.broadcasted_iota(jnp.int32, (WINDOW, 2 * WINDOW), 1) < WINDOW
    scale = SWA_DH ** -0.5
    for w in range(tq // WINDOW):
        qw = q_ref[w * WINDOW:(w + 1) * WINDOW, :]
        kvw = kv[w * WINDOW:(w + 2) * WINDOW, :]
        heads = []
        for hq in range(SWA_HQ):
            hk = hq // GQA_GROUP
            q = qw[:, hq * SWA_DH:(hq + 1) * SWA_DH]
            k = kvw[:, hk * SWA_DH:(hk + 1) * SWA_DH]
            v = kvw[:, SWA_KV + hk * SWA_DH:SWA_KV + (hk + 1) * SWA_DH]
            s = _dot_nt(q, k) * scale + bias_ref[hq]
            if w == 0:
                s = jnp.where(key_in_prev & first, NEG_BIG, s)
            sink = sink_ref[hq]
            m = jnp.maximum(jnp.max(s, -1, keepdims=True), sink)
            p = jnp.exp(s - m)
            denom = jnp.sum(p, -1, keepdims=True) + jnp.exp(sink - m)
            heads.append(_dot(p.astype(BF16), v) / denom)
        o_ref[w * WINDOW:(w + 1) * WINDOW, :] = jnp.concatenate(heads, axis=1).astype(BF16)


def _swa_bias():
    qi = jnp.arange(WINDOW)[:, None]
    ki = jnp.arange(2 * WINDOW)[None, :]
    dist = qi + WINDOW - ki
    valid = (dist >= 0) & (dist < WINDOW)
    slopes = 2.0 ** (-8.0 * jnp.arange(1, SWA_HQ + 1, dtype=F32) / SWA_HQ)
    bias = -slopes[:, None, None] * dist.astype(F32)[None]
    return jnp.where(valid[None], bias, NEG_BIG)


def _swa(qb, kvb, sinks, batch, seq):
    t = qb.shape[0]
    tq = TOKEN_TILE
    per_b = seq // tq
    wpb = tq // WINDOW
    return pl.pallas_call(
        functools.partial(_swa_kernel, tq=tq),
        grid=(batch, per_b),
        in_specs=[
            pl.BlockSpec(memory_space=pltpu.SMEM),
            pl.BlockSpec((tq, SWA_Q), lambda b, j: (b * per_b + j, 0)),
            pl.BlockSpec((tq, 2 * SWA_KV), lambda b, j: (b * per_b + j, 0)),
            pl.BlockSpec((WINDOW, 2 * SWA_KV), lambda b, j: (jnp.maximum((b * per_b + j) * wpb - 1, 0), 0)),
            pl.BlockSpec((SWA_HQ, WINDOW, 2 * WINDOW), lambda b, j: (0, 0, 0)),
        ],
        out_specs=pl.BlockSpec((tq, SWA_Q), lambda b, j: (b * per_b + j, 0)),
        out_shape=jax.ShapeDtypeStruct((t, SWA_Q), BF16),
        compiler_params=_params("parallel", "parallel"),
        name="swa",
    )(sinks, qb, kvb, kvb, _swa_bias())


def _deepnorm(x, y, gt, g, b):
    return _layer_norm(DEEPNORM_ALPHA * x + (1.0 + gt) * y) * g + b


def _merge_kernel(x_ref, ya_ref, yb_ref, uc_ref, ucp_ref, gate_ref, gt_ref, poolw_ref, pscale_ref,
                  wpa_ref, wpb_ref, wpc_ref, wo_ref, lng_ref, lnb_ref, o_ref, *, tm):
    j = pl.program_id(1)
    u = uc_ref[...].astype(F32)
    halo = jnp.where(j == 0, 0.0, ucp_ref[...].astype(F32))
    ue = jnp.concatenate([halo, u], axis=0)
    tpos = (j * tm + lax.broadcasted_iota(jnp.int32, (tm, POOL_GDIM), 0) + 1).astype(F32)
    ycs = []
    for gi, win in enumerate(POOL_WINDOWS):
        a = ue[:, gi * POOL_GDIM:(gi + 1) * POOL_GDIM]
        span = 1
        while span < win:
            a = a[span:] + a[:-span]
            span *= 2
        lo = MAX_POOL - win + 1
        d = a[lo:lo + tm] / jnp.minimum(tpos, float(win)) - u[:, gi * POOL_GDIM:(gi + 1) * POOL_GDIM]
        ycs.append(_dot(d.astype(BF16), poolw_ref[gi]))
    yc = jnp.concatenate(ycs, axis=1) * pscale_ref[...]
    merged = gate_ref[:, 0:D_MODEL].astype(F32) * _dot(ya_ref[...], wpa_ref[...])
    merged = merged + gate_ref[:, D_MODEL:2 * D_MODEL].astype(F32) * _dot(yb_ref[...], wpb_ref[...])
    merged = merged + gate_ref[:, 2 * D_MODEL:3 * D_MODEL].astype(F32) * _dot(yc.astype(BF16), wpc_ref[...])
    y = _dot(merged.astype(BF16), wo_ref[...])
    o_ref[...] = _deepnorm(x_ref[...], y, gt_ref[0], lng_ref[...], lnb_ref[...])


def _merge(x, ya, yb, uc, gates, mod3, pool_w, pool_scale, w_pa, w_pb, w_pc, w_o, ln_g, ln_b, batch, seq):
    t, d = x.shape
    tm = TOKEN_TILE
    per_b = seq // tm
    hpb = tm // MAX_POOL

    def tok(n):
        return pl.BlockSpec((tm, n), lambda b, j: (b * per_b + j, 0))

    def full(shape):
        return pl.BlockSpec(shape, lambda b, j: (0,) * len(shape))

    return pl.pallas_call(
        functools.partial(_merge_kernel, tm=tm),
        grid=(batch, per_b),
        in_specs=[
            tok(d), tok(GDN_V), tok(SWA_Q), tok(POOL_DIM),
            pl.BlockSpec((MAX_POOL, POOL_DIM), lambda b, j: (jnp.maximum((b * per_b + j) * hpb - 1, 0), 0)),
            tok(N_BRANCH * d),
            pl.BlockSpec((1, 1, d), lambda b, j: (b, 0, 2)),
            full((POOL_GROUPS, POOL_GDIM, POOL_GDIM)), full((1, POOL_DIM)),
            full((GDN_V, d)), full((SWA_Q, d)), full((POOL_DIM, d)), full((d, d)),
            full((1, d)), full((1, d)),
        ],
        out_specs=tok(d),
        out_shape=jax.ShapeDtypeStruct((t, d), F32),
        compiler_params=_params("parallel", "parallel"),
        name="merge",
    )(x, ya, yb, uc, uc, gates, mod3, pool_w, pool_scale, w_pa, w_pb, w_pc, w_o, ln_g, ln_b)


def _router_kernel(x_ref, sh_ref, sc_ref, rw_ref, rb_ref, h_ref, topi_ref, topg_ref):
    h = _layer_norm(x_ref[...]) * (1.0 + sc_ref[0]) + sh_ref[0]
    h_ref[...] = h.astype(BF16)
    logits = jnp.dot(h, rw_ref[...], preferred_element_type=F32, precision=lax.Precision.HIGHEST) + rb_ref[...]
    lane = lax.broadcasted_iota(jnp.int32, logits.shape, 1)
    vals, idxs = [], []
    for _ in range(TOP_K):
        m = jnp.max(logits, -1, keepdims=True)
        idx = jnp.min(jnp.where(logits == m, lane, N_EXPERTS), -1, keepdims=True)
        vals.append(m)
        idxs.append(idx)
        logits = jnp.where(lane == idx, -jnp.inf, logits)
    es = [jnp.exp(v - vals[0]) for v in vals]
    denom = es[0] + es[1] + es[2] + es[3]
    topi_ref[...] = jnp.concatenate(idxs, axis=1)
    topg_ref[...] = jnp.concatenate([e / denom for e in es], axis=1)


def _router(x, mod3, router_w, router_b, seq):
    t, d = x.shape
    tm = TOKEN_TILE
    per_b = seq // tm
    return pl.pallas_call(
        _router_kernel,
        grid=(t // tm,),
        in_specs=[
            pl.BlockSpec((tm, d), lambda i: (i, 0)),
            pl.BlockSpec((1, 1, d), lambda i: (i // per_b, 0, 3)),
            pl.BlockSpec((1, 1, d), lambda i: (i // per_b, 0, 4)),
            pl.BlockSpec((d, N_EXPERTS), lambda i: (0, 0)),
            pl.BlockSpec((1, N_EXPERTS), lambda i: (0, 0)),
        ],
        out_specs=[
            pl.BlockSpec((tm, d), lambda i: (i, 0)),
            pl.BlockSpec((tm, TOP_K), lambda i: (i, 0)),
            pl.BlockSpec((tm, TOP_K), lambda i: (i, 0)),
        ],
        out_shape=[
            jax.ShapeDtypeStruct((t, d), BF16),
            jax.ShapeDtypeStruct((t, TOP_K), jnp.int32),
            jax.ShapeDtypeStruct((t, TOP_K), F32),
        ],
        compiler_params=_params("parallel"),
        name="router",
    )(x, mod3, mod3, router_w, router_b)


def _dispatch_plan(topi, n_blocks):
    t = topi.shape[0]
    onehot = (topi[:, :, None] == jnp.arange(N_EXPERTS, dtype=jnp.int32)).any(axis=1).astype(jnp.int32)
    csum = jnp.cumsum(onehot, axis=0)
    counts = csum[-1]
    padded = (counts + ROUTE_BLOCK - 1) // ROUTE_BLOCK * ROUTE_BLOCK
    ends_p = jnp.cumsum(padded)
    pstart = ends_p - padded
    rank = jnp.take_along_axis(csum - onehot, topi, axis=1)
    dest = (pstart[topi] + rank).astype(jnp.int32)
    tok = jnp.broadcast_to(jnp.arange(t, dtype=jnp.int32)[:, None], (t, TOP_K))
    buf_tok = jnp.zeros((n_blocks * ROUTE_BLOCK,), jnp.int32).at[dest.reshape(-1)].set(tok.reshape(-1))
    block_e = jnp.minimum(jnp.searchsorted(ends_p, jnp.arange(n_blocks) * ROUTE_BLOCK, side='right'),
                          N_EXPERTS - 1).astype(jnp.int32)
    n_used = (ends_p[-1] // ROUTE_BLOCK).astype(jnp.int32).reshape(1)
    return dest, buf_tok, block_e, n_used


def _expert_kernel(be_ref, nu_ref, x_ref, w1_ref, b1_ref, w2_ref, b2_ref, o_ref):
    @pl.when(pl.program_id(0) < nu_ref[0])
    def _():
        gu = _dot(x_ref[...], w1_ref[0]) + b1_ref[0]
        glu = jnp.minimum(gu[:, :D_FF], SWIGLU_LIMIT)
        lin = jnp.clip(gu[:, D_FF:], -SWIGLU_LIMIT, SWIGLU_LIMIT)
        act = glu * _sigmoid(SWIGLU_ALPHA * glu) * (lin + 1.0)
        o_ref[...] = (_dot(act.astype(BF16), w2_ref[0]) + b2_ref[0]).astype(BF16)

    @pl.when(pl.program_id(0) >= nu_ref[0])
    def _():
        o_ref[...] = jnp.zeros_like(o_ref)


def _experts(xs, block_e, n_used, w1, b1, w2, b2):
    rows, d = xs.shape
    n_blocks = rows // ROUTE_BLOCK
    grid_spec = pltpu.PrefetchScalarGridSpec(
        num_scalar_prefetch=2,
        grid=(n_blocks,),
        in_specs=[
            pl.BlockSpec((ROUTE_BLOCK, d), lambda i, be, nu: (i, 0)),
            pl.BlockSpec((1, d, 2 * D_FF), lambda i, be, nu: (be[i], 0, 0)),
            pl.BlockSpec((1, 1, 2 * D_FF), lambda i, be, nu: (be[i], 0, 0)),
            pl.BlockSpec((1, D_FF, d), lambda i, be, nu: (be[i], 0, 0)),
            pl.BlockSpec((1, 1, d), lambda i, be, nu: (be[i], 0, 0)),
        ],
        out_specs=pl.BlockSpec((ROUTE_BLOCK, d), lambda i, be, nu: (i, 0)),
    )
    return pl.pallas_call(
        _expert_kernel,
        grid_spec=grid_spec,
        out_shape=jax.ShapeDtypeStruct((rows, d), BF16),
        compiler_params=_params("arbitrary"),
        name="experts",
    )(block_e, n_used, xs, w1, b1, w2, b2)


def _combine_kernel(x_ref, yg_ref, topg_ref, gt_ref, lng_ref, lnb_ref, o_ref):
    g = topg_ref[...]
    y = g[:, 0:1] * yg_ref[0].astype(F32)
    for k in range(1, TOP_K):
        y = y + g[:, k:k + 1] * yg_ref[k].astype(F32)
    o_ref[...] = _deepnorm(x_ref[...], y, gt_ref[0], lng_ref[...], lnb_ref[...])


def _combine(x, yg, topg, mod3, ln_g, ln_b, seq):
    t, d = x.shape
    tm = TOKEN_TILE
    per_b = seq // tm
    return pl.pallas_call(
        _combine_kernel,
        grid=(t // tm,),
        in_specs=[
            pl.BlockSpec((tm, d), lambda i: (i, 0)),
            pl.BlockSpec((TOP_K, tm, d), lambda i: (0, i, 0)),
            pl.BlockSpec((tm, TOP_K), lambda i: (i, 0)),
            pl.BlockSpec((1, 1, d), lambda i: (i // per_b, 0, 5)),
            pl.BlockSpec((1, d), lambda i: (0, 0)),
            pl.BlockSpec((1, d), lambda i: (0, 0)),
        ],
        out_specs=pl.BlockSpec((tm, d), lambda i: (i, 0)),
        out_shape=jax.ShapeDtypeStruct((t, d), F32),
        compiler_params=_params("parallel"),
        name="combine",
    )(x, yg, topg, mod3, ln_g, ln_b)


def _rearranged_w_in(w_in):
    d = w_in.shape[0]
    o = 0
    cols = {}
    for name, n in (("qa", GDN_QK), ("ka", GDN_QK), ("va", GDN_V), ("z", GDN_V), ("a", GDN_HEADS), ("b", GDN_HEADS),
                    ("qb", SWA_Q), ("kb", SWA_KV), ("vb", SWA_KV), ("uc", POOL_DIM), ("gate", N_BRANCH * D_MODEL)):
        cols[name] = w_in[:, o:o + n]
        o += n
    ab_pad = jnp.zeros((d, AB_W - 2 * GDN_HEADS), w_in.dtype)
    return jnp.concatenate([cols["qa"], cols["ka"], cols["va"], cols["z"], cols["a"], cols["b"], ab_pad,
                            cols["qb"], cols["kb"], cols["vb"], cols["uc"], cols["gate"]], axis=1).astype(BF16)


def kernel(x, c, ada_w, ada_b, w_in, conv_w, a_log, dt_bias, gdn_norm_w, sinks, pool_w, pool_scale, w_pa, w_pb, w_pc, w_o, ln1_g, ln1_b, ln2_g, ln2_b, router_w, router_b, exp_w1, exp_b1, exp_w2, exp_b2):
    batch, seq, d = x.shape
    t = batch * seq
    n_blocks = (t * TOP_K + ROUTE_BLOCK - 1) // ROUTE_BLOCK + N_EXPERTS
    mod = _ada_mod(c, ada_w, ada_b)
    xt = x.reshape(t, d)
    lane_pad = jnp.zeros((AB_W - GDN_HEADS,), F32)
    for l in range(DEPTH):
        mod3 = mod[l].reshape(batch, 1, 6 * d)
        qkv, z, ab, qb, kvb, uc, gates = _in_proj(xt, mod3, _rearranged_w_in(w_in[l]), seq)
        gpar = jnp.stack([jnp.concatenate([-jnp.exp(a_log[l]), lane_pad]), jnp.concatenate([dt_bias[l], lane_pad])])
        ya = _gdn(qkv, ab, z, conv_w[l], gpar, gdn_norm_w[l].reshape(1, GDN_DV), batch, seq)
        yb = _swa(qb, kvb, sinks[l], batch, seq)
        xt = _merge(xt, ya, yb, uc, gates, mod3, pool_w[l].astype(BF16), pool_scale[l].reshape(1, POOL_DIM),
                    w_pa[l].astype(BF16), w_pb[l].astype(BF16), w_pc[l].astype(BF16), w_o[l].astype(BF16),
                    ln1_g[l].reshape(1, d), ln1_b[l].reshape(1, d), batch, seq)
        h2, topi, topg = _router(xt, mod3, router_w[l], router_b[l].reshape(1, N_EXPERTS), seq)
        dest, buf_tok, block_e, n_used = _dispatch_plan(topi, n_blocks)
        xs = h2[buf_tok]
        ys = _experts(xs, block_e, n_used, exp_w1[l].astype(BF16), exp_b1[l].reshape(N_EXPERTS, 1, 2 * D_FF),
                      exp_w2[l].astype(BF16), exp_b2[l].reshape(N_EXPERTS, 1, d))
        yg = ys[dest.T]
        xt = _combine(xt, yg, topg, mod3, ln2_g[l].reshape(1, d), ln2_b[l].reshape(1, d), seq)
    return xt.reshape(batch, seq, d)
```

```python
import functools

import jax
import jax.numpy as jnp
from jax import lax
from jax.experimental import pallas as pl
from jax.experimental.pallas import tpu as pltpu
from jax.experimental.pallas import tpu_sc as plsc

D_MODEL = 1024
DEPTH = 4
GDN_HEADS = 4
GDN_DK = 128
GDN_DV = 128
GDN_CONV = 4
GDN_CHUNK = 64
SWA_HQ = 8
SWA_HKV = 2
SWA_DH = 64
WINDOW = 128
POOL_WINDOWS = (2, 4, 8, 16)
POOL_GROUPS = 4
POOL_GDIM = 128
N_BRANCH = 3
N_EXPERTS = 32
TOP_K = 4
D_FF = 1024
ROUTE_BLOCK = 256
SWIGLU_ALPHA = 1.702
SWIGLU_LIMIT = 7.0
LN_EPS = 1e-5
RMS_EPS = 1e-6
DEEPNORM_ALPHA = (2 * DEPTH) ** 0.25

GDN_QK = GDN_HEADS * GDN_DK
GDN_V = GDN_HEADS * GDN_DV
SWA_Q = SWA_HQ * SWA_DH
SWA_KV = SWA_HKV * SWA_DH
POOL_DIM = POOL_GROUPS * POOL_GDIM
GQA_GROUP = SWA_HQ // SWA_HKV
MAX_POOL = max(POOL_WINDOWS)

QKV_W = 2 * GDN_QK + GDN_V
AB_W = 128
SEG_QKV = 0
SEG_Z = SEG_QKV + QKV_W
SEG_AB = SEG_Z + GDN_V
SEG_QB = SEG_AB + AB_W
SEG_KVB = SEG_QB + SWA_Q
SEG_UC = SEG_KVB + 2 * SWA_KV
SEG_GATE = SEG_UC + POOL_DIM
PROJ_W = SEG_GATE + N_BRANCH * D_MODEL

TOKEN_TILE = 512
NEG_BIG = -1e30
VMEM_LIMIT = 56 * 1024 * 1024

PACKED_D = D_MODEL // 2
SC_CORES = 2
SC_SUBCORES = 16
SC_WORKERS = SC_CORES * SC_SUBCORES
SC_CHUNK = 64

F32 = jnp.float32
BF16 = jnp.bfloat16
U32 = jnp.uint32
HI_MASK = 0xFFFF0000


def _pack_rows(x):
    bits = pltpu.bitcast(x.astype(BF16).astype(F32), U32)
    return (bits[:, :PACKED_D] >> 16) | (bits[:, PACKED_D:] & jnp.uint32(HI_MASK))


def _unpack_rows(w):
    lo = pltpu.bitcast(w << 16, F32)
    hi = pltpu.bitcast(w & jnp.uint32(HI_MASK), F32)
    return jnp.concatenate([lo, hi], axis=1)


def _params(*sem):
    return pltpu.CompilerParams(dimension_semantics=sem, vmem_limit_bytes=VMEM_LIMIT)


def _sigmoid(x):
    return 1.0 / (1.0 + jnp.exp(-x))


def _layer_norm(x):
    mu = jnp.mean(x, -1, keepdims=True)
    xc = x - mu
    var = jnp.mean(xc * xc, -1, keepdims=True)
    return xc * lax.rsqrt(var + LN_EPS)


def _dot(a, b):
    return jnp.dot(a, b, preferred_element_type=F32)


def _dot_nt(a, b):
    return lax.dot_general(a, b, (((1,), (1,)), ((), ())), preferred_element_type=F32)


def _dot_tn(a, b):
    return lax.dot_general(a, b, (((0,), (0,)), ((), ())), preferred_element_type=F32)


def _bdot(a, b):
    return lax.dot_general(a, b, (((2,), (1,)), ((0,), (0,))), preferred_element_type=F32)


def _bdot_nt(a, b):
    return lax.dot_general(a, b, (((2,), (2,)), ((0,), (0,))), preferred_element_type=F32)


def _ada_kernel(c_ref, w_ref, b_ref, o_ref):
    c = c_ref[...]
    cond = c * _sigmoid(c)
    o_ref[0] = _dot(cond.astype(BF16), w_ref[0].astype(BF16)) + b_ref[0]


def _ada_mod(c, ada_w, ada_b):
    depth, d, n = ada_w.shape
    b = c.shape[0]
    tn = 1024
    return pl.pallas_call(
        _ada_kernel,
        grid=(depth, n // tn),
        in_specs=[
            pl.BlockSpec((b, d), lambda l, j: (0, 0)),
            pl.BlockSpec((1, d, tn), lambda l, j: (l, 0, j)),
            pl.BlockSpec((1, 1, tn), lambda l, j: (l, 0, j)),
        ],
        out_specs=pl.BlockSpec((1, b, tn), lambda l, j: (l, 0, j)),
        out_shape=jax.ShapeDtypeStruct((depth, b, n), F32),
        compiler_params=_params("parallel", "parallel"),
        name="ada_mod",
    )(c, ada_w, ada_b.reshape(depth, 1, n))


def _proj_kernel(x_ref, sh_ref, sc_ref, w_ref, qkv_ref, z_ref, ab_ref, qb_ref, kvb_ref, uc_ref, gate_ref):
    h = (_layer_norm(x_ref[...]) * (1.0 + sc_ref[0]) + sh_ref[0]).astype(BF16)

    def seg(start, width):
        return _dot(h, w_ref[:, start:start + width])

    for j in range(QKV_W // 512):
        qkv_ref[:, j * 512:(j + 1) * 512] = seg(SEG_QKV + j * 512, 512).astype(BF16)
    z_ref[...] = seg(SEG_Z, GDN_V).astype(BF16)
    ab_ref[...] = seg(SEG_AB, AB_W)
    qb_ref[...] = seg(SEG_QB, SWA_Q).astype(BF16)
    kvb_ref[...] = seg(SEG_KVB, 2 * SWA_KV).astype(BF16)
    uc_ref[...] = seg(SEG_UC, POOL_DIM).astype(BF16)
    for j in range(N_BRANCH * D_MODEL // 512):
        gate_ref[:, j * 512:(j + 1) * 512] = _sigmoid(seg(SEG_GATE + j * 512, 512)).astype(BF16)


def _in_proj(x, mod3, w, seq):
    t, d = x.shape
    tm = TOKEN_TILE
    per_b = seq // tm
    widths = (QKV_W, GDN_V, AB_W, SWA_Q, 2 * SWA_KV, POOL_DIM, N_BRANCH * D_MODEL)
    dtypes = (BF16, BF16, F32, BF16, BF16, BF16, BF16)
    return pl.pallas_call(
        _proj_kernel,
        grid=(t // tm,),
        in_specs=[
            pl.BlockSpec((tm, d), lambda i: (i, 0)),
            pl.BlockSpec((1, 1, d), lambda i: (i // per_b, 0, 0)),
            pl.BlockSpec((1, 1, d), lambda i: (i // per_b, 0, 1)),
            pl.BlockSpec((d, PROJ_W), lambda i: (0, 0)),
        ],
        out_specs=[pl.BlockSpec((tm, n), lambda i: (i, 0)) for n in widths],
        out_shape=[jax.ShapeDtypeStruct((t, n), dt) for n, dt in zip(widths, dtypes)],
        compiler_params=_params("parallel"),
        name="in_proj",
    )(x, mod3, mod3, w)


def _softplus(x):
    return jnp.maximum(x, 0.0) + jnp.log(1.0 + jnp.exp(-jnp.abs(x)))


def _unit_lower_inverse(lm, row, col):
    n = lm.shape[-1]
    eye = (row == col).astype(F32)
    t = None
    s = 1
    while s < n:
        join = ((row // (2 * s)) == (col // (2 * s))) & ((row % (2 * s)) >= s) & ((col % (2 * s)) < s)
        cm = jnp.where(join, lm, 0.0)
        if t is None:
            t = eye - cm
        else:
            p = _bdot(t, cm)
            t = t - _bdot(p, t)
        s *= 2
    return t


def _gdn_kernel(qkv_ref, ab_ref, z_ref, convw_ref, gpar_ref, nw_ref, o_ref, carry_ref, state_ref, *, tb):
    c_len = GDN_CHUNK
    nc = tb // c_len

    @pl.when(pl.program_id(1) == 0)
    def _():
        carry_ref[...] = jnp.zeros_like(carry_ref)
        state_ref[...] = jnp.zeros_like(state_ref)

    xin = qkv_ref[...].astype(F32)
    xe = jnp.concatenate([carry_ref[...], xin], axis=0)
    carry_ref[...] = xin[tb - 8:, :]
    cw = convw_ref[...]
    conv = xin * cw[3:4]
    for j in range(GDN_CONV - 1):
        conv = conv + xe[5 + j:5 + j + tb] * cw[j:j + 1]
    act = conv * _sigmoid(conv)

    ab = ab_ref[...]
    gpar = gpar_ref[...]
    g_all = gpar[0:1] * _softplus(ab + gpar[1:2])
    beta_all = _sigmoid(ab)
    pos = lax.broadcasted_iota(jnp.int32, (tb, AB_W), 0) % c_len
    gc_all = g_all
    s = 1
    while s < c_len:
        gc_all = gc_all + jnp.where(pos >= s, pltpu.roll(gc_all, s, axis=0), 0.0)
        s *= 2

    row = lax.broadcasted_iota(jnp.int32, (nc, c_len, c_len), 1)
    col = lax.broadcasted_iota(jnp.int32, (nc, c_len, c_len), 2)
    causal = row >= col
    strict = row > col
    diag = row == col

    pre = []
    for h in range(GDN_HEADS):
        q = act[:, h * GDN_DK:(h + 1) * GDN_DK]
        k = act[:, GDN_QK + h * GDN_DK:GDN_QK + (h + 1) * GDN_DK]
        v = act[:, 2 * GDN_QK + h * GDN_DV:2 * GDN_QK + (h + 1) * GDN_DV]
        q = q * lax.rsqrt(jnp.sum(q * q, -1, keepdims=True) + 1e-6) * (GDN_DK ** -0.5)
        k = k * lax.rsqrt(jnp.sum(k * k, -1, keepdims=True) + 1e-6)
        gc = jnp.broadcast_to(gc_all[:, h:h + 1], (tb, GDN_DK)).reshape(nc, c_len, GDN_DK)
        beta = jnp.broadcast_to(beta_all[:, GDN_HEADS + h:GDN_HEADS + h + 1], (tb, GDN_DK)).reshape(nc, c_len, GDN_DK)
        q = q.reshape(nc, c_len, GDN_DK)
        k = k.reshape(nc, c_len, GDN_DK)
        v = v.reshape(nc, c_len, GDN_DV)

        gc_i = gc[:, :, :c_len]
        gc_j = jnp.sum(jnp.where(diag, gc_i, 0.0), axis=1, keepdims=True)
        decay = jnp.where(causal, jnp.exp(jnp.where(causal, gc_i - gc_j, 0.0)), 0.0)
        eg = jnp.exp(gc)
        gc_last = gc[:, c_len - 1:c_len, :]
        k_beta = k * beta
        kb16 = k.astype(BF16)
        lower = jnp.where(strict, _bdot_nt(k_beta.astype(BF16), kb16) * decay, 0.0)
        attn = jnp.where(causal, _bdot_nt(q.astype(BF16), kb16) * decay, 0.0)
        tinv = _unit_lower_inverse(lower, row, col)
        u = _bdot(tinv, v * beta)
        w = _bdot(tinv, k_beta * eg)
        q_g = (q * eg).astype(BF16)
        k_g = (k * jnp.exp(gc_last - gc)).astype(BF16)
        pre.append((q_g, k_g, u, w.astype(BF16), attn.astype(BF16), jnp.exp(gc_last)))

    outs = [[None] * nc for _ in range(GDN_HEADS)]
    states = [state_ref[h] for h in range(GDN_HEADS)]
    for c in range(nc):
        for h in range(GDN_HEADS):
            q_g, k_g, u, w, attn, g_last = pre[h]
            st = states[h]
            st16 = st.astype(BF16)
            v_new = u[c] - _dot(w[c], st16)
            outs[h][c] = _dot(q_g[c], st16) + _dot(attn[c], v_new.astype(BF16))
            states[h] = st * g_last[c] + _dot_tn(k_g[c], v_new.astype(BF16))
    nw = nw_ref[...]
    for h in range(GDN_HEADS):
        state_ref[h] = states[h]
        o = jnp.concatenate(outs[h], axis=0)
        zh = z_ref[:, h * GDN_DV:(h + 1) * GDN_DV].astype(F32)
        o = o * lax.rsqrt(jnp.mean(o * o, -1, keepdims=True) + RMS_EPS) * nw * (zh * _sigmoid(zh))
        o_ref[:, h * GDN_DV:(h + 1) * GDN_DV] = o.astype(BF16)


def _gdn(qkv, ab, z, conv_w, gpar, norm_w, batch, seq):
    t = qkv.shape[0]
    tb = TOKEN_TILE
    per_b = seq // tb
    return pl.pallas_call(
        functools.partial(_gdn_kernel, tb=tb),
        grid=(batch, per_b),
        in_specs=[
            pl.BlockSpec((tb, QKV_W), lambda b, j: (b * per_b + j, 0)),
            pl.BlockSpec((tb, AB_W), lambda b, j: (b * per_b + j, 0)),
            pl.BlockSpec((tb, GDN_V), lambda b, j: (b * per_b + j, 0)),
            pl.BlockSpec((GDN_CONV, QKV_W), lambda b, j: (0, 0)),
            pl.BlockSpec((2, AB_W), lambda b, j: (0, 0)),
            pl.BlockSpec((1, GDN_DV), lambda b, j: (0, 0)),
        ],
        out_specs=pl.BlockSpec((tb, GDN_V), lambda b, j: (b * per_b + j, 0)),
        out_shape=jax.ShapeDtypeStruct((t, GDN_V), BF16),
        scratch_shapes=[pltpu.VMEM((8, QKV_W), F32), pltpu.VMEM((GDN_HEADS, GDN_DK, GDN_DV), F32)],
        compiler_params=_params("parallel", "arbitrary"),
        name="gdn",
    )(qkv, ab, z, conv_w, gpar, norm_w)


def _swa_kernel(sink_ref, q_ref, kv_ref, kvp_ref, bias_ref, o_ref, *, tq):
    first = pl.program_id(1) == 0
    kv = jnp.concatenate([kvp_ref[...], kv_ref[...]], axis=0)
    key_in_prev = lax.broadcasted_iota(jnp.int32, (WINDOW, 2 * WINDOW), 1) < WINDOW
    scale = SWA_DH ** -0.5
    for w in range(tq // WINDOW):
        qw = q_ref[w * WINDOW:(w + 1) * WINDOW, :]
        kvw = kv[w * WINDOW:(w + 2) * WINDOW, :]
        heads = []
        for hq in range(SWA_HQ):
            hk = hq // GQA_GROUP
            q = qw[:, hq * SWA_DH:(hq + 1) * SWA_DH]
            k = kvw[:, hk * SWA_DH:(hk + 1) * SWA_DH]
            v = kvw[:, SWA_KV + hk * SWA_DH:SWA_KV + (hk + 1) * SWA_DH]
            s = _dot_nt(q, k) * scale + bias_ref[hq]
            if w == 0:
                s = jnp.where(key_in_prev & first, NEG_BIG, s)
            sink = sink_ref[hq]
            m = jnp.maximum(jnp.max(s, -1, keepdims=True), sink)
            p = jnp.exp(s - m)
            denom = jnp.sum(p, -1, keepdims=True) + jnp.exp(sink - m)
            heads.append(_dot(p.astype(BF16), v) / denom)
        o_ref[w * WINDOW:(w + 1) * WINDOW, :] = jnp.concatenate(heads, axis=1).astype(BF16)


def _swa_bias():
    qi = jnp.arange(WINDOW)[:, None]
    ki = jnp.arange(2 * WINDOW)[None, :]
    dist = qi + WINDOW - ki
    valid = (dist >= 0) & (dist < WINDOW)
    slopes = 2.0 ** (-8.0 * jnp.arange(1, SWA_HQ + 1, dtype=F32) / SWA_HQ)
    bias = -slopes[:, None, None] * dist.astype(F32)[None]
    return jnp.where(valid[None], bias, NEG_BIG)


def _swa(qb, kvb, sinks, batch, seq):
    t = qb.shape[0]
    tq = TOKEN_TILE
    per_b = seq // tq
    wpb = tq // WINDOW
    return pl.pallas_call(
        functools.partial(_swa_kernel, tq=tq),
        grid=(batch, per_b),
        in_specs=[
            pl.BlockSpec(memory_space=pltpu.SMEM),
            pl.BlockSpec((tq, SWA_Q), lambda b, j: (b * per_b + j, 0)),
            pl.BlockSpec((tq, 2 * SWA_KV), lambda b, j: (b * per_b + j, 0)),
            pl.BlockSpec((WINDOW, 2 * SWA_KV), lambda b, j: (jnp.maximum((b * per_b + j) * wpb - 1, 0), 0)),
            pl.BlockSpec((SWA_HQ, WINDOW, 2 * WINDOW), lambda b, j: (0, 0, 0)),
        ],
        out_specs=pl.BlockSpec((tq, SWA_Q), lambda b, j: (b * per_b + j, 0)),
        out_shape=jax.ShapeDtypeStruct((t, SWA_Q), BF16),
        compiler_params=_params("parallel", "parallel"),
        name="swa",
    )(sinks, qb, kvb, kvb, _swa_bias())


def _deepnorm(x, y, gt, g, b):
    return _layer_norm(DEEPNORM_ALPHA * x + (1.0 + gt) * y) * g + b


def _merge_kernel(x_ref, ya_ref, yb_ref, uc_ref, ucp_ref, gate_ref, gt_ref, poolw_ref, pscale_ref,
                  wpa_ref, wpb_ref, wpc_ref, wo_ref, lng_ref, lnb_ref, o_ref, *, tm):
    j = pl.program_id(1)
    u = uc_ref[...].astype(F32)
    halo = jnp.where(j == 0, 0.0, ucp_ref[...].astype(F32))
    ue = jnp.concatenate([halo, u], axis=0)
    tpos = (j * tm + lax.broadcasted_iota(jnp.int32, (tm, POOL_GDIM), 0) + 1).astype(F32)
    ycs = []
    for gi, win in enumerate(POOL_WINDOWS):
        a = ue[:, gi * POOL_GDIM:(gi + 1) * POOL_GDIM]
        span = 1
        while span < win:
            a = a[span:] + a[:-span]
            span *= 2
        lo = MAX_POOL - win + 1
        d = a[lo:lo + tm] / jnp.minimum(tpos, float(win)) - u[:, gi * POOL_GDIM:(gi + 1) * POOL_GDIM]
        ycs.append(_dot(d.astype(BF16), poolw_ref[gi]))
    yc = jnp.concatenate(ycs, axis=1) * pscale_ref[...]
    merged = gate_ref[:, 0:D_MODEL].astype(F32) * _dot(ya_ref[...], wpa_ref[...])
    merged = merged + gate_ref[:, D_MODEL:2 * D_MODEL].astype(F32) * _dot(yb_ref[...], wpb_ref[...])
    merged = merged + gate_ref[:, 2 * D_MODEL:3 * D_MODEL].astype(F32) * _dot(yc.astype(BF16), wpc_ref[...])
    y = _dot(merged.astype(BF16), wo_ref[...])
    o_ref[...] = _deepnorm(x_ref[...], y, gt_ref[0], lng_ref[...], lnb_ref[...])


def _merge(x, ya, yb, uc, gates, mod3, pool_w, pool_scale, w_pa, w_pb, w_pc, w_o, ln_g, ln_b, batch, seq):
    t, d = x.shape
    tm = TOKEN_TILE
    per_b = seq // tm
    hpb = tm // MAX_POOL

    def tok(n):
        return pl.BlockSpec((tm, n), lambda b, j: (b * per_b + j, 0))

    def full(shape):
        return pl.BlockSpec(shape, lambda b, j: (0,) * len(shape))

    return pl.pallas_call(
        functools.partial(_merge_kernel, tm=tm),
        grid=(batch, per_b),
        in_specs=[
            tok(d), tok(GDN_V), tok(SWA_Q), tok(POOL_DIM),
            pl.BlockSpec((MAX_POOL, POOL_DIM), lambda b, j: (jnp.maximum((b * per_b + j) * hpb - 1, 0), 0)),
            tok(N_BRANCH * d),
            pl.BlockSpec((1, 1, d), lambda b, j: (b, 0, 2)),
            full((POOL_GROUPS, POOL_GDIM, POOL_GDIM)), full((1, POOL_DIM)),
            full((GDN_V, d)), full((SWA_Q, d)), full((POOL_DIM, d)), full((d, d)),
            full((1, d)), full((1, d)),
        ],
        out_specs=tok(d),
        out_shape=jax.ShapeDtypeStruct((t, d), F32),
        compiler_params=_params("parallel", "parallel"),
        name="merge",
    )(x, ya, yb, uc, uc, gates, mod3, pool_w, pool_scale, w_pa, w_pb, w_pc, w_o, ln_g, ln_b)


def _router_kernel(x_ref, sh_ref, sc_ref, rwt_ref, rb_ref, tri_ref, hp_ref, topi_ref, topg_ref, rank_ref, cnt_ref,
                   run_ref):
    @pl.when(pl.program_id(0) == 0)
    def _():
        run_ref[...] = jnp.zeros_like(run_ref)

    h = _layer_norm(x_ref[...]) * (1.0 + sc_ref[0]) + sh_ref[0]
    hp_ref[...] = _pack_rows(h)
    logits = lax.dot_general(rwt_ref[...], h, (((1,), (1,)), ((), ())), preferred_element_type=F32,
                             precision=lax.Precision.HIGHEST) + rb_ref[...]
    sub = lax.broadcasted_iota(jnp.int32, logits.shape, 0)
    vals, idxs = [], []
    for _ in range(TOP_K):
        m = jnp.max(logits, 0, keepdims=True)
        idx = jnp.min(jnp.where(logits == m, sub, N_EXPERTS), 0, keepdims=True)
        vals.append(m)
        idxs.append(idx)
        logits = jnp.where(sub == idx, -jnp.inf, logits)
    es = [jnp.exp(v - vals[0]) for v in vals]
    denom = es[0] + es[1] + es[2] + es[3]
    topi_ref[...] = jnp.concatenate(idxs, axis=0)
    topg_ref[...] = jnp.concatenate([e / denom for e in es], axis=0)

    sel = jnp.zeros(logits.shape, F32)
    for idx in idxs:
        sel = sel + (sub == idx).astype(F32)
    before = run_ref[:, 0:1] + _dot(sel.astype(BF16), tri_ref[...])
    ranks = [jnp.sum(jnp.where(sub == idx, before, 0.0), 0, keepdims=True) for idx in idxs]
    rank_ref[...] = jnp.concatenate(ranks, axis=0).astype(jnp.int32)
    run_ref[...] = run_ref[...] + jnp.sum(sel, 1, keepdims=True)
    cnt_ref[...] = run_ref[...].astype(jnp.int32)


def _router(x, mod3, router_wt, router_b, seq):
    t, d = x.shape
    tm = TOKEN_TILE
    per_b = seq // tm
    tri = (jnp.arange(tm)[:, None] < jnp.arange(tm)[None, :]).astype(BF16)
    return pl.pallas_call(
        _router_kernel,
        grid=(t // tm,),
        in_specs=[
            pl.BlockSpec((tm, d), lambda i: (i, 0)),
            pl.BlockSpec((1, 1, d), lambda i: (i // per_b, 0, 3)),
            pl.BlockSpec((1, 1, d), lambda i: (i // per_b, 0, 4)),
            pl.BlockSpec((N_EXPERTS, d), lambda i: (0, 0)),
            pl.BlockSpec((N_EXPERTS, 1), lambda i: (0, 0)),
            pl.BlockSpec((tm, tm), lambda i: (0, 0)),
        ],
        out_specs=[
            pl.BlockSpec((tm, PACKED_D), lambda i: (i, 0)),
            pl.BlockSpec((TOP_K, tm), lambda i: (0, i)),
            pl.BlockSpec((TOP_K, tm), lambda i: (0, i)),
            pl.BlockSpec((TOP_K, tm), lambda i: (0, i)),
            pl.BlockSpec((N_EXPERTS, 128), lambda i: (0, 0)),
        ],
        out_shape=[
            jax.ShapeDtypeStruct((t, PACKED_D), U32),
            jax.ShapeDtypeStruct((TOP_K, t), jnp.int32),
            jax.ShapeDtypeStruct((TOP_K, t), F32),
            jax.ShapeDtypeStruct((TOP_K, t), jnp.int32),
            jax.ShapeDtypeStruct((N_EXPERTS, 128), jnp.int32),
        ],
        scratch_shapes=[pltpu.VMEM((N_EXPERTS, 128), F32)],
        compiler_params=_params("arbitrary"),
        name="router",
    )(x, mod3, mod3, router_wt, router_b, tri)


def _dispatch_plan(topi_t, rank_t, counts, n_blocks):
    padded = (counts + ROUTE_BLOCK - 1) // ROUTE_BLOCK * ROUTE_BLOCK
    ends_p = jnp.cumsum(padded)
    pstart = ends_p - padded
    dest_t = (pstart[topi_t] + rank_t).astype(jnp.int32)
    blk_start = jnp.arange(n_blocks, dtype=jnp.int32) * ROUTE_BLOCK
    block_e = jnp.minimum(jnp.searchsorted(ends_p, blk_start, side='right'), N_EXPERTS - 1).astype(jnp.int32)
    n_valid = jnp.clip(pstart[block_e] + counts[block_e] - blk_start, 0, ROUTE_BLOCK).astype(jnp.int32)
    return dest_t, block_e, n_valid


def _sc_worker_base(per_worker):
    return (lax.axis_index("s") * SC_CORES + lax.axis_index("c")) * per_worker


def _sc_mesh():
    return plsc.VectorSubcoreMesh(core_axis_name="c", subcore_axis_name="s")


def _sc_dispatch(hp, dest_flat, rows):
    t, dp = hp.shape
    per_w = t // SC_WORKERS
    n_chunks = per_w // SC_CHUNK

    @functools.partial(
        pl.kernel, mesh=_sc_mesh(), out_type=jax.ShapeDtypeStruct((rows, dp), hp.dtype),
        scratch_types=[pltpu.VMEM((SC_CHUNK,), jnp.int32), pltpu.VMEM((SC_CHUNK, dp), hp.dtype),
                       pltpu.SemaphoreType.DMA])
    def dispatch(h_hbm, d_hbm, xs_hbm, idx_v, rows_v, sem):
        base = _sc_worker_base(per_w)

        @pl.loop(0, n_chunks)
        def _(i):
            off = pl.multiple_of(base + i * SC_CHUNK, SC_CHUNK)
            pltpu.sync_copy(h_hbm.at[pl.ds(off, SC_CHUNK)], rows_v)
            for k in range(TOP_K):
                pltpu.sync_copy(d_hbm.at[pl.ds(k * t + off, SC_CHUNK)], idx_v)
                pltpu.async_copy(rows_v, xs_hbm.at[idx_v], sem).wait()

    return dispatch(hp, dest_flat)


def _sc_gather(table, idx):
    n = idx.shape[0]
    dp = table.shape[1]
    per_w = n // SC_WORKERS
    n_chunks = per_w // SC_CHUNK

    @functools.partial(
        pl.kernel, mesh=_sc_mesh(), out_type=jax.ShapeDtypeStruct((n, dp), table.dtype),
        scratch_types=[pltpu.VMEM((SC_CHUNK,), jnp.int32), pltpu.VMEM((SC_CHUNK, dp), table.dtype),
                       pltpu.SemaphoreType.DMA])
    def gather(t_hbm, i_hbm, o_hbm, idx_v, rows_v, sem):
        base = _sc_worker_base(per_w)

        @pl.loop(0, n_chunks)
        def _(i):
            off = pl.multiple_of(base + i * SC_CHUNK, SC_CHUNK)
            pltpu.sync_copy(i_hbm.at[pl.ds(off, SC_CHUNK)], idx_v)
            pltpu.async_copy(t_hbm.at[idx_v], rows_v, sem).wait()
            pltpu.sync_copy(rows_v, o_hbm.at[pl.ds(off, SC_CHUNK)])

    return gather(table, idx)


def _expert_kernel(be_ref, nv_ref, x_ref, w1_ref, b1_ref, w2_ref, b2_ref, o_ref, w1b_ref, w2b_ref):
    i = pl.program_id(0)
    n_valid = nv_ref[i]
    new_expert = (i == 0) | (be_ref[i] != be_ref[jnp.maximum(i - 1, 0)])

    @pl.when(new_expert & (n_valid > 0))
    def _():
        w1b_ref[...] = w1_ref[0].astype(BF16)
        w2b_ref[...] = w2_ref[0].astype(BF16)

    @pl.when(n_valid > 0)
    def _():
        live = lax.broadcasted_iota(jnp.int32, (ROUTE_BLOCK, D_MODEL), 0) < n_valid
        x = jnp.where(live, _unpack_rows(x_ref[...]), 0.0).astype(BF16)
        gu = _dot(x, w1b_ref[...]) + b1_ref[0]
        glu = jnp.minimum(gu[:, :D_FF], SWIGLU_LIMIT)
        lin = jnp.clip(gu[:, D_FF:], -SWIGLU_LIMIT, SWIGLU_LIMIT)
        act = glu * _sigmoid(SWIGLU_ALPHA * glu) * (lin + 1.0)
        o_ref[...] = _pack_rows(_dot(act.astype(BF16), w2b_ref[...]) + b2_ref[0])

    @pl.when(n_valid == 0)
    def _():
        o_ref[...] = jnp.zeros_like(o_ref)


def _experts(xs, block_e, n_valid, w1, b1, w2, b2, layer):
    rows, dp = xs.shape
    d = D_MODEL
    n_blocks = rows // ROUTE_BLOCK
    e0 = layer * N_EXPERTS
    grid_spec = pltpu.PrefetchScalarGridSpec(
        num_scalar_prefetch=2,
        grid=(n_blocks,),
        in_specs=[
            pl.BlockSpec((ROUTE_BLOCK, dp), lambda i, be, nv: (i, 0)),
            pl.BlockSpec((1, d, 2 * D_FF), lambda i, be, nv: (e0 + be[i], 0, 0)),
            pl.BlockSpec((1, 1, 2 * D_FF), lambda i, be, nv: (e0 + be[i], 0, 0)),
            pl.BlockSpec((1, D_FF, d), lambda i, be, nv: (e0 + be[i], 0, 0)),
            pl.BlockSpec((1, 1, d), lambda i, be, nv: (e0 + be[i], 0, 0)),
        ],
        out_specs=pl.BlockSpec((ROUTE_BLOCK, dp), lambda i, be, nv: (i, 0)),
        scratch_shapes=[pltpu.VMEM((d, 2 * D_FF), BF16), pltpu.VMEM((D_FF, d), BF16)],
    )
    return pl.pallas_call(
        _expert_kernel,
        grid_spec=grid_spec,
        out_shape=jax.ShapeDtypeStruct((rows, dp), U32),
        compiler_params=_params("arbitrary"),
        name="experts",
    )(block_e, n_valid, xs, w1, b1, w2, b2)


def _combine_kernel(x_ref, yg_ref, topg_ref, gt_ref, lng_ref, lnb_ref, o_ref):
    g = topg_ref[...]
    y = g[:, 0:1] * _unpack_rows(yg_ref[0])
    for k in range(1, TOP_K):
        y = y + g[:, k:k + 1] * _unpack_rows(yg_ref[k])
    o_ref[...] = _deepnorm(x_ref[...], y, gt_ref[0], lng_ref[...], lnb_ref[...])


def _combine(x, yg, topg, mod3, ln_g, ln_b, seq):
    t, d = x.shape
    tm = TOKEN_TILE
    per_b = seq // tm
    return pl.pallas_call(
        _combine_kernel,
        grid=(t // tm,),
        in_specs=[
            pl.BlockSpec((tm, d), lambda i: (i, 0)),
            pl.BlockSpec((TOP_K, tm, PACKED_D), lambda i: (0, i, 0)),
            pl.BlockSpec((tm, TOP_K), lambda i: (i, 0)),
            pl.BlockSpec((1, 1, d), lambda i: (i // per_b, 0, 5)),
            pl.BlockSpec((1, d), lambda i: (0, 0)),
            pl.BlockSpec((1, d), lambda i: (0, 0)),
        ],
        out_specs=pl.BlockSpec((tm, d), lambda i: (i, 0)),
        out_shape=jax.ShapeDtypeStruct((t, d), F32),
        compiler_params=_params("parallel"),
        name="combine",
    )(x, yg, topg, mod3, ln_g, ln_b)


def _rearranged_w_in(w_in):
    d = w_in.shape[0]
    o = 0
    cols = {}
    for name, n in (("qa", GDN_QK), ("ka", GDN_QK), ("va", GDN_V), ("z", GDN_V), ("a", GDN_HEADS), ("b", GDN_HEADS),
                    ("qb", SWA_Q), ("kb", SWA_KV), ("vb", SWA_KV), ("uc", POOL_DIM), ("gate", N_BRANCH * D_MODEL)):
        cols[name] = w_in[:, o:o + n]
        o += n
    ab_pad = jnp.zeros((d, AB_W - 2 * GDN_HEADS), w_in.dtype)
    return jnp.concatenate([cols["qa"], cols["ka"], cols["va"], cols["z"], cols["a"], cols["b"], ab_pad,
                            cols["qb"], cols["kb"], cols["vb"], cols["uc"], cols["gate"]], axis=1).astype(BF16)


def kernel(x, c, ada_w, ada_b, w_in, conv_w, a_log, dt_bias, gdn_norm_w, sinks, pool_w, pool_scale, w_pa, w_pb, w_pc, w_o, ln1_g, ln1_b, ln2_g, ln2_b, router_w, router_b, exp_w1, exp_b1, exp_w2, exp_b2):
    batch, seq, d = x.shape
    t = batch * seq
    n_blocks = (t * TOP_K + ROUTE_BLOCK - 1) // ROUTE_BLOCK + N_EXPERTS
    mod = _ada_mod(c, ada_w, ada_b)
    xt = x.reshape(t, d)
    lane_pad = jnp.zeros((AB_W - GDN_HEADS,), F32)
    rows = n_blocks * ROUTE_BLOCK
    w1_all = exp_w1.reshape(DEPTH * N_EXPERTS, d, 2 * D_FF)
    b1_all = exp_b1.reshape(DEPTH * N_EXPERTS, 1, 2 * D_FF)
    w2_all = exp_w2.reshape(DEPTH * N_EXPERTS, D_FF, d)
    b2_all = exp_b2.reshape(DEPTH * N_EXPERTS, 1, d)
    for l in range(DEPTH):
        mod3 = mod[l].reshape(batch, 1, 6 * d)
        qkv, z, ab, qb, kvb, uc, gates = _in_proj(xt, mod3, _rearranged_w_in(w_in[l]), seq)
        gpar = jnp.stack([jnp.concatenate([-jnp.exp(a_log[l]), lane_pad]), jnp.concatenate([dt_bias[l], lane_pad])])
        ya = _gdn(qkv, ab, z, conv_w[l], gpar, gdn_norm_w[l].reshape(1, GDN_DV), batch, seq)
        yb = _swa(qb, kvb, sinks[l], batch, seq)
        xt = _merge(xt, ya, yb, uc, gates, mod3, pool_w[l].astype(BF16), pool_scale[l].reshape(1, POOL_DIM),
                    w_pa[l].astype(BF16), w_pb[l].astype(BF16), w_pc[l].astype(BF16), w_o[l].astype(BF16),
                    ln1_g[l].reshape(1, d), ln1_b[l].reshape(1, d), batch, seq)
        hp, topi_t, topg_t, rank_t, cnt = _router(xt, mod3, router_w[l].T, router_b[l].reshape(N_EXPERTS, 1), seq)
        dest_t, block_e, n_valid = _dispatch_plan(topi_t, rank_t, cnt[:, 0], n_blocks)
        dest_flat = dest_t.reshape(TOP_K * t)
        xs = _sc_dispatch(hp, dest_flat, rows)
        ys = _experts(xs, block_e, n_valid, w1_all, b1_all, w2_all, b2_all, l)
        yg = _sc_gather(ys, dest_flat).reshape(TOP_K, t, PACKED_D)
        xt = _combine(xt, yg, topg_t.T, mod3, ln2_g[l].reshape(1, d), ln2_b[l].reshape(1, d), seq)
    return xt.reshape(batch, seq, d)
```

```python
import functools

import jax
import jax.numpy as jnp
from jax import lax
from jax.experimental import pallas as pl
from jax.experimental.pallas import tpu as pltpu
from jax.experimental.pallas import tpu_sc as plsc

D_MODEL = 1024
DEPTH = 4
GDN_HEADS = 4
GDN_DK = 128
GDN_DV = 128
GDN_CONV = 4
GDN_CHUNK = 64
SWA_HQ = 8
SWA_HKV = 2
SWA_DH = 64
WINDOW = 128
POOL_WINDOWS = (2, 4, 8, 16)
POOL_GROUPS = 4
POOL_GDIM = 128
N_BRANCH = 3
N_EXPERTS = 32
TOP_K = 4
D_FF = 1024
ROUTE_BLOCK = 512
SWIGLU_ALPHA = 1.702
SWIGLU_LIMIT = 7.0
LN_EPS = 1e-5
RMS_EPS = 1e-6
DEEPNORM_ALPHA = (2 * DEPTH) ** 0.25

GDN_QK = GDN_HEADS * GDN_DK
GDN_V = GDN_HEADS * GDN_DV
SWA_Q = SWA_HQ * SWA_DH
SWA_KV = SWA_HKV * SWA_DH
POOL_DIM = POOL_GROUPS * POOL_GDIM
GQA_GROUP = SWA_HQ // SWA_HKV
MAX_POOL = max(POOL_WINDOWS)

QKV_W = 2 * GDN_QK + GDN_V
AB_W = 128
SEG_QKV = 0
SEG_Z = SEG_QKV + QKV_W
SEG_AB = SEG_Z + GDN_V
SEG_QB = SEG_AB + AB_W
SEG_KVB = SEG_QB + SWA_Q
SEG_UC = SEG_KVB + 2 * SWA_KV
SEG_GATE = SEG_UC + POOL_DIM
PROJ_W = SEG_GATE + N_BRANCH * D_MODEL

TOKEN_TILE = 512
NEG_BIG = -1e30
VMEM_LIMIT = 56 * 1024 * 1024

PACKED_D = D_MODEL // 2
SC_CORES = 2
SC_SUBCORES = 16
SC_WORKERS = SC_CORES * SC_SUBCORES
SC_CHUNK = 64

F32 = jnp.float32
BF16 = jnp.bfloat16
U32 = jnp.uint32
HI_MASK = 0xFFFF0000


def _pack_rows(x):
    bits = pltpu.bitcast(x.astype(BF16).astype(F32), U32)
    return (bits[:, :PACKED_D] >> 16) | (bits[:, PACKED_D:] & jnp.uint32(HI_MASK))


def _unpack_rows(w):
    lo = pltpu.bitcast(w << 16, F32)
    hi = pltpu.bitcast(w & jnp.uint32(HI_MASK), F32)
    return jnp.concatenate([lo, hi], axis=1)


def _params(*sem):
    return pltpu.CompilerParams(dimension_semantics=sem, vmem_limit_bytes=VMEM_LIMIT)


def _sigmoid(x):
    return 1.0 / (1.0 + jnp.exp(-x))


def _layer_norm(x):
    mu = jnp.mean(x, -1, keepdims=True)
    xc = x - mu
    var = jnp.mean(xc * xc, -1, keepdims=True)
    return xc * lax.rsqrt(var + LN_EPS)


def _dot(a, b):
    return jnp.dot(a, b, preferred_element_type=F32)


def _dot_nt(a, b):
    return lax.dot_general(a, b, (((1,), (1,)), ((), ())), preferred_element_type=F32)


def _dot_tn(a, b):
    return lax.dot_general(a, b, (((0,), (0,)), ((), ())), preferred_element_type=F32)


def _bdot(a, b):
    return lax.dot_general(a, b, (((2,), (1,)), ((0,), (0,))), preferred_element_type=F32)


def _bdot_nt(a, b):
    return lax.dot_general(a, b, (((2,), (2,)), ((0,), (0,))), preferred_element_type=F32)


def _ada_kernel(c_ref, w_ref, b_ref, o_ref):
    c = c_ref[...]
    cond = c * _sigmoid(c)
    o_ref[0] = _dot(cond.astype(BF16), w_ref[0].astype(BF16)) + b_ref[0]


def _ada_mod(c, ada_w, ada_b):
    depth, d, n = ada_w.shape
    b = c.shape[0]
    tn = 1024
    return pl.pallas_call(
        _ada_kernel,
        grid=(depth, n // tn),
        in_specs=[
            pl.BlockSpec((b, d), lambda l, j: (0, 0)),
            pl.BlockSpec((1, d, tn), lambda l, j: (l, 0, j)),
            pl.BlockSpec((1, 1, tn), lambda l, j: (l, 0, j)),
        ],
        out_specs=pl.BlockSpec((1, b, tn), lambda l, j: (l, 0, j)),
        out_shape=jax.ShapeDtypeStruct((depth, b, n), F32),
        compiler_params=_params("parallel", "parallel"),
        name="ada_mod",
    )(c, ada_w, ada_b.reshape(depth, 1, n))


def _proj_kernel(x_ref, sh_ref, sc_ref, w_ref, convw_ref, qkv_ref, z_ref, ab_ref, qb_ref, kvb_ref, uc_ref, gate_ref,
                 carry_ref, *, tm, per_b):
    h = (_layer_norm(x_ref[...]) * (1.0 + sc_ref[0]) + sh_ref[0]).astype(BF16)

    def seg(start, width):
        return _dot(h, w_ref[:, start:start + width])

    @pl.when(pl.program_id(0) % per_b == 0)
    def _():
        carry_ref[...] = jnp.zeros_like(carry_ref)

    cw = convw_ref[...]
    for j in range(QKV_W // 512):
        cols = slice(j * 512, (j + 1) * 512)
        xin = seg(SEG_QKV + j * 512, 512)
        xe = jnp.concatenate([carry_ref[:, cols], xin], axis=0)
        carry_ref[:, cols] = xin[tm - 8:, :]
        conv = xin * cw[GDN_CONV - 1:GDN_CONV, cols]
        for tap in range(GDN_CONV - 1):
            conv = conv + xe[5 + tap:5 + tap + tm] * cw[tap:tap + 1, cols]
        qkv_ref[:, cols] = (conv * _sigmoid(conv)).astype(BF16)
    z_ref[...] = seg(SEG_Z, GDN_V).astype(BF16)
    ab_ref[...] = seg(SEG_AB, AB_W)
    qb_ref[...] = seg(SEG_QB, SWA_Q).astype(BF16)
    kvb_ref[...] = seg(SEG_KVB, 2 * SWA_KV).astype(BF16)
    uc_ref[...] = seg(SEG_UC, POOL_DIM).astype(BF16)
    for j in range(N_BRANCH * D_MODEL // 512):
        gate_ref[:, j * 512:(j + 1) * 512] = _sigmoid(seg(SEG_GATE + j * 512, 512)).astype(BF16)


def _in_proj(x, mod3, w, conv_w, seq):
    t, d = x.shape
    tm = TOKEN_TILE
    per_b = seq // tm
    widths = (QKV_W, GDN_V, AB_W, SWA_Q, 2 * SWA_KV, POOL_DIM, N_BRANCH * D_MODEL)
    dtypes = (BF16, BF16, F32, BF16, BF16, BF16, BF16)
    return pl.pallas_call(
        functools.partial(_proj_kernel, tm=tm, per_b=per_b),
        grid=(t // tm,),
        in_specs=[
            pl.BlockSpec((tm, d), lambda i: (i, 0)),
            pl.BlockSpec((1, 1, d), lambda i: (i // per_b, 0, 0)),
            pl.BlockSpec((1, 1, d), lambda i: (i // per_b, 0, 1)),
            pl.BlockSpec((d, PROJ_W), lambda i: (0, 0)),
            pl.BlockSpec((GDN_CONV, QKV_W), lambda i: (0, 0)),
        ],
        out_specs=[pl.BlockSpec((tm, n), lambda i: (i, 0)) for n in widths],
        out_shape=[jax.ShapeDtypeStruct((t, n), dt) for n, dt in zip(widths, dtypes)],
        scratch_shapes=[pltpu.VMEM((8, QKV_W), F32)],
        compiler_params=_params("arbitrary"),
        name="in_proj",
    )(x, mod3, mod3, w, conv_w)


def _softplus(x):
    return jnp.maximum(x, 0.0) + jnp.log(1.0 + jnp.exp(-jnp.abs(x)))


def _unit_lower_inverse(lm, row, col):
    n = lm.shape[-1]
    eye = (row == col).astype(F32)
    t = None
    s = 1
    while s < n:
        join = ((row // (2 * s)) == (col // (2 * s))) & ((row % (2 * s)) >= s) & ((col % (2 * s)) < s)
        cm = jnp.where(join, lm, 0.0)
        if t is None:
            t = eye - cm
        else:
            p = _bdot(t, cm)
            t = t - _bdot(p, t)
        s *= 2
    return t


def _gdn_kernel(qkv_ref, ab_ref, z_ref, gpar_ref, nw_ref, o_ref, state_ref, *, tb):
    c_len = GDN_CHUNK
    nc = tb // c_len

    @pl.when(pl.program_id(1) == 0)
    def _():
        state_ref[...] = jnp.zeros_like(state_ref)

    ab = ab_ref[...]
    gpar = gpar_ref[...]
    g_all = gpar[0:1] * _softplus(ab + gpar[1:2])
    beta_all = _sigmoid(ab)
    pos = lax.broadcasted_iota(jnp.int32, (tb, AB_W), 0) % c_len
    gc_all = g_all
    s = 1
    while s < c_len:
        gc_all = gc_all + jnp.where(pos >= s, pltpu.roll(gc_all, s, axis=0), 0.0)
        s *= 2

    row = lax.broadcasted_iota(jnp.int32, (nc, c_len, c_len), 1)
    col = lax.broadcasted_iota(jnp.int32, (nc, c_len, c_len), 2)
    causal = row >= col
    strict = row > col
    diag = row == col

    pre = []
    for h in range(GDN_HEADS):
        q = qkv_ref[:, h * GDN_DK:(h + 1) * GDN_DK].astype(F32)
        k = qkv_ref[:, GDN_QK + h * GDN_DK:GDN_QK + (h + 1) * GDN_DK].astype(F32)
        v = qkv_ref[:, 2 * GDN_QK + h * GDN_DV:2 * GDN_QK + (h + 1) * GDN_DV].astype(F32)
        q = q * lax.rsqrt(jnp.sum(q * q, -1, keepdims=True) + 1e-6) * (GDN_DK ** -0.5)
        k = k * lax.rsqrt(jnp.sum(k * k, -1, keepdims=True) + 1e-6)
        gc = jnp.broadcast_to(gc_all[:, h:h + 1], (tb, GDN_DK)).reshape(nc, c_len, GDN_DK)
        beta = jnp.broadcast_to(beta_all[:, GDN_HEADS + h:GDN_HEADS + h + 1], (tb, GDN_DK)).reshape(nc, c_len, GDN_DK)
        q = q.reshape(nc, c_len, GDN_DK)
        k = k.reshape(nc, c_len, GDN_DK)
        v = v.reshape(nc, c_len, GDN_DV)

        gc_i = gc[:, :, :c_len]
        gc_j = jnp.sum(jnp.where(diag, gc_i, 0.0), axis=1, keepdims=True)
        decay = jnp.where(causal, jnp.exp(jnp.where(causal, gc_i - gc_j, 0.0)), 0.0)
        eg = jnp.exp(gc)
        gc_last = gc[:, c_len - 1:c_len, :]
        k_beta = k * beta
        kb16 = k.astype(BF16)
        lower = jnp.where(strict, _bdot_nt(k_beta.astype(BF16), kb16) * decay, 0.0)
        attn = jnp.where(causal, _bdot_nt(q.astype(BF16), kb16) * decay, 0.0)
        tinv = _unit_lower_inverse(lower, row, col)
        u = _bdot(tinv, v * beta)
        w = _bdot(tinv, k_beta * eg)
        q_g = (q * eg).astype(BF16)
        k_g = (k * jnp.exp(gc_last - gc)).astype(BF16)
        pre.append((q_g, k_g, u, w.astype(BF16), attn.astype(BF16), jnp.exp(gc_last)))

    outs = [[None] * nc for _ in range(GDN_HEADS)]
    states = [state_ref[h] for h in range(GDN_HEADS)]
    for c in range(nc):
        for h in range(GDN_HEADS):
            q_g, k_g, u, w, attn, g_last = pre[h]
            st = states[h]
            st16 = st.astype(BF16)
            v_new = u[c] - _dot(w[c], st16)
            outs[h][c] = _dot(q_g[c], st16) + _dot(attn[c], v_new.astype(BF16))
            states[h] = st * g_last[c] + _dot_tn(k_g[c], v_new.astype(BF16))
    nw = nw_ref[...]
    for h in range(GDN_HEADS):
        state_ref[h] = states[h]
        o = jnp.concatenate(outs[h], axis=0)
        zh = z_ref[:, h * GDN_DV:(h + 1) * GDN_DV].astype(F32)
        o = o * lax.rsqrt(jnp.mean(o * o, -1, keepdims=True) + RMS_EPS) * nw * (zh * _sigmoid(zh))
        o_ref[:, h * GDN_DV:(h + 1) * GDN_DV] = o.astype(BF16)


def _gdn(qkv, ab, z, gpar, norm_w, batch, seq):
    t = qkv.shape[0]
    tb = TOKEN_TILE
    per_b = seq // tb
    return pl.pallas_call(
        functools.partial(_gdn_kernel, tb=tb),
        grid=(batch, per_b),
        in_specs=[
            pl.BlockSpec((tb, QKV_W), lambda b, j: (b * per_b + j, 0)),
            pl.BlockSpec((tb, AB_W), lambda b, j: (b * per_b + j, 0)),
            pl.BlockSpec((tb, GDN_V), lambda b, j: (b * per_b + j, 0)),
            pl.BlockSpec((2, AB_W), lambda b, j: (0, 0)),
            pl.BlockSpec((1, GDN_DV), lambda b, j: (0, 0)),
        ],
        out_specs=pl.BlockSpec((tb, GDN_V), lambda b, j: (b * per_b + j, 0)),
        out_shape=jax.ShapeDtypeStruct((t, GDN_V), BF16),
        scratch_shapes=[pltpu.VMEM((GDN_HEADS, GDN_DK, GDN_DV), F32)],
        compiler_params=_params("parallel", "arbitrary"),
        name="gdn",
    )(qkv, ab, z, gpar, norm_w)


def _swa_kernel(sink_ref, q_ref, kv_ref, kvp_ref, bias_ref, o_ref, *, tq):
    first = pl.program_id(1) == 0
    kv = jnp.concatenate([kvp_ref[...], kv_ref[...]], axis=0)
    key_in_prev = lax.broadcasted_iota(jnp.int32, (WINDOW, 2 * WINDOW), 1) < WINDOW
    scale = SWA_DH ** -0.5
    for w in range(tq // WINDOW):
        qw = q_ref[w * WINDOW:(w + 1) * WINDOW, :]
        kvw = kv[w * WINDOW:(w + 2) * WINDOW, :]
        heads = []
        for hq in range(SWA_HQ):
            hk = hq // GQA_GROUP
            q = qw[:, hq * SWA_DH:(hq + 1) * SWA_DH]
            k = kvw[:, hk * SWA_DH:(hk + 1) * SWA_DH]
            v = kvw[:, SWA_KV + hk * SWA_DH:SWA_KV + (hk + 1) * SWA_DH]
            s = _dot_nt(q, k) * scale + bias_ref[hq]
            if w == 0:
                s = jnp.where(key_in_prev & first, NEG_BIG, s)
            sink = sink_ref[hq]
            m = jnp.maximum(jnp.max(s, -1, keepdims=True), sink)
            p = jnp.exp(s - m)
            denom = jnp.sum(p, -1, keepdims=True) + jnp.exp(sink - m)
            heads.append(_dot(p.astype(BF16), v) / denom)
        o_ref[w * WINDOW:(w + 1) * WINDOW, :] = jnp.concatenate(heads, axis=1).astype(BF16)


def _swa_bias():
    qi = jnp.arange(WINDOW)[:, None]
    ki = jnp.arange(2 * WINDOW)[None, :]
    dist = qi + WINDOW - ki
    valid = (dist >= 0) & (dist < WINDOW)
    slopes = 2.0 ** (-8.0 * jnp.arange(1, SWA_HQ + 1, dtype=F32) / SWA_HQ)
    bias = -slopes[:, None, None] * dist.astype(F32)[None]
    return jnp.where(valid[None], bias, NEG_BIG)


def _swa(qb, kvb, sinks, batch, seq):
    t = qb.shape[0]
    tq = TOKEN_TILE
    per_b = seq // tq
    wpb = tq // WINDOW
    return pl.pallas_call(
        functools.partial(_swa_kernel, tq=tq),
        grid=(batch, per_b),
        in_specs=[
            pl.BlockSpec(memory_space=pltpu.SMEM),
            pl.BlockSpec((tq, SWA_Q), lambda b, j: (b * per_b + j, 0)),
            pl.BlockSpec((tq, 2 * SWA_KV), lambda b, j: (b * per_b + j, 0)),
            pl.BlockSpec((WINDOW, 2 * SWA_KV), lambda b, j: (jnp.maximum((b * per_b + j) * wpb - 1, 0), 0)),
            pl.BlockSpec((SWA_HQ, WINDOW, 2 * WINDOW), lambda b, j: (0, 0, 0)),
        ],
        out_specs=pl.BlockSpec((tq, SWA_Q), lambda b, j: (b * per_b + j, 0)),
        out_shape=jax.ShapeDtypeStruct((t, SWA_Q), BF16),
        compiler_params=_params("parallel", "parallel"),
        name="swa",
    )(sinks, qb, kvb, kvb, _swa_bias())


def _deepnorm(x, y, gt, g, b):
    return _layer_norm(DEEPNORM_ALPHA * x + (1.0 + gt) * y) * g + b


def _merge_kernel(x_ref, ya_ref, yb_ref, uc_ref, ucp_ref, gate_ref, gt_ref, poolw_ref, pscale_ref,
                  wpa_ref, wpb_ref, wpc_ref, wo_ref, lng_ref, lnb_ref, o_ref, *, tm):
    j = pl.program_id(1)
    u = uc_ref[...].astype(F32)
    halo = jnp.where(j == 0, 0.0, ucp_ref[...].astype(F32))
    ue = jnp.concatenate([halo, u], axis=0)
    tpos = (j * tm + lax.broadcasted_iota(jnp.int32, (tm, POOL_GDIM), 0) + 1).astype(F32)
    ycs = []
    for gi, win in enumerate(POOL_WINDOWS):
        a = ue[:, gi * POOL_GDIM:(gi + 1) * POOL_GDIM]
        span = 1
        while span < win:
            a = a[span:] + a[:-span]
            span *= 2
        lo = MAX_POOL - win + 1
        d = a[lo:lo + tm] / jnp.minimum(tpos, float(win)) - u[:, gi * POOL_GDIM:(gi + 1) * POOL_GDIM]
        ycs.append(_dot(d.astype(BF16), poolw_ref[gi]))
    yc = jnp.concatenate(ycs, axis=1) * pscale_ref[...]
    merged = gate_ref[:, 0:D_MODEL].astype(F32) * _dot(ya_ref[...], wpa_ref[...])
    merged = merged + gate_ref[:, D_MODEL:2 * D_MODEL].astype(F32) * _dot(yb_ref[...], wpb_ref[...])
    merged = merged + gate_ref[:, 2 * D_MODEL:3 * D_MODEL].astype(F32) * _dot(yc.astype(BF16), wpc_ref[...])
    y = _dot(merged.astype(BF16), wo_ref[...])
    o_ref[...] = _deepnorm(x_ref[...], y, gt_ref[0], lng_ref[...], lnb_ref[...])


def _merge(x, ya, yb, uc, gates, mod3, pool_w, pool_scale, w_pa, w_pb, w_pc, w_o, ln_g, ln_b, batch, seq):
    t, d = x.shape
    tm = TOKEN_TILE
    per_b = seq // tm
    hpb = tm // MAX_POOL

    def tok(n):
        return pl.BlockSpec((tm, n), lambda b, j: (b * per_b + j, 0))

    def full(shape):
        return pl.BlockSpec(shape, lambda b, j: (0,) * len(shape))

    return pl.pallas_call(
        functools.partial(_merge_kernel, tm=tm),
        grid=(batch, per_b),
        in_specs=[
            tok(d), tok(GDN_V), tok(SWA_Q), tok(POOL_DIM),
            pl.BlockSpec((MAX_POOL, POOL_DIM), lambda b, j: (jnp.maximum((b * per_b + j) * hpb - 1, 0), 0)),
            tok(N_BRANCH * d),
            pl.BlockSpec((1, 1, d), lambda b, j: (b, 0, 2)),
            full((POOL_GROUPS, POOL_GDIM, POOL_GDIM)), full((1, POOL_DIM)),
            full((GDN_V, d)), full((SWA_Q, d)), full((POOL_DIM, d)), full((d, d)),
            full((1, d)), full((1, d)),
        ],
        out_specs=tok(d),
        out_shape=jax.ShapeDtypeStruct((t, d), F32),
        compiler_params=_params("parallel", "parallel"),
        name="merge",
    )(x, ya, yb, uc, uc, gates, mod3, pool_w, pool_scale, w_pa, w_pb, w_pc, w_o, ln_g, ln_b)


def _router_kernel(x_ref, sh_ref, sc_ref, rwt_ref, rb_ref, tri_ref, hp_ref, topi_ref, topg_ref, rank_ref, cnt_ref,
                   run_ref):
    @pl.when(pl.program_id(0) == 0)
    def _():
        run_ref[...] = jnp.zeros_like(run_ref)

    h = _layer_norm(x_ref[...]) * (1.0 + sc_ref[0]) + sh_ref[0]
    hp_ref[...] = _pack_rows(h)
    logits = lax.dot_general(rwt_ref[...], h, (((1,), (1,)), ((), ())), preferred_element_type=F32,
                             precision=lax.Precision.HIGHEST) + rb_ref[...]
    sub = lax.broadcasted_iota(jnp.int32, logits.shape, 0)
    vals, idxs = [], []
    for _ in range(TOP_K):
        m = jnp.max(logits, 0, keepdims=True)
        idx = jnp.min(jnp.where(logits == m, sub, N_EXPERTS), 0, keepdims=True)
        vals.append(m)
        idxs.append(idx)
        logits = jnp.where(sub == idx, -jnp.inf, logits)
    es = [jnp.exp(v - vals[0]) for v in vals]
    denom = es[0] + es[1] + es[2] + es[3]
    topi_ref[...] = jnp.concatenate(idxs, axis=0)
    topg_ref[...] = jnp.concatenate([e / denom for e in es], axis=0)

    sel = jnp.zeros(logits.shape, F32)
    for idx in idxs:
        sel = sel + (sub == idx).astype(F32)
    before = run_ref[:, 0:1] + _dot(sel.astype(BF16), tri_ref[...])
    ranks = [jnp.sum(jnp.where(sub == idx, before, 0.0), 0, keepdims=True) for idx in idxs]
    rank_ref[...] = jnp.concatenate(ranks, axis=0).astype(jnp.int32)
    run_ref[...] = run_ref[...] + jnp.sum(sel, 1, keepdims=True)
    cnt_ref[...] = run_ref[...].astype(jnp.int32)


def _router(x, mod3, router_wt, router_b, seq):
    t, d = x.shape
    tm = TOKEN_TILE
    per_b = seq // tm
    tri = (jnp.arange(tm)[:, None] < jnp.arange(tm)[None, :]).astype(BF16)
    return pl.pallas_call(
        _router_kernel,
        grid=(t // tm,),
        in_specs=[
            pl.BlockSpec((tm, d), lambda i: (i, 0)),
            pl.BlockSpec((1, 1, d), lambda i: (i // per_b, 0, 3)),
            pl.BlockSpec((1, 1, d), lambda i: (i // per_b, 0, 4)),
            pl.BlockSpec((N_EXPERTS, d), lambda i: (0, 0)),
            pl.BlockSpec((N_EXPERTS, 1), lambda i: (0, 0)),
            pl.BlockSpec((tm, tm), lambda i: (0, 0)),
        ],
        out_specs=[
            pl.BlockSpec((tm, PACKED_D), lambda i: (i, 0)),
            pl.BlockSpec((TOP_K, tm), lambda i: (0, i)),
            pl.BlockSpec((TOP_K, tm), lambda i: (0, i)),
            pl.BlockSpec((TOP_K, tm), lambda i: (0, i)),
            pl.BlockSpec((N_EXPERTS, 128), lambda i: (0, 0)),
        ],
        out_shape=[
            jax.ShapeDtypeStruct((t, PACKED_D), U32),
            jax.ShapeDtypeStruct((TOP_K, t), jnp.int32),
            jax.ShapeDtypeStruct((TOP_K, t), F32),
            jax.ShapeDtypeStruct((TOP_K, t), jnp.int32),
            jax.ShapeDtypeStruct((N_EXPERTS, 128), jnp.int32),
        ],
        scratch_shapes=[pltpu.VMEM((N_EXPERTS, 128), F32)],
        compiler_params=_params("arbitrary"),
        name="router",
    )(x, mod3, mod3, router_wt, router_b, tri)


def _dispatch_plan(topi_t, rank_t, counts, n_blocks):
    padded = (counts + ROUTE_BLOCK - 1) // ROUTE_BLOCK * ROUTE_BLOCK
    ends_p = jnp.cumsum(padded)
    pstart = ends_p - padded
    experts = jnp.arange(N_EXPERTS, dtype=jnp.int32)
    pstart_tok = jnp.sum(jnp.where(topi_t[:, :, None] == experts, pstart, 0), -1)
    dest_t = (pstart_tok + rank_t).astype(jnp.int32)
    blk_start = jnp.arange(n_blocks, dtype=jnp.int32) * ROUTE_BLOCK
    block_e = jnp.minimum(jnp.sum((blk_start[:, None] >= ends_p).astype(jnp.int32), -1), N_EXPERTS - 1)
    blk_end = jnp.sum(jnp.where(block_e[:, None] == experts, pstart + counts, 0), -1)
    n_valid = jnp.clip(blk_end - blk_start, 0, ROUTE_BLOCK).astype(jnp.int32)
    return dest_t, block_e.astype(jnp.int32), n_valid


def _sc_worker_base(per_worker):
    return (lax.axis_index("s") * SC_CORES + lax.axis_index("c")) * per_worker


def _sc_mesh():
    return plsc.VectorSubcoreMesh(core_axis_name="c", subcore_axis_name="s")


def _sc_dispatch(hp, dest_flat, rows):
    t, dp = hp.shape
    per_w = t // SC_WORKERS
    n_chunks = per_w // SC_CHUNK

    @functools.partial(
        pl.kernel, mesh=_sc_mesh(), out_type=jax.ShapeDtypeStruct((rows, dp), hp.dtype),
        scratch_types=[pltpu.VMEM((SC_CHUNK,), jnp.int32), pltpu.VMEM((SC_CHUNK, dp), hp.dtype),
                       pltpu.SemaphoreType.DMA])
    def dispatch(h_hbm, d_hbm, xs_hbm, idx_v, rows_v, sem):
        base = _sc_worker_base(per_w)

        @pl.loop(0, n_chunks)
        def _(i):
            off = pl.multiple_of(base + i * SC_CHUNK, SC_CHUNK)
            pltpu.sync_copy(h_hbm.at[pl.ds(off, SC_CHUNK)], rows_v)
            for k in range(TOP_K):
                pltpu.sync_copy(d_hbm.at[pl.ds(k * t + off, SC_CHUNK)], idx_v)
                pltpu.async_copy(rows_v, xs_hbm.at[idx_v], sem).wait()

    return dispatch(hp, dest_flat)


def _sc_gather(table, idx):
    n = idx.shape[0]
    dp = table.shape[1]
    per_w = n // SC_WORKERS
    n_chunks = per_w // SC_CHUNK

    @functools.partial(
        pl.kernel, mesh=_sc_mesh(), out_type=jax.ShapeDtypeStruct((n, dp), table.dtype),
        scratch_types=[pltpu.VMEM((SC_CHUNK,), jnp.int32), pltpu.VMEM((SC_CHUNK, dp), table.dtype),
                       pltpu.SemaphoreType.DMA])
    def gather(t_hbm, i_hbm, o_hbm, idx_v, rows_v, sem):
        base = _sc_worker_base(per_w)

        @pl.loop(0, n_chunks)
        def _(i):
            off = pl.multiple_of(base + i * SC_CHUNK, SC_CHUNK)
            pltpu.sync_copy(i_hbm.at[pl.ds(off, SC_CHUNK)], idx_v)
            pltpu.async_copy(t_hbm.at[idx_v], rows_v, sem).wait()
            pltpu.sync_copy(rows_v, o_hbm.at[pl.ds(off, SC_CHUNK)])

    return gather(table, idx)


def _expert_kernel(be_ref, nv_ref, x_ref, w1_ref, b1_ref, w2_ref, b2_ref, o_ref, w1b_ref, w2b_ref):
    i = pl.program_id(0)
    n_valid = nv_ref[i]
    new_expert = (i == 0) | (be_ref[i] != be_ref[jnp.maximum(i - 1, 0)])

    @pl.when(new_expert & (n_valid > 0))
    def _():
        w1b_ref[...] = w1_ref[0].astype(BF16)
        w2b_ref[...] = w2_ref[0].astype(BF16)

    @pl.when(n_valid > 0)
    def _():
        live = lax.broadcasted_iota(jnp.int32, (ROUTE_BLOCK, D_MODEL), 0) < n_valid
        x = jnp.where(live, _unpack_rows(x_ref[...]), 0.0).astype(BF16)
        gu = _dot(x, w1b_ref[...]) + b1_ref[0]
        glu = jnp.minimum(gu[:, :D_FF], SWIGLU_LIMIT)
        lin = jnp.clip(gu[:, D_FF:], -SWIGLU_LIMIT, SWIGLU_LIMIT)
        act = glu * _sigmoid(SWIGLU_ALPHA * glu) * (lin + 1.0)
        o_ref[...] = _pack_rows(_dot(act.astype(BF16), w2b_ref[...]) + b2_ref[0])

    @pl.when(n_valid == 0)
    def _():
        o_ref[...] = jnp.zeros_like(o_ref)


def _experts(xs, block_e, n_valid, w1, b1, w2, b2, layer):
    rows, dp = xs.shape
    d = D_MODEL
    n_blocks = rows // ROUTE_BLOCK
    e0 = layer * N_EXPERTS
    grid_spec = pltpu.PrefetchScalarGridSpec(
        num_scalar_prefetch=2,
        grid=(n_blocks,),
        in_specs=[
            pl.BlockSpec((ROUTE_BLOCK, dp), lambda i, be, nv: (i, 0)),
            pl.BlockSpec((1, d, 2 * D_FF), lambda i, be, nv: (e0 + be[i], 0, 0)),
            pl.BlockSpec((1, 1, 2 * D_FF), lambda i, be, nv: (e0 + be[i], 0, 0)),
            pl.BlockSpec((1, D_FF, d), lambda i, be, nv: (e0 + be[i], 0, 0)),
            pl.BlockSpec((1, 1, d), lambda i, be, nv: (e0 + be[i], 0, 0)),
        ],
        out_specs=pl.BlockSpec((ROUTE_BLOCK, dp), lambda i, be, nv: (i, 0)),
        scratch_shapes=[pltpu.VMEM((d, 2 * D_FF), BF16), pltpu.VMEM((D_FF, d), BF16)],
    )
    return pl.pallas_call(
        _expert_kernel,
        grid_spec=grid_spec,
        out_shape=jax.ShapeDtypeStruct((rows, dp), U32),
        compiler_params=_params("arbitrary"),
        name="experts",
    )(block_e, n_valid, xs, w1, b1, w2, b2)


def _combine_kernel(x_ref, yg_ref, topg_ref, gt_ref, lng_ref, lnb_ref, o_ref):
    g = topg_ref[...]
    y = g[:, 0:1] * _unpack_rows(yg_ref[0])
    for k in range(1, TOP_K):
        y = y + g[:, k:k + 1] * _unpack_rows(yg_ref[k])
    o_ref[...] = _deepnorm(x_ref[...], y, gt_ref[0], lng_ref[...], lnb_ref[...])


def _combine(x, yg, topg, mod3, ln_g, ln_b, seq):
    t, d = x.shape
    tm = TOKEN_TILE
    per_b = seq // tm
    return pl.pallas_call(
        _combine_kernel,
        grid=(t // tm,),
        in_specs=[
            pl.BlockSpec((tm, d), lambda i: (i, 0)),
            pl.BlockSpec((TOP_K, tm, PACKED_D), lambda i: (0, i, 0)),
            pl.BlockSpec((tm, TOP_K), lambda i: (i, 0)),
            pl.BlockSpec((1, 1, d), lambda i: (i // per_b, 0, 5)),
            pl.BlockSpec((1, d), lambda i: (0, 0)),
            pl.BlockSpec((1, d), lambda i: (0, 0)),
        ],
        out_specs=pl.BlockSpec((tm, d), lambda i: (i, 0)),
        out_shape=jax.ShapeDtypeStruct((t, d), F32),
        compiler_params=_params("parallel"),
        name="combine",
    )(x, yg, topg, mod3, ln_g, ln_b)


def _rearranged_w_in(w_in):
    d = w_in.shape[0]
    o = 0
    cols = {}
    for name, n in (("qa", GDN_QK), ("ka", GDN_QK), ("va", GDN_V), ("z", GDN_V), ("a", GDN_HEADS), ("b", GDN_HEADS),
                    ("qb", SWA_Q), ("kb", SWA_KV), ("vb", SWA_KV), ("uc", POOL_DIM), ("gate", N_BRANCH * D_MODEL)):
        cols[name] = w_in[:, o:o + n]
        o += n
    ab_pad = jnp.zeros((d, AB_W - 2 * GDN_HEADS), w_in.dtype)
    return jnp.concatenate([cols["qa"], cols["ka"], cols["va"], cols["z"], cols["a"], cols["b"], ab_pad,
                            cols["qb"], cols["kb"], cols["vb"], cols["uc"], cols["gate"]], axis=1).astype(BF16)


def kernel(x, c, ada_w, ada_b, w_in, conv_w, a_log, dt_bias, gdn_norm_w, sinks, pool_w, pool_scale, w_pa, w_pb, w_pc, w_o, ln1_g, ln1_b, ln2_g, ln2_b, router_w, router_b, exp_w1, exp_b1, exp_w2, exp_b2):
    batch, seq, d = x.shape
    t = batch * seq
    n_blocks = (t * TOP_K + ROUTE_BLOCK - 1) // ROUTE_BLOCK + N_EXPERTS
    mod = _ada_mod(c, ada_w, ada_b)
    xt = x.reshape(t, d)
    lane_pad = jnp.zeros((AB_W - GDN_HEADS,), F32)
    rows = n_blocks * ROUTE_BLOCK
    w1_all = exp_w1.reshape(DEPTH * N_EXPERTS, d, 2 * D_FF)
    b1_all = exp_b1.reshape(DEPTH * N_EXPERTS, 1, 2 * D_FF)
    w2_all = exp_w2.reshape(DEPTH * N_EXPERTS, D_FF, d)
    b2_all = exp_b2.reshape(DEPTH * N_EXPERTS, 1, d)
    for l in range(DEPTH):
        mod3 = mod[l].reshape(batch, 1, 6 * d)
        qkv, z, ab, qb, kvb, uc, gates = _in_proj(xt, mod3, _rearranged_w_in(w_in[l]), conv_w[l], seq)
        gpar = jnp.stack([jnp.concatenate([-jnp.exp(a_log[l]), lane_pad]), jnp.concatenate([dt_bias[l], lane_pad])])
        ya = _gdn(qkv, ab, z, gpar, gdn_norm_w[l].reshape(1, GDN_DV), batch, seq)
        yb = _swa(qb, kvb, sinks[l], batch, seq)
        xt = _merge(xt, ya, yb, uc, gates, mod3, pool_w[l].astype(BF16), pool_scale[l].reshape(1, POOL_DIM),
                    w_pa[l].astype(BF16), w_pb[l].astype(BF16), w_pc[l].astype(BF16), w_o[l].astype(BF16),
                    ln1_g[l].reshape(1, d), ln1_b[l].reshape(1, d), batch, seq)
        hp, topi_t, topg_t, rank_t, cnt = _router(xt, mod3, router_w[l].T, router_b[l].reshape(N_EXPERTS, 1), seq)
        dest_t, block_e, n_valid = _dispatch_plan(topi_t, rank_t, cnt[:, 0], n_blocks)
        dest_flat = dest_t.reshape(TOP_K * t)
        xs = _sc_dispatch(hp, dest_flat, rows)
        ys = _experts(xs, block_e, n_valid, w1_all, b1_all, w2_all, b2_all, l)
        yg = _sc_gather(ys, dest_flat).reshape(TOP_K, t, PACKED_D)
        xt = _combine(xt, yg, topg_t.T, mod3, ln2_g[l].reshape(1, d), ln2_b[l].reshape(1, d), seq)
    return xt.reshape(batch, seq, d)
```

```python
import functools

import jax
import jax.numpy as jnp
from jax import lax
from jax.experimental import pallas as pl
from jax.experimental.pallas import tpu as pltpu
from jax.experimental.pallas import tpu_sc as plsc

D_MODEL = 1024
DEPTH = 4
GDN_HEADS = 4
GDN_DK = 128
GDN_DV = 128
GDN_CONV = 4
GDN_CHUNK = 64
SWA_HQ = 8
SWA_HKV = 2
SWA_DH = 64
WINDOW = 128
POOL_WINDOWS = (2, 4, 8, 16)
POOL_GROUPS = 4
POOL_GDIM = 128
N_BRANCH = 3
N_EXPERTS = 32
TOP_K = 4
D_FF = 1024
ROUTE_BLOCK = 512
SWIGLU_ALPHA = 1.702
SWIGLU_LIMIT = 7.0
LN_EPS = 1e-5
RMS_EPS = 1e-6
DEEPNORM_ALPHA = (2 * DEPTH) ** 0.25

GDN_QK = GDN_HEADS * GDN_DK
GDN_V = GDN_HEADS * GDN_DV
SWA_Q = SWA_HQ * SWA_DH
SWA_KV = SWA_HKV * SWA_DH
POOL_DIM = POOL_GROUPS * POOL_GDIM
GQA_GROUP = SWA_HQ // SWA_HKV
MAX_POOL = max(POOL_WINDOWS)

QKV_W = 2 * GDN_QK + GDN_V
AB_W = 128
SEG_QKV = 0
SEG_Z = SEG_QKV + QKV_W
SEG_AB = SEG_Z + GDN_V
SEG_QB = SEG_AB + AB_W
SEG_KVB = SEG_QB + SWA_Q
SEG_UC = SEG_KVB + 2 * SWA_KV
SEG_GATE = SEG_UC + POOL_DIM
PROJ_W = SEG_GATE + N_BRANCH * D_MODEL

TOKEN_TILE = 512
NEG_BIG = -1e30
VMEM_LIMIT = 56 * 1024 * 1024

PACKED_D = D_MODEL // 2
SC_CORES = 2
SC_SUBCORES = 16
SC_WORKERS = SC_CORES * SC_SUBCORES
SC_CHUNK = 64

F32 = jnp.float32
BF16 = jnp.bfloat16
U32 = jnp.uint32
HI_MASK = 0xFFFF0000


def _pack_rows(x):
    bits = pltpu.bitcast(x.astype(BF16).astype(F32), U32)
    return (bits[:, :PACKED_D] >> 16) | (bits[:, PACKED_D:] & jnp.uint32(HI_MASK))


def _unpack_rows(w):
    lo = pltpu.bitcast(w << 16, F32)
    hi = pltpu.bitcast(w & jnp.uint32(HI_MASK), F32)
    return jnp.concatenate([lo, hi], axis=1)


def _params(*sem):
    return pltpu.CompilerParams(dimension_semantics=sem, vmem_limit_bytes=VMEM_LIMIT)


def _sigmoid(x):
    return 1.0 / (1.0 + jnp.exp(-x))


def _layer_norm(x):
    mu = jnp.mean(x, -1, keepdims=True)
    xc = x - mu
    var = jnp.mean(xc * xc, -1, keepdims=True)
    return xc * lax.rsqrt(var + LN_EPS)


def _dot(a, b):
    return jnp.dot(a, b, preferred_element_type=F32)


def _dot_nt(a, b):
    return lax.dot_general(a, b, (((1,), (1,)), ((), ())), preferred_element_type=F32)


def _dot_tn(a, b):
    return lax.dot_general(a, b, (((0,), (0,)), ((), ())), preferred_element_type=F32)


def _bdot(a, b):
    return lax.dot_general(a, b, (((2,), (1,)), ((0,), (0,))), preferred_element_type=F32)


def _bdot_tn(a, b):
    return lax.dot_general(a, b, (((1,), (1,)), ((0,), (0,))), preferred_element_type=F32)


def _bdot_nt(a, b):
    return lax.dot_general(a, b, (((2,), (2,)), ((0,), (0,))), preferred_element_type=F32)


def _ada_kernel(c_ref, w_ref, b_ref, o_ref):
    c = c_ref[...]
    cond = c * _sigmoid(c)
    o_ref[0] = _dot(cond.astype(BF16), w_ref[0].astype(BF16)) + b_ref[0]


def _ada_mod(c, ada_w, ada_b):
    depth, d, n = ada_w.shape
    b = c.shape[0]
    tn = 1024
    return pl.pallas_call(
        _ada_kernel,
        grid=(depth, n // tn),
        in_specs=[
            pl.BlockSpec((b, d), lambda l, j: (0, 0)),
            pl.BlockSpec((1, d, tn), lambda l, j: (l, 0, j)),
            pl.BlockSpec((1, 1, tn), lambda l, j: (l, 0, j)),
        ],
        out_specs=pl.BlockSpec((1, b, tn), lambda l, j: (l, 0, j)),
        out_shape=jax.ShapeDtypeStruct((depth, b, n), F32),
        compiler_params=_params("parallel", "parallel"),
        name="ada_mod",
    )(c, ada_w, ada_b.reshape(depth, 1, n))


def _proj_kernel(x_ref, sh_ref, sc_ref, w_ref, convw_ref, qkv_ref, z_ref, ab_ref, qb_ref, kvb_ref, uc_ref, gate_ref,
                 carry_ref, *, tm, per_b):
    h = (_layer_norm(x_ref[...]) * (1.0 + sc_ref[0]) + sh_ref[0]).astype(BF16)

    def seg(start, width):
        return _dot(h, w_ref[:, start:start + width])

    @pl.when(pl.program_id(0) % per_b == 0)
    def _():
        carry_ref[...] = jnp.zeros_like(carry_ref)

    cw = convw_ref[...]
    for j in range(QKV_W // 512):
        cols = slice(j * 512, (j + 1) * 512)
        xin = seg(SEG_QKV + j * 512, 512)
        xe = jnp.concatenate([carry_ref[:, cols], xin], axis=0)
        carry_ref[:, cols] = xin[tm - 8:, :]
        conv = xin * cw[GDN_CONV - 1:GDN_CONV, cols]
        for tap in range(GDN_CONV - 1):
            conv = conv + xe[5 + tap:5 + tap + tm] * cw[tap:tap + 1, cols]
        qkv_ref[:, cols] = (conv * _sigmoid(conv)).astype(BF16)
    z_ref[...] = seg(SEG_Z, GDN_V).astype(BF16)
    ab_ref[...] = seg(SEG_AB, AB_W)
    qb_ref[...] = seg(SEG_QB, SWA_Q).astype(BF16)
    kvb_ref[...] = seg(SEG_KVB, 2 * SWA_KV).astype(BF16)
    uc_ref[...] = seg(SEG_UC, POOL_DIM).astype(BF16)
    for j in range(N_BRANCH * D_MODEL // 512):
        gate_ref[:, j * 512:(j + 1) * 512] = _sigmoid(seg(SEG_GATE + j * 512, 512)).astype(BF16)


def _in_proj(x, mod3, w, conv_w, seq):
    t, d = x.shape
    tm = TOKEN_TILE
    per_b = seq // tm
    widths = (QKV_W, GDN_V, AB_W, SWA_Q, 2 * SWA_KV, POOL_DIM, N_BRANCH * D_MODEL)
    dtypes = (BF16, BF16, F32, BF16, BF16, BF16, BF16)
    return pl.pallas_call(
        functools.partial(_proj_kernel, tm=tm, per_b=per_b),
        grid=(t // tm,),
        in_specs=[
            pl.BlockSpec((tm, d), lambda i: (i, 0)),
            pl.BlockSpec((1, 1, d), lambda i: (i // per_b, 0, 0)),
            pl.BlockSpec((1, 1, d), lambda i: (i // per_b, 0, 1)),
            pl.BlockSpec((d, PROJ_W), lambda i: (0, 0)),
            pl.BlockSpec((GDN_CONV, QKV_W), lambda i: (0, 0)),
        ],
        out_specs=[pl.BlockSpec((tm, n), lambda i: (i, 0)) for n in widths],
        out_shape=[jax.ShapeDtypeStruct((t, n), dt) for n, dt in zip(widths, dtypes)],
        scratch_shapes=[pltpu.VMEM((8, QKV_W), F32)],
        compiler_params=_params("arbitrary"),
        name="in_proj",
    )(x, mod3, mod3, w, conv_w)


def _softplus(x):
    return jnp.maximum(x, 0.0) + jnp.log(1.0 + jnp.exp(-jnp.abs(x)))


def _unit_lower_inverse(lm, row, col):
    n = lm.shape[-1]
    eye = (row == col).astype(F32)
    t = None
    s = 1
    while s < n:
        join = ((row // (2 * s)) == (col // (2 * s))) & ((row % (2 * s)) >= s) & ((col % (2 * s)) < s)
        cm = jnp.where(join, lm, 0.0)
        if t is None:
            t = eye - cm
        else:
            p = _bdot(t, cm)
            t = t - _bdot(p, t)
        s *= 2
    return t


def _gdn_kernel(qkv_ref, ab_ref, z_ref, gpar_ref, nw_ref, o_ref, state_ref, *, tb):
    c_len = GDN_CHUNK
    nc = tb // c_len

    @pl.when(pl.program_id(1) == 0)
    def _():
        state_ref[...] = jnp.zeros_like(state_ref)

    ab = ab_ref[...]
    gpar = gpar_ref[...]
    g_all = gpar[0:1] * _softplus(ab + gpar[1:2])
    beta_all = _sigmoid(ab)
    pos = lax.broadcasted_iota(jnp.int32, (tb, AB_W), 0) % c_len
    gc_all = g_all
    s = 1
    while s < c_len:
        gc_all = gc_all + jnp.where(pos >= s, pltpu.roll(gc_all, s, axis=0), 0.0)
        s *= 2

    nh = GDN_HEADS
    nb = nc * nh
    row = lax.broadcasted_iota(jnp.int32, (nb, c_len, c_len), 1)
    col = lax.broadcasted_iota(jnp.int32, (nb, c_len, c_len), 2)
    causal = row >= col
    strict = row > col
    diag = row == col

    def chunked(per_head):
        return jnp.stack([per_head(h).reshape(nc, c_len, GDN_DK) for h in range(nh)], axis=1).reshape(nb, c_len, GDN_DK)

    q = chunked(lambda h: qkv_ref[:, h * GDN_DK:(h + 1) * GDN_DK].astype(F32))
    k = chunked(lambda h: qkv_ref[:, GDN_QK + h * GDN_DK:GDN_QK + (h + 1) * GDN_DK].astype(F32))
    v = chunked(lambda h: qkv_ref[:, 2 * GDN_QK + h * GDN_DV:2 * GDN_QK + (h + 1) * GDN_DV].astype(F32))
    gc = chunked(lambda h: jnp.broadcast_to(gc_all[:, h:h + 1], (tb, GDN_DK)))
    beta = chunked(lambda h: jnp.broadcast_to(beta_all[:, nh + h:nh + h + 1], (tb, GDN_DK)))
    q = q * lax.rsqrt(jnp.sum(q * q, -1, keepdims=True) + 1e-6) * (GDN_DK ** -0.5)
    k = k * lax.rsqrt(jnp.sum(k * k, -1, keepdims=True) + 1e-6)

    gc_i = gc[:, :, :c_len]
    gc_j = jnp.sum(jnp.where(diag, gc_i, 0.0), axis=1, keepdims=True)
    decay = jnp.where(causal, jnp.exp(jnp.where(causal, gc_i - gc_j, 0.0)), 0.0)
    eg = jnp.exp(gc)
    gc_last = gc[:, c_len - 1:c_len, :]
    g_last = jnp.exp(gc_last)
    k_beta = k * beta
    k16 = k.astype(BF16)
    lower = jnp.where(strict, _bdot_nt(k_beta.astype(BF16), k16) * decay, 0.0)
    attn = jnp.where(causal, _bdot_nt(q.astype(BF16), k16) * decay, 0.0).astype(BF16)
    tinv = _unit_lower_inverse(lower, row, col)
    tinv16 = tinv.astype(BF16)
    u = _bdot(tinv16, (v * beta).astype(BF16)).astype(BF16)
    w = _bdot(tinv16, (k_beta * eg).astype(BF16)).astype(BF16)
    q_g = q * eg
    k_g = (k * jnp.exp(gc_last - gc)).astype(BF16)

    kw = _bdot_tn(k_g, w).astype(BF16)
    ku = _bdot_tn(k_g, u)
    q_eff = (q_g - _bdot(attn, w)).astype(BF16)
    o_loc = _bdot(attn, u)

    st = state_ref[...]
    outs = []
    for c in range(nc):
        sl = slice(c * nh, (c + 1) * nh)
        st16 = st.astype(BF16)
        outs.append(_bdot(q_eff[sl], st16) + o_loc[sl])
        st = st * g_last[sl] - _bdot(kw[sl], st16) + ku[sl]
    state_ref[...] = st

    nw = nw_ref[...]
    for h in range(nh):
        o = jnp.concatenate([outs[c][h] for c in range(nc)], axis=0)
        zh = z_ref[:, h * GDN_DV:(h + 1) * GDN_DV].astype(F32)
        o = o * lax.rsqrt(jnp.mean(o * o, -1, keepdims=True) + RMS_EPS) * nw * (zh * _sigmoid(zh))
        o_ref[:, h * GDN_DV:(h + 1) * GDN_DV] = o.astype(BF16)


def _gdn(qkv, ab, z, gpar, norm_w, batch, seq):
    t = qkv.shape[0]
    tb = TOKEN_TILE
    per_b = seq // tb
    return pl.pallas_call(
        functools.partial(_gdn_kernel, tb=tb),
        grid=(batch, per_b),
        in_specs=[
            pl.BlockSpec((tb, QKV_W), lambda b, j: (b * per_b + j, 0)),
            pl.BlockSpec((tb, AB_W), lambda b, j: (b * per_b + j, 0)),
            pl.BlockSpec((tb, GDN_V), lambda b, j: (b * per_b + j, 0)),
            pl.BlockSpec((2, AB_W), lambda b, j: (0, 0)),
            pl.BlockSpec((1, GDN_DV), lambda b, j: (0, 0)),
        ],
        out_specs=pl.BlockSpec((tb, GDN_V), lambda b, j: (b * per_b + j, 0)),
        out_shape=jax.ShapeDtypeStruct((t, GDN_V), BF16),
        scratch_shapes=[pltpu.VMEM((GDN_HEADS, GDN_DK, GDN_DV), F32)],
        compiler_params=_params("parallel", "arbitrary"),
        name="gdn",
    )(qkv, ab, z, gpar, norm_w)


def _swa_kernel(sink_ref, q_ref, kv_ref, kvp_ref, bias_ref, o_ref, *, tq):
    first = pl.program_id(1) == 0
    kv = jnp.concatenate([kvp_ref[...], kv_ref[...]], axis=0)
    key_in_prev = lax.broadcasted_iota(jnp.int32, (WINDOW, 2 * WINDOW), 1) < WINDOW
    scale = SWA_DH ** -0.5
    for w in range(tq // WINDOW):
        qw = q_ref[w * WINDOW:(w + 1) * WINDOW, :]
        kvw = kv[w * WINDOW:(w + 2) * WINDOW, :]
        heads = []
        for hq in range(SWA_HQ):
            hk = hq // GQA_GROUP
            q = qw[:, hq * SWA_DH:(hq + 1) * SWA_DH]
            k = kvw[:, hk * SWA_DH:(hk + 1) * SWA_DH]
            v = kvw[:, SWA_KV + hk * SWA_DH:SWA_KV + (hk + 1) * SWA_DH]
            s = _dot_nt(q, k) * scale + bias_ref[hq]
            if w == 0:
                s = jnp.where(key_in_prev & first, NEG_BIG, s)
            sink = sink_ref[hq]
            m = jnp.maximum(jnp.max(s, -1, keepdims=True), sink)
            p = jnp.exp(s - m)
            denom = jnp.sum(p, -1, keepdims=True) + jnp.exp(sink - m)
            heads.append(_dot(p.astype(BF16), v) / denom)
        o_ref[w * WINDOW:(w + 1) * WINDOW, :] = jnp.concatenate(heads, axis=1).astype(BF16)


def _swa_bias():
    qi = jnp.arange(WINDOW)[:, None]
    ki = jnp.arange(2 * WINDOW)[None, :]
    dist = qi + WINDOW - ki
    valid = (dist >= 0) & (dist < WINDOW)
    slopes = 2.0 ** (-8.0 * jnp.arange(1, SWA_HQ + 1, dtype=F32) / SWA_HQ)
    bias = -slopes[:, None, None] * dist.astype(F32)[None]
    return jnp.where(valid[None], bias, NEG_BIG)


def _swa(qb, kvb, sinks, batch, seq):
    t = qb.shape[0]
    tq = TOKEN_TILE
    per_b = seq // tq
    wpb = tq // WINDOW
    return pl.pallas_call(
        functools.partial(_swa_kernel, tq=tq),
        grid=(batch, per_b),
        in_specs=[
            pl.BlockSpec(memory_space=pltpu.SMEM),
            pl.BlockSpec((tq, SWA_Q), lambda b, j: (b * per_b + j, 0)),
            pl.BlockSpec((tq, 2 * SWA_KV), lambda b, j: (b * per_b + j, 0)),
            pl.BlockSpec((WINDOW, 2 * SWA_KV), lambda b, j: (jnp.maximum((b * per_b + j) * wpb - 1, 0), 0)),
            pl.BlockSpec((SWA_HQ, WINDOW, 2 * WINDOW), lambda b, j: (0, 0, 0)),
        ],
        out_specs=pl.BlockSpec((tq, SWA_Q), lambda b, j: (b * per_b + j, 0)),
        out_shape=jax.ShapeDtypeStruct((t, SWA_Q), BF16),
        compiler_params=_params("parallel", "parallel"),
        name="swa",
    )(sinks, qb, kvb, kvb, _swa_bias())


def _deepnorm(x, y, gt, g, b):
    return _layer_norm(DEEPNORM_ALPHA * x + (1.0 + gt) * y) * g + b


def _merge_kernel(x_ref, ya_ref, yb_ref, uc_ref, ucp_ref, gate_ref, gt_ref, poolw_ref, pscale_ref,
                  wpa_ref, wpb_ref, wpc_ref, wo_ref, lng_ref, lnb_ref, o_ref, *, tm):
    j = pl.program_id(1)
    u = uc_ref[...].astype(F32)
    halo = jnp.where(j == 0, 0.0, ucp_ref[...].astype(F32))
    ue = jnp.concatenate([halo, u], axis=0)
    tpos = (j * tm + lax.broadcasted_iota(jnp.int32, (tm, POOL_GDIM), 0) + 1).astype(F32)
    ycs = []
    for gi, win in enumerate(POOL_WINDOWS):
        a = ue[:, gi * POOL_GDIM:(gi + 1) * POOL_GDIM]
        span = 1
        while span < win:
            a = a[span:] + a[:-span]
            span *= 2
        lo = MAX_POOL - win + 1
        d = a[lo:lo + tm] / jnp.minimum(tpos, float(win)) - u[:, gi * POOL_GDIM:(gi + 1) * POOL_GDIM]
        ycs.append(_dot(d.astype(BF16), poolw_ref[gi]))
    yc = jnp.concatenate(ycs, axis=1) * pscale_ref[...]
    merged = gate_ref[:, 0:D_MODEL].astype(F32) * _dot(ya_ref[...], wpa_ref[...])
    merged = merged + gate_ref[:, D_MODEL:2 * D_MODEL].astype(F32) * _dot(yb_ref[...], wpb_ref[...])
    merged = merged + gate_ref[:, 2 * D_MODEL:3 * D_MODEL].astype(F32) * _dot(yc.astype(BF16), wpc_ref[...])
    y = _dot(merged.astype(BF16), wo_ref[...])
    o_ref[...] = _deepnorm(x_ref[...], y, gt_ref[0], lng_ref[...], lnb_ref[...])


def _merge(x, ya, yb, uc, gates, mod3, pool_w, pool_scale, w_pa, w_pb, w_pc, w_o, ln_g, ln_b, batch, seq):
    t, d = x.shape
    tm = TOKEN_TILE
    per_b = seq // tm
    hpb = tm // MAX_POOL

    def tok(n):
        return pl.BlockSpec((tm, n), lambda b, j: (b * per_b + j, 0))

    def full(shape):
        return pl.BlockSpec(shape, lambda b, j: (0,) * len(shape))

    return pl.pallas_call(
        functools.partial(_merge_kernel, tm=tm),
        grid=(batch, per_b),
        in_specs=[
            tok(d), tok(GDN_V), tok(SWA_Q), tok(POOL_DIM),
            pl.BlockSpec((MAX_POOL, POOL_DIM), lambda b, j: (jnp.maximum((b * per_b + j) * hpb - 1, 0), 0)),
            tok(N_BRANCH * d),
            pl.BlockSpec((1, 1, d), lambda b, j: (b, 0, 2)),
            full((POOL_GROUPS, POOL_GDIM, POOL_GDIM)), full((1, POOL_DIM)),
            full((GDN_V, d)), full((SWA_Q, d)), full((POOL_DIM, d)), full((d, d)),
            full((1, d)), full((1, d)),
        ],
        out_specs=tok(d),
        out_shape=jax.ShapeDtypeStruct((t, d), F32),
        compiler_params=_params("parallel", "parallel"),
        name="merge",
    )(x, ya, yb, uc, uc, gates, mod3, pool_w, pool_scale, w_pa, w_pb, w_pc, w_o, ln_g, ln_b)


def _router_kernel(x_ref, sh_ref, sc_ref, rwt_ref, rb_ref, tri_ref, hp_ref, topi_ref, topg_ref, rank_ref, cnt_ref,
                   run_ref):
    @pl.when(pl.program_id(0) == 0)
    def _():
        run_ref[...] = jnp.zeros_like(run_ref)

    h = _layer_norm(x_ref[...]) * (1.0 + sc_ref[0]) + sh_ref[0]
    hp_ref[...] = _pack_rows(h)
    logits = lax.dot_general(rwt_ref[...], h, (((1,), (1,)), ((), ())), preferred_element_type=F32,
                             precision=lax.Precision.HIGHEST) + rb_ref[...]
    sub = lax.broadcasted_iota(jnp.int32, logits.shape, 0)
    vals, idxs = [], []
    for _ in range(TOP_K):
        m = jnp.max(logits, 0, keepdims=True)
        idx = jnp.min(jnp.where(logits == m, sub, N_EXPERTS), 0, keepdims=True)
        vals.append(m)
        idxs.append(idx)
        logits = jnp.where(sub == idx, -jnp.inf, logits)
    es = [jnp.exp(v - vals[0]) for v in vals]
    denom = es[0] + es[1] + es[2] + es[3]
    topi_ref[...] = jnp.concatenate(idxs, axis=0)
    topg_ref[...] = jnp.concatenate([e / denom for e in es], axis=0)

    sel = jnp.zeros(logits.shape, F32)
    for idx in idxs:
        sel = sel + (sub == idx).astype(F32)
    before = run_ref[:, 0:1] + _dot(sel.astype(BF16), tri_ref[...])
    ranks = [jnp.sum(jnp.where(sub == idx, before, 0.0), 0, keepdims=True) for idx in idxs]
    rank_ref[...] = jnp.concatenate(ranks, axis=0).astype(jnp.int32)
    run_ref[...] = run_ref[...] + jnp.sum(sel, 1, keepdims=True)
    cnt_ref[...] = run_ref[...].astype(jnp.int32)


def _router(x, mod3, router_wt, router_b, seq):
    t, d = x.shape
    tm = TOKEN_TILE
    per_b = seq // tm
    tri = (jnp.arange(tm)[:, None] < jnp.arange(tm)[None, :]).astype(BF16)
    return pl.pallas_call(
        _router_kernel,
        grid=(t // tm,),
        in_specs=[
            pl.BlockSpec((tm, d), lambda i: (i, 0)),
            pl.BlockSpec((1, 1, d), lambda i: (i // per_b, 0, 3)),
            pl.BlockSpec((1, 1, d), lambda i: (i // per_b, 0, 4)),
            pl.BlockSpec((N_EXPERTS, d), lambda i: (0, 0)),
            pl.BlockSpec((N_EXPERTS, 1), lambda i: (0, 0)),
            pl.BlockSpec((tm, tm), lambda i: (0, 0)),
        ],
        out_specs=[
            pl.BlockSpec((tm, PACKED_D), lambda i: (i, 0)),
            pl.BlockSpec((TOP_K, tm), lambda i: (0, i)),
            pl.BlockSpec((TOP_K, tm), lambda i: (0, i)),
            pl.BlockSpec((TOP_K, tm), lambda i: (0, i)),
            pl.BlockSpec((N_EXPERTS, 128), lambda i: (0, 0)),
        ],
        out_shape=[
            jax.ShapeDtypeStruct((t, PACKED_D), U32),
            jax.ShapeDtypeStruct((TOP_K, t), jnp.int32),
            jax.ShapeDtypeStruct((TOP_K, t), F32),
            jax.ShapeDtypeStruct((TOP_K, t), jnp.int32),
            jax.ShapeDtypeStruct((N_EXPERTS, 128), jnp.int32),
        ],
        scratch_shapes=[pltpu.VMEM((N_EXPERTS, 128), F32)],
        compiler_params=_params("arbitrary"),
        name="router",
    )(x, mod3, mod3, router_wt, router_b, tri)


def _dispatch_plan(topi_t, rank_t, counts, n_blocks):
    padded = (counts + ROUTE_BLOCK - 1) // ROUTE_BLOCK * ROUTE_BLOCK
    ends_p = jnp.cumsum(padded)
    pstart = ends_p - padded
    experts = jnp.arange(N_EXPERTS, dtype=jnp.int32)
    pstart_tok = jnp.sum(jnp.where(topi_t[:, :, None] == experts, pstart, 0), -1)
    dest_t = (pstart_tok + rank_t).astype(jnp.int32)
    blk_start = jnp.arange(n_blocks, dtype=jnp.int32) * ROUTE_BLOCK
    block_e = jnp.minimum(jnp.sum((blk_start[:, None] >= ends_p).astype(jnp.int32), -1), N_EXPERTS - 1)
    blk_end = jnp.sum(jnp.where(block_e[:, None] == experts, pstart + counts, 0), -1)
    n_valid = jnp.clip(blk_end - blk_start, 0, ROUTE_BLOCK).astype(jnp.int32)
    return dest_t, block_e.astype(jnp.int32), n_valid


def _sc_worker_base(per_worker):
    return (lax.axis_index("s") * SC_CORES + lax.axis_index("c")) * per_worker


def _sc_mesh():
    return plsc.VectorSubcoreMesh(core_axis_name="c", subcore_axis_name="s")


def _sc_dispatch(hp, dest_flat, rows):
    t, dp = hp.shape
    per_w = t // SC_WORKERS
    n_chunks = per_w // SC_CHUNK

    @functools.partial(
        pl.kernel, mesh=_sc_mesh(), out_type=jax.ShapeDtypeStruct((rows, dp), hp.dtype),
        scratch_types=[pltpu.VMEM((SC_CHUNK,), jnp.int32), pltpu.VMEM((SC_CHUNK, dp), hp.dtype),
                       pltpu.SemaphoreType.DMA])
    def dispatch(h_hbm, d_hbm, xs_hbm, idx_v, rows_v, sem):
        base = _sc_worker_base(per_w)

        @pl.loop(0, n_chunks)
        def _(i):
            off = pl.multiple_of(base + i * SC_CHUNK, SC_CHUNK)
            pltpu.sync_copy(h_hbm.at[pl.ds(off, SC_CHUNK)], rows_v)
            for k in range(TOP_K):
                pltpu.sync_copy(d_hbm.at[pl.ds(k * t + off, SC_CHUNK)], idx_v)
                pltpu.async_copy(rows_v, xs_hbm.at[idx_v], sem).wait()

    return dispatch(hp, dest_flat)


def _sc_gather(table, idx):
    n = idx.shape[0]
    dp = table.shape[1]
    per_w = n // SC_WORKERS
    n_chunks = per_w // SC_CHUNK

    @functools.partial(
        pl.kernel, mesh=_sc_mesh(), out_type=jax.ShapeDtypeStruct((n, dp), table.dtype),
        scratch_types=[pltpu.VMEM((SC_CHUNK,), jnp.int32), pltpu.VMEM((SC_CHUNK, dp), table.dtype),
                       pltpu.SemaphoreType.DMA])
    def gather(t_hbm, i_hbm, o_hbm, idx_v, rows_v, sem):
        base = _sc_worker_base(per_w)

        @pl.loop(0, n_chunks)
        def _(i):
            off = pl.multiple_of(base + i * SC_CHUNK, SC_CHUNK)
            pltpu.sync_copy(i_hbm.at[pl.ds(off, SC_CHUNK)], idx_v)
            pltpu.async_copy(t_hbm.at[idx_v], rows_v, sem).wait()
            pltpu.sync_copy(rows_v, o_hbm.at[pl.ds(off, SC_CHUNK)])

    return gather(table, idx)


def _expert_kernel(be_ref, nv_ref, x_ref, w1_ref, b1_ref, w2_ref, b2_ref, o_ref, w1b_ref, w2b_ref):
    i = pl.program_id(0)
    n_valid = nv_ref[i]
    new_expert = (i == 0) | (be_ref[i] != be_ref[jnp.maximum(i - 1, 0)])

    @pl.when(new_expert & (n_valid > 0))
    def _():
        w1b_ref[...] = w1_ref[0].astype(BF16)
        w2b_ref[...] = w2_ref[0].astype(BF16)

    @pl.when(n_valid > 0)
    def _():
        live = lax.broadcasted_iota(jnp.int32, (ROUTE_BLOCK, D_MODEL), 0) < n_valid
        x = jnp.where(live, _unpack_rows(x_ref[...]), 0.0).astype(BF16)
        gu = _dot(x, w1b_ref[...]) + b1_ref[0]
        glu = jnp.minimum(gu[:, :D_FF], SWIGLU_LIMIT)
        lin = jnp.clip(gu[:, D_FF:], -SWIGLU_LIMIT, SWIGLU_LIMIT)
        act = glu * _sigmoid(SWIGLU_ALPHA * glu) * (lin + 1.0)
        o_ref[...] = _pack_rows(_dot(act.astype(BF16), w2b_ref[...]) + b2_ref[0])

    @pl.when(n_valid == 0)
    def _():
        o_ref[...] = jnp.zeros_like(o_ref)


def _experts(xs, block_e, n_valid, w1, b1, w2, b2, layer):
    rows, dp = xs.shape
    d = D_MODEL
    n_blocks = rows // ROUTE_BLOCK
    e0 = layer * N_EXPERTS
    grid_spec = pltpu.PrefetchScalarGridSpec(
        num_scalar_prefetch=2,
        grid=(n_blocks,),
        in_specs=[
            pl.BlockSpec((ROUTE_BLOCK, dp), lambda i, be, nv: (i, 0)),
            pl.BlockSpec((1, d, 2 * D_FF), lambda i, be, nv: (e0 + be[i], 0, 0)),
            pl.BlockSpec((1, 1, 2 * D_FF), lambda i, be, nv: (e0 + be[i], 0, 0)),
            pl.BlockSpec((1, D_FF, d), lambda i, be, nv: (e0 + be[i], 0, 0)),
            pl.BlockSpec((1, 1, d), lambda i, be, nv: (e0 + be[i], 0, 0)),
        ],
        out_specs=pl.BlockSpec((ROUTE_BLOCK, dp), lambda i, be, nv: (i, 0)),
        scratch_shapes=[pltpu.VMEM((d, 2 * D_FF), BF16), pltpu.VMEM((D_FF, d), BF16)],
    )
    return pl.pallas_call(
        _expert_kernel,
        grid_spec=grid_spec,
        out_shape=jax.ShapeDtypeStruct((rows, dp), U32),
        compiler_params=_params("arbitrary"),
        name="experts",
    )(block_e, n_valid, xs, w1, b1, w2, b2)


def _combine_kernel(x_ref, yg_ref, topg_ref, gt_ref, lng_ref, lnb_ref, o_ref):
    g = topg_ref[...]
    y = g[:, 0:1] * _unpack_rows(yg_ref[0])
    for k in range(1, TOP_K):
        y = y + g[:, k:k + 1] * _unpack_rows(yg_ref[k])
    o_ref[...] = _deepnorm(x_ref[...], y, gt_ref[0], lng_ref[...], lnb_ref[...])


def _combine(x, yg, topg, mod3, ln_g, ln_b, seq):
    t, d = x.shape
    tm = TOKEN_TILE
    per_b = seq // tm
    return pl.pallas_call(
        _combine_kernel,
        grid=(t // tm,),
        in_specs=[
            pl.BlockSpec((tm, d), lambda i: (i, 0)),
            pl.BlockSpec((TOP_K, tm, PACKED_D), lambda i: (0, i, 0)),
            pl.BlockSpec((tm, TOP_K), lambda i: (i, 0)),
            pl.BlockSpec((1, 1, d), lambda i: (i // per_b, 0, 5)),
            pl.BlockSpec((1, d), lambda i: (0, 0)),
            pl.BlockSpec((1, d), lambda i: (0, 0)),
        ],
        out_specs=pl.BlockSpec((tm, d), lambda i: (i, 0)),
        out_shape=jax.ShapeDtypeStruct((t, d), F32),
        compiler_params=_params("parallel"),
        name="combine",
    )(x, yg, topg, mod3, ln_g, ln_b)


def _rearranged_w_in(w_in):
    d = w_in.shape[0]
    o = 0
    cols = {}
    for name, n in (("qa", GDN_QK), ("ka", GDN_QK), ("va", GDN_V), ("z", GDN_V), ("a", GDN_HEADS), ("b", GDN_HEADS),
                    ("qb", SWA_Q), ("kb", SWA_KV), ("vb", SWA_KV), ("uc", POOL_DIM), ("gate", N_BRANCH * D_MODEL)):
        cols[name] = w_in[:, o:o + n]
        o += n
    ab_pad = jnp.zeros((d, AB_W - 2 * GDN_HEADS), w_in.dtype)
    return jnp.concatenate([cols["qa"], cols["ka"], cols["va"], cols["z"], cols["a"], cols["b"], ab_pad,
                            cols["qb"], cols["kb"], cols["vb"], cols["uc"], cols["gate"]], axis=1).astype(BF16)


def kernel(x, c, ada_w, ada_b, w_in, conv_w, a_log, dt_bias, gdn_norm_w, sinks, pool_w, pool_scale, w_pa, w_pb, w_pc, w_o, ln1_g, ln1_b, ln2_g, ln2_b, router_w, router_b, exp_w1, exp_b1, exp_w2, exp_b2):
    batch, seq, d = x.shape
    t = batch * seq
    n_blocks = (t * TOP_K + ROUTE_BLOCK - 1) // ROUTE_BLOCK + N_EXPERTS
    mod = _ada_mod(c, ada_w, ada_b)
    xt = x.reshape(t, d)
    lane_pad = jnp.zeros((AB_W - GDN_HEADS,), F32)
    rows = n_blocks * ROUTE_BLOCK
    w1_all = exp_w1.reshape(DEPTH * N_EXPERTS, d, 2 * D_FF)
    b1_all = exp_b1.reshape(DEPTH * N_EXPERTS, 1, 2 * D_FF)
    w2_all = exp_w2.reshape(DEPTH * N_EXPERTS, D_FF, d)
    b2_all = exp_b2.reshape(DEPTH * N_EXPERTS, 1, d)
    for l in range(DEPTH):
        mod3 = mod[l].reshape(batch, 1, 6 * d)
        qkv, z, ab, qb, kvb, uc, gates = _in_proj(xt, mod3, _rearranged_w_in(w_in[l]), conv_w[l], seq)
        gpar = jnp.stack([jnp.concatenate([-jnp.exp(a_log[l]), lane_pad]), jnp.concatenate([dt_bias[l], lane_pad])])
        ya = _gdn(qkv, ab, z, gpar, gdn_norm_w[l].reshape(1, GDN_DV), batch, seq)
        yb = _swa(qb, kvb, sinks[l], batch, seq)
        xt = _merge(xt, ya, yb, uc, gates, mod3, pool_w[l].astype(BF16), pool_scale[l].reshape(1, POOL_DIM),
                    w_pa[l].astype(BF16), w_pb[l].astype(BF16), w_pc[l].astype(BF16), w_o[l].astype(BF16),
                    ln1_g[l].reshape(1, d), ln1_b[l].reshape(1, d), batch, seq)
        hp, topi_t, topg_t, rank_t, cnt = _router(xt, mod3, router_w[l].T, router_b[l].reshape(N_EXPERTS, 1), seq)
        dest_t, block_e, n_valid = _dispatch_plan(topi_t, rank_t, cnt[:, 0], n_blocks)
        dest_flat = dest_t.reshape(TOP_K * t)
        xs = _sc_dispatch(hp, dest_flat, rows)
        ys = _experts(xs, block_e, n_valid, w1_all, b1_all, w2_all, b2_all, l)
        yg = _sc_gather(ys, dest_flat).reshape(TOP_K, t, PACKED_D)
        xt = _combine(xt, yg, topg_t.T, mod3, ln2_g[l].reshape(1, d), ln2_b[l].reshape(1, d), seq)
    return xt.reshape(batch, seq, d)
```

```python
import functools

import jax
import jax.numpy as jnp
from jax import lax
from jax.experimental import pallas as pl
from jax.experimental.pallas import tpu as pltpu
from jax.experimental.pallas import tpu_sc as plsc

D_MODEL = 1024
DEPTH = 4
GDN_HEADS = 4
GDN_DK = 128
GDN_DV = 128
GDN_CONV = 4
GDN_CHUNK = 64
SWA_HQ = 8
SWA_HKV = 2
SWA_DH = 64
WINDOW = 128
POOL_WINDOWS = (2, 4, 8, 16)
POOL_GROUPS = 4
POOL_GDIM = 128
N_BRANCH = 3
N_EXPERTS = 32
TOP_K = 4
D_FF = 1024
ROUTE_BLOCK = 512
SWIGLU_ALPHA = 1.702
SWIGLU_LIMIT = 7.0
LN_EPS = 1e-5
RMS_EPS = 1e-6
DEEPNORM_ALPHA = (2 * DEPTH) ** 0.25

GDN_QK = GDN_HEADS * GDN_DK
GDN_V = GDN_HEADS * GDN_DV
SWA_Q = SWA_HQ * SWA_DH
SWA_KV = SWA_HKV * SWA_DH
POOL_DIM = POOL_GROUPS * POOL_GDIM
GQA_GROUP = SWA_HQ // SWA_HKV
MAX_POOL = max(POOL_WINDOWS)

QKV_W = 2 * GDN_QK + GDN_V
AB_W = 128
SEG_QKV = 0
SEG_Z = SEG_QKV + QKV_W
SEG_AB = SEG_Z + GDN_V
SEG_QB = SEG_AB + AB_W
SEG_KVB = SEG_QB + SWA_Q
SEG_UC = SEG_KVB + 2 * SWA_KV
SEG_GATE = SEG_UC + POOL_DIM
PROJ_W = SEG_GATE + N_BRANCH * D_MODEL

PROJ_CHUNK = 256
TOKEN_TILE = 512
NEG_BIG = -1e30
VMEM_LIMIT = 56 * 1024 * 1024

PACKED_D = D_MODEL // 2
SC_CORES = 2
SC_SUBCORES = 16
SC_WORKERS = SC_CORES * SC_SUBCORES
SC_CHUNK = 64

F32 = jnp.float32
BF16 = jnp.bfloat16
U32 = jnp.uint32
HI_MASK = 0xFFFF0000


def _pack_rows(x):
    bits = pltpu.bitcast(x.astype(BF16).astype(F32), U32)
    return (bits[:, :PACKED_D] >> 16) | (bits[:, PACKED_D:] & jnp.uint32(HI_MASK))


def _unpack_rows(w):
    lo = pltpu.bitcast(w << 16, F32)
    hi = pltpu.bitcast(w & jnp.uint32(HI_MASK), F32)
    return jnp.concatenate([lo, hi], axis=1)


def _params(*sem):
    return pltpu.CompilerParams(dimension_semantics=sem, vmem_limit_bytes=VMEM_LIMIT)


def _sigmoid(x):
    return 1.0 / (1.0 + jnp.exp(-x))


def _layer_norm(x):
    mu = jnp.mean(x, -1, keepdims=True)
    xc = x - mu
    var = jnp.mean(xc * xc, -1, keepdims=True)
    return xc * lax.rsqrt(var + LN_EPS)


def _dot(a, b):
    return jnp.dot(a, b, preferred_element_type=F32)


def _dot_nt(a, b):
    return lax.dot_general(a, b, (((1,), (1,)), ((), ())), preferred_element_type=F32)


def _dot_tn(a, b):
    return lax.dot_general(a, b, (((0,), (0,)), ((), ())), preferred_element_type=F32)


def _bdot(a, b):
    return lax.dot_general(a, b, (((2,), (1,)), ((0,), (0,))), preferred_element_type=F32)


def _bdot_tn(a, b):
    return lax.dot_general(a, b, (((1,), (1,)), ((0,), (0,))), preferred_element_type=F32)


def _bdot_nt(a, b):
    return lax.dot_general(a, b, (((2,), (2,)), ((0,), (0,))), preferred_element_type=F32)


def _ada_kernel(c_ref, w_ref, b_ref, o_ref):
    c = c_ref[...]
    cond = c * _sigmoid(c)
    o_ref[0] = _dot(cond.astype(BF16), w_ref[0].astype(BF16)) + b_ref[0]


def _ada_mod(c, ada_w, ada_b):
    depth, d, n = ada_w.shape
    b = c.shape[0]
    tn = 1024
    return pl.pallas_call(
        _ada_kernel,
        grid=(depth, n // tn),
        in_specs=[
            pl.BlockSpec((b, d), lambda l, j: (0, 0)),
            pl.BlockSpec((1, d, tn), lambda l, j: (l, 0, j)),
            pl.BlockSpec((1, 1, tn), lambda l, j: (l, 0, j)),
        ],
        out_specs=pl.BlockSpec((1, b, tn), lambda l, j: (l, 0, j)),
        out_shape=jax.ShapeDtypeStruct((depth, b, n), F32),
        compiler_params=_params("parallel", "parallel"),
        name="ada_mod",
    )(c, ada_w, ada_b.reshape(depth, 1, n))


def _proj_kernel(x_ref, sh_ref, sc_ref, w_ref, convw_ref, qkv_ref, z_ref, ab_ref, qb_ref, kvb_ref, uc_ref, gate_ref,
                 carry_ref, *, tm, per_b):
    h = (_layer_norm(x_ref[...]) * (1.0 + sc_ref[0]) + sh_ref[0]).astype(BF16)

    def seg(start, width):
        return _dot(h, w_ref[:, start:start + width])

    @pl.when(pl.program_id(0) % per_b == 0)
    def _():
        carry_ref[...] = jnp.zeros_like(carry_ref)

    cw = convw_ref[...]
    for j in range(QKV_W // PROJ_CHUNK):
        cols = slice(j * PROJ_CHUNK, (j + 1) * PROJ_CHUNK)
        xin = seg(SEG_QKV + j * PROJ_CHUNK, PROJ_CHUNK)
        xe = jnp.concatenate([carry_ref[:, cols], xin], axis=0)
        carry_ref[:, cols] = xin[tm - 8:, :]
        conv = xin * cw[GDN_CONV - 1:GDN_CONV, cols]
        for tap in range(GDN_CONV - 1):
            conv = conv + xe[5 + tap:5 + tap + tm] * cw[tap:tap + 1, cols]
        qkv_ref[:, cols] = (conv * _sigmoid(conv)).astype(BF16)
    for ref, start, width in ((z_ref, SEG_Z, GDN_V), (qb_ref, SEG_QB, SWA_Q), (kvb_ref, SEG_KVB, 2 * SWA_KV),
                              (uc_ref, SEG_UC, POOL_DIM)):
        for j in range(width // PROJ_CHUNK):
            cols = slice(j * PROJ_CHUNK, (j + 1) * PROJ_CHUNK)
            ref[:, cols] = seg(start + j * PROJ_CHUNK, PROJ_CHUNK).astype(BF16)
    ab_ref[...] = seg(SEG_AB, AB_W)
    for j in range(N_BRANCH * D_MODEL // PROJ_CHUNK):
        cols = slice(j * PROJ_CHUNK, (j + 1) * PROJ_CHUNK)
        gate_ref[:, cols] = _sigmoid(seg(SEG_GATE + j * PROJ_CHUNK, PROJ_CHUNK)).astype(BF16)


def _in_proj(x, mod3, w, conv_w, seq):
    t, d = x.shape
    tm = TOKEN_TILE
    per_b = seq // tm
    widths = (QKV_W, GDN_V, AB_W, SWA_Q, 2 * SWA_KV, POOL_DIM, N_BRANCH * D_MODEL)
    dtypes = (BF16, BF16, F32, BF16, BF16, BF16, BF16)
    return pl.pallas_call(
        functools.partial(_proj_kernel, tm=tm, per_b=per_b),
        grid=(t // tm,),
        in_specs=[
            pl.BlockSpec((tm, d), lambda i: (i, 0)),
            pl.BlockSpec((1, 1, d), lambda i: (i // per_b, 0, 0)),
            pl.BlockSpec((1, 1, d), lambda i: (i // per_b, 0, 1)),
            pl.BlockSpec((d, PROJ_W), lambda i: (0, 0)),
            pl.BlockSpec((GDN_CONV, QKV_W), lambda i: (0, 0)),
        ],
        out_specs=[pl.BlockSpec((tm, n), lambda i: (i, 0)) for n in widths],
        out_shape=[jax.ShapeDtypeStruct((t, n), dt) for n, dt in zip(widths, dtypes)],
        scratch_shapes=[pltpu.VMEM((8, QKV_W), F32)],
        compiler_params=_params("arbitrary"),
        name="in_proj",
    )(x, mod3, mod3, w, conv_w)


def _softplus(x):
    return jnp.maximum(x, 0.0) + jnp.log(1.0 + jnp.exp(-jnp.abs(x)))


def _unit_lower_inverse(lm, row, col):
    n = lm.shape[-1]
    eye = (row == col).astype(F32)
    t = None
    s = 1
    while s < n:
        join = ((row // (2 * s)) == (col // (2 * s))) & ((row % (2 * s)) >= s) & ((col % (2 * s)) < s)
        cm = jnp.where(join, lm, 0.0)
        if t is None:
            t = eye - cm
        else:
            p = _bdot(t, cm)
            t = t - _bdot(p, t)
        s *= 2
    return t


def _gdn_kernel(qkv_ref, ab_ref, z_ref, gpar_ref, nw_ref, o_ref, state_ref, *, tb):
    c_len = GDN_CHUNK
    nc = tb // c_len

    @pl.when(pl.program_id(1) == 0)
    def _():
        state_ref[...] = jnp.zeros_like(state_ref)

    ab = ab_ref[...]
    gpar = gpar_ref[...]
    g_all = gpar[0:1] * _softplus(ab + gpar[1:2])
    beta_all = _sigmoid(ab)
    pos = lax.broadcasted_iota(jnp.int32, (tb, AB_W), 0) % c_len
    gc_all = g_all
    s = 1
    while s < c_len:
        gc_all = gc_all + jnp.where(pos >= s, pltpu.roll(gc_all, s, axis=0), 0.0)
        s *= 2

    nh = GDN_HEADS
    nb = nc * nh
    row = lax.broadcasted_iota(jnp.int32, (nb, c_len, c_len), 1)
    col = lax.broadcasted_iota(jnp.int32, (nb, c_len, c_len), 2)
    causal = row >= col
    strict = row > col
    diag = row == col

    def chunked(per_head):
        return jnp.stack([per_head(h).reshape(nc, c_len, GDN_DK) for h in range(nh)], axis=1).reshape(nb, c_len, GDN_DK)

    q = chunked(lambda h: qkv_ref[:, h * GDN_DK:(h + 1) * GDN_DK].astype(F32))
    k = chunked(lambda h: qkv_ref[:, GDN_QK + h * GDN_DK:GDN_QK + (h + 1) * GDN_DK].astype(F32))
    v = chunked(lambda h: qkv_ref[:, 2 * GDN_QK + h * GDN_DV:2 * GDN_QK + (h + 1) * GDN_DV].astype(F32))
    gc = chunked(lambda h: jnp.broadcast_to(gc_all[:, h:h + 1], (tb, GDN_DK)))
    beta = chunked(lambda h: jnp.broadcast_to(beta_all[:, nh + h:nh + h + 1], (tb, GDN_DK)))
    q = q * lax.rsqrt(jnp.sum(q * q, -1, keepdims=True) + 1e-6) * (GDN_DK ** -0.5)
    k = k * lax.rsqrt(jnp.sum(k * k, -1, keepdims=True) + 1e-6)

    gc_i = gc[:, :, :c_len]
    gc_j = jnp.sum(jnp.where(diag, gc_i, 0.0), axis=1, keepdims=True)
    decay = jnp.where(causal, jnp.exp(jnp.where(causal, gc_i - gc_j, 0.0)), 0.0)
    eg = jnp.exp(gc)
    gc_last = gc[:, c_len - 1:c_len, :]
    g_last = jnp.exp(gc_last)
    k_beta = k * beta
    k16 = k.astype(BF16)
    lower = jnp.where(strict, _bdot_nt(k_beta.astype(BF16), k16) * decay, 0.0)
    attn = jnp.where(causal, _bdot_nt(q.astype(BF16), k16) * decay, 0.0).astype(BF16)
    tinv = _unit_lower_inverse(lower, row, col)
    tinv16 = tinv.astype(BF16)
    u = _bdot(tinv16, (v * beta).astype(BF16)).astype(BF16)
    w = _bdot(tinv16, (k_beta * eg).astype(BF16)).astype(BF16)
    q_g = q * eg
    k_g = (k * jnp.exp(gc_last - gc)).astype(BF16)

    kw = _bdot_tn(k_g, w).astype(BF16)
    ku = _bdot_tn(k_g, u)
    q_eff = (q_g - _bdot(attn, w)).astype(BF16)
    o_loc = _bdot(attn, u)

    st = state_ref[...]
    outs = []
    for c in range(nc):
        sl = slice(c * nh, (c + 1) * nh)
        st16 = st.astype(BF16)
        outs.append(_bdot(q_eff[sl], st16) + o_loc[sl])
        st = st * g_last[sl] - _bdot(kw[sl], st16) + ku[sl]
    state_ref[...] = st

    nw = nw_ref[...]
    for h in range(nh):
        o = jnp.concatenate([outs[c][h] for c in range(nc)], axis=0)
        zh = z_ref[:, h * GDN_DV:(h + 1) * GDN_DV].astype(F32)
        o = o * lax.rsqrt(jnp.mean(o * o, -1, keepdims=True) + RMS_EPS) * nw * (zh * _sigmoid(zh))
        o_ref[:, h * GDN_DV:(h + 1) * GDN_DV] = o.astype(BF16)


def _gdn(qkv, ab, z, gpar, norm_w, batch, seq):
    t = qkv.shape[0]
    tb = TOKEN_TILE
    per_b = seq // tb
    return pl.pallas_call(
        functools.partial(_gdn_kernel, tb=tb),
        grid=(batch, per_b),
        in_specs=[
            pl.BlockSpec((tb, QKV_W), lambda b, j: (b * per_b + j, 0)),
            pl.BlockSpec((tb, AB_W), lambda b, j: (b * per_b + j, 0)),
            pl.BlockSpec((tb, GDN_V), lambda b, j: (b * per_b + j, 0)),
            pl.BlockSpec((2, AB_W), lambda b, j: (0, 0)),
            pl.BlockSpec((1, GDN_DV), lambda b, j: (0, 0)),
        ],
        out_specs=pl.BlockSpec((tb, GDN_V), lambda b, j: (b * per_b + j, 0)),
        out_shape=jax.ShapeDtypeStruct((t, GDN_V), BF16),
        scratch_shapes=[pltpu.VMEM((GDN_HEADS, GDN_DK, GDN_DV), F32)],
        compiler_params=_params("parallel", "arbitrary"),
        name="gdn",
    )(qkv, ab, z, gpar, norm_w)


def _swa_kernel(sink_ref, q_ref, kv_ref, kvp_ref, bias_ref, o_ref, *, tq):
    nw = tq // WINDOW
    rows = GQA_GROUP * WINDOW
    first = pl.program_id(1) == 0
    kv = jnp.concatenate([kvp_ref[...], kv_ref[...]], axis=0)
    in_prev = lax.broadcasted_iota(jnp.int32, (nw, rows, 2 * WINDOW), 2) < WINDOW
    in_win0 = lax.broadcasted_iota(jnp.int32, (nw, rows, 2 * WINDOW), 0) == 0
    no_prev = in_prev & in_win0 & first
    scale = SWA_DH ** -0.5
    for hk in range(SWA_HKV):
        def windows(col0):
            return jnp.stack([kv[w * WINDOW:(w + 2) * WINDOW, col0:col0 + SWA_DH] for w in range(nw)])
        k = windows(hk * SWA_DH)
        v = windows(SWA_KV + hk * SWA_DH)
        q = jnp.concatenate(
            [q_ref[:, hq * SWA_DH:(hq + 1) * SWA_DH].reshape(nw, WINDOW, SWA_DH)
             for hq in range(hk * GQA_GROUP, (hk + 1) * GQA_GROUP)], axis=1)
        sink = jnp.concatenate(
            [jnp.full((WINDOW, 1), sink_ref[hq], F32) for hq in range(hk * GQA_GROUP, (hk + 1) * GQA_GROUP)], axis=0)
        s = _bdot_nt(q, k) * scale + bias_ref[hk]
        s = jnp.where(no_prev, NEG_BIG, s)
        m = jnp.maximum(jnp.max(s, -1, keepdims=True), sink)
        p = jnp.exp(s - m)
        v_ext = jnp.concatenate([v, jnp.ones((nw, 2 * WINDOW, SWA_DH), BF16)], axis=2)
        pv = _bdot(p.astype(BF16), v_ext)
        denom = pv[:, :, SWA_DH:SWA_DH + 1] + jnp.exp(sink - m)
        o = pv[:, :, :SWA_DH] / denom
        for g in range(GQA_GROUP):
            hq = hk * GQA_GROUP + g
            o_ref[:, hq * SWA_DH:(hq + 1) * SWA_DH] = o[:, g * WINDOW:(g + 1) * WINDOW, :].reshape(tq, SWA_DH).astype(BF16)


def _swa_bias():
    qi = jnp.arange(WINDOW)[:, None]
    ki = jnp.arange(2 * WINDOW)[None, :]
    dist = qi + WINDOW - ki
    valid = (dist >= 0) & (dist < WINDOW)
    slopes = 2.0 ** (-8.0 * jnp.arange(1, SWA_HQ + 1, dtype=F32) / SWA_HQ)
    bias = -slopes[:, None, None] * dist.astype(F32)[None]
    bias = jnp.where(valid[None], bias, NEG_BIG)
    return bias.reshape(SWA_HKV, GQA_GROUP * WINDOW, 2 * WINDOW)


def _swa(qb, kvb, sinks, batch, seq):
    t = qb.shape[0]
    tq = TOKEN_TILE
    per_b = seq // tq
    wpb = tq // WINDOW
    return pl.pallas_call(
        functools.partial(_swa_kernel, tq=tq),
        grid=(batch, per_b),
        in_specs=[
            pl.BlockSpec(memory_space=pltpu.SMEM),
            pl.BlockSpec((tq, SWA_Q), lambda b, j: (b * per_b + j, 0)),
            pl.BlockSpec((tq, 2 * SWA_KV), lambda b, j: (b * per_b + j, 0)),
            pl.BlockSpec((WINDOW, 2 * SWA_KV), lambda b, j: (jnp.maximum((b * per_b + j) * wpb - 1, 0), 0)),
            pl.BlockSpec((SWA_HKV, GQA_GROUP * WINDOW, 2 * WINDOW), lambda b, j: (0, 0, 0)),
        ],
        out_specs=pl.BlockSpec((tq, SWA_Q), lambda b, j: (b * per_b + j, 0)),
        out_shape=jax.ShapeDtypeStruct((t, SWA_Q), BF16),
        compiler_params=_params("parallel", "parallel"),
        name="swa",
    )(sinks, qb, kvb, kvb, _swa_bias())


def _deepnorm(x, y, gt, g, b):
    return _layer_norm(DEEPNORM_ALPHA * x + (1.0 + gt) * y) * g + b


def _merge_kernel(x_ref, ya_ref, yb_ref, uc_ref, ucp_ref, gate_ref, gt_ref, poolw_ref, pscale_ref,
                  wpa_ref, wpb_ref, wpc_ref, wo_ref, lng_ref, lnb_ref, o_ref, *, tm):
    j = pl.program_id(1)
    u = uc_ref[...].astype(F32)
    halo = jnp.where(j == 0, 0.0, ucp_ref[...].astype(F32))
    ue = jnp.concatenate([halo, u], axis=0)
    tpos = (j * tm + lax.broadcasted_iota(jnp.int32, (tm, POOL_GDIM), 0) + 1).astype(F32)
    ycs = []
    for gi, win in enumerate(POOL_WINDOWS):
        a = ue[:, gi * POOL_GDIM:(gi + 1) * POOL_GDIM]
        span = 1
        while span < win:
            a = a[span:] + a[:-span]
            span *= 2
        lo = MAX_POOL - win + 1
        d = a[lo:lo + tm] / jnp.minimum(tpos, float(win)) - u[:, gi * POOL_GDIM:(gi + 1) * POOL_GDIM]
        ycs.append(_dot(d.astype(BF16), poolw_ref[gi]))
    yc = jnp.concatenate(ycs, axis=1) * pscale_ref[...]
    merged = gate_ref[:, 0:D_MODEL].astype(F32) * _dot(ya_ref[...], wpa_ref[...])
    merged = merged + gate_ref[:, D_MODEL:2 * D_MODEL].astype(F32) * _dot(yb_ref[...], wpb_ref[...])
    merged = merged + gate_ref[:, 2 * D_MODEL:3 * D_MODEL].astype(F32) * _dot(yc.astype(BF16), wpc_ref[...])
    y = _dot(merged.astype(BF16), wo_ref[...])
    o_ref[...] = _deepnorm(x_ref[...], y, gt_ref[0], lng_ref[...], lnb_ref[...])


def _merge(x, ya, yb, uc, gates, mod3, pool_w, pool_scale, w_pa, w_pb, w_pc, w_o, ln_g, ln_b, batch, seq):
    t, d = x.shape
    tm = TOKEN_TILE
    per_b = seq // tm
    hpb = tm // MAX_POOL

    def tok(n):
        return pl.BlockSpec((tm, n), lambda b, j: (b * per_b + j, 0))

    def full(shape):
        return pl.BlockSpec(shape, lambda b, j: (0,) * len(shape))

    return pl.pallas_call(
        functools.partial(_merge_kernel, tm=tm),
        grid=(batch, per_b),
        in_specs=[
            tok(d), tok(GDN_V), tok(SWA_Q), tok(POOL_DIM),
            pl.BlockSpec((MAX_POOL, POOL_DIM), lambda b, j: (jnp.maximum((b * per_b + j) * hpb - 1, 0), 0)),
            tok(N_BRANCH * d),
            pl.BlockSpec((1, 1, d), lambda b, j: (b, 0, 2)),
            full((POOL_GROUPS, POOL_GDIM, POOL_GDIM)), full((1, POOL_DIM)),
            full((GDN_V, d)), full((SWA_Q, d)), full((POOL_DIM, d)), full((d, d)),
            full((1, d)), full((1, d)),
        ],
        out_specs=tok(d),
        out_shape=jax.ShapeDtypeStruct((t, d), F32),
        compiler_params=_params("parallel", "parallel"),
        name="merge",
    )(x, ya, yb, uc, uc, gates, mod3, pool_w, pool_scale, w_pa, w_pb, w_pc, w_o, ln_g, ln_b)


def _router_kernel(x_ref, sh_ref, sc_ref, rwt_ref, rb_ref, tri_ref, hp_ref, topi_ref, topg_ref, rank_ref, cnt_ref,
                   run_ref):
    @pl.when(pl.program_id(0) == 0)
    def _():
        run_ref[...] = jnp.zeros_like(run_ref)

    h = _layer_norm(x_ref[...]) * (1.0 + sc_ref[0]) + sh_ref[0]
    hp_ref[...] = _pack_rows(h)
    logits = lax.dot_general(rwt_ref[...], h, (((1,), (1,)), ((), ())), preferred_element_type=F32,
                             precision=lax.Precision.HIGHEST) + rb_ref[...]
    sub = lax.broadcasted_iota(jnp.int32, logits.shape, 0)
    vals, idxs = [], []
    for _ in range(TOP_K):
        m = jnp.max(logits, 0, keepdims=True)
        idx = jnp.min(jnp.where(logits == m, sub, N_EXPERTS), 0, keepdims=True)
        vals.append(m)
        idxs.append(idx)
        logits = jnp.where(sub == idx, -jnp.inf, logits)
    es = [jnp.exp(v - vals[0]) for v in vals]
    denom = es[0] + es[1] + es[2] + es[3]
    topi_ref[...] = jnp.concatenate(idxs, axis=0)
    topg_ref[...] = jnp.concatenate([e / denom for e in es], axis=0)

    sel = jnp.zeros(logits.shape, F32)
    for idx in idxs:
        sel = sel + (sub == idx).astype(F32)
    before = run_ref[:, 0:1] + _dot(sel.astype(BF16), tri_ref[...])
    ranks = [jnp.sum(jnp.where(sub == idx, before, 0.0), 0, keepdims=True) for idx in idxs]
    rank_ref[...] = jnp.concatenate(ranks, axis=0).astype(jnp.int32)
    run_ref[...] = run_ref[...] + jnp.sum(sel, 1, keepdims=True)
    cnt_ref[...] = run_ref[...].astype(jnp.int32)


def _router(x, mod3, router_wt, router_b, seq):
    t, d = x.shape
    tm = TOKEN_TILE
    per_b = seq // tm
    tri = (jnp.arange(tm)[:, None] < jnp.arange(tm)[None, :]).astype(BF16)
    return pl.pallas_call(
        _router_kernel,
        grid=(t // tm,),
        in_specs=[
            pl.BlockSpec((tm, d), lambda i: (i, 0)),
            pl.BlockSpec((1, 1, d), lambda i: (i // per_b, 0, 3)),
            pl.BlockSpec((1, 1, d), lambda i: (i // per_b, 0, 4)),
            pl.BlockSpec((N_EXPERTS, d), lambda i: (0, 0)),
            pl.BlockSpec((N_EXPERTS, 1), lambda i: (0, 0)),
            pl.BlockSpec((tm, tm), lambda i: (0, 0)),
        ],
        out_specs=[
            pl.BlockSpec((tm, PACKED_D), lambda i: (i, 0)),
            pl.BlockSpec((TOP_K, tm), lambda i: (0, i)),
            pl.BlockSpec((TOP_K, tm), lambda i: (0, i)),
            pl.BlockSpec((TOP_K, tm), lambda i: (0, i)),
            pl.BlockSpec((N_EXPERTS, 128), lambda i: (0, 0)),
        ],
        out_shape=[
            jax.ShapeDtypeStruct((t, PACKED_D), U32),
            jax.ShapeDtypeStruct((TOP_K, t), jnp.int32),
            jax.ShapeDtypeStruct((TOP_K, t), F32),
            jax.ShapeDtypeStruct((TOP_K, t), jnp.int32),
            jax.ShapeDtypeStruct((N_EXPERTS, 128), jnp.int32),
        ],
        scratch_shapes=[pltpu.VMEM((N_EXPERTS, 128), F32)],
        compiler_params=_params("arbitrary"),
        name="router",
    )(x, mod3, mod3, router_wt, router_b, tri)


def _dispatch_plan(topi_t, rank_t, counts, n_blocks):
    padded = (counts + ROUTE_BLOCK - 1) // ROUTE_BLOCK * ROUTE_BLOCK
    ends_p = jnp.cumsum(padded)
    pstart = ends_p - padded
    experts = jnp.arange(N_EXPERTS, dtype=jnp.int32)
    pstart_tok = jnp.sum(jnp.where(topi_t[:, :, None] == experts, pstart, 0), -1)
    dest_t = (pstart_tok + rank_t).astype(jnp.int32)
    blk_start = jnp.arange(n_blocks, dtype=jnp.int32) * ROUTE_BLOCK
    block_e = jnp.minimum(jnp.sum((blk_start[:, None] >= ends_p).astype(jnp.int32), -1), N_EXPERTS - 1)
    blk_end = jnp.sum(jnp.where(block_e[:, None] == experts, pstart + counts, 0), -1)
    n_valid = jnp.clip(blk_end - blk_start, 0, ROUTE_BLOCK).astype(jnp.int32)
    return dest_t, block_e.astype(jnp.int32), n_valid


def _sc_mesh():
    return plsc.VectorSubcoreMesh(core_axis_name="c", subcore_axis_name="s")


def _sc_dispatch(hp, dest_flat, rows):
    t, dp = hp.shape
    per_w = t // SC_WORKERS
    n_chunks = per_w // SC_CHUNK

    assert n_chunks % 2 == 0

    @functools.partial(
        pl.kernel, mesh=_sc_mesh(), out_type=jax.ShapeDtypeStruct((rows, dp), hp.dtype),
        scratch_types=[pltpu.VMEM((TOP_K * n_chunks, SC_CHUNK), jnp.int32),
                       pltpu.VMEM((SC_CHUNK, dp), hp.dtype), pltpu.VMEM((SC_CHUNK, dp), hp.dtype),
                       pltpu.SemaphoreType.DMA, pltpu.SemaphoreType.DMA,
                       pltpu.SemaphoreType.DMA, pltpu.SemaphoreType.DMA])
    def dispatch(h_hbm, d_hbm, xs_hbm, idx_v, rows_a, rows_b, load_a, load_b, scat_a, scat_b):
        worker = lax.axis_index("s") * SC_CORES + lax.axis_index("c")
        base = worker * per_w
        for k in range(TOP_K):
            pltpu.sync_copy(d_hbm.at[pl.ds((k * SC_WORKERS + worker) * n_chunks, n_chunks)],
                            idx_v.at[pl.ds(k * n_chunks, n_chunks)])

        def load(i, buf, sem):
            return pltpu.make_async_copy(h_hbm.at[pl.ds(base + i * SC_CHUNK, SC_CHUNK)], buf, sem)

        def scatter(i, k, buf, sem):
            return pltpu.make_async_copy(buf, xs_hbm.at[idx_v.at[k * n_chunks + i]], sem)

        def scatter_all(i, buf, sem):
            for k in range(TOP_K):
                scatter(i, k, buf, sem).start()
            for k in range(TOP_K):
                scatter(i, k, buf, sem).wait()

        load(0, rows_a, load_a).start()

        @pl.loop(0, n_chunks, step=2)
        def _(i):
            load(i, rows_a, load_a).wait()
            load(i + 1, rows_b, load_b).start()
            scatter_all(i, rows_a, scat_a)
            load(i + 1, rows_b, load_b).wait()

            @pl.when(i + 2 < n_chunks)
            def _():
                load(i + 2, rows_a, load_a).start()

            scatter_all(i + 1, rows_b, scat_b)

    return dispatch(hp, dest_flat.reshape(TOP_K * SC_WORKERS * n_chunks, SC_CHUNK))


def _sc_gather(table, idx):
    n = idx.shape[0]
    dp = table.shape[1]
    per_w = n // SC_WORKERS
    n_chunks = per_w // SC_CHUNK

    assert n_chunks % 2 == 0

    @functools.partial(
        pl.kernel, mesh=_sc_mesh(), out_type=jax.ShapeDtypeStruct((n, dp), table.dtype),
        scratch_types=[pltpu.VMEM((n_chunks, SC_CHUNK), jnp.int32),
                       pltpu.VMEM((SC_CHUNK, dp), table.dtype), pltpu.VMEM((SC_CHUNK, dp), table.dtype),
                       pltpu.SemaphoreType.DMA, pltpu.SemaphoreType.DMA,
                       pltpu.SemaphoreType.DMA, pltpu.SemaphoreType.DMA])
    def gather(t_hbm, i_hbm, o_hbm, idx_v, rows_a, rows_b, fetch_a, fetch_b, store_a, store_b):
        worker = lax.axis_index("s") * SC_CORES + lax.axis_index("c")
        base = worker * per_w
        pltpu.sync_copy(i_hbm.at[pl.ds(worker * n_chunks, n_chunks)], idx_v)

        def fetch(i, buf, sem):
            return pltpu.make_async_copy(t_hbm.at[idx_v.at[i]], buf, sem)

        def store(i, buf, sem):
            return pltpu.make_async_copy(buf, o_hbm.at[pl.ds(base + i * SC_CHUNK, SC_CHUNK)], sem)

        fetch(0, rows_a, fetch_a).start()

        @pl.loop(0, n_chunks, step=2)
        def _(i):
            fetch(i, rows_a, fetch_a).wait()

            @pl.when(i > 0)
            def _():
                store(i - 1, rows_b, store_b).wait()

            fetch(i + 1, rows_b, fetch_b).start()
            store(i, rows_a, store_a).start()
            fetch(i + 1, rows_b, fetch_b).wait()
            store(i, rows_a, store_a).wait()

            @pl.when(i + 2 < n_chunks)
            def _():
                fetch(i + 2, rows_a, fetch_a).start()

            store(i + 1, rows_b, store_b).start()

        store(n_chunks - 1, rows_b, store_b).wait()

    return gather(table, idx.reshape(SC_WORKERS * n_chunks, SC_CHUNK))


def _expert_kernel(be_ref, nv_ref, x_ref, w1_ref, b1_ref, w2_ref, b2_ref, o_ref, w1b_ref, w2b_ref):
    i = pl.program_id(0)
    n_valid = nv_ref[i]
    new_expert = (i == 0) | (be_ref[i] != be_ref[jnp.maximum(i - 1, 0)])

    @pl.when(new_expert & (n_valid > 0))
    def _():
        w1b_ref[...] = w1_ref[0].astype(BF16)
        w2b_ref[...] = w2_ref[0].astype(BF16)

    @pl.when(n_valid > 0)
    def _():
        live = lax.broadcasted_iota(jnp.int32, (ROUTE_BLOCK, D_MODEL), 0) < n_valid
        x = jnp.where(live, _unpack_rows(x_ref[...]), 0.0).astype(BF16)
        gu = _dot(x, w1b_ref[...]) + b1_ref[0]
        glu = jnp.minimum(gu[:, :D_FF], SWIGLU_LIMIT)
        lin = jnp.clip(gu[:, D_FF:], -SWIGLU_LIMIT, SWIGLU_LIMIT)
        act = glu * _sigmoid(SWIGLU_ALPHA * glu) * (lin + 1.0)
        o_ref[...] = _pack_rows(_dot(act.astype(BF16), w2b_ref[...]) + b2_ref[0])

    @pl.when(n_valid == 0)
    def _():
        o_ref[...] = jnp.zeros_like(o_ref)


def _experts(xs, block_e, n_valid, w1, b1, w2, b2, layer):
    rows, dp = xs.shape
    d = D_MODEL
    n_blocks = rows // ROUTE_BLOCK
    e0 = layer * N_EXPERTS
    grid_spec = pltpu.PrefetchScalarGridSpec(
        num_scalar_prefetch=2,
        grid=(n_blocks,),
        in_specs=[
            pl.BlockSpec((ROUTE_BLOCK, dp), lambda i, be, nv: (i, 0)),
            pl.BlockSpec((1, d, 2 * D_FF), lambda i, be, nv: (e0 + be[i], 0, 0)),
            pl.BlockSpec((1, 1, 2 * D_FF), lambda i, be, nv: (e0 + be[i], 0, 0)),
            pl.BlockSpec((1, D_FF, d), lambda i, be, nv: (e0 + be[i], 0, 0)),
            pl.BlockSpec((1, 1, d), lambda i, be, nv: (e0 + be[i], 0, 0)),
        ],
        out_specs=pl.BlockSpec((ROUTE_BLOCK, dp), lambda i, be, nv: (i, 0)),
        scratch_shapes=[pltpu.VMEM((d, 2 * D_FF), BF16), pltpu.VMEM((D_FF, d), BF16)],
    )
    return pl.pallas_call(
        _expert_kernel,
        grid_spec=grid_spec,
        out_shape=jax.ShapeDtypeStruct((rows, dp), U32),
        compiler_params=_params("arbitrary"),
        name="experts",
    )(block_e, n_valid, xs, w1, b1, w2, b2)


def _combine_kernel(x_ref, yg_ref, topg_ref, gt_ref, lng_ref, lnb_ref, o_ref):
    g = topg_ref[...]
    y = g[:, 0:1] * _unpack_rows(yg_ref[0])
    for k in range(1, TOP_K):
        y = y + g[:, k:k + 1] * _unpack_rows(yg_ref[k])
    o_ref[...] = _deepnorm(x_ref[...], y, gt_ref[0], lng_ref[...], lnb_ref[...])


def _combine(x, yg, topg, mod3, ln_g, ln_b, seq):
    t, d = x.shape
    tm = TOKEN_TILE
    per_b = seq // tm
    return pl.pallas_call(
        _combine_kernel,
        grid=(t // tm,),
        in_specs=[
            pl.BlockSpec((tm, d), lambda i: (i, 0)),
            pl.BlockSpec((TOP_K, tm, PACKED_D), lambda i: (0, i, 0)),
            pl.BlockSpec((tm, TOP_K), lambda i: (i, 0)),
            pl.BlockSpec((1, 1, d), lambda i: (i // per_b, 0, 5)),
            pl.BlockSpec((1, d), lambda i: (0, 0)),
            pl.BlockSpec((1, d), lambda i: (0, 0)),
        ],
        out_specs=pl.BlockSpec((tm, d), lambda i: (i, 0)),
        out_shape=jax.ShapeDtypeStruct((t, d), F32),
        compiler_params=_params("parallel"),
        name="combine",
    )(x, yg, topg, mod3, ln_g, ln_b)


def _rearranged_w_in(w_in):
    d = w_in.shape[0]
    o = 0
    cols = {}
    for name, n in (("qa", GDN_QK), ("ka", GDN_QK), ("va", GDN_V), ("z", GDN_V), ("a", GDN_HEADS), ("b", GDN_HEADS),
                    ("qb", SWA_Q), ("kb", SWA_KV), ("vb", SWA_KV), ("uc", POOL_DIM), ("gate", N_BRANCH * D_MODEL)):
        cols[name] = w_in[:, o:o + n]
        o += n
    ab_pad = jnp.zeros((d, AB_W - 2 * GDN_HEADS), w_in.dtype)
    return jnp.concatenate([cols["qa"], cols["ka"], cols["va"], cols["z"], cols["a"], cols["b"], ab_pad,
                            cols["qb"], cols["kb"], cols["vb"], cols["uc"], cols["gate"]], axis=1).astype(BF16)


def kernel(x, c, ada_w, ada_b, w_in, conv_w, a_log, dt_bias, gdn_norm_w, sinks, pool_w, pool_scale, w_pa, w_pb, w_pc, w_o, ln1_g, ln1_b, ln2_g, ln2_b, router_w, router_b, exp_w1, exp_b1, exp_w2, exp_b2):
    batch, seq, d = x.shape
    t = batch * seq
    n_blocks = (t * TOP_K + ROUTE_BLOCK - 1) // ROUTE_BLOCK + N_EXPERTS
    mod = _ada_mod(c, ada_w, ada_b)
    xt = x.reshape(t, d)
    lane_pad = jnp.zeros((AB_W - GDN_HEADS,), F32)
    rows = n_blocks * ROUTE_BLOCK
    w1_all = exp_w1.reshape(DEPTH * N_EXPERTS, d, 2 * D_FF)
    b1_all = exp_b1.reshape(DEPTH * N_EXPERTS, 1, 2 * D_FF)
    w2_all = exp_w2.reshape(DEPTH * N_EXPERTS, D_FF, d)
    b2_all = exp_b2.reshape(DEPTH * N_EXPERTS, 1, d)
    for l in range(DEPTH):
        mod3 = mod[l].reshape(batch, 1, 6 * d)
        qkv, z, ab, qb, kvb, uc, gates = _in_proj(xt, mod3, _rearranged_w_in(w_in[l]), conv_w[l], seq)
        gpar = jnp.stack([jnp.concatenate([-jnp.exp(a_log[l]), lane_pad]), jnp.concatenate([dt_bias[l], lane_pad])])
        ya = _gdn(qkv, ab, z, gpar, gdn_norm_w[l].reshape(1, GDN_DV), batch, seq)
        yb = _swa(qb, kvb, sinks[l], batch, seq)
        xt = _merge(xt, ya, yb, uc, gates, mod3, pool_w[l].astype(BF16), pool_scale[l].reshape(1, POOL_DIM),
                    w_pa[l].astype(BF16), w_pb[l].astype(BF16), w_pc[l].astype(BF16), w_o[l].astype(BF16),
                    ln1_g[l].reshape(1, d), ln1_b[l].reshape(1, d), batch, seq)
        hp, topi_t, topg_t, rank_t, cnt = _router(xt, mod3, router_w[l].T, router_b[l].reshape(N_EXPERTS, 1), seq)
        dest_t, block_e, n_valid = _dispatch_plan(topi_t, rank_t, cnt[:, 0], n_blocks)
        dest_flat = dest_t.reshape(TOP_K * t)
        xs = _sc_dispatch(hp, dest_flat, rows)
        ys = _experts(xs, block_e, n_valid, w1_all, b1_all, w2_all, b2_all, l)
        yg = _sc_gather(ys, dest_flat).reshape(TOP_K, t, PACKED_D)
        xt = _combine(xt, yg, topg_t.T, mod3, ln2_g[l].reshape(1, d), ln2_b[l].reshape(1, d), seq)
    return xt.reshape(batch, seq, d)
```

```python
import functools

import jax
import jax.numpy as jnp
from jax import lax
from jax.experimental import pallas as pl
from jax.experimental.pallas import tpu as pltpu
from jax.experimental.pallas import tpu_sc as plsc

D_MODEL = 1024
DEPTH = 4
GDN_HEADS = 4
GDN_DK = 128
GDN_DV = 128
GDN_CONV = 4
GDN_CHUNK = 64
SWA_HQ = 8
SWA_HKV = 2
SWA_DH = 64
WINDOW = 128
POOL_WINDOWS = (2, 4, 8, 16)
POOL_GROUPS = 4
POOL_GDIM = 128
N_BRANCH = 3
N_EXPERTS = 32
TOP_K = 4
D_FF = 1024
ROUTE_BLOCK = 512
SWIGLU_ALPHA = 1.702
SWIGLU_LIMIT = 7.0
LN_EPS = 1e-5
RMS_EPS = 1e-6
DEEPNORM_ALPHA = (2 * DEPTH) ** 0.25

GDN_QK = GDN_HEADS * GDN_DK
GDN_V = GDN_HEADS * GDN_DV
SWA_Q = SWA_HQ * SWA_DH
SWA_KV = SWA_HKV * SWA_DH
POOL_DIM = POOL_GROUPS * POOL_GDIM
GQA_GROUP = SWA_HQ // SWA_HKV
MAX_POOL = max(POOL_WINDOWS)

QKV_W = 2 * GDN_QK + GDN_V
AB_W = 128
SEG_QKV = 0
SEG_Z = SEG_QKV + QKV_W
SEG_AB = SEG_Z + GDN_V
SEG_QB = SEG_AB + AB_W
SEG_KVB = SEG_QB + SWA_Q
SEG_UC = SEG_KVB + 2 * SWA_KV
SEG_GATE = SEG_UC + POOL_DIM
PROJ_W = SEG_GATE + N_BRANCH * D_MODEL

PROJ_CHUNK = 256
TOKEN_TILE = 512
NEG_BIG = -1e30
VMEM_LIMIT = 56 * 1024 * 1024

PACKED_D = D_MODEL // 2
SC_CORES = 2
SC_SUBCORES = 16
SC_WORKERS = SC_CORES * SC_SUBCORES
SC_CHUNK = 64

F32 = jnp.float32
BF16 = jnp.bfloat16
U32 = jnp.uint32
HI_MASK = 0xFFFF0000


def _pack_rows(x):
    bits = pltpu.bitcast(x.astype(BF16).astype(F32), U32)
    return (bits[:, :PACKED_D] >> 16) | (bits[:, PACKED_D:] & jnp.uint32(HI_MASK))


def _unpack_rows(w):
    lo = pltpu.bitcast(w << 16, F32)
    hi = pltpu.bitcast(w & jnp.uint32(HI_MASK), F32)
    return jnp.concatenate([lo, hi], axis=1)


def _params(*sem):
    return pltpu.CompilerParams(dimension_semantics=sem, vmem_limit_bytes=VMEM_LIMIT)


def _sigmoid(x):
    return 1.0 / (1.0 + jnp.exp(-x))


def _layer_norm(x):
    mu = jnp.mean(x, -1, keepdims=True)
    xc = x - mu
    var = jnp.mean(xc * xc, -1, keepdims=True)
    return xc * lax.rsqrt(var + LN_EPS)


def _dot(a, b):
    return jnp.dot(a, b, preferred_element_type=F32)


def _dot_nt(a, b):
    return lax.dot_general(a, b, (((1,), (1,)), ((), ())), preferred_element_type=F32)


def _dot_tn(a, b):
    return lax.dot_general(a, b, (((0,), (0,)), ((), ())), preferred_element_type=F32)


def _bdot(a, b):
    return lax.dot_general(a, b, (((2,), (1,)), ((0,), (0,))), preferred_element_type=F32)


def _bdot_tn(a, b):
    return lax.dot_general(a, b, (((1,), (1,)), ((0,), (0,))), preferred_element_type=F32)


def _bdot_nt(a, b):
    return lax.dot_general(a, b, (((2,), (2,)), ((0,), (0,))), preferred_element_type=F32)


def _ada_kernel(c_ref, w_ref, b_ref, o_ref):
    c = c_ref[...]
    cond = c * _sigmoid(c)
    o_ref[0] = _dot(cond.astype(BF16), w_ref[0].astype(BF16)) + b_ref[0]


def _ada_mod(c, ada_w, ada_b):
    depth, d, n = ada_w.shape
    b = c.shape[0]
    tn = 1024
    return pl.pallas_call(
        _ada_kernel,
        grid=(depth, n // tn),
        in_specs=[
            pl.BlockSpec((b, d), lambda l, j: (0, 0)),
            pl.BlockSpec((1, d, tn), lambda l, j: (l, 0, j)),
            pl.BlockSpec((1, 1, tn), lambda l, j: (l, 0, j)),
        ],
        out_specs=pl.BlockSpec((1, b, tn), lambda l, j: (l, 0, j)),
        out_shape=jax.ShapeDtypeStruct((depth, b, n), F32),
        compiler_params=_params("parallel", "parallel"),
        name="ada_mod",
    )(c, ada_w, ada_b.reshape(depth, 1, n))


def _proj_kernel(x_ref, sh_ref, sc_ref, w_ref, convw_ref, qkv_ref, z_ref, ab_ref, qb_ref, kvb_ref, uc_ref, gate_ref,
                 raw_ref, *, tm, per_b):
    h = (_layer_norm(x_ref[...]) * (1.0 + sc_ref[0]) + sh_ref[0]).astype(BF16)

    def seg(start, width):
        return _dot(h, w_ref[:, start:start + width])

    @pl.when(pl.program_id(0) % per_b == 0)
    def _():
        raw_ref[0:8, :] = jnp.zeros((8, QKV_W), F32)

    @pl.when(pl.program_id(0) % per_b != 0)
    def _():
        raw_ref[0:8, :] = raw_ref[tm:tm + 8, :]

    for j in range(QKV_W // PROJ_CHUNK):
        cols = slice(j * PROJ_CHUNK, (j + 1) * PROJ_CHUNK)
        raw_ref[8:8 + tm, cols] = seg(SEG_QKV + j * PROJ_CHUNK, PROJ_CHUNK)

    def conv_silu(j):
        cols = slice(j * PROJ_CHUNK, (j + 1) * PROJ_CHUNK)
        conv = raw_ref[8:8 + tm, cols] * convw_ref[GDN_CONV - 1:GDN_CONV, cols]
        for tap in range(GDN_CONV - 1):
            conv = conv + raw_ref[5 + tap:5 + tap + tm, cols] * convw_ref[tap:tap + 1, cols]
        qkv_ref[:, cols] = (conv * _sigmoid(conv)).astype(BF16)
    n_conv = QKV_W // PROJ_CHUNK
    done = 0
    for ref, start, width in ((z_ref, SEG_Z, GDN_V), (qb_ref, SEG_QB, SWA_Q), (kvb_ref, SEG_KVB, 2 * SWA_KV),
                              (uc_ref, SEG_UC, POOL_DIM)):
        for j in range(width // PROJ_CHUNK):
            cols = slice(j * PROJ_CHUNK, (j + 1) * PROJ_CHUNK)
            ref[:, cols] = seg(start + j * PROJ_CHUNK, PROJ_CHUNK).astype(BF16)
            if done < n_conv:
                conv_silu(done)
                done += 1
    assert done == n_conv
    ab_ref[...] = seg(SEG_AB, AB_W)
    for j in range(N_BRANCH * D_MODEL // PROJ_CHUNK):
        cols = slice(j * PROJ_CHUNK, (j + 1) * PROJ_CHUNK)
        gate_ref[:, cols] = _sigmoid(seg(SEG_GATE + j * PROJ_CHUNK, PROJ_CHUNK)).astype(BF16)


def _in_proj(x, mod3, w, conv_w, seq):
    t, d = x.shape
    tm = TOKEN_TILE
    per_b = seq // tm
    widths = (QKV_W, GDN_V, AB_W, SWA_Q, 2 * SWA_KV, POOL_DIM, N_BRANCH * D_MODEL)
    dtypes = (BF16, BF16, F32, BF16, BF16, BF16, BF16)
    return pl.pallas_call(
        functools.partial(_proj_kernel, tm=tm, per_b=per_b),
        grid=(t // tm,),
        in_specs=[
            pl.BlockSpec((tm, d), lambda i: (i, 0)),
            pl.BlockSpec((1, 1, d), lambda i: (i // per_b, 0, 0)),
            pl.BlockSpec((1, 1, d), lambda i: (i // per_b, 0, 1)),
            pl.BlockSpec((d, PROJ_W), lambda i: (0, 0)),
            pl.BlockSpec((GDN_CONV, QKV_W), lambda i: (0, 0)),
        ],
        out_specs=[pl.BlockSpec((tm, n), lambda i: (i, 0)) for n in widths],
        out_shape=[jax.ShapeDtypeStruct((t, n), dt) for n, dt in zip(widths, dtypes)],
        scratch_shapes=[pltpu.VMEM((8 + tm, QKV_W), F32)],
        compiler_params=_params("arbitrary"),
        name="in_proj",
    )(x, mod3, mod3, w, conv_w)


def _softplus(x):
    return jnp.maximum(x, 0.0) + jnp.log(1.0 + jnp.exp(-jnp.abs(x)))


def _unit_lower_inverse(lm, row, col):
    n = lm.shape[-1]
    eye = (row == col).astype(F32)
    t = None
    s = 1
    while s < n:
        join = ((row // (2 * s)) == (col // (2 * s))) & ((row % (2 * s)) >= s) & ((col % (2 * s)) < s)
        cm = jnp.where(join, lm, 0.0)
        if t is None:
            t = eye - cm
        else:
            p = _bdot(t, cm)
            t = t - _bdot(p, t)
        s *= 2
    return t


def _gdn_kernel(qkv_ref, ab_ref, z_ref, gpar_ref, nw_ref, o_ref, state_ref, *, tb):
    c_len = GDN_CHUNK
    nc = tb // c_len

    @pl.when(pl.program_id(1) == 0)
    def _():
        state_ref[...] = jnp.zeros_like(state_ref)

    ab = ab_ref[...]
    gpar = gpar_ref[...]
    g_all = gpar[0:1] * _softplus(ab + gpar[1:2])
    beta_all = _sigmoid(ab)
    pos = lax.broadcasted_iota(jnp.int32, (tb, AB_W), 0) % c_len
    gc_all = g_all
    s = 1
    while s < c_len:
        gc_all = gc_all + jnp.where(pos >= s, pltpu.roll(gc_all, s, axis=0), 0.0)
        s *= 2

    nh = GDN_HEADS
    nb = nc * nh
    row = lax.broadcasted_iota(jnp.int32, (nb, c_len, c_len), 1)
    col = lax.broadcasted_iota(jnp.int32, (nb, c_len, c_len), 2)
    causal = row >= col
    strict = row > col
    diag = row == col

    def chunked(per_head):
        return jnp.stack([per_head(h).reshape(nc, c_len, GDN_DK) for h in range(nh)], axis=1).reshape(nb, c_len, GDN_DK)

    q = chunked(lambda h: qkv_ref[:, h * GDN_DK:(h + 1) * GDN_DK].astype(F32))
    k = chunked(lambda h: qkv_ref[:, GDN_QK + h * GDN_DK:GDN_QK + (h + 1) * GDN_DK].astype(F32))
    v = chunked(lambda h: qkv_ref[:, 2 * GDN_QK + h * GDN_DV:2 * GDN_QK + (h + 1) * GDN_DV].astype(F32))
    gc = chunked(lambda h: jnp.broadcast_to(gc_all[:, h:h + 1], (tb, GDN_DK)))
    beta = chunked(lambda h: jnp.broadcast_to(beta_all[:, nh + h:nh + h + 1], (tb, GDN_DK)))
    q = q * lax.rsqrt(jnp.sum(q * q, -1, keepdims=True) + 1e-6) * (GDN_DK ** -0.5)
    k = k * lax.rsqrt(jnp.sum(k * k, -1, keepdims=True) + 1e-6)

    gc_i = gc[:, :, :c_len]
    gc_j = jnp.sum(jnp.where(diag, gc_i, 0.0), axis=1, keepdims=True)
    decay = jnp.where(causal, jnp.exp(jnp.where(causal, gc_i - gc_j, 0.0)), 0.0)
    eg = jnp.exp(gc)
    gc_last = gc[:, c_len - 1:c_len, :]
    g_last = jnp.exp(gc_last)
    k_beta = k * beta
    k16 = k.astype(BF16)
    lower = jnp.where(strict, _bdot_nt(k_beta.astype(BF16), k16) * decay, 0.0)
    attn = jnp.where(causal, _bdot_nt(q.astype(BF16), k16) * decay, 0.0).astype(BF16)
    tinv = _unit_lower_inverse(lower, row, col)
    tinv16 = tinv.astype(BF16)
    u = _bdot(tinv16, (v * beta).astype(BF16)).astype(BF16)
    w = _bdot(tinv16, (k_beta * eg).astype(BF16)).astype(BF16)
    q_g = q * eg
    k_g = (k * jnp.exp(gc_last - gc)).astype(BF16)

    kw = _bdot_tn(k_g, w).astype(BF16)
    ku = _bdot_tn(k_g, u)
    q_eff = (q_g - _bdot(attn, w)).astype(BF16)
    o_loc = _bdot(attn, u)

    st = state_ref[...]
    outs = []
    for c in range(nc):
        sl = slice(c * nh, (c + 1) * nh)
        st16 = st.astype(BF16)
        outs.append(_bdot(q_eff[sl], st16) + o_loc[sl])
        st = st * g_last[sl] - _bdot(kw[sl], st16) + ku[sl]
    state_ref[...] = st

    nw = nw_ref[...]
    for h in range(nh):
        o = jnp.concatenate([outs[c][h] for c in range(nc)], axis=0)
        zh = z_ref[:, h * GDN_DV:(h + 1) * GDN_DV].astype(F32)
        o = o * lax.rsqrt(jnp.mean(o * o, -1, keepdims=True) + RMS_EPS) * nw * (zh * _sigmoid(zh))
        o_ref[:, h * GDN_DV:(h + 1) * GDN_DV] = o.astype(BF16)


def _swa_kernel(sink_ref, q_ref, kv_ref, kvp_ref, bias_ref, o_ref, *, tq):
    nw = tq // WINDOW
    rows = GQA_GROUP * WINDOW
    first = pl.program_id(1) == 0
    kv = jnp.concatenate([kvp_ref[...], kv_ref[...]], axis=0)
    in_prev = lax.broadcasted_iota(jnp.int32, (nw, rows, 2 * WINDOW), 2) < WINDOW
    in_win0 = lax.broadcasted_iota(jnp.int32, (nw, rows, 2 * WINDOW), 0) == 0
    no_prev = in_prev & in_win0 & first
    scale = SWA_DH ** -0.5
    for hk in range(SWA_HKV):
        def windows(col0):
            return jnp.stack([kv[w * WINDOW:(w + 2) * WINDOW, col0:col0 + SWA_DH] for w in range(nw)])
        k = windows(hk * SWA_DH)
        v = windows(SWA_KV + hk * SWA_DH)
        q = jnp.concatenate(
            [q_ref[:, hq * SWA_DH:(hq + 1) * SWA_DH].reshape(nw, WINDOW, SWA_DH)
             for hq in range(hk * GQA_GROUP, (hk + 1) * GQA_GROUP)], axis=1)
        sink = jnp.concatenate(
            [jnp.full((WINDOW, 1), sink_ref[hq], F32) for hq in range(hk * GQA_GROUP, (hk + 1) * GQA_GROUP)], axis=0)
        s = _bdot_nt(q, k) * scale + bias_ref[hk]
        s = jnp.where(no_prev, NEG_BIG, s)
        m = jnp.maximum(jnp.max(s, -1, keepdims=True), sink)
        p = jnp.exp(s - m)
        v_ext = jnp.concatenate([v, jnp.ones((nw, 2 * WINDOW, SWA_DH), BF16)], axis=2)
        pv = _bdot(p.astype(BF16), v_ext)
        denom = pv[:, :, SWA_DH:SWA_DH + 1] + jnp.exp(sink - m)
        o = pv[:, :, :SWA_DH] / denom
        for g in range(GQA_GROUP):
            hq = hk * GQA_GROUP + g
            o_ref[:, hq * SWA_DH:(hq + 1) * SWA_DH] = o[:, g * WINDOW:(g + 1) * WINDOW, :].reshape(tq, SWA_DH).astype(BF16)


def _swa_bias():
    qi = jnp.arange(WINDOW)[:, None]
    ki = jnp.arange(2 * WINDOW)[None, :]
    dist = qi + WINDOW - ki
    valid = (dist >= 0) & (dist < WINDOW)
    slopes = 2.0 ** (-8.0 * jnp.arange(1, SWA_HQ + 1, dtype=F32) / SWA_HQ)
    bias = -slopes[:, None, None] * dist.astype(F32)[None]
    bias = jnp.where(valid[None], bias, NEG_BIG)
    return bias.reshape(SWA_HKV, GQA_GROUP * WINDOW, 2 * WINDOW)


def _deepnorm(x, y, gt, g, b):
    return _layer_norm(DEEPNORM_ALPHA * x + (1.0 + gt) * y) * g + b


def _merge_kernel(x_ref, ya_ref, yb_ref, uc_ref, ucp_ref, gate_ref, gt_ref, poolw_ref, pscale_ref,
                  wpa_ref, wpb_ref, wpc_ref, wo_ref, lng_ref, lnb_ref, o_ref, *, tm):
    j = pl.program_id(1)
    u = uc_ref[...].astype(F32)
    halo = jnp.where(j == 0, 0.0, ucp_ref[...].astype(F32))
    ue = jnp.concatenate([halo, u], axis=0)
    tpos = (j * tm + lax.broadcasted_iota(jnp.int32, (tm, POOL_GDIM), 0) + 1).astype(F32)
    ycs = []
    for gi, win in enumerate(POOL_WINDOWS):
        a = ue[:, gi * POOL_GDIM:(gi + 1) * POOL_GDIM]
        span = 1
        while span < win:
            a = a[span:] + a[:-span]
            span *= 2
        lo = MAX_POOL - win + 1
        d = a[lo:lo + tm] / jnp.minimum(tpos, float(win)) - u[:, gi * POOL_GDIM:(gi + 1) * POOL_GDIM]
        ycs.append(_dot(d.astype(BF16), poolw_ref[gi]))
    yc = jnp.concatenate(ycs, axis=1) * pscale_ref[...]
    merged = gate_ref[:, 0:D_MODEL].astype(F32) * _dot(ya_ref[...], wpa_ref[...])
    merged = merged + gate_ref[:, D_MODEL:2 * D_MODEL].astype(F32) * _dot(yb_ref[...], wpb_ref[...])
    merged = merged + gate_ref[:, 2 * D_MODEL:3 * D_MODEL].astype(F32) * _dot(yc.astype(BF16), wpc_ref[...])
    y = _dot(merged.astype(BF16), wo_ref[...])
    o_ref[...] = _deepnorm(x_ref[...], y, gt_ref[0], lng_ref[...], lnb_ref[...])


def _mixer_kernel(sink_ref, x_ref, qkv_ref, ab_ref, z_ref, gpar_ref, nw_ref, qb_ref, kvb_ref, kvp_ref, bias_ref,
                  uc_ref, ucp_ref, gate_ref, gt_ref, poolw_ref, pscale_ref, wpa_ref, wpb_ref, wpc_ref, wo_ref,
                  lng_ref, lnb_ref, o_ref, state_ref, ya_ref, yb_ref, *, tm):
    _gdn_kernel(qkv_ref, ab_ref, z_ref, gpar_ref, nw_ref, ya_ref, state_ref, tb=tm)
    _swa_kernel(sink_ref, qb_ref, kvb_ref, kvp_ref, bias_ref, yb_ref, tq=tm)
    _merge_kernel(x_ref, ya_ref, yb_ref, uc_ref, ucp_ref, gate_ref, gt_ref, poolw_ref, pscale_ref,
                  wpa_ref, wpb_ref, wpc_ref, wo_ref, lng_ref, lnb_ref, o_ref, tm=tm)


def _mixer(x, qkv, ab, z, qb, kvb, uc, gates, mod3, gpar, norm_w, sinks, pool_w, pool_scale, w_pa, w_pb, w_pc, w_o,
           ln_g, ln_b, batch, seq):
    t, d = x.shape
    tm = TOKEN_TILE
    per_b = seq // tm

    def tok(n):
        return pl.BlockSpec((tm, n), lambda b, j: (b * per_b + j, 0))

    def prev_rows(rows, n):
        return pl.BlockSpec((rows, n), lambda b, j: (jnp.maximum((b * per_b + j) * (tm // rows) - 1, 0), 0))

    def full(shape):
        return pl.BlockSpec(shape, lambda b, j: (0,) * len(shape))

    return pl.pallas_call(
        functools.partial(_mixer_kernel, tm=tm),
        grid=(batch, per_b),
        in_specs=[
            pl.BlockSpec(memory_space=pltpu.SMEM),
            tok(d), tok(QKV_W), tok(AB_W), tok(GDN_V), full((2, AB_W)), full((1, GDN_DV)),
            tok(SWA_Q), tok(2 * SWA_KV), prev_rows(WINDOW, 2 * SWA_KV),
            full((SWA_HKV, GQA_GROUP * WINDOW, 2 * WINDOW)),
            tok(POOL_DIM), prev_rows(MAX_POOL, POOL_DIM),
            tok(N_BRANCH * d),
            pl.BlockSpec((1, 1, d), lambda b, j: (b, 0, 2)),
            full((POOL_GROUPS, POOL_GDIM, POOL_GDIM)), full((1, POOL_DIM)),
            full((GDN_V, d)), full((SWA_Q, d)), full((POOL_DIM, d)), full((d, d)),
            full((1, d)), full((1, d)),
        ],
        out_specs=tok(d),
        out_shape=jax.ShapeDtypeStruct((t, d), F32),
        scratch_shapes=[pltpu.VMEM((GDN_HEADS, GDN_DK, GDN_DV), F32), pltpu.VMEM((tm, GDN_V), BF16),
                        pltpu.VMEM((tm, SWA_Q), BF16)],
        compiler_params=_params("parallel", "arbitrary"),
        name="mixer",
    )(sinks, x, qkv, ab, z, gpar, norm_w, qb, kvb, kvb, _swa_bias(), uc, uc, gates, mod3, pool_w, pool_scale,
      w_pa, w_pb, w_pc, w_o, ln_g, ln_b)


def _router_kernel(x_ref, sh_ref, sc_ref, rwt_ref, rb_ref, tri_ref, hp_ref, topi_ref, topg_ref, rank_ref, cnt_ref,
                   run_ref):
    @pl.when(pl.program_id(0) == 0)
    def _():
        run_ref[...] = jnp.zeros_like(run_ref)

    h = _layer_norm(x_ref[...]) * (1.0 + sc_ref[0]) + sh_ref[0]
    hp_ref[...] = _pack_rows(h)
    logits = lax.dot_general(rwt_ref[...], h, (((1,), (1,)), ((), ())), preferred_element_type=F32,
                             precision=lax.Precision.HIGHEST) + rb_ref[...]
    sub = lax.broadcasted_iota(jnp.int32, logits.shape, 0)
    vals, idxs = [], []
    for _ in range(TOP_K):
        m = jnp.max(logits, 0, keepdims=True)
        idx = jnp.min(jnp.where(logits == m, sub, N_EXPERTS), 0, keepdims=True)
        vals.append(m)
        idxs.append(idx)
        logits = jnp.where(sub == idx, -jnp.inf, logits)
    es = [jnp.exp(v - vals[0]) for v in vals]
    denom = es[0] + es[1] + es[2] + es[3]
    topi_ref[...] = jnp.concatenate(idxs, axis=0)
    topg_ref[...] = jnp.concatenate([e / denom for e in es], axis=0)

    sel = jnp.zeros(logits.shape, F32)
    for idx in idxs:
        sel = sel + (sub == idx).astype(F32)
    before = run_ref[:, 0:1] + _dot(sel.astype(BF16), tri_ref[...])
    ranks = [jnp.sum(jnp.where(sub == idx, before, 0.0), 0, keepdims=True) for idx in idxs]
    rank_ref[...] = jnp.concatenate(ranks, axis=0).astype(jnp.int32)
    run_ref[...] = run_ref[...] + jnp.sum(sel, 1, keepdims=True)
    cnt_ref[...] = run_ref[...].astype(jnp.int32)


def _router(x, mod3, router_wt, router_b, seq):
    t, d = x.shape
    tm = TOKEN_TILE
    per_b = seq // tm
    tri = (jnp.arange(tm)[:, None] < jnp.arange(tm)[None, :]).astype(BF16)
    return pl.pallas_call(
        _router_kernel,
        grid=(t // tm,),
        in_specs=[
            pl.BlockSpec((tm, d), lambda i: (i, 0)),
            pl.BlockSpec((1, 1, d), lambda i: (i // per_b, 0, 3)),
            pl.BlockSpec((1, 1, d), lambda i: (i // per_b, 0, 4)),
            pl.BlockSpec((N_EXPERTS, d), lambda i: (0, 0)),
            pl.BlockSpec((N_EXPERTS, 1), lambda i: (0, 0)),
            pl.BlockSpec((tm, tm), lambda i: (0, 0)),
        ],
        out_specs=[
            pl.BlockSpec((tm, PACKED_D), lambda i: (i, 0)),
            pl.BlockSpec((TOP_K, tm), lambda i: (0, i)),
            pl.BlockSpec((TOP_K, tm), lambda i: (0, i)),
            pl.BlockSpec((TOP_K, tm), lambda i: (0, i)),
            pl.BlockSpec((N_EXPERTS, 128), lambda i: (0, 0)),
        ],
        out_shape=[
            jax.ShapeDtypeStruct((t, PACKED_D), U32),
            jax.ShapeDtypeStruct((TOP_K, t), jnp.int32),
            jax.ShapeDtypeStruct((TOP_K, t), F32),
            jax.ShapeDtypeStruct((TOP_K, t), jnp.int32),
            jax.ShapeDtypeStruct((N_EXPERTS, 128), jnp.int32),
        ],
        scratch_shapes=[pltpu.VMEM((N_EXPERTS, 128), F32)],
        compiler_params=_params("arbitrary"),
        name="router",
    )(x, mod3, mod3, router_wt, router_b, tri)


def _dispatch_plan(topi_t, rank_t, counts, n_blocks):
    padded = (counts + ROUTE_BLOCK - 1) // ROUTE_BLOCK * ROUTE_BLOCK
    ends_p = jnp.cumsum(padded)
    pstart = ends_p - padded
    experts = jnp.arange(N_EXPERTS, dtype=jnp.int32)
    pstart_tok = jnp.sum(jnp.where(topi_t[:, :, None] == experts, pstart, 0), -1)
    dest_t = (pstart_tok + rank_t).astype(jnp.int32)
    blk_start = jnp.arange(n_blocks, dtype=jnp.int32) * ROUTE_BLOCK
    block_e = jnp.minimum(jnp.sum((blk_start[:, None] >= ends_p).astype(jnp.int32), -1), N_EXPERTS - 1)
    blk_end = jnp.sum(jnp.where(block_e[:, None] == experts, pstart + counts, 0), -1)
    n_valid = jnp.clip(blk_end - blk_start, 0, ROUTE_BLOCK).astype(jnp.int32)
    return dest_t, block_e.astype(jnp.int32), n_valid


def _sc_mesh():
    return plsc.VectorSubcoreMesh(core_axis_name="c", subcore_axis_name="s")


def _sc_dispatch(hp, dest_flat, rows):
    t, dp = hp.shape
    per_w = t // SC_WORKERS
    n_chunks = per_w // SC_CHUNK

    assert n_chunks % 2 == 0

    @functools.partial(
        pl.kernel, mesh=_sc_mesh(), out_type=jax.ShapeDtypeStruct((rows, dp), hp.dtype),
        scratch_types=[pltpu.VMEM((TOP_K * n_chunks, SC_CHUNK), jnp.int32),
                       pltpu.VMEM((SC_CHUNK, dp), hp.dtype), pltpu.VMEM((SC_CHUNK, dp), hp.dtype),
                       pltpu.SemaphoreType.DMA, pltpu.SemaphoreType.DMA,
                       pltpu.SemaphoreType.DMA, pltpu.SemaphoreType.DMA])
    def dispatch(h_hbm, d_hbm, xs_hbm, idx_v, rows_a, rows_b, load_a, load_b, scat_a, scat_b):
        worker = lax.axis_index("s") * SC_CORES + lax.axis_index("c")
        base = worker * per_w
        for k in range(TOP_K):
            pltpu.sync_copy(d_hbm.at[pl.ds((k * SC_WORKERS + worker) * n_chunks, n_chunks)],
                            idx_v.at[pl.ds(k * n_chunks, n_chunks)])

        def load(i, buf, sem):
            return pltpu.make_async_copy(h_hbm.at[pl.ds(base + i * SC_CHUNK, SC_CHUNK)], buf, sem)

        def scatter(i, k, buf, sem):
            return pltpu.make_async_copy(buf, xs_hbm.at[idx_v.at[k * n_chunks + i]], sem)

        def scatter_all(i, buf, sem):
            for k in range(TOP_K):
                scatter(i, k, buf, sem).start()
            for k in range(TOP_K):
                scatter(i, k, buf, sem).wait()

        load(0, rows_a, load_a).start()

        @pl.loop(0, n_chunks, step=2)
        def _(i):
            load(i, rows_a, load_a).wait()
            load(i + 1, rows_b, load_b).start()
            scatter_all(i, rows_a, scat_a)
            load(i + 1, rows_b, load_b).wait()

            @pl.when(i + 2 < n_chunks)
            def _():
                load(i + 2, rows_a, load_a).start()

            scatter_all(i + 1, rows_b, scat_b)

    return dispatch(hp, dest_flat.reshape(TOP_K * SC_WORKERS * n_chunks, SC_CHUNK))


def _sc_gather(table, idx):
    n = idx.shape[0]
    dp = table.shape[1]
    per_w = n // SC_WORKERS
    n_chunks = per_w // SC_CHUNK

    assert n_chunks % 2 == 0

    @functools.partial(
        pl.kernel, mesh=_sc_mesh(), out_type=jax.ShapeDtypeStruct((n, dp), table.dtype),
        scratch_types=[pltpu.VMEM((n_chunks, SC_CHUNK), jnp.int32),
                       pltpu.VMEM((SC_CHUNK, dp), table.dtype), pltpu.VMEM((SC_CHUNK, dp), table.dtype),
                       pltpu.SemaphoreType.DMA, pltpu.SemaphoreType.DMA,
                       pltpu.SemaphoreType.DMA, pltpu.SemaphoreType.DMA])
    def gather(t_hbm, i_hbm, o_hbm, idx_v, rows_a, rows_b, fetch_a, fetch_b, store_a, store_b):
        worker = lax.axis_index("s") * SC_CORES + lax.axis_index("c")
        base = worker * per_w
        pltpu.sync_copy(i_hbm.at[pl.ds(worker * n_chunks, n_chunks)], idx_v)

        def fetch(i, buf, sem):
            return pltpu.make_async_copy(t_hbm.at[idx_v.at[i]], buf, sem)

        def store(i, buf, sem):
            return pltpu.make_async_copy(buf, o_hbm.at[pl.ds(base + i * SC_CHUNK, SC_CHUNK)], sem)

        fetch(0, rows_a, fetch_a).start()

        @pl.loop(0, n_chunks, step=2)
        def _(i):
            fetch(i, rows_a, fetch_a).wait()

            @pl.when(i > 0)
            def _():
                store(i - 1, rows_b, store_b).wait()

            fetch(i + 1, rows_b, fetch_b).start()
            store(i, rows_a, store_a).start()
            fetch(i + 1, rows_b, fetch_b).wait()
            store(i, rows_a, store_a).wait()

            @pl.when(i + 2 < n_chunks)
            def _():
                fetch(i + 2, rows_a, fetch_a).start()

            store(i + 1, rows_b, store_b).start()

        store(n_chunks - 1, rows_b, store_b).wait()

    return gather(table, idx.reshape(SC_WORKERS * n_chunks, SC_CHUNK))


def _expert_kernel(be_ref, nv_ref, x_ref, w1_ref, b1_ref, w2_ref, b2_ref, o_ref, w1b_ref, w2b_ref):
    i = pl.program_id(0)
    n_valid = nv_ref[i]
    new_expert = (i == 0) | (be_ref[i] != be_ref[jnp.maximum(i - 1, 0)])

    @pl.when(new_expert & (n_valid > 0))
    def _():
        w1b_ref[...] = w1_ref[0].astype(BF16)
        w2b_ref[...] = w2_ref[0].astype(BF16)

    @pl.when(n_valid > 0)
    def _():
        live = lax.broadcasted_iota(jnp.int32, (ROUTE_BLOCK, D_MODEL), 0) < n_valid
        x = jnp.where(live, _unpack_rows(x_ref[...]), 0.0).astype(BF16)
        gu = _dot(x, w1b_ref[...]) + b1_ref[0]
        glu = jnp.minimum(gu[:, :D_FF], SWIGLU_LIMIT)
        lin = jnp.clip(gu[:, D_FF:], -SWIGLU_LIMIT, SWIGLU_LIMIT)
        act = glu * _sigmoid(SWIGLU_ALPHA * glu) * (lin + 1.0)
        o_ref[...] = _pack_rows(_dot(act.astype(BF16), w2b_ref[...]) + b2_ref[0])

    @pl.when(n_valid == 0)
    def _():
        o_ref[...] = jnp.zeros_like(o_ref)


def _experts(xs, block_e, n_valid, w1, b1, w2, b2, layer):
    rows, dp = xs.shape
    d = D_MODEL
    n_blocks = rows // ROUTE_BLOCK
    e0 = layer * N_EXPERTS
    grid_spec = pltpu.PrefetchScalarGridSpec(
        num_scalar_prefetch=2,
        grid=(n_blocks,),
        in_specs=[
            pl.BlockSpec((ROUTE_BLOCK, dp), lambda i, be, nv: (i, 0)),
            pl.BlockSpec((1, d, 2 * D_FF), lambda i, be, nv: (e0 + be[i], 0, 0)),
            pl.BlockSpec((1, 1, 2 * D_FF), lambda i, be, nv: (e0 + be[i], 0, 0)),
            pl.BlockSpec((1, D_FF, d), lambda i, be, nv: (e0 + be[i], 0, 0)),
            pl.BlockSpec((1, 1, d), lambda i, be, nv: (e0 + be[i], 0, 0)),
        ],
        out_specs=pl.BlockSpec((ROUTE_BLOCK, dp), lambda i, be, nv: (i, 0)),
        scratch_shapes=[pltpu.VMEM((d, 2 * D_FF), BF16), pltpu.VMEM((D_FF, d), BF16)],
    )
    return pl.pallas_call(
        _expert_kernel,
        grid_spec=grid_spec,
        out_shape=jax.ShapeDtypeStruct((rows, dp), U32),
        compiler_params=_params("arbitrary"),
        name="experts",
    )(block_e, n_valid, xs, w1, b1, w2, b2)


def _combine_kernel(x_ref, yg_ref, topg_ref, gt_ref, lng_ref, lnb_ref, o_ref):
    g = topg_ref[...]
    y = g[:, 0:1] * _unpack_rows(yg_ref[0])
    for k in range(1, TOP_K):
        y = y + g[:, k:k + 1] * _unpack_rows(yg_ref[k])
    o_ref[...] = _deepnorm(x_ref[...], y, gt_ref[0], lng_ref[...], lnb_ref[...])


def _combine(x, yg, topg, mod3, ln_g, ln_b, seq):
    t, d = x.shape
    tm = TOKEN_TILE
    per_b = seq // tm
    return pl.pallas_call(
        _combine_kernel,
        grid=(t // tm,),
        in_specs=[
            pl.BlockSpec((tm, d), lambda i: (i, 0)),
            pl.BlockSpec((TOP_K, tm, PACKED_D), lambda i: (0, i, 0)),
            pl.BlockSpec((tm, TOP_K), lambda i: (i, 0)),
            pl.BlockSpec((1, 1, d), lambda i: (i // per_b, 0, 5)),
            pl.BlockSpec((1, d), lambda i: (0, 0)),
            pl.BlockSpec((1, d), lambda i: (0, 0)),
        ],
        out_specs=pl.BlockSpec((tm, d), lambda i: (i, 0)),
        out_shape=jax.ShapeDtypeStruct((t, d), F32),
        compiler_params=_params("parallel"),
        name="combine",
    )(x, yg, topg, mod3, ln_g, ln_b)


def _rearranged_w_in(w_in):
    d = w_in.shape[0]
    o = 0
    cols = {}
    for name, n in (("qa", GDN_QK), ("ka", GDN_QK), ("va", GDN_V), ("z", GDN_V), ("a", GDN_HEADS), ("b", GDN_HEADS),
                    ("qb", SWA_Q), ("kb", SWA_KV), ("vb", SWA_KV), ("uc", POOL_DIM), ("gate", N_BRANCH * D_MODEL)):
        cols[name] = w_in[:, o:o + n]
        o += n
    ab_pad = jnp.zeros((d, AB_W - 2 * GDN_HEADS), w_in.dtype)
    return jnp.concatenate([cols["qa"], cols["ka"], cols["va"], cols["z"], cols["a"], cols["b"], ab_pad,
                            cols["qb"], cols["kb"], cols["vb"], cols["uc"], cols["gate"]], axis=1).astype(BF16)


def kernel(x, c, ada_w, ada_b, w_in, conv_w, a_log, dt_bias, gdn_norm_w, sinks, pool_w, pool_scale, w_pa, w_pb, w_pc, w_o, ln1_g, ln1_b, ln2_g, ln2_b, router_w, router_b, exp_w1, exp_b1, exp_w2, exp_b2):
    batch, seq, d = x.shape
    t = batch * seq
    n_blocks = (t * TOP_K + ROUTE_BLOCK - 1) // ROUTE_BLOCK + N_EXPERTS
    mod = _ada_mod(c, ada_w, ada_b)
    xt = x.reshape(t, d)
    lane_pad = jnp.zeros((AB_W - GDN_HEADS,), F32)
    rows = n_blocks * ROUTE_BLOCK
    w1_all = exp_w1.reshape(DEPTH * N_EXPERTS, d, 2 * D_FF)
    b1_all = exp_b1.reshape(DEPTH * N_EXPERTS, 1, 2 * D_FF)
    w2_all = exp_w2.reshape(DEPTH * N_EXPERTS, D_FF, d)
    b2_all = exp_b2.reshape(DEPTH * N_EXPERTS, 1, d)
    for l in range(DEPTH):
        mod3 = mod[l].reshape(batch, 1, 6 * d)
        qkv, z, ab, qb, kvb, uc, gates = _in_proj(xt, mod3, _rearranged_w_in(w_in[l]), conv_w[l], seq)
        gpar = jnp.stack([jnp.concatenate([-jnp.exp(a_log[l]), lane_pad]), jnp.concatenate([dt_bias[l], lane_pad])])
        xt = _mixer(xt, qkv, ab, z, qb, kvb, uc, gates, mod3, gpar, gdn_norm_w[l].reshape(1, GDN_DV), sinks[l],
                    pool_w[l].astype(BF16), pool_scale[l].reshape(1, POOL_DIM),
                    w_pa[l].astype(BF16), w_pb[l].astype(BF16), w_pc[l].astype(BF16), w_o[l].astype(BF16),
                    ln1_g[l].reshape(1, d), ln1_b[l].reshape(1, d), batch, seq)
        hp, topi_t, topg_t, rank_t, cnt = _router(xt, mod3, router_w[l].T, router_b[l].reshape(N_EXPERTS, 1), seq)
        dest_t, block_e, n_valid = _dispatch_plan(topi_t, rank_t, cnt[:, 0], n_blocks)
        dest_flat = dest_t.reshape(TOP_K * t)
        xs = _sc_dispatch(hp, dest_flat, rows)
        ys = _experts(xs, block_e, n_valid, w1_all, b1_all, w2_all, b2_all, l)
        yg = _sc_gather(ys, dest_flat).reshape(TOP_K, t, PACKED_D)
        xt = _combine(xt, yg, topg_t.T, mod3, ln2_g[l].reshape(1, d), ln2_b[l].reshape(1, d), seq)
    return xt.reshape(batch, seq, d)
```

```python
import functools

import jax
import jax.numpy as jnp
from jax import lax
from jax.experimental import pallas as pl
from jax.experimental.pallas import tpu as pltpu
from jax.experimental.pallas import tpu_sc as plsc

D_MODEL = 1024
DEPTH = 4
GDN_HEADS = 4
GDN_DK = 128
GDN_DV = 128
GDN_CONV = 4
GDN_CHUNK = 64
SWA_HQ = 8
SWA_HKV = 2
SWA_DH = 64
WINDOW = 128
POOL_WINDOWS = (2, 4, 8, 16)
POOL_GROUPS = 4
POOL_GDIM = 128
N_BRANCH = 3
N_EXPERTS = 32
TOP_K = 4
D_FF = 1024
ROUTE_BLOCK = 512
SWIGLU_ALPHA = 1.702
SWIGLU_LIMIT = 7.0
LN_EPS = 1e-5
RMS_EPS = 1e-6
DEEPNORM_ALPHA = (2 * DEPTH) ** 0.25

GDN_QK = GDN_HEADS * GDN_DK
GDN_V = GDN_HEADS * GDN_DV
SWA_Q = SWA_HQ * SWA_DH
SWA_KV = SWA_HKV * SWA_DH
POOL_DIM = POOL_GROUPS * POOL_GDIM
GQA_GROUP = SWA_HQ // SWA_HKV
MAX_POOL = max(POOL_WINDOWS)

QKV_W = 2 * GDN_QK + GDN_V
AB_W = 128
SEG_QKV = 0
SEG_Z = SEG_QKV + QKV_W
SEG_AB = SEG_Z + GDN_V
SEG_QB = SEG_AB + AB_W
SEG_KVB = SEG_QB + SWA_Q
SEG_UC = SEG_KVB + 2 * SWA_KV
SEG_GATE = SEG_UC + POOL_DIM
PROJ_W = SEG_GATE + N_BRANCH * D_MODEL

PROJ_CHUNK = 256
TOKEN_TILE = 512
NEG_BIG = -1e30
VMEM_LIMIT = 56 * 1024 * 1024

PACKED_D = D_MODEL // 2
SC_CORES = 2
SC_SUBCORES = 16
SC_WORKERS = SC_CORES * SC_SUBCORES
SC_CHUNK = 64

F32 = jnp.float32
BF16 = jnp.bfloat16
U32 = jnp.uint32
HI_MASK = 0xFFFF0000


def _pack_rows(x):
    bits = pltpu.bitcast(x.astype(BF16).astype(F32), U32)
    return (bits[:, :PACKED_D] >> 16) | (bits[:, PACKED_D:] & jnp.uint32(HI_MASK))


def _unpack_rows(w):
    lo = pltpu.bitcast(w << 16, F32)
    hi = pltpu.bitcast(w & jnp.uint32(HI_MASK), F32)
    return jnp.concatenate([lo, hi], axis=1)


def _params(*sem):
    return pltpu.CompilerParams(dimension_semantics=sem, vmem_limit_bytes=VMEM_LIMIT)


def _sigmoid(x):
    return 0.5 * jnp.tanh(0.5 * x) + 0.5


def _layer_norm(x):
    mu = jnp.mean(x, -1, keepdims=True)
    xc = x - mu
    var = jnp.mean(xc * xc, -1, keepdims=True)
    return xc * lax.rsqrt(var + LN_EPS)


def _dot(a, b):
    return jnp.dot(a, b, preferred_element_type=F32)


def _dot_nt(a, b):
    return lax.dot_general(a, b, (((1,), (1,)), ((), ())), preferred_element_type=F32)


def _dot_tn(a, b):
    return lax.dot_general(a, b, (((0,), (0,)), ((), ())), preferred_element_type=F32)


def _bdot(a, b):
    return lax.dot_general(a, b, (((2,), (1,)), ((0,), (0,))), preferred_element_type=F32)


def _bdot_tn(a, b):
    return lax.dot_general(a, b, (((1,), (1,)), ((0,), (0,))), preferred_element_type=F32)


def _bdot_nt(a, b):
    return lax.dot_general(a, b, (((2,), (2,)), ((0,), (0,))), preferred_element_type=F32)


def _ada_kernel(c_ref, w_ref, b_ref, o_ref):
    c = c_ref[...]
    cond = c * _sigmoid(c)
    o_ref[0] = _dot(cond.astype(BF16), w_ref[0].astype(BF16)) + b_ref[0]


def _ada_mod(c, ada_w, ada_b):
    depth, d, n = ada_w.shape
    b = c.shape[0]
    tn = 1024
    return pl.pallas_call(
        _ada_kernel,
        grid=(depth, n // tn),
        in_specs=[
            pl.BlockSpec((b, d), lambda l, j: (0, 0)),
            pl.BlockSpec((1, d, tn), lambda l, j: (l, 0, j)),
            pl.BlockSpec((1, 1, tn), lambda l, j: (l, 0, j)),
        ],
        out_specs=pl.BlockSpec((1, b, tn), lambda l, j: (l, 0, j)),
        out_shape=jax.ShapeDtypeStruct((depth, b, n), F32),
        compiler_params=_params("parallel", "parallel"),
        name="ada_mod",
    )(c, ada_w, ada_b.reshape(depth, 1, n))


def _proj_kernel(x_ref, sh_ref, sc_ref, w_ref, convw_ref, qkv_ref, z_ref, ab_ref, qb_ref, kvb_ref, uc_ref, gate_ref,
                 raw_ref, *, tm, per_b):
    h = (_layer_norm(x_ref[...]) * (1.0 + sc_ref[0]) + sh_ref[0]).astype(BF16)

    def seg(start, width):
        return _dot(h, w_ref[:, start:start + width])

    @pl.when(pl.program_id(0) % per_b == 0)
    def _():
        raw_ref[0:8, :] = jnp.zeros((8, QKV_W), F32)

    @pl.when(pl.program_id(0) % per_b != 0)
    def _():
        raw_ref[0:8, :] = raw_ref[tm:tm + 8, :]

    for j in range(QKV_W // PROJ_CHUNK):
        cols = slice(j * PROJ_CHUNK, (j + 1) * PROJ_CHUNK)
        raw_ref[8:8 + tm, cols] = seg(SEG_QKV + j * PROJ_CHUNK, PROJ_CHUNK)

    def conv_silu(j):
        cols = slice(j * PROJ_CHUNK, (j + 1) * PROJ_CHUNK)
        conv = raw_ref[8:8 + tm, cols] * convw_ref[GDN_CONV - 1:GDN_CONV, cols]
        for tap in range(GDN_CONV - 1):
            conv = conv + raw_ref[5 + tap:5 + tap + tm, cols] * convw_ref[tap:tap + 1, cols]
        qkv_ref[:, cols] = (conv * _sigmoid(conv)).astype(BF16)
    n_conv = QKV_W // PROJ_CHUNK
    done = 0
    for ref, start, width in ((z_ref, SEG_Z, GDN_V), (qb_ref, SEG_QB, SWA_Q), (kvb_ref, SEG_KVB, 2 * SWA_KV),
                              (uc_ref, SEG_UC, POOL_DIM)):
        for j in range(width // PROJ_CHUNK):
            cols = slice(j * PROJ_CHUNK, (j + 1) * PROJ_CHUNK)
            ref[:, cols] = seg(start + j * PROJ_CHUNK, PROJ_CHUNK).astype(BF16)
            if done < n_conv:
                conv_silu(done)
                done += 1
    assert done == n_conv
    ab_ref[...] = seg(SEG_AB, AB_W)
    gate_ref[...] = _sigmoid(seg(SEG_GATE, N_BRANCH * D_MODEL)).astype(BF16)


def _in_proj(x, mod3, w, conv_w, seq, layer):
    t, d = x.shape
    tm = TOKEN_TILE
    per_b = seq // tm
    widths = (QKV_W, GDN_V, AB_W, SWA_Q, 2 * SWA_KV, POOL_DIM, N_BRANCH * D_MODEL)
    dtypes = (BF16, BF16, F32, BF16, BF16, BF16, BF16)
    return pl.pallas_call(
        functools.partial(_proj_kernel, tm=tm, per_b=per_b),
        grid=(t // tm,),
        in_specs=[
            pl.BlockSpec((tm, d), lambda i: (i, 0)),
            pl.BlockSpec((1, 1, d), lambda i: (i // per_b, 0, 0)),
            pl.BlockSpec((1, 1, d), lambda i: (i // per_b, 0, 1)),
            pl.BlockSpec((None, d, PROJ_W), lambda i: (layer, 0, 0)),
            pl.BlockSpec((None, GDN_CONV, QKV_W), lambda i: (layer, 0, 0)),
        ],
        out_specs=[pl.BlockSpec((tm, n), lambda i: (i, 0)) for n in widths],
        out_shape=[jax.ShapeDtypeStruct((t, n), dt) for n, dt in zip(widths, dtypes)],
        scratch_shapes=[pltpu.VMEM((8 + tm, QKV_W), F32)],
        compiler_params=_params("arbitrary"),
        name="in_proj",
    )(x, mod3, mod3, w, conv_w)


def _softplus(x):
    return jnp.maximum(x, 0.0) + jnp.log(1.0 + jnp.exp(-jnp.abs(x)))


def _unit_lower_inverse(lm, row, col):
    n = lm.shape[-1]
    eye = (row == col).astype(F32)
    t = None
    s = 1
    while s < n:
        join = ((row // (2 * s)) == (col // (2 * s))) & ((row % (2 * s)) >= s) & ((col % (2 * s)) < s)
        cm = jnp.where(join, lm, 0.0)
        if t is None:
            t = eye - cm
        else:
            p = _bdot(t, cm)
            t = t - _bdot(p, t)
        s *= 2
    return t


def _gdn_kernel(qkv_ref, ab_ref, z_ref, gpar_ref, nw_ref, o_ref, state_ref, *, tb):
    c_len = GDN_CHUNK
    nc = tb // c_len

    @pl.when(pl.program_id(1) == 0)
    def _():
        state_ref[...] = jnp.zeros_like(state_ref)

    ab = ab_ref[...]
    gpar = gpar_ref[...]
    g_all = gpar[0:1] * _softplus(ab + gpar[1:2])
    beta_all = _sigmoid(ab)
    pos = lax.broadcasted_iota(jnp.int32, (tb, AB_W), 0) % c_len
    gc_all = g_all
    s = 1
    while s < c_len:
        gc_all = gc_all + jnp.where(pos >= s, pltpu.roll(gc_all, s, axis=0), 0.0)
        s *= 2

    nh = GDN_HEADS
    nb = nc * nh
    row = lax.broadcasted_iota(jnp.int32, (nb, c_len, c_len), 1)
    col = lax.broadcasted_iota(jnp.int32, (nb, c_len, c_len), 2)
    causal = row >= col
    strict = row > col
    diag = row == col

    def chunked(per_head):
        return jnp.stack([per_head(h).reshape(nc, c_len, GDN_DK) for h in range(nh)], axis=1).reshape(nb, c_len, GDN_DK)

    q = chunked(lambda h: qkv_ref[:, h * GDN_DK:(h + 1) * GDN_DK].astype(F32))
    k = chunked(lambda h: qkv_ref[:, GDN_QK + h * GDN_DK:GDN_QK + (h + 1) * GDN_DK].astype(F32))
    v = chunked(lambda h: qkv_ref[:, 2 * GDN_QK + h * GDN_DV:2 * GDN_QK + (h + 1) * GDN_DV].astype(F32))
    gc = chunked(lambda h: jnp.broadcast_to(gc_all[:, h:h + 1], (tb, GDN_DK)))
    beta = chunked(lambda h: jnp.broadcast_to(beta_all[:, nh + h:nh + h + 1], (tb, GDN_DK)))
    q = q * lax.rsqrt(jnp.sum(q * q, -1, keepdims=True) + 1e-6) * (GDN_DK ** -0.5)
    k = k * lax.rsqrt(jnp.sum(k * k, -1, keepdims=True) + 1e-6)

    gc_i = gc[:, :, :c_len]
    gc_j = jnp.sum(jnp.where(diag, gc_i, 0.0), axis=1, keepdims=True)
    decay = jnp.where(causal, jnp.exp(jnp.where(causal, gc_i - gc_j, 0.0)), 0.0)
    eg = jnp.exp(gc)
    gc_last = gc[:, c_len - 1:c_len, :]
    g_last = jnp.exp(gc_last)
    k_beta = k * beta
    kq = _bdot_nt(jnp.concatenate([k_beta, q], axis=1).astype(BF16), k.astype(BF16))
    lower = jnp.where(strict, kq[:, :c_len] * decay, 0.0)
    attn = jnp.where(causal, kq[:, c_len:] * decay, 0.0).astype(BF16)
    tinv = _unit_lower_inverse(lower, row, col)
    uw = _bdot(tinv.astype(BF16), jnp.concatenate([v * beta, k_beta * eg], axis=2).astype(BF16)).astype(BF16)
    q_g = q * eg
    k_g = (k * jnp.exp(gc_last - gc)).astype(BF16)

    k_uw = _bdot_tn(k_g, uw)
    a_uw = _bdot(attn, uw)
    ku = k_uw[:, :, :GDN_DV]
    o_loc = a_uw[:, :, :GDN_DV]
    qk_eff = jnp.concatenate([q_g - a_uw[:, :, GDN_DV:], k_uw[:, :, GDN_DV:]], axis=1).astype(BF16)

    st = state_ref[...]
    outs = []
    for c in range(nc):
        sl = slice(c * nh, (c + 1) * nh)
        prod = _bdot(qk_eff[sl], st.astype(BF16))
        outs.append(prod[:, :c_len] + o_loc[sl])
        st = st * g_last[sl] - prod[:, c_len:] + ku[sl]
    state_ref[...] = st

    nw = nw_ref[...]
    for h in range(nh):
        o = jnp.concatenate([outs[c][h] for c in range(nc)], axis=0)
        zh = z_ref[:, h * GDN_DV:(h + 1) * GDN_DV].astype(F32)
        o = o * lax.rsqrt(jnp.mean(o * o, -1, keepdims=True) + RMS_EPS) * nw * (zh * _sigmoid(zh))
        o_ref[:, h * GDN_DV:(h + 1) * GDN_DV] = o.astype(BF16)


def _swa_kernel(sink_ref, q_ref, kv_ref, kvp_ref, bias_ref, o_ref, *, tq):
    nw = tq // WINDOW
    rows = GQA_GROUP * WINDOW
    first = pl.program_id(1) == 0
    kv = jnp.concatenate([kvp_ref[...], kv_ref[...]], axis=0)
    in_prev = lax.broadcasted_iota(jnp.int32, (nw, rows, 2 * WINDOW), 2) < WINDOW
    in_win0 = lax.broadcasted_iota(jnp.int32, (nw, rows, 2 * WINDOW), 0) == 0
    no_prev = in_prev & in_win0 & first
    scale = SWA_DH ** -0.5
    for hk in range(SWA_HKV):
        def windows(col0):
            return jnp.stack([kv[w * WINDOW:(w + 2) * WINDOW, col0:col0 + SWA_DH] for w in range(nw)])
        k = windows(hk * SWA_DH)
        v = windows(SWA_KV + hk * SWA_DH)
        q = jnp.concatenate(
            [q_ref[:, hq * SWA_DH:(hq + 1) * SWA_DH].reshape(nw, WINDOW, SWA_DH)
             for hq in range(hk * GQA_GROUP, (hk + 1) * GQA_GROUP)], axis=1)
        sink = jnp.concatenate(
            [jnp.full((WINDOW, 1), sink_ref[hq], F32) for hq in range(hk * GQA_GROUP, (hk + 1) * GQA_GROUP)], axis=0)
        s = _bdot_nt(q, k) * scale + bias_ref[hk]
        s = jnp.where(no_prev, NEG_BIG, s)
        m = jnp.maximum(jnp.max(s, -1, keepdims=True), sink)
        p = jnp.exp(s - m)
        v_ext = jnp.concatenate([v, jnp.ones((nw, 2 * WINDOW, SWA_DH), BF16)], axis=2)
        pv = _bdot(p.astype(BF16), v_ext)
        denom = pv[:, :, SWA_DH:SWA_DH + 1] + jnp.exp(sink - m)
        o = pv[:, :, :SWA_DH] / denom
        for g in range(GQA_GROUP):
            hq = hk * GQA_GROUP + g
            o_ref[:, hq * SWA_DH:(hq + 1) * SWA_DH] = o[:, g * WINDOW:(g + 1) * WINDOW, :].reshape(tq, SWA_DH).astype(BF16)


def _swa_bias():
    qi = jnp.arange(WINDOW)[:, None]
    ki = jnp.arange(2 * WINDOW)[None, :]
    dist = qi + WINDOW - ki
    valid = (dist >= 0) & (dist < WINDOW)
    slopes = 2.0 ** (-8.0 * jnp.arange(1, SWA_HQ + 1, dtype=F32) / SWA_HQ)
    bias = -slopes[:, None, None] * dist.astype(F32)[None]
    bias = jnp.where(valid[None], bias, NEG_BIG)
    return bias.reshape(SWA_HKV, GQA_GROUP * WINDOW, 2 * WINDOW)


def _deepnorm(x, y, gt, g, b):
    return _layer_norm(DEEPNORM_ALPHA * x + (1.0 + gt) * y) * g + b


def _merge_kernel(x_ref, ya_ref, yb_ref, uc_ref, ucp_ref, gate_ref, gt_ref, poolw_ref, pscale_ref,
                  wpa_ref, wpb_ref, wpc_ref, wo_ref, lng_ref, lnb_ref, o_ref, *, tm):
    j = pl.program_id(1)
    u = uc_ref[...].astype(F32)
    halo = jnp.where(j == 0, 0.0, ucp_ref[...].astype(F32))
    ue = jnp.concatenate([halo, u], axis=0)
    tpos = (j * tm + lax.broadcasted_iota(jnp.int32, (tm, POOL_GDIM), 0) + 1).astype(F32)
    ycs = []
    for gi, win in enumerate(POOL_WINDOWS):
        a = ue[:, gi * POOL_GDIM:(gi + 1) * POOL_GDIM]
        span = 1
        while span < win:
            a = a[span:] + a[:-span]
            span *= 2
        lo = MAX_POOL - win + 1
        d = a[lo:lo + tm] / jnp.minimum(tpos, float(win)) - u[:, gi * POOL_GDIM:(gi + 1) * POOL_GDIM]
        ycs.append(_dot(d.astype(BF16), poolw_ref[gi]))
    yc = jnp.concatenate(ycs, axis=1) * pscale_ref[...]
    merged = gate_ref[:, 0:D_MODEL].astype(F32) * _dot(ya_ref[...], wpa_ref[...])
    merged = merged + gate_ref[:, D_MODEL:2 * D_MODEL].astype(F32) * _dot(yb_ref[...], wpb_ref[...])
    merged = merged + gate_ref[:, 2 * D_MODEL:3 * D_MODEL].astype(F32) * _dot(yc.astype(BF16), wpc_ref[...])
    y = _dot(merged.astype(BF16), wo_ref[...])
    o_ref[...] = _deepnorm(x_ref[...], y, gt_ref[0], lng_ref[...], lnb_ref[...])


def _mixer_kernel(sink_ref, x_ref, qkv_ref, ab_ref, z_ref, gpar_ref, nw_ref, qb_ref, kvb_ref, kvp_ref, bias_ref,
                  uc_ref, ucp_ref, gate_ref, gt_ref, poolw_ref, pscale_ref, wpa_ref, wpb_ref, wpc_ref, wo_ref,
                  lng_ref, lnb_ref, o_ref, state_ref, ya_ref, yb_ref, *, tm):
    _gdn_kernel(qkv_ref, ab_ref, z_ref, gpar_ref, nw_ref, ya_ref, state_ref, tb=tm)
    _swa_kernel(sink_ref, qb_ref, kvb_ref, kvp_ref, bias_ref, yb_ref, tq=tm)
    _merge_kernel(x_ref, ya_ref, yb_ref, uc_ref, ucp_ref, gate_ref, gt_ref, poolw_ref, pscale_ref,
                  wpa_ref, wpb_ref, wpc_ref, wo_ref, lng_ref, lnb_ref, o_ref, tm=tm)


def _mixer(x, qkv, ab, z, qb, kvb, uc, gates, mod3, gpar, norm_w, sinks, pool_w, pool_scale, w_pa, w_pb, w_pc, w_o,
           ln_g, ln_b, batch, seq):
    t, d = x.shape
    tm = TOKEN_TILE
    per_b = seq // tm

    def tok(n):
        return pl.BlockSpec((tm, n), lambda b, j: (b * per_b + j, 0))

    def prev_rows(rows, n):
        return pl.BlockSpec((rows, n), lambda b, j: (jnp.maximum((b * per_b + j) * (tm // rows) - 1, 0), 0))

    def full(shape):
        return pl.BlockSpec(shape, lambda b, j: (0,) * len(shape))

    return pl.pallas_call(
        functools.partial(_mixer_kernel, tm=tm),
        grid=(batch, per_b),
        in_specs=[
            pl.BlockSpec(memory_space=pltpu.SMEM),
            tok(d), tok(QKV_W), tok(AB_W), tok(GDN_V), full((2, AB_W)), full((1, GDN_DV)),
            tok(SWA_Q), tok(2 * SWA_KV), prev_rows(WINDOW, 2 * SWA_KV),
            full((SWA_HKV, GQA_GROUP * WINDOW, 2 * WINDOW)),
            tok(POOL_DIM), prev_rows(MAX_POOL, POOL_DIM),
            tok(N_BRANCH * d),
            pl.BlockSpec((1, 1, d), lambda b, j: (b, 0, 2)),
            full((POOL_GROUPS, POOL_GDIM, POOL_GDIM)), full((1, POOL_DIM)),
            full((GDN_V, d)), full((SWA_Q, d)), full((POOL_DIM, d)), full((d, d)),
            full((1, d)), full((1, d)),
        ],
        out_specs=tok(d),
        out_shape=jax.ShapeDtypeStruct((t, d), F32),
        scratch_shapes=[pltpu.VMEM((GDN_HEADS, GDN_DK, GDN_DV), F32), pltpu.VMEM((tm, GDN_V), BF16),
                        pltpu.VMEM((tm, SWA_Q), BF16)],
        compiler_params=_params("parallel", "arbitrary"),
        name="mixer",
    )(sinks, x, qkv, ab, z, gpar, norm_w, qb, kvb, kvb, _swa_bias(), uc, uc, gates, mod3, pool_w, pool_scale,
      w_pa, w_pb, w_pc, w_o, ln_g, ln_b)


def _router_kernel(x_ref, sh_ref, sc_ref, rwt_ref, rb_ref, tri_ref, hp_ref, topi_ref, topg_ref, rank_ref, cnt_ref,
                   run_ref):
    @pl.when(pl.program_id(0) == 0)
    def _():
        run_ref[...] = jnp.zeros_like(run_ref)

    h = _layer_norm(x_ref[...]) * (1.0 + sc_ref[0]) + sh_ref[0]
    hp_ref[...] = _pack_rows(h)
    logits = lax.dot_general(rwt_ref[...], h, (((1,), (1,)), ((), ())), preferred_element_type=F32,
                             precision=lax.Precision.HIGHEST) + rb_ref[...]
    sub = lax.broadcasted_iota(jnp.int32, logits.shape, 0)
    vals, idxs = [], []
    for _ in range(TOP_K):
        m = jnp.max(logits, 0, keepdims=True)
        idx = jnp.min(jnp.where(logits == m, sub, N_EXPERTS), 0, keepdims=True)
        vals.append(m)
        idxs.append(idx)
        logits = jnp.where(sub == idx, -jnp.inf, logits)
    es = [jnp.exp(v - vals[0]) for v in vals]
    denom = es[0] + es[1] + es[2] + es[3]
    topi_ref[...] = jnp.concatenate(idxs, axis=0)
    topg_ref[...] = jnp.concatenate([e / denom for e in es], axis=0)

    sel = jnp.zeros(logits.shape, F32)
    for idx in idxs:
        sel = sel + (sub == idx).astype(F32)
    before = run_ref[:, 0:1] + _dot(sel.astype(BF16), tri_ref[...])
    ranks = [jnp.sum(jnp.where(sub == idx, before, 0.0), 0, keepdims=True) for idx in idxs]
    rank_ref[...] = jnp.concatenate(ranks, axis=0).astype(jnp.int32)
    run_ref[...] = run_ref[...] + jnp.sum(sel, 1, keepdims=True)
    cnt_ref[...] = run_ref[...].astype(jnp.int32)


def _router(x, mod3, router_wt, router_b, seq):
    t, d = x.shape
    tm = TOKEN_TILE
    per_b = seq // tm
    tri = (jnp.arange(tm)[:, None] < jnp.arange(tm)[None, :]).astype(BF16)
    return pl.pallas_call(
        _router_kernel,
        grid=(t // tm,),
        in_specs=[
            pl.BlockSpec((tm, d), lambda i: (i, 0)),
            pl.BlockSpec((1, 1, d), lambda i: (i // per_b, 0, 3)),
            pl.BlockSpec((1, 1, d), lambda i: (i // per_b, 0, 4)),
            pl.BlockSpec((N_EXPERTS, d), lambda i: (0, 0)),
            pl.BlockSpec((N_EXPERTS, 1), lambda i: (0, 0)),
            pl.BlockSpec((tm, tm), lambda i: (0, 0)),
        ],
        out_specs=[
            pl.BlockSpec((tm, PACKED_D), lambda i: (i, 0)),
            pl.BlockSpec((TOP_K, tm), lambda i: (0, i)),
            pl.BlockSpec((TOP_K, tm), lambda i: (0, i)),
            pl.BlockSpec((TOP_K, tm), lambda i: (0, i)),
            pl.BlockSpec((N_EXPERTS, 128), lambda i: (0, 0)),
        ],
        out_shape=[
            jax.ShapeDtypeStruct((t, PACKED_D), U32),
            jax.ShapeDtypeStruct((TOP_K, t), jnp.int32),
            jax.ShapeDtypeStruct((TOP_K, t), F32),
            jax.ShapeDtypeStruct((TOP_K, t), jnp.int32),
            jax.ShapeDtypeStruct((N_EXPERTS, 128), jnp.int32),
        ],
        scratch_shapes=[pltpu.VMEM((N_EXPERTS, 128), F32)],
        compiler_params=_params("arbitrary"),
        name="router",
    )(x, mod3, mod3, router_wt, router_b, tri)


def _dispatch_plan(topi_t, rank_t, counts, n_blocks):
    padded = (counts + ROUTE_BLOCK - 1) // ROUTE_BLOCK * ROUTE_BLOCK
    ends_p = jnp.cumsum(padded)
    pstart = ends_p - padded
    experts = jnp.arange(N_EXPERTS, dtype=jnp.int32)
    pstart_tok = jnp.sum(jnp.where(topi_t[:, :, None] == experts, pstart, 0), -1)
    dest_t = (pstart_tok + rank_t).astype(jnp.int32)
    blk_start = jnp.arange(n_blocks, dtype=jnp.int32) * ROUTE_BLOCK
    block_e = jnp.minimum(jnp.sum((blk_start[:, None] >= ends_p).astype(jnp.int32), -1), N_EXPERTS - 1)
    blk_end = jnp.sum(jnp.where(block_e[:, None] == experts, pstart + counts, 0), -1)
    n_valid = jnp.clip(blk_end - blk_start, 0, ROUTE_BLOCK).astype(jnp.int32)
    return dest_t, block_e.astype(jnp.int32), n_valid


def _sc_mesh():
    return plsc.VectorSubcoreMesh(core_axis_name="c", subcore_axis_name="s")


def _sc_dispatch(hp, dest_flat, rows):
    t, dp = hp.shape
    per_w = t // SC_WORKERS
    n_chunks = per_w // SC_CHUNK

    assert n_chunks % 2 == 0

    @functools.partial(
        pl.kernel, mesh=_sc_mesh(), out_type=jax.ShapeDtypeStruct((rows, dp), hp.dtype),
        scratch_types=[pltpu.VMEM((TOP_K * n_chunks, SC_CHUNK), jnp.int32),
                       pltpu.VMEM((SC_CHUNK, dp), hp.dtype), pltpu.VMEM((SC_CHUNK, dp), hp.dtype),
                       pltpu.SemaphoreType.DMA, pltpu.SemaphoreType.DMA,
                       pltpu.SemaphoreType.DMA, pltpu.SemaphoreType.DMA])
    def dispatch(h_hbm, d_hbm, xs_hbm, idx_v, rows_a, rows_b, load_a, load_b, scat_a, scat_b):
        worker = lax.axis_index("s") * SC_CORES + lax.axis_index("c")
        base = worker * per_w
        for k in range(TOP_K):
            pltpu.sync_copy(d_hbm.at[pl.ds((k * SC_WORKERS + worker) * n_chunks, n_chunks)],
                            idx_v.at[pl.ds(k * n_chunks, n_chunks)])

        def load(i, buf, sem):
            return pltpu.make_async_copy(h_hbm.at[pl.ds(base + i * SC_CHUNK, SC_CHUNK)], buf, sem)

        def scatter(i, k, buf, sem):
            return pltpu.make_async_copy(buf, xs_hbm.at[idx_v.at[k * n_chunks + i]], sem)

        def scatter_all(i, buf, sem):
            for k in range(TOP_K):
                scatter(i, k, buf, sem).start()
            for k in range(TOP_K):
                scatter(i, k, buf, sem).wait()

        load(0, rows_a, load_a).start()

        @pl.loop(0, n_chunks, step=2)
        def _(i):
            load(i, rows_a, load_a).wait()
            load(i + 1, rows_b, load_b).start()
            scatter_all(i, rows_a, scat_a)
            load(i + 1, rows_b, load_b).wait()

            @pl.when(i + 2 < n_chunks)
            def _():
                load(i + 2, rows_a, load_a).start()

            scatter_all(i + 1, rows_b, scat_b)

    return dispatch(hp, dest_flat.reshape(TOP_K * SC_WORKERS * n_chunks, SC_CHUNK))


def _sc_gather(table, idx):
    n = idx.shape[0]
    dp = table.shape[1]
    per_w = n // SC_WORKERS
    n_chunks = per_w // SC_CHUNK

    assert n_chunks % 2 == 0

    @functools.partial(
        pl.kernel, mesh=_sc_mesh(), out_type=jax.ShapeDtypeStruct((n, dp), table.dtype),
        scratch_types=[pltpu.VMEM((n_chunks, SC_CHUNK), jnp.int32),
                       pltpu.VMEM((SC_CHUNK, dp), table.dtype), pltpu.VMEM((SC_CHUNK, dp), table.dtype),
                       pltpu.SemaphoreType.DMA, pltpu.SemaphoreType.DMA,
                       pltpu.SemaphoreType.DMA, pltpu.SemaphoreType.DMA])
    def gather(t_hbm, i_hbm, o_hbm, idx_v, rows_a, rows_b, fetch_a, fetch_b, store_a, store_b):
        worker = lax.axis_index("s") * SC_CORES + lax.axis_index("c")
        base = worker * per_w
        pltpu.sync_copy(i_hbm.at[pl.ds(worker * n_chunks, n_chunks)], idx_v)

        def fetch(i, buf, sem):
            return pltpu.make_async_copy(t_hbm.at[idx_v.at[i]], buf, sem)

        def store(i, buf, sem):
            return pltpu.make_async_copy(buf, o_hbm.at[pl.ds(base + i * SC_CHUNK, SC_CHUNK)], sem)

        fetch(0, rows_a, fetch_a).start()

        @pl.loop(0, n_chunks, step=2)
        def _(i):
            fetch(i, rows_a, fetch_a).wait()

            @pl.when(i > 0)
            def _():
                store(i - 1, rows_b, store_b).wait()

            fetch(i + 1, rows_b, fetch_b).start()
            store(i, rows_a, store_a).start()
            fetch(i + 1, rows_b, fetch_b).wait()
            store(i, rows_a, store_a).wait()

            @pl.when(i + 2 < n_chunks)
            def _():
                fetch(i + 2, rows_a, fetch_a).start()

            store(i + 1, rows_b, store_b).start()

        store(n_chunks - 1, rows_b, store_b).wait()

    return gather(table, idx.reshape(SC_WORKERS * n_chunks, SC_CHUNK))


def _expert_kernel(be_ref, nv_ref, x_ref, w1_ref, b1_ref, w2_ref, b2_ref, o_ref, w1b_ref, w2b_ref):
    i = pl.program_id(0)
    n_valid = nv_ref[i]
    new_expert = (i == 0) | (be_ref[i] != be_ref[jnp.maximum(i - 1, 0)])

    @pl.when(new_expert & (n_valid > 0))
    def _():
        w1b_ref[...] = w1_ref[0].astype(BF16)
        w2b_ref[...] = w2_ref[0].astype(BF16)

    @pl.when(n_valid > 0)
    def _():
        live = lax.broadcasted_iota(jnp.int32, (ROUTE_BLOCK, D_MODEL), 0) < n_valid
        x = jnp.where(live, _unpack_rows(x_ref[...]), 0.0).astype(BF16)
        gu = _dot(x, w1b_ref[...]) + b1_ref[0]
        glu = jnp.minimum(gu[:, :D_FF], SWIGLU_LIMIT)
        lin = jnp.clip(gu[:, D_FF:], -SWIGLU_LIMIT, SWIGLU_LIMIT)
        act = glu * _sigmoid(SWIGLU_ALPHA * glu) * (lin + 1.0)
        o_ref[...] = _pack_rows(_dot(act.astype(BF16), w2b_ref[...]) + b2_ref[0])

    @pl.when(n_valid == 0)
    def _():
        o_ref[...] = jnp.zeros_like(o_ref)


def _experts(xs, block_e, n_valid, w1, b1, w2, b2, layer):
    rows, dp = xs.shape
    d = D_MODEL
    n_blocks = rows // ROUTE_BLOCK
    e0 = layer * N_EXPERTS
    grid_spec = pltpu.PrefetchScalarGridSpec(
        num_scalar_prefetch=2,
        grid=(n_blocks,),
        in_specs=[
            pl.BlockSpec((ROUTE_BLOCK, dp), lambda i, be, nv: (i, 0)),
            pl.BlockSpec((1, d, 2 * D_FF), lambda i, be, nv: (e0 + be[i], 0, 0)),
            pl.BlockSpec((1, 1, 2 * D_FF), lambda i, be, nv: (e0 + be[i], 0, 0)),
            pl.BlockSpec((1, D_FF, d), lambda i, be, nv: (e0 + be[i], 0, 0)),
            pl.BlockSpec((1, 1, d), lambda i, be, nv: (e0 + be[i], 0, 0)),
        ],
        out_specs=pl.BlockSpec((ROUTE_BLOCK, dp), lambda i, be, nv: (i, 0)),
        scratch_shapes=[pltpu.VMEM((d, 2 * D_FF), BF16), pltpu.VMEM((D_FF, d), BF16)],
    )
    return pl.pallas_call(
        _expert_kernel,
        grid_spec=grid_spec,
        out_shape=jax.ShapeDtypeStruct((rows, dp), U32),
        compiler_params=_params("arbitrary"),
        name="experts",
    )(block_e, n_valid, xs, w1, b1, w2, b2)


def _combine_kernel(x_ref, yg_ref, topg_ref, gt_ref, lng_ref, lnb_ref, o_ref):
    g_t = topg_ref[...]
    g = jnp.concatenate([g_t, jnp.zeros((128 - TOP_K, g_t.shape[1]), F32)], axis=0).T
    y = g[:, 0:1] * _unpack_rows(yg_ref[0])
    for k in range(1, TOP_K):
        y = y + g[:, k:k + 1] * _unpack_rows(yg_ref[k])
    o_ref[...] = _deepnorm(x_ref[...], y, gt_ref[0], lng_ref[...], lnb_ref[...])


def _combine(x, yg, topg, mod3, ln_g, ln_b, seq):
    t, d = x.shape
    tm = TOKEN_TILE
    per_b = seq // tm
    return pl.pallas_call(
        _combine_kernel,
        grid=(t // tm,),
        in_specs=[
            pl.BlockSpec((tm, d), lambda i: (i, 0)),
            pl.BlockSpec((TOP_K, tm, PACKED_D), lambda i: (0, i, 0)),
            pl.BlockSpec((TOP_K, tm), lambda i: (0, i)),
            pl.BlockSpec((1, 1, d), lambda i: (i // per_b, 0, 5)),
            pl.BlockSpec((1, d), lambda i: (0, 0)),
            pl.BlockSpec((1, d), lambda i: (0, 0)),
        ],
        out_specs=pl.BlockSpec((tm, d), lambda i: (i, 0)),
        out_shape=jax.ShapeDtypeStruct((t, d), F32),
        compiler_params=_params("parallel"),
        name="combine",
    )(x, yg, topg, mod3, ln_g, ln_b)


def _rearranged_w_in(w_in):
    depth, d, _ = w_in.shape
    ab_end = SEG_AB + 2 * GDN_HEADS
    ab_pad = jnp.zeros((depth, d, AB_W - 2 * GDN_HEADS), w_in.dtype)
    return jnp.concatenate([w_in[:, :, :ab_end], ab_pad, w_in[:, :, ab_end:]], axis=2).astype(BF16)


def kernel(x, c, ada_w, ada_b, w_in, conv_w, a_log, dt_bias, gdn_norm_w, sinks, pool_w, pool_scale, w_pa, w_pb, w_pc, w_o, ln1_g, ln1_b, ln2_g, ln2_b, router_w, router_b, exp_w1, exp_b1, exp_w2, exp_b2):
    batch, seq, d = x.shape
    t = batch * seq
    n_blocks = (t * TOP_K + ROUTE_BLOCK - 1) // ROUTE_BLOCK + N_EXPERTS
    mod = _ada_mod(c, ada_w, ada_b)
    xt = x.reshape(t, d)
    lane_pad = jnp.zeros((AB_W - GDN_HEADS,), F32)
    rows = n_blocks * ROUTE_BLOCK
    w_in_all = _rearranged_w_in(w_in)
    w1_all = exp_w1.reshape(DEPTH * N_EXPERTS, d, 2 * D_FF)
    b1_all = exp_b1.reshape(DEPTH * N_EXPERTS, 1, 2 * D_FF)
    w2_all = exp_w2.reshape(DEPTH * N_EXPERTS, D_FF, d)
    b2_all = exp_b2.reshape(DEPTH * N_EXPERTS, 1, d)
    for l in range(DEPTH):
        mod3 = mod[l].reshape(batch, 1, 6 * d)
        qkv, z, ab, qb, kvb, uc, gates = _in_proj(xt, mod3, w_in_all, conv_w, seq, l)
        gpar = jnp.stack([jnp.concatenate([-jnp.exp(a_log[l]), lane_pad]), jnp.concatenate([dt_bias[l], lane_pad])])
        xt = _mixer(xt, qkv, ab, z, qb, kvb, uc, gates, mod3, gpar, gdn_norm_w[l].reshape(1, GDN_DV), sinks[l],
                    pool_w[l].astype(BF16), pool_scale[l].reshape(1, POOL_DIM),
                    w_pa[l].astype(BF16), w_pb[l].astype(BF16), w_pc[l].astype(BF16), w_o[l].astype(BF16),
                    ln1_g[l].reshape(1, d), ln1_b[l].reshape(1, d), batch, seq)
        hp, topi_t, topg_t, rank_t, cnt = _router(xt, mod3, router_w[l].T, router_b[l].reshape(N_EXPERTS, 1), seq)
        dest_t, block_e, n_valid = _dispatch_plan(topi_t, rank_t, cnt[:, 0], n_blocks)
        dest_flat = dest_t.reshape(TOP_K * t)
        xs = _sc_dispatch(hp, dest_flat, rows)
        ys = _experts(xs, block_e, n_valid, w1_all, b1_all, w2_all, b2_all, l)
        yg = _sc_gather(ys, dest_flat).reshape(TOP_K, t, PACKED_D)
        xt = _combine(xt, yg, topg_t, mod3, ln2_g[l].reshape(1, d), ln2_b[l].reshape(1, d), seq)
    return xt.reshape(batch, seq, d)
```

```python
import functools

import jax
import jax.numpy as jnp
from jax import lax
from jax.experimental import pallas as pl
from jax.experimental.pallas import tpu as pltpu
from jax.experimental.pallas import tpu_sc as plsc

D_MODEL = 1024
DEPTH = 4
GDN_HEADS = 4
GDN_DK = 128
GDN_DV = 128
GDN_CONV = 4
GDN_CHUNK = 64
SWA_HQ = 8
SWA_HKV = 2
SWA_DH = 64
WINDOW = 128
POOL_WINDOWS = (2, 4, 8, 16)
POOL_GROUPS = 4
POOL_GDIM = 128
N_BRANCH = 3
N_EXPERTS = 32
TOP_K = 4
D_FF = 1024
ROUTE_BLOCK = 512
SWIGLU_ALPHA = 1.702
SWIGLU_LIMIT = 7.0
LN_EPS = 1e-5
RMS_EPS = 1e-6
DEEPNORM_ALPHA = (2 * DEPTH) ** 0.25

GDN_QK = GDN_HEADS * GDN_DK
GDN_V = GDN_HEADS * GDN_DV
SWA_Q = SWA_HQ * SWA_DH
SWA_KV = SWA_HKV * SWA_DH
POOL_DIM = POOL_GROUPS * POOL_GDIM
GQA_GROUP = SWA_HQ // SWA_HKV
MAX_POOL = max(POOL_WINDOWS)

QKV_W = 2 * GDN_QK + GDN_V
AB_W = 128
SEG_QKV = 0
SEG_Z = SEG_QKV + QKV_W
SEG_AB = SEG_Z + GDN_V
SEG_QB = SEG_AB + AB_W
SEG_KVB = SEG_QB + SWA_Q
SEG_UC = SEG_KVB + 2 * SWA_KV
SEG_GATE = SEG_UC + POOL_DIM
PROJ_W = SEG_GATE + N_BRANCH * D_MODEL

PROJ_CHUNK = 256
TOKEN_TILE = 512
NEG_BIG = -1e30
VMEM_LIMIT = 56 * 1024 * 1024

PACKED_D = D_MODEL // 2
SC_CORES = 2
SC_SUBCORES = 16
SC_WORKERS = SC_CORES * SC_SUBCORES
SC_CHUNK = 64

F32 = jnp.float32
BF16 = jnp.bfloat16
U32 = jnp.uint32
HI_MASK = 0xFFFF0000


def _pack_rows(x):
    bits = pltpu.bitcast(x.astype(BF16).astype(F32), U32)
    return (bits[:, :PACKED_D] >> 16) | (bits[:, PACKED_D:] & jnp.uint32(HI_MASK))


def _unpack_rows(w):
    lo = pltpu.bitcast(w << 16, F32)
    hi = pltpu.bitcast(w & jnp.uint32(HI_MASK), F32)
    return jnp.concatenate([lo, hi], axis=1)


def _params(*sem):
    return pltpu.CompilerParams(dimension_semantics=sem, vmem_limit_bytes=VMEM_LIMIT)


def _sigmoid(x):
    return 0.5 * jnp.tanh(0.5 * x) + 0.5


def _layer_norm(x):
    mu = jnp.mean(x, -1, keepdims=True)
    xc = x - mu
    var = jnp.mean(xc * xc, -1, keepdims=True)
    return xc * lax.rsqrt(var + LN_EPS)


def _dot(a, b):
    return jnp.dot(a, b, preferred_element_type=F32)


def _dot_nt(a, b):
    return lax.dot_general(a, b, (((1,), (1,)), ((), ())), preferred_element_type=F32)


def _dot_tn(a, b):
    return lax.dot_general(a, b, (((0,), (0,)), ((), ())), preferred_element_type=F32)


def _bdot(a, b):
    return lax.dot_general(a, b, (((2,), (1,)), ((0,), (0,))), preferred_element_type=F32)


def _bdot_tn(a, b):
    return lax.dot_general(a, b, (((1,), (1,)), ((0,), (0,))), preferred_element_type=F32)


def _bdot_nt(a, b):
    return lax.dot_general(a, b, (((2,), (2,)), ((0,), (0,))), preferred_element_type=F32)


def _ada_kernel(c_ref, w_ref, b_ref, o_ref):
    c = c_ref[...]
    cond = c * _sigmoid(c)
    o_ref[0] = _dot(cond.astype(BF16), w_ref[0].astype(BF16)) + b_ref[0]


def _ada_mod(c, ada_w, ada_b):
    depth, d, n = ada_w.shape
    b = c.shape[0]
    tn = 1024
    return pl.pallas_call(
        _ada_kernel,
        grid=(depth, n // tn),
        in_specs=[
            pl.BlockSpec((b, d), lambda l, j: (0, 0)),
            pl.BlockSpec((1, d, tn), lambda l, j: (l, 0, j)),
            pl.BlockSpec((1, 1, tn), lambda l, j: (l, 0, j)),
        ],
        out_specs=pl.BlockSpec((1, b, tn), lambda l, j: (l, 0, j)),
        out_shape=jax.ShapeDtypeStruct((depth, b, n), F32),
        compiler_params=_params("parallel", "parallel"),
        name="ada_mod",
    )(c, ada_w, ada_b.reshape(depth, 1, n))


def _proj_kernel(x_ref, sh_ref, sc_ref, w_ref, convw_ref, qkv_ref, z_ref, ab_ref, qb_ref, kvb_ref, uc_ref, gate_ref,
                 raw_ref, *, tm, per_b):
    h = (_layer_norm(x_ref[...]) * (1.0 + sc_ref[0]) + sh_ref[0]).astype(BF16)

    def seg(start, width):
        return _dot(h, w_ref[:, start:start + width])

    @pl.when(pl.program_id(0) % per_b == 0)
    def _():
        raw_ref[0:8, :] = jnp.zeros((8, QKV_W), F32)

    @pl.when(pl.program_id(0) % per_b != 0)
    def _():
        raw_ref[0:8, :] = raw_ref[tm:tm + 8, :]

    for j in range(QKV_W // PROJ_CHUNK):
        cols = slice(j * PROJ_CHUNK, (j + 1) * PROJ_CHUNK)
        raw_ref[8:8 + tm, cols] = seg(SEG_QKV + j * PROJ_CHUNK, PROJ_CHUNK)

    def conv_silu(j):
        cols = slice(j * PROJ_CHUNK, (j + 1) * PROJ_CHUNK)
        conv = raw_ref[8:8 + tm, cols] * convw_ref[GDN_CONV - 1:GDN_CONV, cols]
        for tap in range(GDN_CONV - 1):
            conv = conv + raw_ref[5 + tap:5 + tap + tm, cols] * convw_ref[tap:tap + 1, cols]
        qkv_ref[:, cols] = (conv * _sigmoid(conv)).astype(BF16)
    n_conv = QKV_W // PROJ_CHUNK
    done = 0
    for ref, start, width in ((z_ref, SEG_Z, GDN_V), (qb_ref, SEG_QB, SWA_Q), (kvb_ref, SEG_KVB, 2 * SWA_KV),
                              (uc_ref, SEG_UC, POOL_DIM)):
        for j in range(width // PROJ_CHUNK):
            cols = slice(j * PROJ_CHUNK, (j + 1) * PROJ_CHUNK)
            ref[:, cols] = seg(start + j * PROJ_CHUNK, PROJ_CHUNK).astype(BF16)
            if done < n_conv:
                conv_silu(done)
                done += 1
    assert done == n_conv
    ab_ref[...] = seg(SEG_AB, AB_W)
    gate_ref[...] = _sigmoid(seg(SEG_GATE, N_BRANCH * D_MODEL)).astype(BF16)


def _in_proj(x, mod3, w, conv_w, seq, layer):
    t, d = x.shape
    tm = TOKEN_TILE
    per_b = seq // tm
    widths = (QKV_W, GDN_V, AB_W, SWA_Q, 2 * SWA_KV, POOL_DIM, N_BRANCH * D_MODEL)
    dtypes = (BF16, BF16, F32, BF16, BF16, BF16, BF16)
    return pl.pallas_call(
        functools.partial(_proj_kernel, tm=tm, per_b=per_b),
        grid=(t // tm,),
        in_specs=[
            pl.BlockSpec((tm, d), lambda i: (i, 0)),
            pl.BlockSpec((1, 1, d), lambda i: (i // per_b, 0, 0)),
            pl.BlockSpec((1, 1, d), lambda i: (i // per_b, 0, 1)),
            pl.BlockSpec((None, d, PROJ_W), lambda i: (layer, 0, 0)),
            pl.BlockSpec((None, GDN_CONV, QKV_W), lambda i: (layer, 0, 0)),
        ],
        out_specs=[pl.BlockSpec((tm, n), lambda i: (i, 0)) for n in widths],
        out_shape=[jax.ShapeDtypeStruct((t, n), dt) for n, dt in zip(widths, dtypes)],
        scratch_shapes=[pltpu.VMEM((8 + tm, QKV_W), F32)],
        compiler_params=_params("arbitrary"),
        name="in_proj",
    )(x, mod3, mod3, w, conv_w)


def _softplus(x):
    return jnp.maximum(x, 0.0) + jnp.log(1.0 + jnp.exp(-jnp.abs(x)))


def _unit_lower_inverse(lm, row, col, between_levels=lambda: None):
    n = lm.shape[-1]
    eye = (row == col).astype(F32)
    t = None
    s = 1
    while s < n:
        join = ((row // (2 * s)) == (col // (2 * s))) & ((row % (2 * s)) >= s) & ((col % (2 * s)) < s)
        cm = jnp.where(join, lm, 0.0)
        if t is None:
            t = eye - cm
        else:
            p = _bdot(t, cm)
            t = t - _bdot(p, t)
            between_levels()
        s *= 2
    return t


def _gdn_kernel(qkv_ref, ab_ref, z_ref, gpar_ref, nw_ref, o_ref, state_ref, *, tb, filler=lambda: None):
    c_len = GDN_CHUNK
    nc = tb // c_len

    @pl.when(pl.program_id(1) == 0)
    def _():
        state_ref[...] = jnp.zeros_like(state_ref)

    ab = ab_ref[...]
    gpar = gpar_ref[...]
    g_all = gpar[0:1] * _softplus(ab + gpar[1:2])
    beta_all = _sigmoid(ab)
    pos = lax.broadcasted_iota(jnp.int32, (tb, AB_W), 0) % c_len
    gc_all = g_all
    s = 1
    while s < c_len:
        gc_all = gc_all + jnp.where(pos >= s, pltpu.roll(gc_all, s, axis=0), 0.0)
        s *= 2

    nh = GDN_HEADS
    nb = nc * nh
    row = lax.broadcasted_iota(jnp.int32, (nb, c_len, c_len), 1)
    col = lax.broadcasted_iota(jnp.int32, (nb, c_len, c_len), 2)
    causal = row >= col
    strict = row > col
    diag = row == col

    def chunked(per_head):
        return jnp.stack([per_head(h).reshape(nc, c_len, GDN_DK) for h in range(nh)], axis=1).reshape(nb, c_len, GDN_DK)

    q = chunked(lambda h: qkv_ref[:, h * GDN_DK:(h + 1) * GDN_DK].astype(F32))
    k = chunked(lambda h: qkv_ref[:, GDN_QK + h * GDN_DK:GDN_QK + (h + 1) * GDN_DK].astype(F32))
    v = chunked(lambda h: qkv_ref[:, 2 * GDN_QK + h * GDN_DV:2 * GDN_QK + (h + 1) * GDN_DV].astype(F32))
    gc = chunked(lambda h: jnp.broadcast_to(gc_all[:, h:h + 1], (tb, GDN_DK)))
    beta = chunked(lambda h: jnp.broadcast_to(beta_all[:, nh + h:nh + h + 1], (tb, GDN_DK)))
    q = q * lax.rsqrt(jnp.sum(q * q, -1, keepdims=True) + 1e-6) * (GDN_DK ** -0.5)
    k = k * lax.rsqrt(jnp.sum(k * k, -1, keepdims=True) + 1e-6)

    gc_i = gc[:, :, :c_len]
    gc_j = jnp.sum(jnp.where(diag, gc_i, 0.0), axis=1, keepdims=True)
    decay = jnp.where(causal, jnp.exp(jnp.where(causal, gc_i - gc_j, 0.0)), 0.0)
    eg = jnp.exp(gc)
    gc_last = gc[:, c_len - 1:c_len, :]
    g_last = jnp.exp(gc_last)
    k_beta = k * beta
    kq = _bdot_nt(jnp.concatenate([k_beta, q], axis=1).astype(BF16), k.astype(BF16))
    lower = jnp.where(strict, kq[:, :c_len] * decay, 0.0)
    attn = jnp.where(causal, kq[:, c_len:] * decay, 0.0).astype(BF16)
    tinv = _unit_lower_inverse(lower, row, col, filler)
    uw = _bdot(tinv.astype(BF16), jnp.concatenate([v * beta, k_beta * eg], axis=2).astype(BF16)).astype(BF16)
    q_g = q * eg
    k_g = (k * jnp.exp(gc_last - gc)).astype(BF16)

    k_uw = _bdot_tn(k_g, uw)
    a_uw = _bdot(attn, uw)
    ku = k_uw[:, :, :GDN_DV]
    o_loc = a_uw[:, :, :GDN_DV]
    qk_eff = jnp.concatenate([q_g - a_uw[:, :, GDN_DV:], k_uw[:, :, GDN_DV:]], axis=1).astype(BF16)

    st = state_ref[...]
    outs = []
    for c in range(nc):
        sl = slice(c * nh, (c + 1) * nh)
        prod = _bdot(qk_eff[sl], st.astype(BF16))
        outs.append(prod[:, :c_len] + o_loc[sl])
        st = st * g_last[sl] - prod[:, c_len:] + ku[sl]
        filler()
    state_ref[...] = st

    nw = nw_ref[...]
    for h in range(nh):
        o = jnp.concatenate([outs[c][h] for c in range(nc)], axis=0)
        zh = z_ref[:, h * GDN_DV:(h + 1) * GDN_DV].astype(F32)
        o = o * lax.rsqrt(jnp.mean(o * o, -1, keepdims=True) + RMS_EPS) * nw * (zh * _sigmoid(zh))
        o_ref[:, h * GDN_DV:(h + 1) * GDN_DV] = o.astype(BF16)


def _swa_units(sink_ref, q_ref, kv_ref, kvp_ref, bias_ref, o_ref, *, tq):
    nw = tq // WINDOW
    rows = GQA_GROUP * WINDOW
    scale = SWA_DH ** -0.5

    def unit(hk, w):
        heads = range(hk * GQA_GROUP, (hk + 1) * GQA_GROUP)
        tok = slice(w * WINDOW, (w + 1) * WINDOW)

        def keys(col0):
            cols = slice(col0, col0 + SWA_DH)
            prev = kvp_ref[:, cols] if w == 0 else kv_ref[(w - 1) * WINDOW:w * WINDOW, cols]
            return jnp.concatenate([prev, kv_ref[tok, cols]], axis=0)

        k = keys(hk * SWA_DH)
        v = keys(SWA_KV + hk * SWA_DH)
        q = jnp.concatenate([q_ref[tok, hq * SWA_DH:(hq + 1) * SWA_DH] for hq in heads], axis=0)
        sink = jnp.concatenate([jnp.full((WINDOW, 1), sink_ref[hq], F32) for hq in heads], axis=0)
        qi = lax.broadcasted_iota(jnp.int32, (rows, WINDOW), 0) % WINDOW
        kj = lax.broadcasted_iota(jnp.int32, (rows, WINDOW), 1)
        from_prev = kj > qi
        s2 = _dot_nt(q, k)
        s = jnp.where(from_prev, s2[:, :WINDOW], s2[:, WINDOW:]) * scale + bias_ref[hk]
        if w == 0:
            s = jnp.where(from_prev & (pl.program_id(1) == 0), NEG_BIG, s)
        m = jnp.maximum(jnp.max(s, -1, keepdims=True), sink)
        p = jnp.exp(s - m)
        p2 = jnp.concatenate([jnp.where(from_prev, p, 0.0), jnp.where(from_prev, 0.0, p)], axis=1).astype(BF16)
        v_ext = jnp.concatenate([v, jnp.ones((2 * WINDOW, SWA_DH), BF16)], axis=1)
        pv = _dot(p2, v_ext)
        o = pv[:, :SWA_DH] / (pv[:, SWA_DH:SWA_DH + 1] + jnp.exp(sink - m))
        for g, hq in enumerate(heads):
            o_ref[tok, hq * SWA_DH:(hq + 1) * SWA_DH] = o[g * WINDOW:(g + 1) * WINDOW, :].astype(BF16)

    return [functools.partial(unit, hk, w) for hk in range(SWA_HKV) for w in range(nw)]


def _swa_bias():
    qi = jnp.arange(WINDOW)[:, None]
    kj = jnp.arange(WINDOW)[None, :]
    dist = jnp.where(kj > qi, qi - kj + WINDOW, qi - kj).astype(F32)
    slopes = 2.0 ** (-8.0 * jnp.arange(1, SWA_HQ + 1, dtype=F32) / SWA_HQ)
    bias = -slopes[:, None, None] * dist[None]
    return bias.reshape(SWA_HKV, GQA_GROUP * WINDOW, WINDOW)


def _deepnorm(x, y, gt, g, b):
    return _layer_norm(DEEPNORM_ALPHA * x + (1.0 + gt) * y) * g + b


def _merge_kernel(x_ref, ya_ref, yb_ref, uc_ref, ucp_ref, gate_ref, gt_ref, poolw_ref, pscale_ref,
                  wpa_ref, wpb_ref, wpc_ref, wo_ref, lng_ref, lnb_ref, o_ref, *, tm):
    j = pl.program_id(1)
    u = uc_ref[...].astype(F32)
    halo = jnp.where(j == 0, 0.0, ucp_ref[...].astype(F32))
    ue = jnp.concatenate([halo, u], axis=0)
    tpos = (j * tm + lax.broadcasted_iota(jnp.int32, (tm, POOL_GDIM), 0) + 1).astype(F32)
    ycs = []
    for gi, win in enumerate(POOL_WINDOWS):
        a = ue[:, gi * POOL_GDIM:(gi + 1) * POOL_GDIM]
        span = 1
        while span < win:
            a = a[span:] + a[:-span]
            span *= 2
        lo = MAX_POOL - win + 1
        d = a[lo:lo + tm] / jnp.minimum(tpos, float(win)) - u[:, gi * POOL_GDIM:(gi + 1) * POOL_GDIM]
        ycs.append(_dot(d.astype(BF16), poolw_ref[gi]))
    yc = jnp.concatenate(ycs, axis=1) * pscale_ref[...]
    merged = gate_ref[:, 0:D_MODEL].astype(F32) * _dot(ya_ref[...], wpa_ref[...])
    merged = merged + gate_ref[:, D_MODEL:2 * D_MODEL].astype(F32) * _dot(yb_ref[...], wpb_ref[...])
    merged = merged + gate_ref[:, 2 * D_MODEL:3 * D_MODEL].astype(F32) * _dot(yc.astype(BF16), wpc_ref[...])
    y = _dot(merged.astype(BF16), wo_ref[...])
    o_ref[...] = _deepnorm(x_ref[...], y, gt_ref[0], lng_ref[...], lnb_ref[...])


def _mixer_kernel(sink_ref, x_ref, qkv_ref, ab_ref, z_ref, gpar_ref, nw_ref, qb_ref, kvb_ref, kvp_ref, bias_ref,
                  uc_ref, ucp_ref, gate_ref, gt_ref, poolw_ref, pscale_ref, wpa_ref, wpb_ref, wpc_ref, wo_ref,
                  lng_ref, lnb_ref, o_ref, state_ref, ya_ref, yb_ref, *, tm):
    pending = iter(_swa_units(sink_ref, qb_ref, kvb_ref, kvp_ref, bias_ref, yb_ref, tq=tm))

    def filler():
        unit = next(pending, None)
        if unit is not None:
            unit()

    _gdn_kernel(qkv_ref, ab_ref, z_ref, gpar_ref, nw_ref, ya_ref, state_ref, tb=tm, filler=filler)
    for unit in pending:
        unit()
    _merge_kernel(x_ref, ya_ref, yb_ref, uc_ref, ucp_ref, gate_ref, gt_ref, poolw_ref, pscale_ref,
                  wpa_ref, wpb_ref, wpc_ref, wo_ref, lng_ref, lnb_ref, o_ref, tm=tm)


def _mixer(x, qkv, ab, z, qb, kvb, uc, gates, mod3, gpar, norm_w, sinks, pool_w, pool_scale, w_pa, w_pb, w_pc, w_o,
           ln_g, ln_b, batch, seq):
    t, d = x.shape
    tm = TOKEN_TILE
    per_b = seq // tm

    def tok(n):
        return pl.BlockSpec((tm, n), lambda b, j: (b * per_b + j, 0))

    def prev_rows(rows, n):
        return pl.BlockSpec((rows, n), lambda b, j: (jnp.maximum((b * per_b + j) * (tm // rows) - 1, 0), 0))

    def full(shape):
        return pl.BlockSpec(shape, lambda b, j: (0,) * len(shape))

    return pl.pallas_call(
        functools.partial(_mixer_kernel, tm=tm),
        grid=(batch, per_b),
        in_specs=[
            pl.BlockSpec(memory_space=pltpu.SMEM),
            tok(d), tok(QKV_W), tok(AB_W), tok(GDN_V), full((2, AB_W)), full((1, GDN_DV)),
            tok(SWA_Q), tok(2 * SWA_KV), prev_rows(WINDOW, 2 * SWA_KV),
            full((SWA_HKV, GQA_GROUP * WINDOW, WINDOW)),
            tok(POOL_DIM), prev_rows(MAX_POOL, POOL_DIM),
            tok(N_BRANCH * d),
            pl.BlockSpec((1, 1, d), lambda b, j: (b, 0, 2)),
            full((POOL_GROUPS, POOL_GDIM, POOL_GDIM)), full((1, POOL_DIM)),
            full((GDN_V, d)), full((SWA_Q, d)), full((POOL_DIM, d)), full((d, d)),
            full((1, d)), full((1, d)),
        ],
        out_specs=tok(d),
        out_shape=jax.ShapeDtypeStruct((t, d), F32),
        scratch_shapes=[pltpu.VMEM((GDN_HEADS, GDN_DK, GDN_DV), F32), pltpu.VMEM((tm, GDN_V), BF16),
                        pltpu.VMEM((tm, SWA_Q), BF16)],
        compiler_params=_params("parallel", "arbitrary"),
        name="mixer",
    )(sinks, x, qkv, ab, z, gpar, norm_w, qb, kvb, kvb, _swa_bias(), uc, uc, gates, mod3, pool_w, pool_scale,
      w_pa, w_pb, w_pc, w_o, ln_g, ln_b)


def _router_kernel(x_ref, sh_ref, sc_ref, rwt_ref, rb_ref, tri_ref, hp_ref, topi_ref, topg_ref, rank_ref, cnt_ref,
                   run_ref):
    @pl.when(pl.program_id(0) == 0)
    def _():
        run_ref[...] = jnp.zeros_like(run_ref)

    h = _layer_norm(x_ref[...]) * (1.0 + sc_ref[0]) + sh_ref[0]
    hp_ref[...] = _pack_rows(h)
    logits = lax.dot_general(rwt_ref[...], h, (((1,), (1,)), ((), ())), preferred_element_type=F32,
                             precision=lax.Precision.HIGHEST) + rb_ref[...]
    sub = lax.broadcasted_iota(jnp.int32, logits.shape, 0)
    vals, idxs = [], []
    for _ in range(TOP_K):
        m = jnp.max(logits, 0, keepdims=True)
        idx = jnp.min(jnp.where(logits == m, sub, N_EXPERTS), 0, keepdims=True)
        vals.append(m)
        idxs.append(idx)
        logits = jnp.where(sub == idx, -jnp.inf, logits)
    es = [jnp.exp(v - vals[0]) for v in vals]
    denom = es[0] + es[1] + es[2] + es[3]
    topi_ref[...] = jnp.concatenate(idxs, axis=0)
    topg_ref[...] = jnp.concatenate([e / denom for e in es], axis=0)

    sel = jnp.zeros(logits.shape, F32)
    for idx in idxs:
        sel = sel + (sub == idx).astype(F32)
    before = run_ref[:, 0:1] + _dot(sel.astype(BF16), tri_ref[...])
    ranks = [jnp.sum(jnp.where(sub == idx, before, 0.0), 0, keepdims=True) for idx in idxs]
    rank_ref[...] = jnp.concatenate(ranks, axis=0).astype(jnp.int32)
    run_ref[...] = run_ref[...] + jnp.sum(sel, 1, keepdims=True)
    cnt_ref[...] = run_ref[...].astype(jnp.int32)


def _router(x, mod3, router_wt, router_b, seq):
    t, d = x.shape
    tm = TOKEN_TILE
    per_b = seq // tm
    tri = (jnp.arange(tm)[:, None] < jnp.arange(tm)[None, :]).astype(BF16)
    return pl.pallas_call(
        _router_kernel,
        grid=(t // tm,),
        in_specs=[
            pl.BlockSpec((tm, d), lambda i: (i, 0)),
            pl.BlockSpec((1, 1, d), lambda i: (i // per_b, 0, 3)),
            pl.BlockSpec((1, 1, d), lambda i: (i // per_b, 0, 4)),
            pl.BlockSpec((N_EXPERTS, d), lambda i: (0, 0)),
            pl.BlockSpec((N_EXPERTS, 1), lambda i: (0, 0)),
            pl.BlockSpec((tm, tm), lambda i: (0, 0)),
        ],
        out_specs=[
            pl.BlockSpec((tm, PACKED_D), lambda i: (i, 0)),
            pl.BlockSpec((TOP_K, tm), lambda i: (0, i)),
            pl.BlockSpec((TOP_K, tm), lambda i: (0, i)),
            pl.BlockSpec((TOP_K, tm), lambda i: (0, i)),
            pl.BlockSpec((N_EXPERTS, 128), lambda i: (0, 0)),
        ],
        out_shape=[
            jax.ShapeDtypeStruct((t, PACKED_D), U32),
            jax.ShapeDtypeStruct((TOP_K, t), jnp.int32),
            jax.ShapeDtypeStruct((TOP_K, t), F32),
            jax.ShapeDtypeStruct((TOP_K, t), jnp.int32),
            jax.ShapeDtypeStruct((N_EXPERTS, 128), jnp.int32),
        ],
        scratch_shapes=[pltpu.VMEM((N_EXPERTS, 128), F32)],
        compiler_params=_params("arbitrary"),
        name="router",
    )(x, mod3, mod3, router_wt, router_b, tri)


def _dispatch_plan(topi_t, rank_t, counts, n_blocks):
    padded = (counts + ROUTE_BLOCK - 1) // ROUTE_BLOCK * ROUTE_BLOCK
    ends_p = jnp.cumsum(padded)
    pstart = ends_p - padded
    experts = jnp.arange(N_EXPERTS, dtype=jnp.int32)
    pstart_tok = jnp.sum(jnp.where(topi_t[:, :, None] == experts, pstart, 0), -1)
    dest_t = (pstart_tok + rank_t).astype(jnp.int32)
    blk_start = jnp.arange(n_blocks, dtype=jnp.int32) * ROUTE_BLOCK
    block_e = jnp.minimum(jnp.sum((blk_start[:, None] >= ends_p).astype(jnp.int32), -1), N_EXPERTS - 1)
    blk_end = jnp.sum(jnp.where(block_e[:, None] == experts, pstart + counts, 0), -1)
    n_valid = jnp.clip(blk_end - blk_start, 0, ROUTE_BLOCK).astype(jnp.int32)
    return dest_t, block_e.astype(jnp.int32), n_valid


def _sc_mesh():
    return plsc.VectorSubcoreMesh(core_axis_name="c", subcore_axis_name="s")


def _sc_dispatch(hp, dest_flat, rows):
    t, dp = hp.shape
    per_w = t // SC_WORKERS
    n_chunks = per_w // SC_CHUNK

    assert n_chunks % 2 == 0

    @functools.partial(
        pl.kernel, mesh=_sc_mesh(), out_type=jax.ShapeDtypeStruct((rows, dp), hp.dtype),
        scratch_types=[pltpu.VMEM((TOP_K * n_chunks, SC_CHUNK), jnp.int32),
                       pltpu.VMEM((SC_CHUNK, dp), hp.dtype), pltpu.VMEM((SC_CHUNK, dp), hp.dtype),
                       pltpu.SemaphoreType.DMA, pltpu.SemaphoreType.DMA,
                       pltpu.SemaphoreType.DMA, pltpu.SemaphoreType.DMA])
    def dispatch(h_hbm, d_hbm, xs_hbm, idx_v, rows_a, rows_b, load_a, load_b, scat_a, scat_b):
        worker = lax.axis_index("s") * SC_CORES + lax.axis_index("c")
        base = worker * per_w
        for k in range(TOP_K):
            pltpu.sync_copy(d_hbm.at[pl.ds((k * SC_WORKERS + worker) * n_chunks, n_chunks)],
                            idx_v.at[pl.ds(k * n_chunks, n_chunks)])

        def load(i, buf, sem):
            return pltpu.make_async_copy(h_hbm.at[pl.ds(base + i * SC_CHUNK, SC_CHUNK)], buf, sem)

        def scatter(i, k, buf, sem):
            return pltpu.make_async_copy(buf, xs_hbm.at[idx_v.at[k * n_chunks + i]], sem)

        def scatter_all(i, buf, sem):
            for k in range(TOP_K):
                scatter(i, k, buf, sem).start()
            for k in range(TOP_K):
                scatter(i, k, buf, sem).wait()

        load(0, rows_a, load_a).start()

        @pl.loop(0, n_chunks, step=2)
        def _(i):
            load(i, rows_a, load_a).wait()
            load(i + 1, rows_b, load_b).start()
            scatter_all(i, rows_a, scat_a)
            load(i + 1, rows_b, load_b).wait()

            @pl.when(i + 2 < n_chunks)
            def _():
                load(i + 2, rows_a, load_a).start()

            scatter_all(i + 1, rows_b, scat_b)

    return dispatch(hp, dest_flat.reshape(TOP_K * SC_WORKERS * n_chunks, SC_CHUNK))


def _sc_gather(table, idx):
    n = idx.shape[0]
    dp = table.shape[1]
    per_w = n // SC_WORKERS
    n_chunks = per_w // SC_CHUNK

    assert n_chunks % 2 == 0

    @functools.partial(
        pl.kernel, mesh=_sc_mesh(), out_type=jax.ShapeDtypeStruct((n, dp), table.dtype),
        scratch_types=[pltpu.VMEM((n_chunks, SC_CHUNK), jnp.int32),
                       pltpu.VMEM((SC_CHUNK, dp), table.dtype), pltpu.VMEM((SC_CHUNK, dp), table.dtype),
                       pltpu.SemaphoreType.DMA, pltpu.SemaphoreType.DMA,
                       pltpu.SemaphoreType.DMA, pltpu.SemaphoreType.DMA])
    def gather(t_hbm, i_hbm, o_hbm, idx_v, rows_a, rows_b, fetch_a, fetch_b, store_a, store_b):
        worker = lax.axis_index("s") * SC_CORES + lax.axis_index("c")
        base = worker * per_w
        pltpu.sync_copy(i_hbm.at[pl.ds(worker * n_chunks, n_chunks)], idx_v)

        def fetch(i, buf, sem):
            return pltpu.make_async_copy(t_hbm.at[idx_v.at[i]], buf, sem)

        def store(i, buf, sem):
            return pltpu.make_async_copy(buf, o_hbm.at[pl.ds(base + i * SC_CHUNK, SC_CHUNK)], sem)

        fetch(0, rows_a, fetch_a).start()

        @pl.loop(0, n_chunks, step=2)
        def _(i):
            fetch(i, rows_a, fetch_a).wait()

            @pl.when(i > 0)
            def _():
                store(i - 1, rows_b, store_b).wait()

            fetch(i + 1, rows_b, fetch_b).start()
            store(i, rows_a, store_a).start()
            fetch(i + 1, rows_b, fetch_b).wait()
            store(i, rows_a, store_a).wait()

            @pl.when(i + 2 < n_chunks)
            def _():
                fetch(i + 2, rows_a, fetch_a).start()

            store(i + 1, rows_b, store_b).start()

        store(n_chunks - 1, rows_b, store_b).wait()

    return gather(table, idx.reshape(SC_WORKERS * n_chunks, SC_CHUNK))


def _expert_kernel(be_ref, nv_ref, x_ref, w1_ref, b1_ref, w2_ref, b2_ref, o_ref, w1b_ref, w2b_ref):
    i = pl.program_id(0)
    n_valid = nv_ref[i]
    new_expert = (i == 0) | (be_ref[i] != be_ref[jnp.maximum(i - 1, 0)])

    @pl.when(new_expert & (n_valid > 0))
    def _():
        w1b_ref[...] = w1_ref[0].astype(BF16)
        w2b_ref[...] = w2_ref[0].astype(BF16)

    @pl.when(n_valid > 0)
    def _():
        live = lax.broadcasted_iota(jnp.int32, (ROUTE_BLOCK, D_MODEL), 0) < n_valid
        x = jnp.where(live, _unpack_rows(x_ref[...]), 0.0).astype(BF16)
        gu = _dot(x, w1b_ref[...]) + b1_ref[0]
        glu = jnp.minimum(gu[:, :D_FF], SWIGLU_LIMIT)
        lin = jnp.clip(gu[:, D_FF:], -SWIGLU_LIMIT, SWIGLU_LIMIT)
        act = glu * _sigmoid(SWIGLU_ALPHA * glu) * (lin + 1.0)
        o_ref[...] = _pack_rows(_dot(act.astype(BF16), w2b_ref[...]) + b2_ref[0])

    @pl.when(n_valid == 0)
    def _():
        o_ref[...] = jnp.zeros_like(o_ref)


def _experts(xs, block_e, n_valid, w1, b1, w2, b2, layer):
    rows, dp = xs.shape
    d = D_MODEL
    n_blocks = rows // ROUTE_BLOCK
    e0 = layer * N_EXPERTS
    grid_spec = pltpu.PrefetchScalarGridSpec(
        num_scalar_prefetch=2,
        grid=(n_blocks,),
        in_specs=[
            pl.BlockSpec((ROUTE_BLOCK, dp), lambda i, be, nv: (i, 0)),
            pl.BlockSpec((1, d, 2 * D_FF), lambda i, be, nv: (e0 + be[i], 0, 0)),
            pl.BlockSpec((1, 1, 2 * D_FF), lambda i, be, nv: (e0 + be[i], 0, 0)),
            pl.BlockSpec((1, D_FF, d), lambda i, be, nv: (e0 + be[i], 0, 0)),
            pl.BlockSpec((1, 1, d), lambda i, be, nv: (e0 + be[i], 0, 0)),
        ],
        out_specs=pl.BlockSpec((ROUTE_BLOCK, dp), lambda i, be, nv: (i, 0)),
        scratch_shapes=[pltpu.VMEM((d, 2 * D_FF), BF16), pltpu.VMEM((D_FF, d), BF16)],
    )
    return pl.pallas_call(
        _expert_kernel,
        grid_spec=grid_spec,
        out_shape=jax.ShapeDtypeStruct((rows, dp), U32),
        compiler_params=_params("arbitrary"),
        name="experts",
    )(block_e, n_valid, xs, w1, b1, w2, b2)


def _combine_kernel(x_ref, yg_ref, topg_ref, gt_ref, lng_ref, lnb_ref, o_ref):
    g_t = topg_ref[...]
    g = jnp.concatenate([g_t, jnp.zeros((128 - TOP_K, g_t.shape[1]), F32)], axis=0).T
    y = g[:, 0:1] * _unpack_rows(yg_ref[0])
    for k in range(1, TOP_K):
        y = y + g[:, k:k + 1] * _unpack_rows(yg_ref[k])
    o_ref[...] = _deepnorm(x_ref[...], y, gt_ref[0], lng_ref[...], lnb_ref[...])


def _combine(x, yg, topg, mod3, ln_g, ln_b, seq):
    t, d = x.shape
    tm = TOKEN_TILE
    per_b = seq // tm
    return pl.pallas_call(
        _combine_kernel,
        grid=(t // tm,),
        in_specs=[
            pl.BlockSpec((tm, d), lambda i: (i, 0)),
            pl.BlockSpec((TOP_K, tm, PACKED_D), lambda i: (0, i, 0)),
            pl.BlockSpec((TOP_K, tm), lambda i: (0, i)),
            pl.BlockSpec((1, 1, d), lambda i: (i // per_b, 0, 5)),
            pl.BlockSpec((1, d), lambda i: (0, 0)),
            pl.BlockSpec((1, d), lambda i: (0, 0)),
        ],
        out_specs=pl.BlockSpec((tm, d), lambda i: (i, 0)),
        out_shape=jax.ShapeDtypeStruct((t, d), F32),
        compiler_params=_params("parallel"),
        name="combine",
    )(x, yg, topg, mod3, ln_g, ln_b)


def _rearranged_w_in(w_in):
    depth, d, _ = w_in.shape
    ab_end = SEG_AB + 2 * GDN_HEADS
    ab_pad = jnp.zeros((depth, d, AB_W - 2 * GDN_HEADS), w_in.dtype)
    return jnp.concatenate([w_in[:, :, :ab_end], ab_pad, w_in[:, :, ab_end:]], axis=2).astype(BF16)


def kernel(x, c, ada_w, ada_b, w_in, conv_w, a_log, dt_bias, gdn_norm_w, sinks, pool_w, pool_scale, w_pa, w_pb, w_pc, w_o, ln1_g, ln1_b, ln2_g, ln2_b, router_w, router_b, exp_w1, exp_b1, exp_w2, exp_b2):
    batch, seq, d = x.shape
    t = batch * seq
    n_blocks = (t * TOP_K + ROUTE_BLOCK - 1) // ROUTE_BLOCK + N_EXPERTS
    mod = _ada_mod(c, ada_w, ada_b)
    xt = x.reshape(t, d)
    lane_pad = jnp.zeros((AB_W - GDN_HEADS,), F32)
    rows = n_blocks * ROUTE_BLOCK
    w_in_all = _rearranged_w_in(w_in)
    w1_all = exp_w1.reshape(DEPTH * N_EXPERTS, d, 2 * D_FF)
    b1_all = exp_b1.reshape(DEPTH * N_EXPERTS, 1, 2 * D_FF)
    w2_all = exp_w2.reshape(DEPTH * N_EXPERTS, D_FF, d)
    b2_all = exp_b2.reshape(DEPTH * N_EXPERTS, 1, d)
    for l in range(DEPTH):
        mod3 = mod[l].reshape(batch, 1, 6 * d)
        qkv, z, ab, qb, kvb, uc, gates = _in_proj(xt, mod3, w_in_all, conv_w, seq, l)
        gpar = jnp.stack([jnp.concatenate([-jnp.exp(a_log[l]), lane_pad]), jnp.concatenate([dt_bias[l], lane_pad])])
        xt = _mixer(xt, qkv, ab, z, qb, kvb, uc, gates, mod3, gpar, gdn_norm_w[l].reshape(1, GDN_DV), sinks[l],
                    pool_w[l].astype(BF16), pool_scale[l].reshape(1, POOL_DIM),
                    w_pa[l].astype(BF16), w_pb[l].astype(BF16), w_pc[l].astype(BF16), w_o[l].astype(BF16),
                    ln1_g[l].reshape(1, d), ln1_b[l].reshape(1, d), batch, seq)
        hp, topi_t, topg_t, rank_t, cnt = _router(xt, mod3, router_w[l].T, router_b[l].reshape(N_EXPERTS, 1), seq)
        dest_t, block_e, n_valid = _dispatch_plan(topi_t, rank_t, cnt[:, 0], n_blocks)
        dest_flat = dest_t.reshape(TOP_K * t)
        xs = _sc_dispatch(hp, dest_flat, rows)
        ys = _experts(xs, block_e, n_valid, w1_all, b1_all, w2_all, b2_all, l)
        yg = _sc_gather(ys, dest_flat).reshape(TOP_K, t, PACKED_D)
        xt = _combine(xt, yg, topg_t, mod3, ln2_g[l].reshape(1, d), ln2_b[l].reshape(1, d), seq)
    return xt.reshape(batch, seq, d)
```

```python
import functools

import jax
import jax.numpy as jnp
from jax import lax
from jax.experimental import pallas as pl
from jax.experimental.pallas import tpu as pltpu
from jax.experimental.pallas import tpu_sc as plsc

D_MODEL = 1024
DEPTH = 4
GDN_HEADS = 4
GDN_DK = 128
GDN_DV = 128
GDN_CONV = 4
GDN_CHUNK = 64
SWA_HQ = 8
SWA_HKV = 2
SWA_DH = 64
WINDOW = 128
POOL_WINDOWS = (2, 4, 8, 16)
POOL_GROUPS = 4
POOL_GDIM = 128
N_BRANCH = 3
N_EXPERTS = 32
TOP_K = 4
D_FF = 1024
ROUTE_BLOCK = 512
SWIGLU_ALPHA = 1.702
SWIGLU_LIMIT = 7.0
LN_EPS = 1e-5
RMS_EPS = 1e-6
DEEPNORM_ALPHA = (2 * DEPTH) ** 0.25

GDN_QK = GDN_HEADS * GDN_DK
GDN_V = GDN_HEADS * GDN_DV
SWA_Q = SWA_HQ * SWA_DH
SWA_KV = SWA_HKV * SWA_DH
POOL_DIM = POOL_GROUPS * POOL_GDIM
GQA_GROUP = SWA_HQ // SWA_HKV
MAX_POOL = max(POOL_WINDOWS)

QKV_W = 2 * GDN_QK + GDN_V
AB_W = 128
SEG_QKV = 0
SEG_Z = SEG_QKV + QKV_W
SEG_AB = SEG_Z + GDN_V
SEG_QB = SEG_AB + AB_W
SEG_KVB = SEG_QB + SWA_Q
SEG_UC = SEG_KVB + 2 * SWA_KV
SEG_GATE = SEG_UC + POOL_DIM
PROJ_W = SEG_GATE + N_BRANCH * D_MODEL

PROJ_CHUNK = 256
TOKEN_TILE = 512
NEG_BIG = -1e30
VMEM_LIMIT = 56 * 1024 * 1024

PACKED_D = D_MODEL // 2
SC_CORES = 2
SC_SUBCORES = 16
SC_WORKERS = SC_CORES * SC_SUBCORES
SC_CHUNK = 64

F32 = jnp.float32
BF16 = jnp.bfloat16
U32 = jnp.uint32
HI_MASK = 0xFFFF0000


def _pack_rows(x):
    bits = pltpu.bitcast(x.astype(BF16).astype(F32), U32)
    return (bits[:, :PACKED_D] >> 16) | (bits[:, PACKED_D:] & jnp.uint32(HI_MASK))


def _unpack_rows(w):
    lo = pltpu.bitcast(w << 16, F32)
    hi = pltpu.bitcast(w & jnp.uint32(HI_MASK), F32)
    return jnp.concatenate([lo, hi], axis=1)


def _params(*sem):
    return pltpu.CompilerParams(dimension_semantics=sem, vmem_limit_bytes=VMEM_LIMIT)


def _sigmoid(x):
    return 0.5 * jnp.tanh(0.5 * x) + 0.5


def _layer_norm(x):
    mu = jnp.mean(x, -1, keepdims=True)
    xc = x - mu
    var = jnp.mean(xc * xc, -1, keepdims=True)
    return xc * lax.rsqrt(var + LN_EPS)


def _dot(a, b):
    return jnp.dot(a, b, preferred_element_type=F32)


def _dot_nt(a, b):
    return lax.dot_general(a, b, (((1,), (1,)), ((), ())), preferred_element_type=F32)


def _dot_tn(a, b):
    return lax.dot_general(a, b, (((0,), (0,)), ((), ())), preferred_element_type=F32)


def _bdot(a, b):
    return lax.dot_general(a, b, (((2,), (1,)), ((0,), (0,))), preferred_element_type=F32)


def _bdot_tn(a, b):
    return lax.dot_general(a, b, (((1,), (1,)), ((0,), (0,))), preferred_element_type=F32)


def _bdot_nt(a, b):
    return lax.dot_general(a, b, (((2,), (2,)), ((0,), (0,))), preferred_element_type=F32)


def _ada_kernel(c_ref, w_ref, b_ref, o_ref):
    c = c_ref[...]
    cond = c * _sigmoid(c)
    o_ref[0] = _dot(cond.astype(BF16), w_ref[0].astype(BF16)) + b_ref[0]


def _ada_mod(c, ada_w, ada_b):
    depth, d, n = ada_w.shape
    b = c.shape[0]
    tn = 1024
    return pl.pallas_call(
        _ada_kernel,
        grid=(depth, n // tn),
        in_specs=[
            pl.BlockSpec((b, d), lambda l, j: (0, 0)),
            pl.BlockSpec((1, d, tn), lambda l, j: (l, 0, j)),
            pl.BlockSpec((1, 1, tn), lambda l, j: (l, 0, j)),
        ],
        out_specs=pl.BlockSpec((1, b, tn), lambda l, j: (l, 0, j)),
        out_shape=jax.ShapeDtypeStruct((depth, b, n), F32),
        compiler_params=_params("parallel", "parallel"),
        name="ada_mod",
    )(c, ada_w, ada_b.reshape(depth, 1, n))


HEAD_UNITS = 4


def _moe_combine(x, yg_rows, gates_t, gt, ln_g, ln_b):
    g = jnp.concatenate([gates_t, jnp.zeros((128 - TOP_K, gates_t.shape[1]), F32)], axis=0).T
    y = g[:, 0:1] * _unpack_rows(yg_rows[0])
    for k in range(1, TOP_K):
        y = y + g[:, k:k + 1] * _unpack_rows(yg_rows[k])
    return _deepnorm(x, y, gt, ln_g, ln_b)


def _proj_kernel(*refs, tm, per_b, fused):
    if fused:
        (x_ref, yg_ref, topg_ref, gt_ref, lng_ref, lnb_ref, sh_ref, sc_ref, w_ref, convw_ref,
         xo_ref, qkv_ref, z_ref, ab_ref, qb_ref, kvb_ref, uc_ref, gate_ref, h_ref, raw_ref) = refs
    else:
        (x_ref, sh_ref, sc_ref, w_ref, convw_ref,
         qkv_ref, z_ref, ab_ref, qb_ref, kvb_ref, uc_ref, gate_ref, h_ref, raw_ref) = refs
    i = pl.program_id(0)
    slot = i % 2

    @pl.when(i == 0)
    def _():
        h_ref[1] = jnp.zeros(h_ref.shape[1:], BF16)

    rows_per_unit = tm // HEAD_UNITS

    def head_unit(u):
        r = slice(u * rows_per_unit, (u + 1) * rows_per_unit)
        x = x_ref[r, :]
        if fused:
            x = _moe_combine(x, [yg_ref[k, r, :] for k in range(TOP_K)], topg_ref[:, r], gt_ref[0],
                             lng_ref[...], lnb_ref[...])
            xo_ref[r, :] = x
        h_ref[slot, r, :] = (_layer_norm(x) * (1.0 + sc_ref[0]) + sh_ref[0]).astype(BF16)

    h = h_ref[1 - slot]

    def seg(start, width):
        return _dot(h, w_ref[:, start:start + width])

    seq_start = (i == 0) | ((i + per_b - 1) % per_b == 0)

    @pl.when(seq_start)
    def _():
        raw_ref[0:8, :] = jnp.zeros((8, QKV_W), F32)

    @pl.when(jnp.logical_not(seq_start))
    def _():
        raw_ref[0:8, :] = raw_ref[tm:tm + 8, :]

    for j in range(QKV_W // PROJ_CHUNK):
        cols = slice(j * PROJ_CHUNK, (j + 1) * PROJ_CHUNK)
        raw_ref[8:8 + tm, cols] = seg(SEG_QKV + j * PROJ_CHUNK, PROJ_CHUNK)
        if j < HEAD_UNITS:
            head_unit(j)

    def conv_silu(j):
        cols = slice(j * PROJ_CHUNK, (j + 1) * PROJ_CHUNK)
        conv = raw_ref[8:8 + tm, cols] * convw_ref[GDN_CONV - 1:GDN_CONV, cols]
        for tap in range(GDN_CONV - 1):
            conv = conv + raw_ref[5 + tap:5 + tap + tm, cols] * convw_ref[tap:tap + 1, cols]
        qkv_ref[:, cols] = (conv * _sigmoid(conv)).astype(BF16)
    n_conv = QKV_W // PROJ_CHUNK
    done = 0
    for ref, start, width in ((z_ref, SEG_Z, GDN_V), (qb_ref, SEG_QB, SWA_Q), (kvb_ref, SEG_KVB, 2 * SWA_KV),
                              (uc_ref, SEG_UC, POOL_DIM)):
        for j in range(width // PROJ_CHUNK):
            cols = slice(j * PROJ_CHUNK, (j + 1) * PROJ_CHUNK)
            ref[:, cols] = seg(start + j * PROJ_CHUNK, PROJ_CHUNK).astype(BF16)
            if done < n_conv:
                conv_silu(done)
                done += 1
    assert done == n_conv and HEAD_UNITS <= n_conv
    ab_ref[...] = seg(SEG_AB, AB_W)
    gate_ref[...] = _sigmoid(seg(SEG_GATE, N_BRANCH * D_MODEL)).astype(BF16)


def _in_proj(x, mod3, w, conv_w, seq, layer, moe=None):
    t, d = x.shape
    tm = TOKEN_TILE
    per_b = seq // tm
    n_tiles = t // tm
    widths = (QKV_W, GDN_V, AB_W, SWA_Q, 2 * SWA_KV, POOL_DIM, N_BRANCH * D_MODEL)
    dtypes = (BF16, BF16, F32, BF16, BF16, BF16, BF16)

    def head(i):
        return jnp.minimum(i, n_tiles - 1)

    def proj(i):
        return jnp.maximum(i - 1, 0)

    def mod_spec(k):
        return pl.BlockSpec((1, 1, d), lambda i: (head(i) // per_b, 0, k))

    in_specs = [pl.BlockSpec((tm, d), lambda i: (head(i), 0))]
    operands = [x]
    out_specs = [pl.BlockSpec((tm, n), lambda i: (proj(i), 0)) for n in widths]
    out_shape = [jax.ShapeDtypeStruct((t, n), dt) for n, dt in zip(widths, dtypes)]
    if moe is not None:
        yg, topg_t, mod3_prev, ln_g, ln_b = moe
        in_specs += [pl.BlockSpec((TOP_K, tm, PACKED_D), lambda i: (0, head(i), 0)),
                     pl.BlockSpec((TOP_K, tm), lambda i: (0, head(i))),
                     mod_spec(5),
                     pl.BlockSpec((1, d), lambda i: (0, 0)), pl.BlockSpec((1, d), lambda i: (0, 0))]
        operands += [yg, topg_t, mod3_prev, ln_g, ln_b]
        out_specs = [pl.BlockSpec((tm, d), lambda i: (head(i), 0))] + out_specs
        out_shape = [jax.ShapeDtypeStruct((t, d), F32)] + out_shape
    in_specs += [mod_spec(0), mod_spec(1),
                 pl.BlockSpec((None, d, PROJ_W), lambda i: (layer, 0, 0), pipeline_mode=pl.Buffered(1)),
                 pl.BlockSpec((None, GDN_CONV, QKV_W), lambda i: (layer, 0, 0))]
    operands += [mod3, mod3, w, conv_w]
    return pl.pallas_call(
        functools.partial(_proj_kernel, tm=tm, per_b=per_b, fused=moe is not None),
        grid=(n_tiles + 1,),
        in_specs=in_specs,
        out_specs=out_specs,
        out_shape=out_shape,
        scratch_shapes=[pltpu.VMEM((2, tm, d), BF16), pltpu.VMEM((8 + tm, QKV_W), F32)],
        compiler_params=_params("arbitrary"),
        name="in_proj",
    )(*operands)


def _softplus(x):
    return jnp.maximum(x, 0.0) + jnp.log(1.0 + jnp.exp(-jnp.abs(x)))


def _unit_lower_inverse(lm, row, col, between_levels=lambda: None):
    n = lm.shape[-1]
    eye = (row == col).astype(F32)
    t = None
    s = 1
    while s < n:
        join = ((row // (2 * s)) == (col // (2 * s))) & ((row % (2 * s)) >= s) & ((col % (2 * s)) < s)
        cm = jnp.where(join, lm, 0.0)
        if t is None:
            t = eye - cm
        else:
            p = _bdot(t, cm)
            t = t - _bdot(p, t)
            between_levels()
        s *= 2
    return t


def _gdn_kernel(qkv_ref, ab_ref, z_ref, gpar_ref, nw_ref, o_ref, state_ref, *, tb, filler=lambda: None):
    c_len = GDN_CHUNK
    nc = tb // c_len

    @pl.when(pl.program_id(1) == 0)
    def _():
        state_ref[...] = jnp.zeros_like(state_ref)

    ab = ab_ref[...]
    gpar = gpar_ref[...]
    g_all = gpar[0:1] * _softplus(ab + gpar[1:2])
    beta_all = _sigmoid(ab)
    pos = lax.broadcasted_iota(jnp.int32, (tb, AB_W), 0) % c_len
    gc_all = g_all
    s = 1
    while s < c_len:
        gc_all = gc_all + jnp.where(pos >= s, pltpu.roll(gc_all, s, axis=0), 0.0)
        s *= 2

    nh = GDN_HEADS
    nb = nc * nh
    row = lax.broadcasted_iota(jnp.int32, (nb, c_len, c_len), 1)
    col = lax.broadcasted_iota(jnp.int32, (nb, c_len, c_len), 2)
    causal = row >= col
    strict = row > col
    diag = row == col

    def chunked(per_head):
        return jnp.stack([per_head(h).reshape(nc, c_len, GDN_DK) for h in range(nh)], axis=1).reshape(nb, c_len, GDN_DK)

    q = chunked(lambda h: qkv_ref[:, h * GDN_DK:(h + 1) * GDN_DK].astype(F32))
    k = chunked(lambda h: qkv_ref[:, GDN_QK + h * GDN_DK:GDN_QK + (h + 1) * GDN_DK].astype(F32))
    v = chunked(lambda h: qkv_ref[:, 2 * GDN_QK + h * GDN_DV:2 * GDN_QK + (h + 1) * GDN_DV].astype(F32))
    gc = chunked(lambda h: jnp.broadcast_to(gc_all[:, h:h + 1], (tb, GDN_DK)))
    beta = chunked(lambda h: jnp.broadcast_to(beta_all[:, nh + h:nh + h + 1], (tb, GDN_DK)))
    q = q * lax.rsqrt(jnp.sum(q * q, -1, keepdims=True) + 1e-6) * (GDN_DK ** -0.5)
    k = k * lax.rsqrt(jnp.sum(k * k, -1, keepdims=True) + 1e-6)

    gc_i = gc[:, :, :c_len]
    gc_j = jnp.sum(jnp.where(diag, gc_i, 0.0), axis=1, keepdims=True)
    decay = jnp.where(causal, jnp.exp(jnp.where(causal, gc_i - gc_j, 0.0)), 0.0)
    eg = jnp.exp(gc)
    gc_last = gc[:, c_len - 1:c_len, :]
    g_last = jnp.exp(gc_last)
    k_beta = k * beta
    kq = _bdot_nt(jnp.concatenate([k_beta, q], axis=1).astype(BF16), k.astype(BF16))
    lower = jnp.where(strict, kq[:, :c_len] * decay, 0.0)
    attn = jnp.where(causal, kq[:, c_len:] * decay, 0.0).astype(BF16)
    tinv = _unit_lower_inverse(lower, row, col, filler)
    uw = _bdot(tinv.astype(BF16), jnp.concatenate([v * beta, k_beta * eg], axis=2).astype(BF16)).astype(BF16)
    q_g = q * eg
    k_g = (k * jnp.exp(gc_last - gc)).astype(BF16)

    k_uw = _bdot_tn(k_g, uw)
    a_uw = _bdot(attn, uw)
    ku = k_uw[:, :, :GDN_DV]
    o_loc = a_uw[:, :, :GDN_DV]
    qk_eff = jnp.concatenate([q_g - a_uw[:, :, GDN_DV:], k_uw[:, :, GDN_DV:]], axis=1).astype(BF16)

    st = state_ref[...]
    outs = []
    for c in range(nc):
        sl = slice(c * nh, (c + 1) * nh)
        prod = _bdot(qk_eff[sl], st.astype(BF16))
        outs.append(prod[:, :c_len] + o_loc[sl])
        st = st * g_last[sl] - prod[:, c_len:] + ku[sl]
        filler()
    state_ref[...] = st

    nw = nw_ref[...]
    for h in range(nh):
        o = jnp.concatenate([outs[c][h] for c in range(nc)], axis=0)
        zh = z_ref[:, h * GDN_DV:(h + 1) * GDN_DV].astype(F32)
        o = o * lax.rsqrt(jnp.mean(o * o, -1, keepdims=True) + RMS_EPS) * nw * (zh * _sigmoid(zh))
        o_ref[:, h * GDN_DV:(h + 1) * GDN_DV] = o.astype(BF16)


def _swa_units(sink_ref, q_ref, kv_ref, kvp_ref, bias_ref, o_ref, *, tq):
    nw = tq // WINDOW
    rows = GQA_GROUP * WINDOW
    scale = SWA_DH ** -0.5

    def unit(hk, w):
        heads = range(hk * GQA_GROUP, (hk + 1) * GQA_GROUP)
        tok = slice(w * WINDOW, (w + 1) * WINDOW)

        def keys(col0):
            cols = slice(col0, col0 + SWA_DH)
            prev = kvp_ref[:, cols] if w == 0 else kv_ref[(w - 1) * WINDOW:w * WINDOW, cols]
            return jnp.concatenate([prev, kv_ref[tok, cols]], axis=0)

        k = keys(hk * SWA_DH)
        v = keys(SWA_KV + hk * SWA_DH)
        q = jnp.concatenate([q_ref[tok, hq * SWA_DH:(hq + 1) * SWA_DH] for hq in heads], axis=0)
        sink = jnp.concatenate([jnp.full((WINDOW, 1), sink_ref[hq], F32) for hq in heads], axis=0)
        qi = lax.broadcasted_iota(jnp.int32, (rows, WINDOW), 0) % WINDOW
        kj = lax.broadcasted_iota(jnp.int32, (rows, WINDOW), 1)
        from_prev = kj > qi
        s2 = _dot_nt(q, k)
        s = jnp.where(from_prev, s2[:, :WINDOW], s2[:, WINDOW:]) * scale + bias_ref[hk]
        if w == 0:
            s = jnp.where(from_prev & (pl.program_id(1) == 0), NEG_BIG, s)
        m = jnp.maximum(jnp.max(s, -1, keepdims=True), sink)
        p = jnp.exp(s - m)
        p2 = jnp.concatenate([jnp.where(from_prev, p, 0.0), jnp.where(from_prev, 0.0, p)], axis=1).astype(BF16)
        v_ext = jnp.concatenate([v, jnp.ones((2 * WINDOW, SWA_DH), BF16)], axis=1)
        pv = _dot(p2, v_ext)
        o = pv[:, :SWA_DH] / (pv[:, SWA_DH:SWA_DH + 1] + jnp.exp(sink - m))
        for g, hq in enumerate(heads):
            o_ref[tok, hq * SWA_DH:(hq + 1) * SWA_DH] = o[g * WINDOW:(g + 1) * WINDOW, :].astype(BF16)

    return [functools.partial(unit, hk, w) for hk in range(SWA_HKV) for w in range(nw)]


def _swa_bias():
    qi = jnp.arange(WINDOW)[:, None]
    kj = jnp.arange(WINDOW)[None, :]
    dist = jnp.where(kj > qi, qi - kj + WINDOW, qi - kj).astype(F32)
    slopes = 2.0 ** (-8.0 * jnp.arange(1, SWA_HQ + 1, dtype=F32) / SWA_HQ)
    bias = -slopes[:, None, None] * dist[None]
    return bias.reshape(SWA_HKV, GQA_GROUP * WINDOW, WINDOW)


def _deepnorm(x, y, gt, g, b):
    return _layer_norm(DEEPNORM_ALPHA * x + (1.0 + gt) * y) * g + b


def _merge_kernel(x_ref, ya_ref, yb_ref, uc_ref, ucp_ref, gate_ref, gt_ref, poolw_ref, pscale_ref,
                  wpa_ref, wpb_ref, wpc_ref, wo_ref, lng_ref, lnb_ref, o_ref, *, tm):
    j = pl.program_id(1)
    u = uc_ref[...].astype(F32)
    halo = jnp.where(j == 0, 0.0, ucp_ref[...].astype(F32))
    ue = jnp.concatenate([halo, u], axis=0)
    tpos = (j * tm + lax.broadcasted_iota(jnp.int32, (tm, POOL_GDIM), 0) + 1).astype(F32)
    ycs = []
    for gi, win in enumerate(POOL_WINDOWS):
        a = ue[:, gi * POOL_GDIM:(gi + 1) * POOL_GDIM]
        span = 1
        while span < win:
            a = a[span:] + a[:-span]
            span *= 2
        lo = MAX_POOL - win + 1
        d = a[lo:lo + tm] / jnp.minimum(tpos, float(win)) - u[:, gi * POOL_GDIM:(gi + 1) * POOL_GDIM]
        ycs.append(_dot(d.astype(BF16), poolw_ref[gi]))
    yc = jnp.concatenate(ycs, axis=1) * pscale_ref[...]
    merged = gate_ref[:, 0:D_MODEL].astype(F32) * _dot(ya_ref[...], wpa_ref[...])
    merged = merged + gate_ref[:, D_MODEL:2 * D_MODEL].astype(F32) * _dot(yb_ref[...], wpb_ref[...])
    merged = merged + gate_ref[:, 2 * D_MODEL:3 * D_MODEL].astype(F32) * _dot(yc.astype(BF16), wpc_ref[...])
    y = _dot(merged.astype(BF16), wo_ref[...])
    o_ref[...] = _deepnorm(x_ref[...], y, gt_ref[0], lng_ref[...], lnb_ref[...])


def _mixer_kernel(sink_ref, x_ref, qkv_ref, ab_ref, z_ref, gpar_ref, nw_ref, qb_ref, kvb_ref, kvp_ref, bias_ref,
                  uc_ref, ucp_ref, gate_ref, gt_ref, poolw_ref, pscale_ref, wpa_ref, wpb_ref, wpc_ref, wo_ref,
                  lng_ref, lnb_ref, o_ref, state_ref, ya_ref, yb_ref, *, tm):
    pending = iter(_swa_units(sink_ref, qb_ref, kvb_ref, kvp_ref, bias_ref, yb_ref, tq=tm))

    def filler():
        unit = next(pending, None)
        if unit is not None:
            unit()

    _gdn_kernel(qkv_ref, ab_ref, z_ref, gpar_ref, nw_ref, ya_ref, state_ref, tb=tm, filler=filler)
    for unit in pending:
        unit()
    _merge_kernel(x_ref, ya_ref, yb_ref, uc_ref, ucp_ref, gate_ref, gt_ref, poolw_ref, pscale_ref,
                  wpa_ref, wpb_ref, wpc_ref, wo_ref, lng_ref, lnb_ref, o_ref, tm=tm)


def _mixer(x, qkv, ab, z, qb, kvb, uc, gates, mod3, gpar, norm_w, sinks, pool_w, pool_scale, w_pa, w_pb, w_pc, w_o,
           ln_g, ln_b, batch, seq):
    t, d = x.shape
    tm = TOKEN_TILE
    per_b = seq // tm

    def tok(n):
        return pl.BlockSpec((tm, n), lambda b, j: (b * per_b + j, 0))

    def prev_rows(rows, n):
        return pl.BlockSpec((rows, n), lambda b, j: (jnp.maximum((b * per_b + j) * (tm // rows) - 1, 0), 0))

    def full(shape):
        return pl.BlockSpec(shape, lambda b, j: (0,) * len(shape))

    return pl.pallas_call(
        functools.partial(_mixer_kernel, tm=tm),
        grid=(batch, per_b),
        in_specs=[
            pl.BlockSpec(memory_space=pltpu.SMEM),
            tok(d), tok(QKV_W), tok(AB_W), tok(GDN_V), full((2, AB_W)), full((1, GDN_DV)),
            tok(SWA_Q), tok(2 * SWA_KV), prev_rows(WINDOW, 2 * SWA_KV),
            full((SWA_HKV, GQA_GROUP * WINDOW, WINDOW)),
            tok(POOL_DIM), prev_rows(MAX_POOL, POOL_DIM),
            tok(N_BRANCH * d),
            pl.BlockSpec((1, 1, d), lambda b, j: (b, 0, 2)),
            full((POOL_GROUPS, POOL_GDIM, POOL_GDIM)), full((1, POOL_DIM)),
            full((GDN_V, d)), full((SWA_Q, d)), full((POOL_DIM, d)), full((d, d)),
            full((1, d)), full((1, d)),
        ],
        out_specs=tok(d),
        out_shape=jax.ShapeDtypeStruct((t, d), F32),
        scratch_shapes=[pltpu.VMEM((GDN_HEADS, GDN_DK, GDN_DV), F32), pltpu.VMEM((tm, GDN_V), BF16),
                        pltpu.VMEM((tm, SWA_Q), BF16)],
        compiler_params=_params("parallel", "arbitrary"),
        name="mixer",
    )(sinks, x, qkv, ab, z, gpar, norm_w, qb, kvb, kvb, _swa_bias(), uc, uc, gates, mod3, pool_w, pool_scale,
      w_pa, w_pb, w_pc, w_o, ln_g, ln_b)


def _router_kernel(x_ref, sh_ref, sc_ref, rwt_ref, rb_ref, tri_ref, hp_ref, topi_ref, topg_ref, rank_ref, cnt_ref,
                   run_ref):
    @pl.when(pl.program_id(0) == 0)
    def _():
        run_ref[...] = jnp.zeros_like(run_ref)

    h = _layer_norm(x_ref[...]) * (1.0 + sc_ref[0]) + sh_ref[0]
    hp_ref[...] = _pack_rows(h)
    logits = lax.dot_general(rwt_ref[...], h, (((1,), (1,)), ((), ())), preferred_element_type=F32,
                             precision=lax.Precision.HIGHEST) + rb_ref[...]
    sub = lax.broadcasted_iota(jnp.int32, logits.shape, 0)
    vals, idxs = [], []
    for _ in range(TOP_K):
        m = jnp.max(logits, 0, keepdims=True)
        idx = jnp.min(jnp.where(logits == m, sub, N_EXPERTS), 0, keepdims=True)
        vals.append(m)
        idxs.append(idx)
        logits = jnp.where(sub == idx, -jnp.inf, logits)
    es = [jnp.exp(v - vals[0]) for v in vals]
    denom = es[0] + es[1] + es[2] + es[3]
    topi_ref[...] = jnp.concatenate(idxs, axis=0)
    topg_ref[...] = jnp.concatenate([e / denom for e in es], axis=0)

    sel = jnp.zeros(logits.shape, F32)
    for idx in idxs:
        sel = sel + (sub == idx).astype(F32)
    before = run_ref[:, 0:1] + _dot(sel.astype(BF16), tri_ref[...])
    ranks = [jnp.sum(jnp.where(sub == idx, before, 0.0), 0, keepdims=True) for idx in idxs]
    rank_ref[...] = jnp.concatenate(ranks, axis=0).astype(jnp.int32)
    run_ref[...] = run_ref[...] + jnp.sum(sel, 1, keepdims=True)
    cnt_ref[...] = run_ref[...].astype(jnp.int32)


def _router(x, mod3, router_wt, router_b, seq):
    t, d = x.shape
    tm = TOKEN_TILE
    per_b = seq // tm
    tri = (jnp.arange(tm)[:, None] < jnp.arange(tm)[None, :]).astype(BF16)
    return pl.pallas_call(
        _router_kernel,
        grid=(t // tm,),
        in_specs=[
            pl.BlockSpec((tm, d), lambda i: (i, 0)),
            pl.BlockSpec((1, 1, d), lambda i: (i // per_b, 0, 3)),
            pl.BlockSpec((1, 1, d), lambda i: (i // per_b, 0, 4)),
            pl.BlockSpec((N_EXPERTS, d), lambda i: (0, 0)),
            pl.BlockSpec((N_EXPERTS, 1), lambda i: (0, 0)),
            pl.BlockSpec((tm, tm), lambda i: (0, 0)),
        ],
        out_specs=[
            pl.BlockSpec((tm, PACKED_D), lambda i: (i, 0)),
            pl.BlockSpec((TOP_K, tm), lambda i: (0, i)),
            pl.BlockSpec((TOP_K, tm), lambda i: (0, i)),
            pl.BlockSpec((TOP_K, tm), lambda i: (0, i)),
            pl.BlockSpec((N_EXPERTS, 128), lambda i: (0, 0)),
        ],
        out_shape=[
            jax.ShapeDtypeStruct((t, PACKED_D), U32),
            jax.ShapeDtypeStruct((TOP_K, t), jnp.int32),
            jax.ShapeDtypeStruct((TOP_K, t), F32),
            jax.ShapeDtypeStruct((TOP_K, t), jnp.int32),
            jax.ShapeDtypeStruct((N_EXPERTS, 128), jnp.int32),
        ],
        scratch_shapes=[pltpu.VMEM((N_EXPERTS, 128), F32)],
        compiler_params=_params("arbitrary"),
        name="router",
    )(x, mod3, mod3, router_wt, router_b, tri)


def _dispatch_plan(topi_t, rank_t, counts, n_blocks):
    padded = (counts + ROUTE_BLOCK - 1) // ROUTE_BLOCK * ROUTE_BLOCK
    ends_p = jnp.cumsum(padded)
    pstart = ends_p - padded
    experts = jnp.arange(N_EXPERTS, dtype=jnp.int32)
    pstart_tok = jnp.sum(jnp.where(topi_t[:, :, None] == experts, pstart, 0), -1)
    dest_t = (pstart_tok + rank_t).astype(jnp.int32)
    blk_start = jnp.arange(n_blocks, dtype=jnp.int32) * ROUTE_BLOCK
    block_e = jnp.minimum(jnp.sum((blk_start[:, None] >= ends_p).astype(jnp.int32), -1), N_EXPERTS - 1)
    blk_end = jnp.sum(jnp.where(block_e[:, None] == experts, pstart + counts, 0), -1)
    n_valid = jnp.clip(blk_end - blk_start, 0, ROUTE_BLOCK).astype(jnp.int32)
    return dest_t, block_e.astype(jnp.int32), n_valid


def _sc_mesh():
    return plsc.VectorSubcoreMesh(core_axis_name="c", subcore_axis_name="s")


def _sc_dispatch(hp, dest_flat, rows):
    t, dp = hp.shape
    per_w = t // SC_WORKERS
    n_chunks = per_w // SC_CHUNK

    assert n_chunks % 2 == 0

    @functools.partial(
        pl.kernel, mesh=_sc_mesh(), out_type=jax.ShapeDtypeStruct((rows, dp), hp.dtype),
        scratch_types=[pltpu.VMEM((TOP_K * n_chunks, SC_CHUNK), jnp.int32),
                       pltpu.VMEM((SC_CHUNK, dp), hp.dtype), pltpu.VMEM((SC_CHUNK, dp), hp.dtype),
                       pltpu.SemaphoreType.DMA, pltpu.SemaphoreType.DMA,
                       pltpu.SemaphoreType.DMA, pltpu.SemaphoreType.DMA])
    def dispatch(h_hbm, d_hbm, xs_hbm, idx_v, rows_a, rows_b, load_a, load_b, scat_a, scat_b):
        worker = lax.axis_index("s") * SC_CORES + lax.axis_index("c")
        base = worker * per_w
        for k in range(TOP_K):
            pltpu.sync_copy(d_hbm.at[pl.ds((k * SC_WORKERS + worker) * n_chunks, n_chunks)],
                            idx_v.at[pl.ds(k * n_chunks, n_chunks)])

        def load(i, buf, sem):
            return pltpu.make_async_copy(h_hbm.at[pl.ds(base + i * SC_CHUNK, SC_CHUNK)], buf, sem)

        def scatter(i, k, buf, sem):
            return pltpu.make_async_copy(buf, xs_hbm.at[idx_v.at[k * n_chunks + i]], sem)

        def scatter_all(i, buf, sem):
            for k in range(TOP_K):
                scatter(i, k, buf, sem).start()
            for k in range(TOP_K):
                scatter(i, k, buf, sem).wait()

        load(0, rows_a, load_a).start()

        @pl.loop(0, n_chunks, step=2)
        def _(i):
            load(i, rows_a, load_a).wait()
            load(i + 1, rows_b, load_b).start()
            scatter_all(i, rows_a, scat_a)
            load(i + 1, rows_b, load_b).wait()

            @pl.when(i + 2 < n_chunks)
            def _():
                load(i + 2, rows_a, load_a).start()

            scatter_all(i + 1, rows_b, scat_b)

    return dispatch(hp, dest_flat.reshape(TOP_K * SC_WORKERS * n_chunks, SC_CHUNK))


def _sc_gather(table, idx):
    n = idx.shape[0]
    dp = table.shape[1]
    per_w = n // SC_WORKERS
    n_chunks = per_w // SC_CHUNK

    assert n_chunks % 2 == 0

    @functools.partial(
        pl.kernel, mesh=_sc_mesh(), out_type=jax.ShapeDtypeStruct((n, dp), table.dtype),
        scratch_types=[pltpu.VMEM((n_chunks, SC_CHUNK), jnp.int32),
                       pltpu.VMEM((SC_CHUNK, dp), table.dtype), pltpu.VMEM((SC_CHUNK, dp), table.dtype),
                       pltpu.SemaphoreType.DMA, pltpu.SemaphoreType.DMA,
                       pltpu.SemaphoreType.DMA, pltpu.SemaphoreType.DMA])
    def gather(t_hbm, i_hbm, o_hbm, idx_v, rows_a, rows_b, fetch_a, fetch_b, store_a, store_b):
        worker = lax.axis_index("s") * SC_CORES + lax.axis_index("c")
        base = worker * per_w
        pltpu.sync_copy(i_hbm.at[pl.ds(worker * n_chunks, n_chunks)], idx_v)

        def fetch(i, buf, sem):
            return pltpu.make_async_copy(t_hbm.at[idx_v.at[i]], buf, sem)

        def store(i, buf, sem):
            return pltpu.make_async_copy(buf, o_hbm.at[pl.ds(base + i * SC_CHUNK, SC_CHUNK)], sem)

        fetch(0, rows_a, fetch_a).start()

        @pl.loop(0, n_chunks, step=2)
        def _(i):
            fetch(i, rows_a, fetch_a).wait()

            @pl.when(i > 0)
            def _():
                store(i - 1, rows_b, store_b).wait()

            fetch(i + 1, rows_b, fetch_b).start()
            store(i, rows_a, store_a).start()
            fetch(i + 1, rows_b, fetch_b).wait()
            store(i, rows_a, store_a).wait()

            @pl.when(i + 2 < n_chunks)
            def _():
                fetch(i + 2, rows_a, fetch_a).start()

            store(i + 1, rows_b, store_b).start()

        store(n_chunks - 1, rows_b, store_b).wait()

    return gather(table, idx.reshape(SC_WORKERS * n_chunks, SC_CHUNK))


def _expert_kernel(be_ref, nv_ref, x_ref, w1_ref, b1_ref, w2_ref, b2_ref, o_ref, w1b_ref, w2b_ref):
    i = pl.program_id(0)
    n_valid = nv_ref[i]
    new_expert = (i == 0) | (be_ref[i] != be_ref[jnp.maximum(i - 1, 0)])

    @pl.when(new_expert & (n_valid > 0))
    def _():
        w1b_ref[...] = w1_ref[0].astype(BF16)
        w2b_ref[...] = w2_ref[0].astype(BF16)

    @pl.when(n_valid > 0)
    def _():
        live = lax.broadcasted_iota(jnp.int32, (ROUTE_BLOCK, D_MODEL), 0) < n_valid
        x = jnp.where(live, _unpack_rows(x_ref[...]), 0.0).astype(BF16)
        gu = _dot(x, w1b_ref[...]) + b1_ref[0]
        glu = jnp.minimum(gu[:, :D_FF], SWIGLU_LIMIT)
        lin = jnp.clip(gu[:, D_FF:], -SWIGLU_LIMIT, SWIGLU_LIMIT)
        act = glu * _sigmoid(SWIGLU_ALPHA * glu) * (lin + 1.0)
        o_ref[...] = _pack_rows(_dot(act.astype(BF16), w2b_ref[...]) + b2_ref[0])

    @pl.when(n_valid == 0)
    def _():
        o_ref[...] = jnp.zeros_like(o_ref)


def _experts(xs, block_e, n_valid, w1, b1, w2, b2, layer):
    rows, dp = xs.shape
    d = D_MODEL
    n_blocks = rows // ROUTE_BLOCK
    e0 = layer * N_EXPERTS
    grid_spec = pltpu.PrefetchScalarGridSpec(
        num_scalar_prefetch=2,
        grid=(n_blocks,),
        in_specs=[
            pl.BlockSpec((ROUTE_BLOCK, dp), lambda i, be, nv: (i, 0)),
            pl.BlockSpec((1, d, 2 * D_FF), lambda i, be, nv: (e0 + be[i], 0, 0)),
            pl.BlockSpec((1, 1, 2 * D_FF), lambda i, be, nv: (e0 + be[i], 0, 0)),
            pl.BlockSpec((1, D_FF, d), lambda i, be, nv: (e0 + be[i], 0, 0)),
            pl.BlockSpec((1, 1, d), lambda i, be, nv: (e0 + be[i], 0, 0)),
        ],
        out_specs=pl.BlockSpec((ROUTE_BLOCK, dp), lambda i, be, nv: (i, 0)),
        scratch_shapes=[pltpu.VMEM((d, 2 * D_FF), BF16), pltpu.VMEM((D_FF, d), BF16)],
    )
    return pl.pallas_call(
        _expert_kernel,
        grid_spec=grid_spec,
        out_shape=jax.ShapeDtypeStruct((rows, dp), U32),
        compiler_params=_params("arbitrary"),
        name="experts",
    )(block_e, n_valid, xs, w1, b1, w2, b2)


def _combine_kernel(x_ref, yg_ref, topg_ref, gt_ref, lng_ref, lnb_ref, o_ref):
    o_ref[...] = _moe_combine(x_ref[...], [yg_ref[k] for k in range(TOP_K)], topg_ref[...], gt_ref[0],
                              lng_ref[...], lnb_ref[...])


def _combine(x, yg, topg, mod3, ln_g, ln_b, seq):
    t, d = x.shape
    tm = TOKEN_TILE
    per_b = seq // tm
    return pl.pallas_call(
        _combine_kernel,
        grid=(t // tm,),
        in_specs=[
            pl.BlockSpec((tm, d), lambda i: (i, 0)),
            pl.BlockSpec((TOP_K, tm, PACKED_D), lambda i: (0, i, 0)),
            pl.BlockSpec((TOP_K, tm), lambda i: (0, i)),
            pl.BlockSpec((1, 1, d), lambda i: (i // per_b, 0, 5)),
            pl.BlockSpec((1, d), lambda i: (0, 0)),
            pl.BlockSpec((1, d), lambda i: (0, 0)),
        ],
        out_specs=pl.BlockSpec((tm, d), lambda i: (i, 0)),
        out_shape=jax.ShapeDtypeStruct((t, d), F32),
        compiler_params=_params("parallel"),
        name="combine",
    )(x, yg, topg, mod3, ln_g, ln_b)


def _rearranged_w_in(w_in):
    depth, d, _ = w_in.shape
    ab_end = SEG_AB + 2 * GDN_HEADS
    ab_pad = jnp.zeros((depth, d, AB_W - 2 * GDN_HEADS), w_in.dtype)
    return jnp.concatenate([w_in[:, :, :ab_end], ab_pad, w_in[:, :, ab_end:]], axis=2).astype(BF16)


def kernel(x, c, ada_w, ada_b, w_in, conv_w, a_log, dt_bias, gdn_norm_w, sinks, pool_w, pool_scale, w_pa, w_pb, w_pc, w_o, ln1_g, ln1_b, ln2_g, ln2_b, router_w, router_b, exp_w1, exp_b1, exp_w2, exp_b2):
    batch, seq, d = x.shape
    t = batch * seq
    n_blocks = (t * TOP_K + ROUTE_BLOCK - 1) // ROUTE_BLOCK + N_EXPERTS
    mod = _ada_mod(c, ada_w, ada_b)
    xt = x.reshape(t, d)
    lane_pad = jnp.zeros((AB_W - GDN_HEADS,), F32)
    rows = n_blocks * ROUTE_BLOCK
    w_in_all = _rearranged_w_in(w_in)
    w1_all = exp_w1.reshape(DEPTH * N_EXPERTS, d, 2 * D_FF)
    b1_all = exp_b1.reshape(DEPTH * N_EXPERTS, 1, 2 * D_FF)
    w2_all = exp_w2.reshape(DEPTH * N_EXPERTS, D_FF, d)
    b2_all = exp_b2.reshape(DEPTH * N_EXPERTS, 1, d)
    moe = None
    for l in range(DEPTH):
        mod3 = mod[l].reshape(batch, 1, 6 * d)
        if moe is None:
            qkv, z, ab, qb, kvb, uc, gates = _in_proj(xt, mod3, w_in_all, conv_w, seq, l)
        else:
            xt, qkv, z, ab, qb, kvb, uc, gates = _in_proj(xt, mod3, w_in_all, conv_w, seq, l, moe)
        gpar = jnp.stack([jnp.concatenate([-jnp.exp(a_log[l]), lane_pad]), jnp.concatenate([dt_bias[l], lane_pad])])
        xt = _mixer(xt, qkv, ab, z, qb, kvb, uc, gates, mod3, gpar, gdn_norm_w[l].reshape(1, GDN_DV), sinks[l],
                    pool_w[l].astype(BF16), pool_scale[l].reshape(1, POOL_DIM),
                    w_pa[l].astype(BF16), w_pb[l].astype(BF16), w_pc[l].astype(BF16), w_o[l].astype(BF16),
                    ln1_g[l].reshape(1, d), ln1_b[l].reshape(1, d), batch, seq)
        hp, topi_t, topg_t, rank_t, cnt = _router(xt, mod3, router_w[l].T, router_b[l].reshape(N_EXPERTS, 1), seq)
        dest_t, block_e, n_valid = _dispatch_plan(topi_t, rank_t, cnt[:, 0], n_blocks)
        dest_flat = dest_t.reshape(TOP_K * t)
        xs = _sc_dispatch(hp, dest_flat, rows)
        ys = _experts(xs, block_e, n_valid, w1_all, b1_all, w2_all, b2_all, l)
        yg = _sc_gather(ys, dest_flat).reshape(TOP_K, t, PACKED_D)
        moe = (yg, topg_t, mod3, ln2_g[l].reshape(1, d), ln2_b[l].reshape(1, d))
    xt = _combine(xt, *moe, seq)
    return xt.reshape(batch, seq, d)
```

```python
import functools

import jax
import jax.numpy as jnp
from jax import lax
from jax.experimental import pallas as pl
from jax.experimental.pallas import tpu as pltpu
from jax.experimental.pallas import tpu_sc as plsc

D_MODEL = 1024
DEPTH = 4
GDN_HEADS = 4
GDN_DK = 128
GDN_DV = 128
GDN_CONV = 4
GDN_CHUNK = 64
SWA_HQ = 8
SWA_HKV = 2
SWA_DH = 64
WINDOW = 128
POOL_WINDOWS = (2, 4, 8, 16)
POOL_GROUPS = 4
POOL_GDIM = 128
N_BRANCH = 3
N_EXPERTS = 32
TOP_K = 4
D_FF = 1024
ROUTE_BLOCK = 512
SWIGLU_ALPHA = 1.702
SWIGLU_LIMIT = 7.0
LN_EPS = 1e-5
RMS_EPS = 1e-6
DEEPNORM_ALPHA = (2 * DEPTH) ** 0.25

GDN_QK = GDN_HEADS * GDN_DK
GDN_V = GDN_HEADS * GDN_DV
SWA_Q = SWA_HQ * SWA_DH
SWA_KV = SWA_HKV * SWA_DH
POOL_DIM = POOL_GROUPS * POOL_GDIM
GQA_GROUP = SWA_HQ // SWA_HKV
MAX_POOL = max(POOL_WINDOWS)

QKV_W = 2 * GDN_QK + GDN_V
AB_W = 128
SEG_QKV = 0
SEG_Z = SEG_QKV + QKV_W
SEG_AB = SEG_Z + GDN_V
SEG_QB = SEG_AB + AB_W
SEG_KVB = SEG_QB + SWA_Q
SEG_UC = SEG_KVB + 2 * SWA_KV
SEG_GATE = SEG_UC + POOL_DIM
PROJ_W = SEG_GATE + N_BRANCH * D_MODEL

PROJ_CHUNK = 256
TOKEN_TILE = 512
NEG_BIG = -1e30
VMEM_LIMIT = 56 * 1024 * 1024

PACKED_D = D_MODEL // 2
SC_CORES = 2
SC_SUBCORES = 16
SC_WORKERS = SC_CORES * SC_SUBCORES
SC_CHUNK = 64

F32 = jnp.float32
BF16 = jnp.bfloat16
U32 = jnp.uint32
HI_MASK = 0xFFFF0000


def _pack_rows(x):
    bits = pltpu.bitcast(x.astype(BF16).astype(F32), U32)
    return (bits[:, :PACKED_D] >> 16) | (bits[:, PACKED_D:] & jnp.uint32(HI_MASK))


def _unpack_rows(w):
    lo = pltpu.bitcast(w << 16, F32)
    hi = pltpu.bitcast(w & jnp.uint32(HI_MASK), F32)
    return jnp.concatenate([lo, hi], axis=1)


def _params(*sem):
    return pltpu.CompilerParams(dimension_semantics=sem, vmem_limit_bytes=VMEM_LIMIT)


def _sigmoid(x):
    return 0.5 * jnp.tanh(0.5 * x) + 0.5


def _layer_norm(x):
    mu = jnp.mean(x, -1, keepdims=True)
    xc = x - mu
    var = jnp.mean(xc * xc, -1, keepdims=True)
    return xc * lax.rsqrt(var + LN_EPS)


def _dot(a, b):
    return jnp.dot(a, b, preferred_element_type=F32)


def _dot_nt(a, b):
    return lax.dot_general(a, b, (((1,), (1,)), ((), ())), preferred_element_type=F32)


def _dot_tn(a, b):
    return lax.dot_general(a, b, (((0,), (0,)), ((), ())), preferred_element_type=F32)


def _bdot(a, b):
    return lax.dot_general(a, b, (((2,), (1,)), ((0,), (0,))), preferred_element_type=F32)


def _bdot_tn(a, b):
    return lax.dot_general(a, b, (((1,), (1,)), ((0,), (0,))), preferred_element_type=F32)


def _bdot_nt(a, b):
    return lax.dot_general(a, b, (((2,), (2,)), ((0,), (0,))), preferred_element_type=F32)


def _ada_kernel(c_ref, w_ref, b_ref, o_ref):
    c = c_ref[...]
    cond = c * _sigmoid(c)
    o_ref[0] = _dot(cond.astype(BF16), w_ref[0].astype(BF16)) + b_ref[0]


def _ada_mod(c, ada_w, ada_b):
    depth, d, n = ada_w.shape
    b = c.shape[0]
    tn = 1024
    return pl.pallas_call(
        _ada_kernel,
        grid=(depth, n // tn),
        in_specs=[
            pl.BlockSpec((b, d), lambda l, j: (0, 0)),
            pl.BlockSpec((1, d, tn), lambda l, j: (l, 0, j)),
            pl.BlockSpec((1, 1, tn), lambda l, j: (l, 0, j)),
        ],
        out_specs=pl.BlockSpec((1, b, tn), lambda l, j: (l, 0, j)),
        out_shape=jax.ShapeDtypeStruct((depth, b, n), F32),
        compiler_params=_params("parallel", "parallel"),
        name="ada_mod",
    )(c, ada_w, ada_b.reshape(depth, 1, n))


HEAD_UNITS = 4


def _moe_combine(x, yg_rows, gates_t, gt, ln_g, ln_b):
    g = jnp.concatenate([gates_t, jnp.zeros((128 - TOP_K, gates_t.shape[1]), F32)], axis=0).T
    y = g[:, 0:1] * _unpack_rows(yg_rows[0])
    for k in range(1, TOP_K):
        y = y + g[:, k:k + 1] * _unpack_rows(yg_rows[k])
    return _deepnorm(x, y, gt, ln_g, ln_b)


def _proj_kernel(*refs, tm, per_b, fused):
    if fused:
        (x_ref, yg_ref, topg_ref, gt_ref, lng_ref, lnb_ref, sh_ref, sc_ref, w_ref, convw_ref,
         xo_ref, qkv_ref, z_ref, ab_ref, qb_ref, kvb_ref, uc_ref, gate_ref, h_ref, raw_ref) = refs
    else:
        (x_ref, sh_ref, sc_ref, w_ref, convw_ref,
         qkv_ref, z_ref, ab_ref, qb_ref, kvb_ref, uc_ref, gate_ref, h_ref, raw_ref) = refs
    i = pl.program_id(0)
    slot = i % 2

    @pl.when(i == 0)
    def _():
        h_ref[1] = jnp.zeros(h_ref.shape[1:], BF16)

    rows_per_unit = tm // HEAD_UNITS

    def head_unit(u):
        r = slice(u * rows_per_unit, (u + 1) * rows_per_unit)
        x = x_ref[r, :]
        if fused:
            x = _moe_combine(x, [yg_ref[k, r, :] for k in range(TOP_K)], topg_ref[:, r], gt_ref[0],
                             lng_ref[...], lnb_ref[...])
            xo_ref[r, :] = x
        h_ref[slot, r, :] = (_layer_norm(x) * (1.0 + sc_ref[0]) + sh_ref[0]).astype(BF16)

    h = h_ref[1 - slot]

    def seg(start, width):
        return _dot(h, w_ref[:, start:start + width])

    seq_start = (i == 0) | ((i + per_b - 1) % per_b == 0)

    @pl.when(seq_start)
    def _():
        raw_ref[0:8, :] = jnp.zeros((8, QKV_W), F32)

    @pl.when(jnp.logical_not(seq_start))
    def _():
        raw_ref[0:8, :] = raw_ref[tm:tm + 8, :]

    for j in range(QKV_W // PROJ_CHUNK):
        cols = slice(j * PROJ_CHUNK, (j + 1) * PROJ_CHUNK)
        raw_ref[8:8 + tm, cols] = seg(SEG_QKV + j * PROJ_CHUNK, PROJ_CHUNK)
        if j < HEAD_UNITS:
            head_unit(j)

    def conv_silu(j):
        cols = slice(j * PROJ_CHUNK, (j + 1) * PROJ_CHUNK)
        conv = raw_ref[8:8 + tm, cols] * convw_ref[GDN_CONV - 1:GDN_CONV, cols]
        for tap in range(GDN_CONV - 1):
            conv = conv + raw_ref[5 + tap:5 + tap + tm, cols] * convw_ref[tap:tap + 1, cols]
        qkv_ref[:, cols] = (conv * _sigmoid(conv)).astype(BF16)
    n_conv = QKV_W // PROJ_CHUNK
    done = 0
    for ref, start, width in ((z_ref, SEG_Z, GDN_V), (qb_ref, SEG_QB, SWA_Q), (kvb_ref, SEG_KVB, 2 * SWA_KV),
                              (uc_ref, SEG_UC, POOL_DIM)):
        for j in range(width // PROJ_CHUNK):
            cols = slice(j * PROJ_CHUNK, (j + 1) * PROJ_CHUNK)
            ref[:, cols] = seg(start + j * PROJ_CHUNK, PROJ_CHUNK).astype(BF16)
            if done < n_conv:
                conv_silu(done)
                done += 1
    assert done == n_conv and HEAD_UNITS <= n_conv
    ab_ref[...] = seg(SEG_AB, AB_W)
    gate_ref[...] = _sigmoid(seg(SEG_GATE, N_BRANCH * D_MODEL)).astype(BF16)


def _in_proj(x, mod3, w, conv_w, seq, layer, moe=None):
    t, d = x.shape
    tm = TOKEN_TILE
    per_b = seq // tm
    n_tiles = t // tm
    widths = (QKV_W, GDN_V, AB_W, SWA_Q, 2 * SWA_KV, POOL_DIM, N_BRANCH * D_MODEL)
    dtypes = (BF16, BF16, F32, BF16, BF16, BF16, BF16)

    def head(i):
        return jnp.minimum(i, n_tiles - 1)

    def proj(i):
        return jnp.maximum(i - 1, 0)

    def mod_spec(k):
        return pl.BlockSpec((1, 1, d), lambda i: (head(i) // per_b, 0, k))

    in_specs = [pl.BlockSpec((tm, d), lambda i: (head(i), 0))]
    operands = [x]
    out_specs = [pl.BlockSpec((tm, n), lambda i: (proj(i), 0)) for n in widths]
    out_shape = [jax.ShapeDtypeStruct((t, n), dt) for n, dt in zip(widths, dtypes)]
    if moe is not None:
        yg, topg_t, mod3_prev, ln_g, ln_b = moe
        in_specs += [pl.BlockSpec((TOP_K, tm, PACKED_D), lambda i: (0, head(i), 0)),
                     pl.BlockSpec((TOP_K, tm), lambda i: (0, head(i))),
                     mod_spec(5),
                     pl.BlockSpec((1, d), lambda i: (0, 0)), pl.BlockSpec((1, d), lambda i: (0, 0))]
        operands += [yg, topg_t, mod3_prev, ln_g, ln_b]
        out_specs = [pl.BlockSpec((tm, d), lambda i: (head(i), 0))] + out_specs
        out_shape = [jax.ShapeDtypeStruct((t, d), F32)] + out_shape
    in_specs += [mod_spec(0), mod_spec(1),
                 pl.BlockSpec((None, d, PROJ_W), lambda i: (layer, 0, 0), pipeline_mode=pl.Buffered(1)),
                 pl.BlockSpec((None, GDN_CONV, QKV_W), lambda i: (layer, 0, 0))]
    operands += [mod3, mod3, w, conv_w]
    return pl.pallas_call(
        functools.partial(_proj_kernel, tm=tm, per_b=per_b, fused=moe is not None),
        grid=(n_tiles + 1,),
        in_specs=in_specs,
        out_specs=out_specs,
        out_shape=out_shape,
        scratch_shapes=[pltpu.VMEM((2, tm, d), BF16), pltpu.VMEM((8 + tm, QKV_W), F32)],
        compiler_params=_params("arbitrary"),
        name="in_proj",
    )(*operands)


def _softplus(x):
    return jnp.maximum(x, 0.0) + jnp.log(1.0 + jnp.exp(-jnp.abs(x)))


def _unit_lower_inverse(lm, row, col, between_levels=lambda: None):
    n = lm.shape[-1]
    eye = (row == col).astype(F32)
    t = None
    s = 1
    while s < n:
        join = ((row // (2 * s)) == (col // (2 * s))) & ((row % (2 * s)) >= s) & ((col % (2 * s)) < s)
        cm = jnp.where(join, lm, 0.0)
        if t is None:
            t = eye - cm
        else:
            p = _bdot(t, cm)
            t = t - _bdot(p, t)
            between_levels()
        s *= 2
    return t


def _gdn_kernel(qkv_ref, ab_ref, z_ref, gpar_ref, nw_ref, o_ref, state_ref, *, tb, filler=lambda: None):
    c_len = GDN_CHUNK
    nc = tb // c_len

    @pl.when(pl.program_id(1) == 0)
    def _():
        state_ref[...] = jnp.zeros_like(state_ref)

    ab = ab_ref[...]
    gpar = gpar_ref[...]
    g_all = gpar[0:1] * _softplus(ab + gpar[1:2])
    beta_all = _sigmoid(ab)
    pos = lax.broadcasted_iota(jnp.int32, (tb, AB_W), 0) % c_len
    gc_all = g_all
    s = 1
    while s < c_len:
        gc_all = gc_all + jnp.where(pos >= s, pltpu.roll(gc_all, s, axis=0), 0.0)
        s *= 2

    nh = GDN_HEADS
    nb = nc * nh
    row = lax.broadcasted_iota(jnp.int32, (nb, c_len, c_len), 1)
    col = lax.broadcasted_iota(jnp.int32, (nb, c_len, c_len), 2)
    causal = row >= col
    strict = row > col
    diag = row == col

    def chunked(per_head):
        return jnp.stack([per_head(h).reshape(nc, c_len, GDN_DK) for h in range(nh)], axis=1).reshape(nb, c_len, GDN_DK)

    q = chunked(lambda h: qkv_ref[:, h * GDN_DK:(h + 1) * GDN_DK].astype(F32))
    k = chunked(lambda h: qkv_ref[:, GDN_QK + h * GDN_DK:GDN_QK + (h + 1) * GDN_DK].astype(F32))
    v = chunked(lambda h: qkv_ref[:, 2 * GDN_QK + h * GDN_DV:2 * GDN_QK + (h + 1) * GDN_DV].astype(F32))
    gc = chunked(lambda h: jnp.broadcast_to(gc_all[:, h:h + 1], (tb, GDN_DK)))
    beta = chunked(lambda h: jnp.broadcast_to(beta_all[:, nh + h:nh + h + 1], (tb, GDN_DK)))
    q = q * lax.rsqrt(jnp.sum(q * q, -1, keepdims=True) + 1e-6) * (GDN_DK ** -0.5)
    k = k * lax.rsqrt(jnp.sum(k * k, -1, keepdims=True) + 1e-6)

    gc_i = gc[:, :, :c_len]
    gc_j = jnp.sum(jnp.where(diag, gc_i, 0.0), axis=1, keepdims=True)
    decay = jnp.where(causal, jnp.exp(jnp.where(causal, gc_i - gc_j, 0.0)), 0.0)
    eg = jnp.exp(gc)
    gc_last = gc[:, c_len - 1:c_len, :]
    g_last = jnp.exp(gc_last)
    k_beta = k * beta
    kq = _bdot_nt(jnp.concatenate([k_beta, q], axis=1).astype(BF16), k.astype(BF16))
    lower = jnp.where(strict, kq[:, :c_len] * decay, 0.0)
    attn = jnp.where(causal, kq[:, c_len:] * decay, 0.0).astype(BF16)
    tinv = _unit_lower_inverse(lower, row, col, filler)
    uw = _bdot(tinv.astype(BF16), jnp.concatenate([v * beta, k_beta * eg], axis=2).astype(BF16)).astype(BF16)
    q_g = q * eg
    k_g = (k * jnp.exp(gc_last - gc)).astype(BF16)

    k_uw = _bdot_tn(k_g, uw)
    a_uw = _bdot(attn, uw)
    ku = k_uw[:, :, :GDN_DV]
    o_loc = a_uw[:, :, :GDN_DV]
    qk_eff = jnp.concatenate([q_g - a_uw[:, :, GDN_DV:], k_uw[:, :, GDN_DV:]], axis=1).astype(BF16)

    st = state_ref[...]
    outs = []
    for c in range(nc):
        sl = slice(c * nh, (c + 1) * nh)
        prod = _bdot(qk_eff[sl], st.astype(BF16))
        outs.append(prod[:, :c_len] + o_loc[sl])
        st = st * g_last[sl] - prod[:, c_len:] + ku[sl]
        filler()
    state_ref[...] = st

    nw = nw_ref[...]
    for h in range(nh):
        o = jnp.concatenate([outs[c][h] for c in range(nc)], axis=0)
        zh = z_ref[:, h * GDN_DV:(h + 1) * GDN_DV].astype(F32)
        o = o * lax.rsqrt(jnp.mean(o * o, -1, keepdims=True) + RMS_EPS) * nw * (zh * _sigmoid(zh))
        o_ref[:, h * GDN_DV:(h + 1) * GDN_DV] = o.astype(BF16)


def _swa_units(sink_ref, q_ref, kv_ref, kvp_ref, bias_ref, o_ref, *, tq):
    nw = tq // WINDOW
    rows = GQA_GROUP * WINDOW
    scale = SWA_DH ** -0.5

    def unit(hk, w):
        heads = range(hk * GQA_GROUP, (hk + 1) * GQA_GROUP)
        tok = slice(w * WINDOW, (w + 1) * WINDOW)

        def keys(col0):
            cols = slice(col0, col0 + SWA_DH)
            prev = kvp_ref[:, cols] if w == 0 else kv_ref[(w - 1) * WINDOW:w * WINDOW, cols]
            return jnp.concatenate([prev, kv_ref[tok, cols]], axis=0)

        k = keys(hk * SWA_DH)
        v = keys(SWA_KV + hk * SWA_DH)
        q = jnp.concatenate([q_ref[tok, hq * SWA_DH:(hq + 1) * SWA_DH] for hq in heads], axis=0)
        sink = jnp.concatenate([jnp.full((WINDOW, 1), sink_ref[hq], F32) for hq in heads], axis=0)
        qi = lax.broadcasted_iota(jnp.int32, (rows, WINDOW), 0) % WINDOW
        kj = lax.broadcasted_iota(jnp.int32, (rows, WINDOW), 1)
        from_prev = kj > qi
        s2 = _dot_nt(q, k)
        s = jnp.where(from_prev, s2[:, :WINDOW], s2[:, WINDOW:]) * scale + bias_ref[hk]
        if w == 0:
            s = jnp.where(from_prev & (pl.program_id(1) == 0), NEG_BIG, s)
        m = jnp.maximum(jnp.max(s, -1, keepdims=True), sink)
        p = jnp.exp(s - m)
        p2 = jnp.concatenate([jnp.where(from_prev, p, 0.0), jnp.where(from_prev, 0.0, p)], axis=1).astype(BF16)
        v_ext = jnp.concatenate([v, jnp.ones((2 * WINDOW, SWA_DH), BF16)], axis=1)
        pv = _dot(p2, v_ext)
        o = pv[:, :SWA_DH] / (pv[:, SWA_DH:SWA_DH + 1] + jnp.exp(sink - m))
        for g, hq in enumerate(heads):
            o_ref[tok, hq * SWA_DH:(hq + 1) * SWA_DH] = o[g * WINDOW:(g + 1) * WINDOW, :].astype(BF16)

    return [functools.partial(unit, hk, w) for hk in range(SWA_HKV) for w in range(nw)]


def _swa_bias():
    qi = jnp.arange(WINDOW)[:, None]
    kj = jnp.arange(WINDOW)[None, :]
    dist = jnp.where(kj > qi, qi - kj + WINDOW, qi - kj).astype(F32)
    slopes = 2.0 ** (-8.0 * jnp.arange(1, SWA_HQ + 1, dtype=F32) / SWA_HQ)
    bias = -slopes[:, None, None] * dist[None]
    return bias.reshape(SWA_HKV, GQA_GROUP * WINDOW, WINDOW)


def _deepnorm(x, y, gt, g, b):
    return _layer_norm(DEEPNORM_ALPHA * x + (1.0 + gt) * y) * g + b


def _merge_kernel(x_ref, ya_ref, yb_ref, uc_ref, ucp_ref, gate_ref, gt_ref, poolw_ref, pscale_ref,
                  wpa_ref, wpb_ref, wpc_ref, wo_ref, lng_ref, lnb_ref, o_ref, *, tm):
    j = pl.program_id(1)
    u = uc_ref[...].astype(F32)
    halo = jnp.where(j == 0, 0.0, ucp_ref[...].astype(F32))
    ue = jnp.concatenate([halo, u], axis=0)
    tpos = (j * tm + lax.broadcasted_iota(jnp.int32, (tm, POOL_GDIM), 0) + 1).astype(F32)
    ycs = []
    for gi, win in enumerate(POOL_WINDOWS):
        a = ue[:, gi * POOL_GDIM:(gi + 1) * POOL_GDIM]
        span = 1
        while span < win:
            a = a[span:] + a[:-span]
            span *= 2
        lo = MAX_POOL - win + 1
        d = a[lo:lo + tm] / jnp.minimum(tpos, float(win)) - u[:, gi * POOL_GDIM:(gi + 1) * POOL_GDIM]
        ycs.append(_dot(d.astype(BF16), poolw_ref[gi]))
    yc = jnp.concatenate(ycs, axis=1) * pscale_ref[...]
    merged = gate_ref[:, 0:D_MODEL].astype(F32) * _dot(ya_ref[...], wpa_ref[...])
    merged = merged + gate_ref[:, D_MODEL:2 * D_MODEL].astype(F32) * _dot(yb_ref[...], wpb_ref[...])
    merged = merged + gate_ref[:, 2 * D_MODEL:3 * D_MODEL].astype(F32) * _dot(yc.astype(BF16), wpc_ref[...])
    y = _dot(merged.astype(BF16), wo_ref[...])
    o_ref[...] = _deepnorm(x_ref[...], y, gt_ref[0], lng_ref[...], lnb_ref[...])


def _mixer_kernel(sink_ref, x_ref, qkv_ref, ab_ref, z_ref, gpar_ref, nw_ref, qb_ref, kvb_ref, kvp_ref, bias_ref,
                  uc_ref, ucp_ref, gate_ref, gt_ref, poolw_ref, pscale_ref, wpa_ref, wpb_ref, wpc_ref, wo_ref,
                  lng_ref, lnb_ref, o_ref, state_ref, ya_ref, yb_ref, *, tm):
    pending = iter(_swa_units(sink_ref, qb_ref, kvb_ref, kvp_ref, bias_ref, yb_ref, tq=tm))

    def filler():
        unit = next(pending, None)
        if unit is not None:
            unit()

    _gdn_kernel(qkv_ref, ab_ref, z_ref, gpar_ref, nw_ref, ya_ref, state_ref, tb=tm, filler=filler)
    for unit in pending:
        unit()
    _merge_kernel(x_ref, ya_ref, yb_ref, uc_ref, ucp_ref, gate_ref, gt_ref, poolw_ref, pscale_ref,
                  wpa_ref, wpb_ref, wpc_ref, wo_ref, lng_ref, lnb_ref, o_ref, tm=tm)


def _mixer(x, qkv, ab, z, qb, kvb, uc, gates, mod3, gpar, norm_w, sinks, pool_w, pool_scale, w_pa, w_pb, w_pc, w_o,
           ln_g, ln_b, batch, seq):
    t, d = x.shape
    tm = TOKEN_TILE
    per_b = seq // tm

    def tok(n):
        return pl.BlockSpec((tm, n), lambda b, j: (b * per_b + j, 0))

    def prev_rows(rows, n):
        return pl.BlockSpec((rows, n), lambda b, j: (jnp.maximum((b * per_b + j) * (tm // rows) - 1, 0), 0))

    def full(shape):
        return pl.BlockSpec(shape, lambda b, j: (0,) * len(shape))

    return pl.pallas_call(
        functools.partial(_mixer_kernel, tm=tm),
        grid=(batch, per_b),
        in_specs=[
            pl.BlockSpec(memory_space=pltpu.SMEM),
            tok(d), tok(QKV_W), tok(AB_W), tok(GDN_V), full((2, AB_W)), full((1, GDN_DV)),
            tok(SWA_Q), tok(2 * SWA_KV), prev_rows(WINDOW, 2 * SWA_KV),
            full((SWA_HKV, GQA_GROUP * WINDOW, WINDOW)),
            tok(POOL_DIM), prev_rows(MAX_POOL, POOL_DIM),
            tok(N_BRANCH * d),
            pl.BlockSpec((1, 1, d), lambda b, j: (b, 0, 2)),
            full((POOL_GROUPS, POOL_GDIM, POOL_GDIM)), full((1, POOL_DIM)),
            full((GDN_V, d)), full((SWA_Q, d)), full((POOL_DIM, d)), full((d, d)),
            full((1, d)), full((1, d)),
        ],
        out_specs=tok(d),
        out_shape=jax.ShapeDtypeStruct((t, d), F32),
        scratch_shapes=[pltpu.VMEM((GDN_HEADS, GDN_DK, GDN_DV), F32), pltpu.VMEM((tm, GDN_V), BF16),
                        pltpu.VMEM((tm, SWA_Q), BF16)],
        compiler_params=_params("parallel", "arbitrary"),
        name="mixer",
    )(sinks, x, qkv, ab, z, gpar, norm_w, qb, kvb, kvb, _swa_bias(), uc, uc, gates, mod3, pool_w, pool_scale,
      w_pa, w_pb, w_pc, w_o, ln_g, ln_b)


def _router_kernel(x_ref, sh_ref, sc_ref, rwt_ref, rb_ref, tri_ref, hp_ref, topi_ref, topg_ref, rank_ref, cnt_ref,
                   run_ref):
    @pl.when(pl.program_id(0) == 0)
    def _():
        run_ref[...] = jnp.zeros_like(run_ref)

    h = _layer_norm(x_ref[...]) * (1.0 + sc_ref[0]) + sh_ref[0]
    hp_ref[...] = _pack_rows(h)
    logits = lax.dot_general(rwt_ref[...], h, (((1,), (1,)), ((), ())), preferred_element_type=F32,
                             precision=lax.Precision.HIGHEST) + rb_ref[...]
    sub = lax.broadcasted_iota(jnp.int32, logits.shape, 0)
    vals, idxs = [], []
    for _ in range(TOP_K):
        m = jnp.max(logits, 0, keepdims=True)
        idx = jnp.min(jnp.where(logits == m, sub, N_EXPERTS), 0, keepdims=True)
        vals.append(m)
        idxs.append(idx)
        logits = jnp.where(sub == idx, -jnp.inf, logits)
    es = [jnp.exp(v - vals[0]) for v in vals]
    denom = es[0] + es[1] + es[2] + es[3]
    topi_ref[...] = jnp.concatenate(idxs, axis=0)
    topg_ref[...] = jnp.concatenate([e / denom for e in es], axis=0)

    sel = jnp.zeros(logits.shape, F32)
    for idx in idxs:
        sel = sel + (sub == idx).astype(F32)
    before = run_ref[:, 0:1] + _dot(sel.astype(BF16), tri_ref[...])
    ranks = [jnp.sum(jnp.where(sub == idx, before, 0.0), 0, keepdims=True) for idx in idxs]
    rank_ref[...] = jnp.concatenate(ranks, axis=0).astype(jnp.int32)
    run_ref[...] = run_ref[...] + jnp.sum(sel, 1, keepdims=True)
    cnt_ref[...] = run_ref[...].astype(jnp.int32)


def _router(x, mod3, router_wt, router_b, seq):
    t, d = x.shape
    tm = TOKEN_TILE
    per_b = seq // tm
    tri = (jnp.arange(tm)[:, None] < jnp.arange(tm)[None, :]).astype(BF16)
    return pl.pallas_call(
        _router_kernel,
        grid=(t // tm,),
        in_specs=[
            pl.BlockSpec((tm, d), lambda i: (i, 0)),
            pl.BlockSpec((1, 1, d), lambda i: (i // per_b, 0, 3)),
            pl.BlockSpec((1, 1, d), lambda i: (i // per_b, 0, 4)),
            pl.BlockSpec((N_EXPERTS, d), lambda i: (0, 0)),
            pl.BlockSpec((N_EXPERTS, 1), lambda i: (0, 0)),
            pl.BlockSpec((tm, tm), lambda i: (0, 0)),
        ],
        out_specs=[
            pl.BlockSpec((tm, PACKED_D), lambda i: (i, 0)),
            pl.BlockSpec((TOP_K, tm), lambda i: (0, i)),
            pl.BlockSpec((TOP_K, tm), lambda i: (0, i)),
            pl.BlockSpec((TOP_K, tm), lambda i: (0, i)),
            pl.BlockSpec((N_EXPERTS, 128), lambda i: (0, 0)),
        ],
        out_shape=[
            jax.ShapeDtypeStruct((t, PACKED_D), U32),
            jax.ShapeDtypeStruct((TOP_K, t), jnp.int32),
            jax.ShapeDtypeStruct((TOP_K, t), F32),
            jax.ShapeDtypeStruct((TOP_K, t), jnp.int32),
            jax.ShapeDtypeStruct((N_EXPERTS, 128), jnp.int32),
        ],
        scratch_shapes=[pltpu.VMEM((N_EXPERTS, 128), F32)],
        compiler_params=_params("arbitrary"),
        name="router",
    )(x, mod3, mod3, router_wt, router_b, tri)


def _dispatch_plan(topi_t, rank_t, counts):
    n_blk = (counts + ROUTE_BLOCK - 1) // ROUTE_BLOCK
    first_blk = jnp.cumsum(n_blk) - n_blk
    pstart = first_blk * ROUTE_BLOCK
    experts = jnp.arange(N_EXPERTS, dtype=jnp.int32)
    pstart_tok = jnp.sum(jnp.where(topi_t[:, :, None] == experts, pstart, 0), -1)
    dest_t = (pstart_tok + rank_t).astype(jnp.int32)
    return dest_t, first_blk.astype(jnp.int32), n_blk.astype(jnp.int32)


def _sc_mesh():
    return plsc.VectorSubcoreMesh(core_axis_name="c", subcore_axis_name="s")


def _sc_dispatch(hp, dest_flat, rows):
    t, dp = hp.shape
    per_w = t // SC_WORKERS
    n_chunks = per_w // SC_CHUNK

    assert n_chunks % 2 == 0

    @functools.partial(
        pl.kernel, mesh=_sc_mesh(), out_type=jax.ShapeDtypeStruct((rows, dp), hp.dtype),
        scratch_types=[pltpu.VMEM((TOP_K * n_chunks, SC_CHUNK), jnp.int32),
                       pltpu.VMEM((SC_CHUNK, dp), hp.dtype), pltpu.VMEM((SC_CHUNK, dp), hp.dtype),
                       pltpu.SemaphoreType.DMA, pltpu.SemaphoreType.DMA,
                       pltpu.SemaphoreType.DMA, pltpu.SemaphoreType.DMA])
    def dispatch(h_hbm, d_hbm, xs_hbm, idx_v, rows_a, rows_b, load_a, load_b, scat_a, scat_b):
        worker = lax.axis_index("s") * SC_CORES + lax.axis_index("c")
        base = worker * per_w
        for k in range(TOP_K):
            pltpu.sync_copy(d_hbm.at[pl.ds((k * SC_WORKERS + worker) * n_chunks, n_chunks)],
                            idx_v.at[pl.ds(k * n_chunks, n_chunks)])

        def load(i, buf, sem):
            return pltpu.make_async_copy(h_hbm.at[pl.ds(base + i * SC_CHUNK, SC_CHUNK)], buf, sem)

        def scatter(i, k, buf, sem):
            return pltpu.make_async_copy(buf, xs_hbm.at[idx_v.at[k * n_chunks + i]], sem)

        def scatter_all(i, buf, sem):
            for k in range(TOP_K):
                scatter(i, k, buf, sem).start()
            for k in range(TOP_K):
                scatter(i, k, buf, sem).wait()

        load(0, rows_a, load_a).start()

        @pl.loop(0, n_chunks, step=2)
        def _(i):
            load(i, rows_a, load_a).wait()
            load(i + 1, rows_b, load_b).start()
            scatter_all(i, rows_a, scat_a)
            load(i + 1, rows_b, load_b).wait()

            @pl.when(i + 2 < n_chunks)
            def _():
                load(i + 2, rows_a, load_a).start()

            scatter_all(i + 1, rows_b, scat_b)

    return dispatch(hp, dest_flat.reshape(TOP_K * SC_WORKERS * n_chunks, SC_CHUNK))


def _sc_gather(table, idx):
    n = idx.shape[0]
    dp = table.shape[1]
    per_w = n // SC_WORKERS
    n_chunks = per_w // SC_CHUNK

    assert n_chunks % 2 == 0

    @functools.partial(
        pl.kernel, mesh=_sc_mesh(), out_type=jax.ShapeDtypeStruct((n, dp), table.dtype),
        scratch_types=[pltpu.VMEM((n_chunks, SC_CHUNK), jnp.int32),
                       pltpu.VMEM((SC_CHUNK, dp), table.dtype), pltpu.VMEM((SC_CHUNK, dp), table.dtype),
                       pltpu.SemaphoreType.DMA, pltpu.SemaphoreType.DMA,
                       pltpu.SemaphoreType.DMA, pltpu.SemaphoreType.DMA])
    def gather(t_hbm, i_hbm, o_hbm, idx_v, rows_a, rows_b, fetch_a, fetch_b, store_a, store_b):
        worker = lax.axis_index("s") * SC_CORES + lax.axis_index("c")
        base = worker * per_w
        pltpu.sync_copy(i_hbm.at[pl.ds(worker * n_chunks, n_chunks)], idx_v)

        def fetch(i, buf, sem):
            return pltpu.make_async_copy(t_hbm.at[idx_v.at[i]], buf, sem)

        def store(i, buf, sem):
            return pltpu.make_async_copy(buf, o_hbm.at[pl.ds(base + i * SC_CHUNK, SC_CHUNK)], sem)

        fetch(0, rows_a, fetch_a).start()

        @pl.loop(0, n_chunks, step=2)
        def _(i):
            fetch(i, rows_a, fetch_a).wait()

            @pl.when(i > 0)
            def _():
                store(i - 1, rows_b, store_b).wait()

            fetch(i + 1, rows_b, fetch_b).start()
            store(i, rows_a, store_a).start()
            fetch(i + 1, rows_b, fetch_b).wait()
            store(i, rows_a, store_a).wait()

            @pl.when(i + 2 < n_chunks)
            def _():
                fetch(i + 2, rows_a, fetch_a).start()

            store(i + 1, rows_b, store_b).start()

        store(n_chunks - 1, rows_b, store_b).wait()

    return gather(table, idx.reshape(SC_WORKERS * n_chunks, SC_CHUNK))


def _expert_kernel(first_ref, nblk_ref, cnt_ref, xs_hbm, w1_ref, b1_ref, w2_ref, b2_ref, ys_hbm,
                   w1b_ref, w2b_ref, xbuf, obuf, in_sem, out_sem):
    e = pl.program_id(0)
    n = nblk_ref[e]
    first = first_ref[e]
    count = cnt_ref[e]

    def rows_of(b):
        return pl.ds(pl.multiple_of((first + b) * ROUTE_BLOCK, ROUTE_BLOCK), ROUTE_BLOCK)

    def in_copy(b, slot):
        return pltpu.make_async_copy(xs_hbm.at[rows_of(b)], xbuf.at[slot], in_sem.at[slot])

    def out_copy(b, slot):
        return pltpu.make_async_copy(obuf.at[slot], ys_hbm.at[rows_of(b)], out_sem.at[slot])

    @pl.when(n > 0)
    def _():
        in_copy(0, 0).start()
        w1b_ref[...] = w1_ref[0].astype(BF16)
        w2b_ref[...] = w2_ref[0].astype(BF16)

        def block(b, carry):
            slot = b % 2
            in_copy(b, slot).wait()

            @pl.when(b + 1 < n)
            def _():
                in_copy(b + 1, 1 - slot).start()

            @pl.when(b >= 2)
            def _():
                out_copy(b - 2, slot).wait()

            live = lax.broadcasted_iota(jnp.int32, (ROUTE_BLOCK, D_MODEL), 0) < count - b * ROUTE_BLOCK
            x = jnp.where(live, _unpack_rows(xbuf[slot]), 0.0).astype(BF16)
            gu = _dot(x, w1b_ref[...]) + b1_ref[0]
            glu = jnp.minimum(gu[:, :D_FF], SWIGLU_LIMIT)
            lin = jnp.clip(gu[:, D_FF:], -SWIGLU_LIMIT, SWIGLU_LIMIT)
            act = glu * _sigmoid(SWIGLU_ALPHA * glu) * (lin + 1.0)
            obuf[slot] = _pack_rows(_dot(act.astype(BF16), w2b_ref[...]) + b2_ref[0])
            out_copy(b, slot).start()
            return carry

        lax.fori_loop(0, n, block, 0)

        @pl.when(n >= 2)
        def _():
            out_copy(n - 2, n % 2).wait()

        out_copy(n - 1, (n - 1) % 2).wait()


def _experts(xs, first_blk, n_blk, counts, w1, b1, w2, b2, layer):
    rows, dp = xs.shape
    d = D_MODEL
    e0 = layer * N_EXPERTS
    grid_spec = pltpu.PrefetchScalarGridSpec(
        num_scalar_prefetch=3,
        grid=(N_EXPERTS,),
        in_specs=[
            pl.BlockSpec(memory_space=pl.ANY),
            pl.BlockSpec((1, d, 2 * D_FF), lambda e, *_: (e0 + e, 0, 0)),
            pl.BlockSpec((1, 1, 2 * D_FF), lambda e, *_: (e0 + e, 0, 0)),
            pl.BlockSpec((1, D_FF, d), lambda e, *_: (e0 + e, 0, 0)),
            pl.BlockSpec((1, 1, d), lambda e, *_: (e0 + e, 0, 0)),
        ],
        out_specs=pl.BlockSpec(memory_space=pl.ANY),
        scratch_shapes=[pltpu.VMEM((d, 2 * D_FF), BF16), pltpu.VMEM((D_FF, d), BF16),
                        pltpu.VMEM((2, ROUTE_BLOCK, dp), U32), pltpu.VMEM((2, ROUTE_BLOCK, dp), U32),
                        pltpu.SemaphoreType.DMA((2,)), pltpu.SemaphoreType.DMA((2,))],
    )
    return pl.pallas_call(
        _expert_kernel,
        grid_spec=grid_spec,
        out_shape=jax.ShapeDtypeStruct((rows, dp), U32),
        compiler_params=_params("arbitrary"),
        name="experts",
    )(first_blk, n_blk, counts.astype(jnp.int32), xs, w1, b1, w2, b2)


def _combine_kernel(x_ref, yg_ref, topg_ref, gt_ref, lng_ref, lnb_ref, o_ref):
    o_ref[...] = _moe_combine(x_ref[...], [yg_ref[k] for k in range(TOP_K)], topg_ref[...], gt_ref[0],
                              lng_ref[...], lnb_ref[...])


def _combine(x, yg, topg, mod3, ln_g, ln_b, seq):
    t, d = x.shape
    tm = TOKEN_TILE
    per_b = seq // tm
    return pl.pallas_call(
        _combine_kernel,
        grid=(t // tm,),
        in_specs=[
            pl.BlockSpec((tm, d), lambda i: (i, 0)),
            pl.BlockSpec((TOP_K, tm, PACKED_D), lambda i: (0, i, 0)),
            pl.BlockSpec((TOP_K, tm), lambda i: (0, i)),
            pl.BlockSpec((1, 1, d), lambda i: (i // per_b, 0, 5)),
            pl.BlockSpec((1, d), lambda i: (0, 0)),
            pl.BlockSpec((1, d), lambda i: (0, 0)),
        ],
        out_specs=pl.BlockSpec((tm, d), lambda i: (i, 0)),
        out_shape=jax.ShapeDtypeStruct((t, d), F32),
        compiler_params=_params("parallel"),
        name="combine",
    )(x, yg, topg, mod3, ln_g, ln_b)


def _rearranged_w_in(w_in):
    depth, d, _ = w_in.shape
    ab_end = SEG_AB + 2 * GDN_HEADS
    ab_pad = jnp.zeros((depth, d, AB_W - 2 * GDN_HEADS), w_in.dtype)
    return jnp.concatenate([w_in[:, :, :ab_end], ab_pad, w_in[:, :, ab_end:]], axis=2).astype(BF16)


def kernel(x, c, ada_w, ada_b, w_in, conv_w, a_log, dt_bias, gdn_norm_w, sinks, pool_w, pool_scale, w_pa, w_pb, w_pc, w_o, ln1_g, ln1_b, ln2_g, ln2_b, router_w, router_b, exp_w1, exp_b1, exp_w2, exp_b2):
    batch, seq, d = x.shape
    t = batch * seq
    n_blocks = (t * TOP_K + ROUTE_BLOCK - 1) // ROUTE_BLOCK + N_EXPERTS
    mod = _ada_mod(c, ada_w, ada_b)
    xt = x.reshape(t, d)
    lane_pad = jnp.zeros((AB_W - GDN_HEADS,), F32)
    rows = n_blocks * ROUTE_BLOCK
    w_in_all = _rearranged_w_in(w_in)
    w1_all = exp_w1.reshape(DEPTH * N_EXPERTS, d, 2 * D_FF)
    b1_all = exp_b1.reshape(DEPTH * N_EXPERTS, 1, 2 * D_FF)
    w2_all = exp_w2.reshape(DEPTH * N_EXPERTS, D_FF, d)
    b2_all = exp_b2.reshape(DEPTH * N_EXPERTS, 1, d)
    moe = None
    for l in range(DEPTH):
        mod3 = mod[l].reshape(batch, 1, 6 * d)
        if moe is None:
            qkv, z, ab, qb, kvb, uc, gates = _in_proj(xt, mod3, w_in_all, conv_w, seq, l)
        else:
            xt, qkv, z, ab, qb, kvb, uc, gates = _in_proj(xt, mod3, w_in_all, conv_w, seq, l, moe)
        gpar = jnp.stack([jnp.concatenate([-jnp.exp(a_log[l]), lane_pad]), jnp.concatenate([dt_bias[l], lane_pad])])
        xt = _mixer(xt, qkv, ab, z, qb, kvb, uc, gates, mod3, gpar, gdn_norm_w[l].reshape(1, GDN_DV), sinks[l],
                    pool_w[l].astype(BF16), pool_scale[l].reshape(1, POOL_DIM),
                    w_pa[l].astype(BF16), w_pb[l].astype(BF16), w_pc[l].astype(BF16), w_o[l].astype(BF16),
                    ln1_g[l].reshape(1, d), ln1_b[l].reshape(1, d), batch, seq)
        hp, topi_t, topg_t, rank_t, cnt = _router(xt, mod3, router_w[l].T, router_b[l].reshape(N_EXPERTS, 1), seq)
        counts = cnt[:, 0]
        dest_t, first_blk, n_blk = _dispatch_plan(topi_t, rank_t, counts)
        dest_flat = dest_t.reshape(TOP_K * t)
        xs = _sc_dispatch(hp, dest_flat, rows)
        ys = _experts(xs, first_blk, n_blk, counts, w1_all, b1_all, w2_all, b2_all, l)
        yg = _sc_gather(ys, dest_flat).reshape(TOP_K, t, PACKED_D)
        moe = (yg, topg_t, mod3, ln2_g[l].reshape(1, d), ln2_b[l].reshape(1, d))
    xt = _combine(xt, *moe, seq)
    return xt.reshape(batch, seq, d)
```

```python
import functools

import jax
import jax.numpy as jnp
from jax import lax
from jax.experimental import pallas as pl
from jax.experimental.pallas import tpu as pltpu
from jax.experimental.pallas import tpu_sc as plsc

D_MODEL = 1024
DEPTH = 4
GDN_HEADS = 4
GDN_DK = 128
GDN_DV = 128
GDN_CONV = 4
GDN_CHUNK = 64
SWA_HQ = 8
SWA_HKV = 2
SWA_DH = 64
WINDOW = 128
POOL_WINDOWS = (2, 4, 8, 16)
POOL_GROUPS = 4
POOL_GDIM = 128
N_BRANCH = 3
N_EXPERTS = 32
TOP_K = 4
D_FF = 1024
ROUTE_BLOCK = 1024
EXPERT_ROWS = 512
SWIGLU_ALPHA = 1.702
SWIGLU_LIMIT = 7.0
LN_EPS = 1e-5
RMS_EPS = 1e-6
DEEPNORM_ALPHA = (2 * DEPTH) ** 0.25

GDN_QK = GDN_HEADS * GDN_DK
GDN_V = GDN_HEADS * GDN_DV
SWA_Q = SWA_HQ * SWA_DH
SWA_KV = SWA_HKV * SWA_DH
POOL_DIM = POOL_GROUPS * POOL_GDIM
GQA_GROUP = SWA_HQ // SWA_HKV
MAX_POOL = max(POOL_WINDOWS)

QKV_W = 2 * GDN_QK + GDN_V
AB_W = 128
SEG_QKV = 0
SEG_Z = SEG_QKV + QKV_W
SEG_AB = SEG_Z + GDN_V
SEG_QB = SEG_AB + AB_W
SEG_KVB = SEG_QB + SWA_Q
SEG_UC = SEG_KVB + 2 * SWA_KV
SEG_GATE = SEG_UC + POOL_DIM
PROJ_W = SEG_GATE + N_BRANCH * D_MODEL

PROJ_CHUNK = 256
TOKEN_TILE = 512
NEG_BIG = -1e30
VMEM_LIMIT = 56 * 1024 * 1024

PACKED_D = D_MODEL // 2
SC_CORES = 2
SC_SUBCORES = 16
SC_WORKERS = SC_CORES * SC_SUBCORES
SC_CHUNK = 64

F32 = jnp.float32
BF16 = jnp.bfloat16
U32 = jnp.uint32
HI_MASK = 0xFFFF0000


def _pack_rows(x):
    bits = pltpu.bitcast(x.astype(BF16).astype(F32), U32)
    return (bits[:, :PACKED_D] >> 16) | (bits[:, PACKED_D:] & jnp.uint32(HI_MASK))


def _unpack_rows(w):
    lo = pltpu.bitcast(w << 16, F32)
    hi = pltpu.bitcast(w & jnp.uint32(HI_MASK), F32)
    return jnp.concatenate([lo, hi], axis=1)


def _params(*sem):
    return pltpu.CompilerParams(dimension_semantics=sem, vmem_limit_bytes=VMEM_LIMIT)


def _sigmoid(x):
    return 0.5 * jnp.tanh(0.5 * x) + 0.5


def _layer_norm(x):
    mu = jnp.mean(x, -1, keepdims=True)
    xc = x - mu
    var = jnp.mean(xc * xc, -1, keepdims=True)
    return xc * lax.rsqrt(var + LN_EPS)


def _dot(a, b):
    return jnp.dot(a, b, preferred_element_type=F32)


def _dot_nt(a, b):
    return lax.dot_general(a, b, (((1,), (1,)), ((), ())), preferred_element_type=F32)


def _dot_tn(a, b):
    return lax.dot_general(a, b, (((0,), (0,)), ((), ())), preferred_element_type=F32)


def _bdot(a, b):
    return lax.dot_general(a, b, (((2,), (1,)), ((0,), (0,))), preferred_element_type=F32)


def _bdot_tn(a, b):
    return lax.dot_general(a, b, (((1,), (1,)), ((0,), (0,))), preferred_element_type=F32)


def _bdot_nt(a, b):
    return lax.dot_general(a, b, (((2,), (2,)), ((0,), (0,))), preferred_element_type=F32)


def _ada_kernel(c_ref, w_ref, b_ref, o_ref):
    c = c_ref[...]
    cond = c * _sigmoid(c)
    o_ref[0] = _dot(cond.astype(BF16), w_ref[0].astype(BF16)) + b_ref[0]


def _ada_mod(c, ada_w, ada_b):
    depth, d, n = ada_w.shape
    b = c.shape[0]
    tn = 1024
    return pl.pallas_call(
        _ada_kernel,
        grid=(depth, n // tn),
        in_specs=[
            pl.BlockSpec((b, d), lambda l, j: (0, 0)),
            pl.BlockSpec((1, d, tn), lambda l, j: (l, 0, j)),
            pl.BlockSpec((1, 1, tn), lambda l, j: (l, 0, j)),
        ],
        out_specs=pl.BlockSpec((1, b, tn), lambda l, j: (l, 0, j)),
        out_shape=jax.ShapeDtypeStruct((depth, b, n), F32),
        compiler_params=_params("parallel", "parallel"),
        name="ada_mod",
    )(c, ada_w, ada_b.reshape(depth, 1, n))


HEAD_UNITS = 4


def _moe_combine(x, yg_rows, gates_t, gt, ln_g, ln_b):
    g = jnp.concatenate([gates_t, jnp.zeros((128 - TOP_K, gates_t.shape[1]), F32)], axis=0).T
    y = g[:, 0:1] * _unpack_rows(yg_rows[0])
    for k in range(1, TOP_K):
        y = y + g[:, k:k + 1] * _unpack_rows(yg_rows[k])
    return _deepnorm(x, y, gt, ln_g, ln_b)


def _proj_kernel(*refs, tm, per_b, fused):
    if fused:
        (x_ref, yg_ref, topg_ref, gt_ref, lng_ref, lnb_ref, sh_ref, sc_ref, w_ref, convw_ref,
         xo_ref, qkv_ref, z_ref, ab_ref, qb_ref, kvb_ref, uc_ref, gate_ref, h_ref, raw_ref) = refs
    else:
        (x_ref, sh_ref, sc_ref, w_ref, convw_ref,
         qkv_ref, z_ref, ab_ref, qb_ref, kvb_ref, uc_ref, gate_ref, h_ref, raw_ref) = refs
    i = pl.program_id(0)
    slot = i % 2

    @pl.when(i == 0)
    def _():
        h_ref[1] = jnp.zeros(h_ref.shape[1:], BF16)

    rows_per_unit = tm // HEAD_UNITS

    def head_unit(u):
        r = slice(u * rows_per_unit, (u + 1) * rows_per_unit)
        x = x_ref[r, :]
        if fused:
            x = _moe_combine(x, [yg_ref[k, r, :] for k in range(TOP_K)], topg_ref[:, r], gt_ref[0],
                             lng_ref[...], lnb_ref[...])
            xo_ref[r, :] = x
        h_ref[slot, r, :] = (_layer_norm(x) * (1.0 + sc_ref[0]) + sh_ref[0]).astype(BF16)

    h = h_ref[1 - slot]

    def seg(start, width):
        return _dot(h, w_ref[:, start:start + width])

    seq_start = (i == 0) | ((i + per_b - 1) % per_b == 0)

    @pl.when(seq_start)
    def _():
        raw_ref[0:8, :] = jnp.zeros((8, QKV_W), F32)

    @pl.when(jnp.logical_not(seq_start))
    def _():
        raw_ref[0:8, :] = raw_ref[tm:tm + 8, :]

    for j in range(QKV_W // PROJ_CHUNK):
        cols = slice(j * PROJ_CHUNK, (j + 1) * PROJ_CHUNK)
        raw_ref[8:8 + tm, cols] = seg(SEG_QKV + j * PROJ_CHUNK, PROJ_CHUNK)
        if j < HEAD_UNITS:
            head_unit(j)

    def conv_silu(j):
        cols = slice(j * PROJ_CHUNK, (j + 1) * PROJ_CHUNK)
        conv = raw_ref[8:8 + tm, cols] * convw_ref[GDN_CONV - 1:GDN_CONV, cols]
        for tap in range(GDN_CONV - 1):
            conv = conv + raw_ref[5 + tap:5 + tap + tm, cols] * convw_ref[tap:tap + 1, cols]
        qkv_ref[:, cols] = (conv * _sigmoid(conv)).astype(BF16)
    n_conv = QKV_W // PROJ_CHUNK
    done = 0
    for ref, start, width in ((z_ref, SEG_Z, GDN_V), (qb_ref, SEG_QB, SWA_Q), (kvb_ref, SEG_KVB, 2 * SWA_KV),
                              (uc_ref, SEG_UC, POOL_DIM)):
        for j in range(width // PROJ_CHUNK):
            cols = slice(j * PROJ_CHUNK, (j + 1) * PROJ_CHUNK)
            ref[:, cols] = seg(start + j * PROJ_CHUNK, PROJ_CHUNK).astype(BF16)
            if done < n_conv:
                conv_silu(done)
                done += 1
    assert done == n_conv and HEAD_UNITS <= n_conv
    ab_ref[...] = seg(SEG_AB, AB_W)
    gate_ref[...] = _sigmoid(seg(SEG_GATE, N_BRANCH * D_MODEL)).astype(BF16)


def _in_proj(x, mod3, w, conv_w, seq, layer, moe=None):
    t, d = x.shape
    tm = TOKEN_TILE
    per_b = seq // tm
    n_tiles = t // tm
    widths = (QKV_W, GDN_V, AB_W, SWA_Q, 2 * SWA_KV, POOL_DIM, N_BRANCH * D_MODEL)
    dtypes = (BF16, BF16, F32, BF16, BF16, BF16, BF16)

    def head(i):
        return jnp.minimum(i, n_tiles - 1)

    def proj(i):
        return jnp.maximum(i - 1, 0)

    def mod_spec(k):
        return pl.BlockSpec((1, 1, d), lambda i: (head(i) // per_b, 0, k))

    in_specs = [pl.BlockSpec((tm, d), lambda i: (head(i), 0))]
    operands = [x]
    out_specs = [pl.BlockSpec((tm, n), lambda i: (proj(i), 0)) for n in widths]
    out_shape = [jax.ShapeDtypeStruct((t, n), dt) for n, dt in zip(widths, dtypes)]
    if moe is not None:
        yg, topg_t, mod3_prev, ln_g, ln_b = moe
        in_specs += [pl.BlockSpec((TOP_K, tm, PACKED_D), lambda i: (0, head(i), 0)),
                     pl.BlockSpec((TOP_K, tm), lambda i: (0, head(i))),
                     mod_spec(5),
                     pl.BlockSpec((1, d), lambda i: (0, 0)), pl.BlockSpec((1, d), lambda i: (0, 0))]
        operands += [yg, topg_t, mod3_prev, ln_g, ln_b]
        out_specs = [pl.BlockSpec((tm, d), lambda i: (head(i), 0))] + out_specs
        out_shape = [jax.ShapeDtypeStruct((t, d), F32)] + out_shape
    in_specs += [mod_spec(0), mod_spec(1),
                 pl.BlockSpec((None, d, PROJ_W), lambda i: (layer, 0, 0), pipeline_mode=pl.Buffered(1)),
                 pl.BlockSpec((None, GDN_CONV, QKV_W), lambda i: (layer, 0, 0))]
    operands += [mod3, mod3, w, conv_w]
    return pl.pallas_call(
        functools.partial(_proj_kernel, tm=tm, per_b=per_b, fused=moe is not None),
        grid=(n_tiles + 1,),
        in_specs=in_specs,
        out_specs=out_specs,
        out_shape=out_shape,
        scratch_shapes=[pltpu.VMEM((2, tm, d), BF16), pltpu.VMEM((8 + tm, QKV_W), F32)],
        compiler_params=_params("arbitrary"),
        name="in_proj",
    )(*operands)


def _softplus(x):
    return jnp.maximum(x, 0.0) + jnp.log(1.0 + jnp.exp(-jnp.abs(x)))


def _unit_lower_inverse(lm, row, col, between_levels=lambda: None):
    n = lm.shape[-1]
    eye = (row == col).astype(F32)
    t = None
    s = 1
    while s < n:
        join = ((row // (2 * s)) == (col // (2 * s))) & ((row % (2 * s)) >= s) & ((col % (2 * s)) < s)
        cm = jnp.where(join, lm, 0.0)
        if t is None:
            t = eye - cm
        else:
            p = _bdot(t, cm)
            t = t - _bdot(p, t)
            between_levels()
        s *= 2
    return t


def _gdn_kernel(qkv_ref, ab_ref, z_ref, gpar_ref, nw_ref, o_ref, state_ref, *, tb, filler=lambda: None):
    c_len = GDN_CHUNK
    nc = tb // c_len

    @pl.when(pl.program_id(1) == 0)
    def _():
        state_ref[...] = jnp.zeros_like(state_ref)

    ab = ab_ref[...]
    gpar = gpar_ref[...]
    g_all = gpar[0:1] * _softplus(ab + gpar[1:2])
    beta_all = _sigmoid(ab)
    pos = lax.broadcasted_iota(jnp.int32, (tb, AB_W), 0) % c_len
    gc_all = g_all
    s = 1
    while s < c_len:
        gc_all = gc_all + jnp.where(pos >= s, pltpu.roll(gc_all, s, axis=0), 0.0)
        s *= 2

    nh = GDN_HEADS
    nb = nc * nh
    row = lax.broadcasted_iota(jnp.int32, (nb, c_len, c_len), 1)
    col = lax.broadcasted_iota(jnp.int32, (nb, c_len, c_len), 2)
    causal = row >= col
    strict = row > col
    diag = row == col

    def chunked(per_head):
        return jnp.stack([per_head(h).reshape(nc, c_len, GDN_DK) for h in range(nh)], axis=1).reshape(nb, c_len, GDN_DK)

    q = chunked(lambda h: qkv_ref[:, h * GDN_DK:(h + 1) * GDN_DK].astype(F32))
    k = chunked(lambda h: qkv_ref[:, GDN_QK + h * GDN_DK:GDN_QK + (h + 1) * GDN_DK].astype(F32))
    v = chunked(lambda h: qkv_ref[:, 2 * GDN_QK + h * GDN_DV:2 * GDN_QK + (h + 1) * GDN_DV].astype(F32))
    gc = chunked(lambda h: jnp.broadcast_to(gc_all[:, h:h + 1], (tb, GDN_DK)))
    beta = chunked(lambda h: jnp.broadcast_to(beta_all[:, nh + h:nh + h + 1], (tb, GDN_DK)))
    q = q * lax.rsqrt(jnp.sum(q * q, -1, keepdims=True) + 1e-6) * (GDN_DK ** -0.5)
    k = k * lax.rsqrt(jnp.sum(k * k, -1, keepdims=True) + 1e-6)

    gc_i = gc[:, :, :c_len]
    gc_j = jnp.sum(jnp.where(diag, gc_i, 0.0), axis=1, keepdims=True)
    decay = jnp.where(causal, jnp.exp(jnp.where(causal, gc_i - gc_j, 0.0)), 0.0)
    eg = jnp.exp(gc)
    gc_last = gc[:, c_len - 1:c_len, :]
    g_last = jnp.exp(gc_last)
    k_beta = k * beta
    kq = _bdot_nt(jnp.concatenate([k_beta, q], axis=1).astype(BF16), k.astype(BF16))
    lower = jnp.where(strict, kq[:, :c_len] * decay, 0.0)
    attn = jnp.where(causal, kq[:, c_len:] * decay, 0.0).astype(BF16)
    tinv = _unit_lower_inverse(lower, row, col, filler)
    uw = _bdot(tinv.astype(BF16), jnp.concatenate([v * beta, k_beta * eg], axis=2).astype(BF16)).astype(BF16)
    q_g = q * eg
    k_g = (k * jnp.exp(gc_last - gc)).astype(BF16)

    k_uw = _bdot_tn(k_g, uw)
    a_uw = _bdot(attn, uw)
    ku = k_uw[:, :, :GDN_DV]
    o_loc = a_uw[:, :, :GDN_DV]
    qk_eff = jnp.concatenate([q_g - a_uw[:, :, GDN_DV:], k_uw[:, :, GDN_DV:]], axis=1).astype(BF16)

    st = state_ref[...]
    outs = []
    for c in range(nc):
        sl = slice(c * nh, (c + 1) * nh)
        prod = _bdot(qk_eff[sl], st.astype(BF16))
        outs.append(prod[:, :c_len] + o_loc[sl])
        st = st * g_last[sl] - prod[:, c_len:] + ku[sl]
        filler()
    state_ref[...] = st

    nw = nw_ref[...]
    for h in range(nh):
        o = jnp.concatenate([outs[c][h] for c in range(nc)], axis=0)
        zh = z_ref[:, h * GDN_DV:(h + 1) * GDN_DV].astype(F32)
        o = o * lax.rsqrt(jnp.mean(o * o, -1, keepdims=True) + RMS_EPS) * nw * (zh * _sigmoid(zh))
        o_ref[:, h * GDN_DV:(h + 1) * GDN_DV] = o.astype(BF16)


def _swa_units(sink_ref, q_ref, kv_ref, kvp_ref, bias_ref, o_ref, *, tq):
    nw = tq // WINDOW
    rows = GQA_GROUP * WINDOW
    scale = SWA_DH ** -0.5

    def unit(hk, w):
        heads = range(hk * GQA_GROUP, (hk + 1) * GQA_GROUP)
        tok = slice(w * WINDOW, (w + 1) * WINDOW)

        def keys(col0):
            cols = slice(col0, col0 + SWA_DH)
            prev = kvp_ref[:, cols] if w == 0 else kv_ref[(w - 1) * WINDOW:w * WINDOW, cols]
            return jnp.concatenate([prev, kv_ref[tok, cols]], axis=0)

        k = keys(hk * SWA_DH)
        v = keys(SWA_KV + hk * SWA_DH)
        q = jnp.concatenate([q_ref[tok, hq * SWA_DH:(hq + 1) * SWA_DH] for hq in heads], axis=0)
        sink = jnp.concatenate([jnp.full((WINDOW, 1), sink_ref[hq], F32) for hq in heads], axis=0)
        qi = lax.broadcasted_iota(jnp.int32, (rows, WINDOW), 0) % WINDOW
        kj = lax.broadcasted_iota(jnp.int32, (rows, WINDOW), 1)
        from_prev = kj > qi
        s2 = _dot_nt(q, k)
        s = jnp.where(from_prev, s2[:, :WINDOW], s2[:, WINDOW:]) * scale + bias_ref[hk]
        if w == 0:
            s = jnp.where(from_prev & (pl.program_id(1) == 0), NEG_BIG, s)
        m = jnp.maximum(jnp.max(s, -1, keepdims=True), sink)
        p = jnp.exp(s - m)
        p2 = jnp.concatenate([jnp.where(from_prev, p, 0.0), jnp.where(from_prev, 0.0, p)], axis=1).astype(BF16)
        v_ext = jnp.concatenate([v, jnp.ones((2 * WINDOW, SWA_DH), BF16)], axis=1)
        pv = _dot(p2, v_ext)
        o = pv[:, :SWA_DH] / (pv[:, SWA_DH:SWA_DH + 1] + jnp.exp(sink - m))
        for g, hq in enumerate(heads):
            o_ref[tok, hq * SWA_DH:(hq + 1) * SWA_DH] = o[g * WINDOW:(g + 1) * WINDOW, :].astype(BF16)

    return [functools.partial(unit, hk, w) for hk in range(SWA_HKV) for w in range(nw)]


def _swa_bias():
    qi = jnp.arange(WINDOW)[:, None]
    kj = jnp.arange(WINDOW)[None, :]
    dist = jnp.where(kj > qi, qi - kj + WINDOW, qi - kj).astype(F32)
    slopes = 2.0 ** (-8.0 * jnp.arange(1, SWA_HQ + 1, dtype=F32) / SWA_HQ)
    bias = -slopes[:, None, None] * dist[None]
    return bias.reshape(SWA_HKV, GQA_GROUP * WINDOW, WINDOW)


def _deepnorm(x, y, gt, g, b):
    return _layer_norm(DEEPNORM_ALPHA * x + (1.0 + gt) * y) * g + b


def _merge_kernel(x_ref, ya_ref, yb_ref, uc_ref, ucp_ref, gate_ref, gt_ref, poolw_ref, pscale_ref,
                  wpa_ref, wpb_ref, wpc_ref, wo_ref, lng_ref, lnb_ref, o_ref, *, tm):
    j = pl.program_id(1)
    u = uc_ref[...].astype(F32)
    halo = jnp.where(j == 0, 0.0, ucp_ref[...].astype(F32))
    ue = jnp.concatenate([halo, u], axis=0)
    tpos = (j * tm + lax.broadcasted_iota(jnp.int32, (tm, POOL_GDIM), 0) + 1).astype(F32)
    ycs = []
    for gi, win in enumerate(POOL_WINDOWS):
        a = ue[:, gi * POOL_GDIM:(gi + 1) * POOL_GDIM]
        span = 1
        while span < win:
            a = a[span:] + a[:-span]
            span *= 2
        lo = MAX_POOL - win + 1
        d = a[lo:lo + tm] / jnp.minimum(tpos, float(win)) - u[:, gi * POOL_GDIM:(gi + 1) * POOL_GDIM]
        ycs.append(_dot(d.astype(BF16), poolw_ref[gi]))
    yc = jnp.concatenate(ycs, axis=1) * pscale_ref[...]
    merged = gate_ref[:, 0:D_MODEL].astype(F32) * _dot(ya_ref[...], wpa_ref[...])
    merged = merged + gate_ref[:, D_MODEL:2 * D_MODEL].astype(F32) * _dot(yb_ref[...], wpb_ref[...])
    merged = merged + gate_ref[:, 2 * D_MODEL:3 * D_MODEL].astype(F32) * _dot(yc.astype(BF16), wpc_ref[...])
    y = _dot(merged.astype(BF16), wo_ref[...])
    o_ref[...] = _deepnorm(x_ref[...], y, gt_ref[0], lng_ref[...], lnb_ref[...])


def _mixer_kernel(sink_ref, x_ref, qkv_ref, ab_ref, z_ref, gpar_ref, nw_ref, qb_ref, kvb_ref, kvp_ref, bias_ref,
                  uc_ref, ucp_ref, gate_ref, gt_ref, poolw_ref, pscale_ref, wpa_ref, wpb_ref, wpc_ref, wo_ref,
                  lng_ref, lnb_ref, o_ref, state_ref, ya_ref, yb_ref, *, tm):
    pending = iter(_swa_units(sink_ref, qb_ref, kvb_ref, kvp_ref, bias_ref, yb_ref, tq=tm))

    def filler():
        unit = next(pending, None)
        if unit is not None:
            unit()

    _gdn_kernel(qkv_ref, ab_ref, z_ref, gpar_ref, nw_ref, ya_ref, state_ref, tb=tm, filler=filler)
    for unit in pending:
        unit()
    _merge_kernel(x_ref, ya_ref, yb_ref, uc_ref, ucp_ref, gate_ref, gt_ref, poolw_ref, pscale_ref,
                  wpa_ref, wpb_ref, wpc_ref, wo_ref, lng_ref, lnb_ref, o_ref, tm=tm)


def _mixer(x, qkv, ab, z, qb, kvb, uc, gates, mod3, gpar, norm_w, sinks, pool_w, pool_scale, w_pa, w_pb, w_pc, w_o,
           ln_g, ln_b, batch, seq):
    t, d = x.shape
    tm = TOKEN_TILE
    per_b = seq // tm

    def tok(n):
        return pl.BlockSpec((tm, n), lambda b, j: (b * per_b + j, 0))

    def prev_rows(rows, n):
        return pl.BlockSpec((rows, n), lambda b, j: (jnp.maximum((b * per_b + j) * (tm // rows) - 1, 0), 0))

    def full(shape):
        return pl.BlockSpec(shape, lambda b, j: (0,) * len(shape))

    return pl.pallas_call(
        functools.partial(_mixer_kernel, tm=tm),
        grid=(batch, per_b),
        in_specs=[
            pl.BlockSpec(memory_space=pltpu.SMEM),
            tok(d), tok(QKV_W), tok(AB_W), tok(GDN_V), full((2, AB_W)), full((1, GDN_DV)),
            tok(SWA_Q), tok(2 * SWA_KV), prev_rows(WINDOW, 2 * SWA_KV),
            full((SWA_HKV, GQA_GROUP * WINDOW, WINDOW)),
            tok(POOL_DIM), prev_rows(MAX_POOL, POOL_DIM),
            tok(N_BRANCH * d),
            pl.BlockSpec((1, 1, d), lambda b, j: (b, 0, 2)),
            full((POOL_GROUPS, POOL_GDIM, POOL_GDIM)), full((1, POOL_DIM)),
            full((GDN_V, d)), full((SWA_Q, d)), full((POOL_DIM, d)), full((d, d)),
            full((1, d)), full((1, d)),
        ],
        out_specs=tok(d),
        out_shape=jax.ShapeDtypeStruct((t, d), F32),
        scratch_shapes=[pltpu.VMEM((GDN_HEADS, GDN_DK, GDN_DV), F32), pltpu.VMEM((tm, GDN_V), BF16),
                        pltpu.VMEM((tm, SWA_Q), BF16)],
        compiler_params=_params("parallel", "arbitrary"),
        name="mixer",
    )(sinks, x, qkv, ab, z, gpar, norm_w, qb, kvb, kvb, _swa_bias(), uc, uc, gates, mod3, pool_w, pool_scale,
      w_pa, w_pb, w_pc, w_o, ln_g, ln_b)


def _router_kernel(x_ref, sh_ref, sc_ref, rwt_ref, rb_ref, tri_ref, hp_ref, topi_ref, topg_ref, rank_ref, cnt_ref,
                   run_ref):
    @pl.when(pl.program_id(0) == 0)
    def _():
        run_ref[...] = jnp.zeros_like(run_ref)

    h = _layer_norm(x_ref[...]) * (1.0 + sc_ref[0]) + sh_ref[0]
    hp_ref[...] = _pack_rows(h)
    logits = lax.dot_general(rwt_ref[...], h, (((1,), (1,)), ((), ())), preferred_element_type=F32,
                             precision=lax.Precision.HIGHEST) + rb_ref[...]
    sub = lax.broadcasted_iota(jnp.int32, logits.shape, 0)
    vals, idxs = [], []
    for _ in range(TOP_K):
        m = jnp.max(logits, 0, keepdims=True)
        idx = jnp.min(jnp.where(logits == m, sub, N_EXPERTS), 0, keepdims=True)
        vals.append(m)
        idxs.append(idx)
        logits = jnp.where(sub == idx, -jnp.inf, logits)
    es = [jnp.exp(v - vals[0]) for v in vals]
    denom = es[0] + es[1] + es[2] + es[3]
    topi_ref[...] = jnp.concatenate(idxs, axis=0)
    topg_ref[...] = jnp.concatenate([e / denom for e in es], axis=0)

    sel = jnp.zeros(logits.shape, F32)
    for idx in idxs:
        sel = sel + (sub == idx).astype(F32)
    before = run_ref[:, 0:1] + _dot(sel.astype(BF16), tri_ref[...])
    ranks = [jnp.sum(jnp.where(sub == idx, before, 0.0), 0, keepdims=True) for idx in idxs]
    rank_ref[...] = jnp.concatenate(ranks, axis=0).astype(jnp.int32)
    run_ref[...] = run_ref[...] + jnp.sum(sel, 1, keepdims=True)
    cnt_ref[...] = run_ref[...].astype(jnp.int32)


def _router(x, mod3, router_wt, router_b, seq):
    t, d = x.shape
    tm = TOKEN_TILE
    per_b = seq // tm
    tri = (jnp.arange(tm)[:, None] < jnp.arange(tm)[None, :]).astype(BF16)
    return pl.pallas_call(
        _router_kernel,
        grid=(t // tm,),
        in_specs=[
            pl.BlockSpec((tm, d), lambda i: (i, 0)),
            pl.BlockSpec((1, 1, d), lambda i: (i // per_b, 0, 3)),
            pl.BlockSpec((1, 1, d), lambda i: (i // per_b, 0, 4)),
            pl.BlockSpec((N_EXPERTS, d), lambda i: (0, 0)),
            pl.BlockSpec((N_EXPERTS, 1), lambda i: (0, 0)),
            pl.BlockSpec((tm, tm), lambda i: (0, 0)),
        ],
        out_specs=[
            pl.BlockSpec((tm, PACKED_D), lambda i: (i, 0)),
            pl.BlockSpec((TOP_K, tm), lambda i: (0, i)),
            pl.BlockSpec((TOP_K, tm), lambda i: (0, i)),
            pl.BlockSpec((TOP_K, tm), lambda i: (0, i)),
            pl.BlockSpec((N_EXPERTS, 128), lambda i: (0, 0)),
        ],
        out_shape=[
            jax.ShapeDtypeStruct((t, PACKED_D), U32),
            jax.ShapeDtypeStruct((TOP_K, t), jnp.int32),
            jax.ShapeDtypeStruct((TOP_K, t), F32),
            jax.ShapeDtypeStruct((TOP_K, t), jnp.int32),
            jax.ShapeDtypeStruct((N_EXPERTS, 128), jnp.int32),
        ],
        scratch_shapes=[pltpu.VMEM((N_EXPERTS, 128), F32)],
        compiler_params=_params("arbitrary"),
        name="router",
    )(x, mod3, mod3, router_wt, router_b, tri)


def _dispatch_plan(topi_t, rank_t, counts, n_blocks):
    padded = (counts + ROUTE_BLOCK - 1) // ROUTE_BLOCK * ROUTE_BLOCK
    ends_p = jnp.cumsum(padded)
    pstart = ends_p - padded
    experts = jnp.arange(N_EXPERTS, dtype=jnp.int32)
    pstart_tok = jnp.sum(jnp.where(topi_t[:, :, None] == experts, pstart, 0), -1)
    dest_t = (pstart_tok + rank_t).astype(jnp.int32)
    blk_start = jnp.arange(n_blocks, dtype=jnp.int32) * ROUTE_BLOCK
    block_e = jnp.minimum(jnp.sum((blk_start[:, None] >= ends_p).astype(jnp.int32), -1), N_EXPERTS - 1)
    blk_end = jnp.sum(jnp.where(block_e[:, None] == experts, pstart + counts, 0), -1)
    n_valid = jnp.clip(blk_end - blk_start, 0, ROUTE_BLOCK).astype(jnp.int32)
    return dest_t, block_e.astype(jnp.int32), n_valid


def _sc_mesh():
    return plsc.VectorSubcoreMesh(core_axis_name="c", subcore_axis_name="s")


def _sc_dispatch(hp, dest_flat, rows):
    t, dp = hp.shape
    per_w = t // SC_WORKERS
    n_chunks = per_w // SC_CHUNK

    assert n_chunks % 2 == 0

    @functools.partial(
        pl.kernel, mesh=_sc_mesh(), out_type=jax.ShapeDtypeStruct((rows, dp), hp.dtype),
        scratch_types=[pltpu.VMEM((TOP_K * n_chunks, SC_CHUNK), jnp.int32),
                       pltpu.VMEM((SC_CHUNK, dp), hp.dtype), pltpu.VMEM((SC_CHUNK, dp), hp.dtype),
                       pltpu.SemaphoreType.DMA, pltpu.SemaphoreType.DMA,
                       pltpu.SemaphoreType.DMA, pltpu.SemaphoreType.DMA])
    def dispatch(h_hbm, d_hbm, xs_hbm, idx_v, rows_a, rows_b, load_a, load_b, scat_a, scat_b):
        worker = lax.axis_index("s") * SC_CORES + lax.axis_index("c")
        base = worker * per_w
        for k in range(TOP_K):
            pltpu.sync_copy(d_hbm.at[pl.ds((k * SC_WORKERS + worker) * n_chunks, n_chunks)],
                            idx_v.at[pl.ds(k * n_chunks, n_chunks)])

        def load(i, buf, sem):
            return pltpu.make_async_copy(h_hbm.at[pl.ds(base + i * SC_CHUNK, SC_CHUNK)], buf, sem)

        def scatter(i, k, buf, sem):
            return pltpu.make_async_copy(buf, xs_hbm.at[idx_v.at[k * n_chunks + i]], sem)

        def scatter_all(i, buf, sem):
            for k in range(TOP_K):
                scatter(i, k, buf, sem).start()
            for k in range(TOP_K):
                scatter(i, k, buf, sem).wait()

        load(0, rows_a, load_a).start()

        @pl.loop(0, n_chunks, step=2)
        def _(i):
            load(i, rows_a, load_a).wait()
            load(i + 1, rows_b, load_b).start()
            scatter_all(i, rows_a, scat_a)
            load(i + 1, rows_b, load_b).wait()

            @pl.when(i + 2 < n_chunks)
            def _():
                load(i + 2, rows_a, load_a).start()

            scatter_all(i + 1, rows_b, scat_b)

    return dispatch(hp, dest_flat.reshape(TOP_K * SC_WORKERS * n_chunks, SC_CHUNK))


def _sc_gather(table, idx):
    n = idx.shape[0]
    dp = table.shape[1]
    per_w = n // SC_WORKERS
    n_chunks = per_w // SC_CHUNK

    assert n_chunks % 2 == 0

    @functools.partial(
        pl.kernel, mesh=_sc_mesh(), out_type=jax.ShapeDtypeStruct((n, dp), table.dtype),
        scratch_types=[pltpu.VMEM((n_chunks, SC_CHUNK), jnp.int32),
                       pltpu.VMEM((SC_CHUNK, dp), table.dtype), pltpu.VMEM((SC_CHUNK, dp), table.dtype),
                       pltpu.SemaphoreType.DMA, pltpu.SemaphoreType.DMA,
                       pltpu.SemaphoreType.DMA, pltpu.SemaphoreType.DMA])
    def gather(t_hbm, i_hbm, o_hbm, idx_v, rows_a, rows_b, fetch_a, fetch_b, store_a, store_b):
        worker = lax.axis_index("s") * SC_CORES + lax.axis_index("c")
        base = worker * per_w
        pltpu.sync_copy(i_hbm.at[pl.ds(worker * n_chunks, n_chunks)], idx_v)

        def fetch(i, buf, sem):
            return pltpu.make_async_copy(t_hbm.at[idx_v.at[i]], buf, sem)

        def store(i, buf, sem):
            return pltpu.make_async_copy(buf, o_hbm.at[pl.ds(base + i * SC_CHUNK, SC_CHUNK)], sem)

        fetch(0, rows_a, fetch_a).start()

        @pl.loop(0, n_chunks, step=2)
        def _(i):
            fetch(i, rows_a, fetch_a).wait()

            @pl.when(i > 0)
            def _():
                store(i - 1, rows_b, store_b).wait()

            fetch(i + 1, rows_b, fetch_b).start()
            store(i, rows_a, store_a).start()
            fetch(i + 1, rows_b, fetch_b).wait()
            store(i, rows_a, store_a).wait()

            @pl.when(i + 2 < n_chunks)
            def _():
                fetch(i + 2, rows_a, fetch_a).start()

            store(i + 1, rows_b, store_b).start()

        store(n_chunks - 1, rows_b, store_b).wait()

    return gather(table, idx.reshape(SC_WORKERS * n_chunks, SC_CHUNK))


def _expert_kernel(be_ref, nv_ref, x_ref, w1_ref, b1_ref, w2_ref, b2_ref, o_ref, w1b_ref, w2b_ref):
    i = pl.program_id(0)
    n_valid = nv_ref[i]
    new_expert = (i == 0) | (be_ref[i] != be_ref[jnp.maximum(i - 1, 0)])

    @pl.when(new_expert & (n_valid > 0))
    def _():
        w1b_ref[...] = w1_ref[0].astype(BF16)
        w2b_ref[...] = w2_ref[0].astype(BF16)

    for piece in range(ROUTE_BLOCK // EXPERT_ROWS):
        r = slice(piece * EXPERT_ROWS, (piece + 1) * EXPERT_ROWS)
        left = n_valid - piece * EXPERT_ROWS

        @pl.when(left > 0)
        def _():
            live = lax.broadcasted_iota(jnp.int32, (EXPERT_ROWS, D_MODEL), 0) < left
            x = jnp.where(live, _unpack_rows(x_ref[r, :]), 0.0).astype(BF16)
            gu = _dot(x, w1b_ref[...]) + b1_ref[0]
            glu = jnp.minimum(gu[:, :D_FF], SWIGLU_LIMIT)
            lin = jnp.clip(gu[:, D_FF:], -SWIGLU_LIMIT, SWIGLU_LIMIT)
            act = glu * _sigmoid(SWIGLU_ALPHA * glu) * (lin + 1.0)
            o_ref[r, :] = _pack_rows(_dot(act.astype(BF16), w2b_ref[...]) + b2_ref[0])

        @pl.when(left <= 0)
        def _():
            o_ref[r, :] = jnp.zeros((EXPERT_ROWS, PACKED_D), U32)


def _experts(xs, block_e, n_valid, w1, b1, w2, b2, layer):
    rows, dp = xs.shape
    d = D_MODEL
    n_blocks = rows // ROUTE_BLOCK
    e0 = layer * N_EXPERTS
    grid_spec = pltpu.PrefetchScalarGridSpec(
        num_scalar_prefetch=2,
        grid=(n_blocks,),
        in_specs=[
            pl.BlockSpec((ROUTE_BLOCK, dp), lambda i, be, nv: (i, 0)),
            pl.BlockSpec((1, d, 2 * D_FF), lambda i, be, nv: (e0 + be[i], 0, 0)),
            pl.BlockSpec((1, 1, 2 * D_FF), lambda i, be, nv: (e0 + be[i], 0, 0)),
            pl.BlockSpec((1, D_FF, d), lambda i, be, nv: (e0 + be[i], 0, 0)),
            pl.BlockSpec((1, 1, d), lambda i, be, nv: (e0 + be[i], 0, 0)),
        ],
        out_specs=pl.BlockSpec((ROUTE_BLOCK, dp), lambda i, be, nv: (i, 0)),
        scratch_shapes=[pltpu.VMEM((d, 2 * D_FF), BF16), pltpu.VMEM((D_FF, d), BF16)],
    )
    return pl.pallas_call(
        _expert_kernel,
        grid_spec=grid_spec,
        out_shape=jax.ShapeDtypeStruct((rows, dp), U32),
        compiler_params=_params("arbitrary"),
        name="experts",
    )(block_e, n_valid, xs, w1, b1, w2, b2)


def _combine_kernel(x_ref, yg_ref, topg_ref, gt_ref, lng_ref, lnb_ref, o_ref):
    o_ref[...] = _moe_combine(x_ref[...], [yg_ref[k] for k in range(TOP_K)], topg_ref[...], gt_ref[0],
                              lng_ref[...], lnb_ref[...])


def _combine(x, yg, topg, mod3, ln_g, ln_b, seq):
    t, d = x.shape
    tm = TOKEN_TILE
    per_b = seq // tm
    return pl.pallas_call(
        _combine_kernel,
        grid=(t // tm,),
        in_specs=[
            pl.BlockSpec((tm, d), lambda i: (i, 0)),
            pl.BlockSpec((TOP_K, tm, PACKED_D), lambda i: (0, i, 0)),
            pl.BlockSpec((TOP_K, tm), lambda i: (0, i)),
            pl.BlockSpec((1, 1, d), lambda i: (i // per_b, 0, 5)),
            pl.BlockSpec((1, d), lambda i: (0, 0)),
            pl.BlockSpec((1, d), lambda i: (0, 0)),
        ],
        out_specs=pl.BlockSpec((tm, d), lambda i: (i, 0)),
        out_shape=jax.ShapeDtypeStruct((t, d), F32),
        compiler_params=_params("parallel"),
        name="combine",
    )(x, yg, topg, mod3, ln_g, ln_b)


def _rearranged_w_in(w_in):
    depth, d, _ = w_in.shape
    ab_end = SEG_AB + 2 * GDN_HEADS
    ab_pad = jnp.zeros((depth, d, AB_W - 2 * GDN_HEADS), w_in.dtype)
    return jnp.concatenate([w_in[:, :, :ab_end], ab_pad, w_in[:, :, ab_end:]], axis=2).astype(BF16)


def kernel(x, c, ada_w, ada_b, w_in, conv_w, a_log, dt_bias, gdn_norm_w, sinks, pool_w, pool_scale, w_pa, w_pb, w_pc, w_o, ln1_g, ln1_b, ln2_g, ln2_b, router_w, router_b, exp_w1, exp_b1, exp_w2, exp_b2):
    batch, seq, d = x.shape
    t = batch * seq
    n_blocks = (t * TOP_K + ROUTE_BLOCK - 1) // ROUTE_BLOCK + N_EXPERTS
    mod = _ada_mod(c, ada_w, ada_b)
    xt = x.reshape(t, d)
    lane_pad = jnp.zeros((AB_W - GDN_HEADS,), F32)
    rows = n_blocks * ROUTE_BLOCK
    w_in_all = _rearranged_w_in(w_in)
    w1_all = exp_w1.reshape(DEPTH * N_EXPERTS, d, 2 * D_FF)
    b1_all = exp_b1.reshape(DEPTH * N_EXPERTS, 1, 2 * D_FF)
    w2_all = exp_w2.reshape(DEPTH * N_EXPERTS, D_FF, d)
    b2_all = exp_b2.reshape(DEPTH * N_EXPERTS, 1, d)
    moe = None
    for l in range(DEPTH):
        mod3 = mod[l].reshape(batch, 1, 6 * d)
        if moe is None:
            qkv, z, ab, qb, kvb, uc, gates = _in_proj(xt, mod3, w_in_all, conv_w, seq, l)
        else:
            xt, qkv, z, ab, qb, kvb, uc, gates = _in_proj(xt, mod3, w_in_all, conv_w, seq, l, moe)
        gpar = jnp.stack([jnp.concatenate([-jnp.exp(a_log[l]), lane_pad]), jnp.concatenate([dt_bias[l], lane_pad])])
        xt = _mixer(xt, qkv, ab, z, qb, kvb, uc, gates, mod3, gpar, gdn_norm_w[l].reshape(1, GDN_DV), sinks[l],
                    pool_w[l].astype(BF16), pool_scale[l].reshape(1, POOL_DIM),
                    w_pa[l].astype(BF16), w_pb[l].astype(BF16), w_pc[l].astype(BF16), w_o[l].astype(BF16),
                    ln1_g[l].reshape(1, d), ln1_b[l].reshape(1, d), batch, seq)
        hp, topi_t, topg_t, rank_t, cnt = _router(xt, mod3, router_w[l].T, router_b[l].reshape(N_EXPERTS, 1), seq)
        dest_t, block_e, n_valid = _dispatch_plan(topi_t, rank_t, cnt[:, 0], n_blocks)
        dest_flat = dest_t.reshape(TOP_K * t)
        xs = _sc_dispatch(hp, dest_flat, rows)
        ys = _experts(xs, block_e, n_valid, w1_all, b1_all, w2_all, b2_all, l)
        yg = _sc_gather(ys, dest_flat).reshape(TOP_K, t, PACKED_D)
        moe = (yg, topg_t, mod3, ln2_g[l].reshape(1, d), ln2_b[l].reshape(1, d))
    xt = _combine(xt, *moe, seq)
    return xt.reshape(batch, seq, d)
```

```python
import functools

import jax
import jax.numpy as jnp
from jax import lax
from jax.experimental import pallas as pl
from jax.experimental.pallas import tpu as pltpu
from jax.experimental.pallas import tpu_sc as plsc

D_MODEL = 1024
DEPTH = 4
GDN_HEADS = 4
GDN_DK = 128
GDN_DV = 128
GDN_CONV = 4
GDN_CHUNK = 64
SWA_HQ = 8
SWA_HKV = 2
SWA_DH = 64
WINDOW = 128
POOL_WINDOWS = (2, 4, 8, 16)
POOL_GROUPS = 4
POOL_GDIM = 128
N_BRANCH = 3
N_EXPERTS = 32
TOP_K = 4
D_FF = 1024
ROUTE_BLOCK = 1024
EXPERT_ROWS = 512
SWIGLU_ALPHA = 1.702
SWIGLU_LIMIT = 7.0
LN_EPS = 1e-5
RMS_EPS = 1e-6
DEEPNORM_ALPHA = (2 * DEPTH) ** 0.25

GDN_QK = GDN_HEADS * GDN_DK
GDN_V = GDN_HEADS * GDN_DV
SWA_Q = SWA_HQ * SWA_DH
SWA_KV = SWA_HKV * SWA_DH
POOL_DIM = POOL_GROUPS * POOL_GDIM
GQA_GROUP = SWA_HQ // SWA_HKV
MAX_POOL = max(POOL_WINDOWS)

QKV_W = 2 * GDN_QK + GDN_V
AB_W = 128
SEG_QKV = 0
SEG_Z = SEG_QKV + QKV_W
SEG_AB = SEG_Z + GDN_V
SEG_QB = SEG_AB + AB_W
SEG_KVB = SEG_QB + SWA_Q
SEG_UC = SEG_KVB + 2 * SWA_KV
SEG_GATE = SEG_UC + POOL_DIM
PROJ_W = SEG_GATE + N_BRANCH * D_MODEL

PROJ_CHUNK = 256
TOKEN_TILE = 512
ROUTER_TILE = 1024
NEG_BIG = -1e30
VMEM_LIMIT = 56 * 1024 * 1024

PACKED_D = D_MODEL // 2
SC_CORES = 2
SC_SUBCORES = 16
SC_WORKERS = SC_CORES * SC_SUBCORES
SC_CHUNK = 64

F32 = jnp.float32
BF16 = jnp.bfloat16
U32 = jnp.uint32
HI_MASK = 0xFFFF0000


def _pack_rows(x):
    bits = pltpu.bitcast(x.astype(BF16).astype(F32), U32)
    return (bits[:, :PACKED_D] >> 16) | (bits[:, PACKED_D:] & jnp.uint32(HI_MASK))


def _unpack_rows(w):
    lo = pltpu.bitcast(w << 16, F32)
    hi = pltpu.bitcast(w & jnp.uint32(HI_MASK), F32)
    return jnp.concatenate([lo, hi], axis=1)


def _params(*sem):
    return pltpu.CompilerParams(dimension_semantics=sem, vmem_limit_bytes=VMEM_LIMIT)


def _sigmoid(x):
    return 0.5 * jnp.tanh(0.5 * x) + 0.5


def _layer_norm(x):
    mu = jnp.mean(x, -1, keepdims=True)
    xc = x - mu
    var = jnp.mean(xc * xc, -1, keepdims=True)
    return xc * lax.rsqrt(var + LN_EPS)


def _dot(a, b):
    return jnp.dot(a, b, preferred_element_type=F32)


def _dot_nt(a, b):
    return lax.dot_general(a, b, (((1,), (1,)), ((), ())), preferred_element_type=F32)


def _bdot(a, b):
    return lax.dot_general(a, b, (((2,), (1,)), ((0,), (0,))), preferred_element_type=F32)


def _bdot_tn(a, b):
    return lax.dot_general(a, b, (((1,), (1,)), ((0,), (0,))), preferred_element_type=F32)


def _bdot_nt(a, b):
    return lax.dot_general(a, b, (((2,), (2,)), ((0,), (0,))), preferred_element_type=F32)


def _ada_kernel(c_ref, w_ref, b_ref, o_ref):
    c = c_ref[...]
    cond = c * _sigmoid(c)
    o_ref[0] = _dot(cond.astype(BF16), w_ref[0].astype(BF16)) + b_ref[0]


def _ada_mod(c, ada_w, ada_b):
    depth, d, n = ada_w.shape
    b = c.shape[0]
    tn = 1024
    return pl.pallas_call(
        _ada_kernel,
        grid=(depth, n // tn),
        in_specs=[
            pl.BlockSpec((b, d), lambda l, j: (0, 0)),
            pl.BlockSpec((1, d, tn), lambda l, j: (l, 0, j)),
            pl.BlockSpec((1, 1, tn), lambda l, j: (l, 0, j)),
        ],
        out_specs=pl.BlockSpec((1, b, tn), lambda l, j: (l, 0, j)),
        out_shape=jax.ShapeDtypeStruct((depth, b, n), F32),
        compiler_params=_params("parallel", "parallel"),
        name="ada_mod",
    )(c, ada_w, ada_b.reshape(depth, 1, n))


HEAD_UNITS = 4


def _moe_combine(x, yg_rows, gates_t, gt, ln_g, ln_b):
    g = jnp.concatenate([gates_t, jnp.zeros((128 - TOP_K, gates_t.shape[1]), F32)], axis=0).T
    y = g[:, 0:1] * _unpack_rows(yg_rows[0])
    for k in range(1, TOP_K):
        y = y + g[:, k:k + 1] * _unpack_rows(yg_rows[k])
    return _deepnorm(x, y, gt, ln_g, ln_b)


def _proj_kernel(*refs, tm, per_b, fused):
    if fused:
        (x_ref, yg_ref, topg_ref, gt_ref, lng_ref, lnb_ref, sh_ref, sc_ref, w_ref, convw_ref,
         xo_ref, qkv_ref, z_ref, ab_ref, qb_ref, kvb_ref, uc_ref, gate_ref, h_ref, raw_ref) = refs
    else:
        (x_ref, sh_ref, sc_ref, w_ref, convw_ref,
         qkv_ref, z_ref, ab_ref, qb_ref, kvb_ref, uc_ref, gate_ref, h_ref, raw_ref) = refs
    i = pl.program_id(0)
    slot = i % 2

    @pl.when(i == 0)
    def _():
        h_ref[1] = jnp.zeros(h_ref.shape[1:], BF16)

    rows_per_unit = tm // HEAD_UNITS

    def head_unit(u):
        r = slice(u * rows_per_unit, (u + 1) * rows_per_unit)
        x = x_ref[r, :]
        if fused:
            x = _moe_combine(x, [yg_ref[k, r, :] for k in range(TOP_K)], topg_ref[:, r], gt_ref[0],
                             lng_ref[...], lnb_ref[...])
            xo_ref[r, :] = x
        h_ref[slot, r, :] = (_layer_norm(x) * (1.0 + sc_ref[0]) + sh_ref[0]).astype(BF16)

    h = h_ref[1 - slot]

    def seg(start, width):
        return _dot(h, w_ref[:, start:start + width])

    seq_start = (i == 0) | ((i + per_b - 1) % per_b == 0)

    @pl.when(seq_start)
    def _():
        raw_ref[0:8, :] = jnp.zeros((8, QKV_W), F32)

    @pl.when(jnp.logical_not(seq_start))
    def _():
        raw_ref[0:8, :] = raw_ref[tm:tm + 8, :]

    for j in range(QKV_W // PROJ_CHUNK):
        cols = slice(j * PROJ_CHUNK, (j + 1) * PROJ_CHUNK)
        raw_ref[8:8 + tm, cols] = seg(SEG_QKV + j * PROJ_CHUNK, PROJ_CHUNK)
        if j < HEAD_UNITS:
            head_unit(j)

    def conv_silu(j):
        cols = slice(j * PROJ_CHUNK, (j + 1) * PROJ_CHUNK)
        conv = raw_ref[8:8 + tm, cols] * convw_ref[GDN_CONV - 1:GDN_CONV, cols]
        for tap in range(GDN_CONV - 1):
            conv = conv + raw_ref[5 + tap:5 + tap + tm, cols] * convw_ref[tap:tap + 1, cols]
        qkv_ref[:, cols] = (conv * _sigmoid(conv)).astype(BF16)
    n_conv = QKV_W // PROJ_CHUNK
    done = 0
    for ref, start, width in ((z_ref, SEG_Z, GDN_V), (qb_ref, SEG_QB, SWA_Q), (kvb_ref, SEG_KVB, 2 * SWA_KV),
                              (uc_ref, SEG_UC, POOL_DIM)):
        for j in range(width // PROJ_CHUNK):
            cols = slice(j * PROJ_CHUNK, (j + 1) * PROJ_CHUNK)
            ref[:, cols] = seg(start + j * PROJ_CHUNK, PROJ_CHUNK).astype(BF16)
            if done < n_conv:
                conv_silu(done)
                done += 1
    assert done == n_conv and HEAD_UNITS <= n_conv
    ab_ref[...] = seg(SEG_AB, AB_W)
    gate_ref[...] = _sigmoid(seg(SEG_GATE, N_BRANCH * D_MODEL)).astype(BF16)


def _in_proj(x, mod3, w, conv_w, seq, layer, moe=None):
    t, d = x.shape
    tm = TOKEN_TILE
    per_b = seq // tm
    n_tiles = t // tm
    widths = (QKV_W, GDN_V, AB_W, SWA_Q, 2 * SWA_KV, POOL_DIM, N_BRANCH * D_MODEL)
    dtypes = (BF16, BF16, F32, BF16, BF16, BF16, BF16)

    def head(i):
        return jnp.minimum(i, n_tiles - 1)

    def proj(i):
        return jnp.maximum(i - 1, 0)

    def mod_spec(k):
        return pl.BlockSpec((1, 1, d), lambda i: (head(i) // per_b, 0, k))

    in_specs = [pl.BlockSpec((tm, d), lambda i: (head(i), 0))]
    operands = [x]
    out_specs = [pl.BlockSpec((tm, n), lambda i: (proj(i), 0)) for n in widths]
    out_shape = [jax.ShapeDtypeStruct((t, n), dt) for n, dt in zip(widths, dtypes)]
    if moe is not None:
        yg, topg_t, mod3_prev, ln_g, ln_b = moe
        in_specs += [pl.BlockSpec((TOP_K, tm, PACKED_D), lambda i: (0, head(i), 0)),
                     pl.BlockSpec((TOP_K, tm), lambda i: (0, head(i))),
                     mod_spec(5),
                     pl.BlockSpec((1, d), lambda i: (0, 0)), pl.BlockSpec((1, d), lambda i: (0, 0))]
        operands += [yg, topg_t, mod3_prev, ln_g, ln_b]
        out_specs = [pl.BlockSpec((tm, d), lambda i: (head(i), 0))] + out_specs
        out_shape = [jax.ShapeDtypeStruct((t, d), F32)] + out_shape
    in_specs += [mod_spec(0), mod_spec(1),
                 pl.BlockSpec((None, d, PROJ_W), lambda i: (layer, 0, 0), pipeline_mode=pl.Buffered(1)),
                 pl.BlockSpec((None, GDN_CONV, QKV_W), lambda i: (layer, 0, 0))]
    operands += [mod3, mod3, w, conv_w]
    return pl.pallas_call(
        functools.partial(_proj_kernel, tm=tm, per_b=per_b, fused=moe is not None),
        grid=(n_tiles + 1,),
        in_specs=in_specs,
        out_specs=out_specs,
        out_shape=out_shape,
        scratch_shapes=[pltpu.VMEM((2, tm, d), BF16), pltpu.VMEM((8 + tm, QKV_W), F32)],
        compiler_params=_params("arbitrary"),
        name="in_proj",
    )(*operands)


def _softplus(x):
    return jnp.maximum(x, 0.0) + jnp.log(1.0 + jnp.exp(-jnp.abs(x)))


def _unit_lower_inverse(lm, row, col, between_levels=lambda: None):
    n = lm.shape[-1]
    eye = (row == col).astype(F32)
    t = None
    s = 1
    while s < n:
        join = ((row // (2 * s)) == (col // (2 * s))) & ((row % (2 * s)) >= s) & ((col % (2 * s)) < s)
        cm = jnp.where(join, lm, 0.0)
        if t is None:
            t = eye - cm
        else:
            p = _bdot(t, cm)
            t = t - _bdot(p, t)
            between_levels()
        s *= 2
    return t


def _gdn_kernel(qkv_ref, ab_ref, z_ref, gpar_ref, nw_ref, o_ref, state_ref, *, tb, filler=lambda: None):
    c_len = GDN_CHUNK
    nc = tb // c_len

    @pl.when(pl.program_id(1) == 0)
    def _():
        state_ref[...] = jnp.zeros_like(state_ref)

    ab = ab_ref[...]
    gpar = gpar_ref[...]
    g_all = gpar[0:1] * _softplus(ab + gpar[1:2])
    beta_all = _sigmoid(ab)
    pos = lax.broadcasted_iota(jnp.int32, (tb, AB_W), 0) % c_len
    gc_all = g_all
    s = 1
    while s < c_len:
        gc_all = gc_all + jnp.where(pos >= s, pltpu.roll(gc_all, s, axis=0), 0.0)
        s *= 2

    nh = GDN_HEADS
    nb = nc * nh
    row = lax.broadcasted_iota(jnp.int32, (nb, c_len, c_len), 1)
    col = lax.broadcasted_iota(jnp.int32, (nb, c_len, c_len), 2)
    causal = row >= col
    strict = row > col
    diag = row == col

    def chunked(per_head):
        return jnp.stack([per_head(h).reshape(nc, c_len, GDN_DK) for h in range(nh)], axis=1).reshape(nb, c_len, GDN_DK)

    q = chunked(lambda h: qkv_ref[:, h * GDN_DK:(h + 1) * GDN_DK].astype(F32))
    k = chunked(lambda h: qkv_ref[:, GDN_QK + h * GDN_DK:GDN_QK + (h + 1) * GDN_DK].astype(F32))
    v = chunked(lambda h: qkv_ref[:, 2 * GDN_QK + h * GDN_DV:2 * GDN_QK + (h + 1) * GDN_DV].astype(F32))
    gc = chunked(lambda h: jnp.broadcast_to(gc_all[:, h:h + 1], (tb, GDN_DK)))
    beta = chunked(lambda h: jnp.broadcast_to(beta_all[:, nh + h:nh + h + 1], (tb, GDN_DK)))
    q = q * lax.rsqrt(jnp.sum(q * q, -1, keepdims=True) + 1e-6) * (GDN_DK ** -0.5)
    k = k * lax.rsqrt(jnp.sum(k * k, -1, keepdims=True) + 1e-6)

    gc_i = gc[:, :, :c_len]
    gc_j = jnp.sum(jnp.where(diag, gc_i, 0.0), axis=1, keepdims=True)
    decay = jnp.where(causal, jnp.exp(jnp.where(causal, gc_i - gc_j, 0.0)), 0.0)
    eg = jnp.exp(gc)
    gc_last = gc[:, c_len - 1:c_len, :]
    g_last = jnp.exp(gc_last)
    k_beta = k * beta
    kq = _bdot_nt(jnp.concatenate([k_beta, q], axis=1).astype(BF16), k.astype(BF16))
    lower = jnp.where(strict, kq[:, :c_len] * decay, 0.0)
    attn = jnp.where(causal, kq[:, c_len:] * decay, 0.0).astype(BF16)
    tinv = _unit_lower_inverse(lower, row, col, filler)
    uw = _bdot(tinv.astype(BF16), jnp.concatenate([v * beta, k_beta * eg], axis=2).astype(BF16)).astype(BF16)
    q_g = q * eg
    k_g = (k * jnp.exp(gc_last - gc)).astype(BF16)

    k_uw = _bdot_tn(k_g, uw)
    a_uw = _bdot(attn, uw)
    ku = k_uw[:, :, :GDN_DV]
    o_loc = a_uw[:, :, :GDN_DV]
    qk_eff = jnp.concatenate([q_g - a_uw[:, :, GDN_DV:], k_uw[:, :, GDN_DV:]], axis=1).astype(BF16)

    st = state_ref[...]
    outs = []
    for c in range(nc):
        sl = slice(c * nh, (c + 1) * nh)
        prod = _bdot(qk_eff[sl], st.astype(BF16))
        outs.append(prod[:, :c_len] + o_loc[sl])
        st = st * g_last[sl] - prod[:, c_len:] + ku[sl]
        filler()
    state_ref[...] = st

    nw = nw_ref[...]
    for h in range(nh):
        o = jnp.concatenate([outs[c][h] for c in range(nc)], axis=0)
        zh = z_ref[:, h * GDN_DV:(h + 1) * GDN_DV].astype(F32)
        o = o * lax.rsqrt(jnp.mean(o * o, -1, keepdims=True) + RMS_EPS) * nw * (zh * _sigmoid(zh))
        o_ref[:, h * GDN_DV:(h + 1) * GDN_DV] = o.astype(BF16)


def _swa_units(sink_ref, q_ref, kv_ref, kvp_ref, bias_ref, o_ref, *, tq):
    nw = tq // WINDOW
    rows = GQA_GROUP * WINDOW
    scale = SWA_DH ** -0.5

    def unit(hk, w):
        heads = range(hk * GQA_GROUP, (hk + 1) * GQA_GROUP)
        tok = slice(w * WINDOW, (w + 1) * WINDOW)

        def keys(col0):
            cols = slice(col0, col0 + SWA_DH)
            prev = kvp_ref[:, cols] if w == 0 else kv_ref[(w - 1) * WINDOW:w * WINDOW, cols]
            return jnp.concatenate([prev, kv_ref[tok, cols]], axis=0)

        k = keys(hk * SWA_DH)
        v = keys(SWA_KV + hk * SWA_DH)
        q = jnp.concatenate([q_ref[tok, hq * SWA_DH:(hq + 1) * SWA_DH] for hq in heads], axis=0)
        sink = jnp.concatenate([jnp.full((WINDOW, 1), sink_ref[hq], F32) for hq in heads], axis=0)
        qi = lax.broadcasted_iota(jnp.int32, (rows, WINDOW), 0) % WINDOW
        kj = lax.broadcasted_iota(jnp.int32, (rows, WINDOW), 1)
        from_prev = kj > qi
        s2 = _dot_nt(q, k)
        s = jnp.where(from_prev, s2[:, :WINDOW], s2[:, WINDOW:]) * scale + bias_ref[hk]
        if w == 0:
            s = jnp.where(from_prev & (pl.program_id(1) == 0), NEG_BIG, s)
        m = jnp.maximum(jnp.max(s, -1, keepdims=True), sink)
        p = jnp.exp(s - m)
        p2 = jnp.concatenate([jnp.where(from_prev, p, 0.0), jnp.where(from_prev, 0.0, p)], axis=1).astype(BF16)
        v_ext = jnp.concatenate([v, jnp.ones((2 * WINDOW, SWA_DH), BF16)], axis=1)
        pv = _dot(p2, v_ext)
        o = pv[:, :SWA_DH] / (pv[:, SWA_DH:SWA_DH + 1] + jnp.exp(sink - m))
        for g, hq in enumerate(heads):
            o_ref[tok, hq * SWA_DH:(hq + 1) * SWA_DH] = o[g * WINDOW:(g + 1) * WINDOW, :].astype(BF16)

    return [functools.partial(unit, hk, w) for hk in range(SWA_HKV) for w in range(nw)]


def _swa_bias():
    qi = jnp.arange(WINDOW)[:, None]
    kj = jnp.arange(WINDOW)[None, :]
    dist = jnp.where(kj > qi, qi - kj + WINDOW, qi - kj).astype(F32)
    slopes = 2.0 ** (-8.0 * jnp.arange(1, SWA_HQ + 1, dtype=F32) / SWA_HQ)
    bias = -slopes[:, None, None] * dist[None]
    return bias.reshape(SWA_HKV, GQA_GROUP * WINDOW, WINDOW)


def _deepnorm(x, y, gt, g, b):
    return _layer_norm(DEEPNORM_ALPHA * x + (1.0 + gt) * y) * g + b


def _merge_kernel(x_ref, ya_ref, yb_ref, uc_ref, ucp_ref, gate_ref, gt_ref, poolw_ref, pscale_ref,
                  wpa_ref, wpb_ref, wpc_ref, wo_ref, lng_ref, lnb_ref, o_ref, *, tm):
    j = pl.program_id(1)
    u = uc_ref[...].astype(F32)
    halo = jnp.where(j == 0, 0.0, ucp_ref[...].astype(F32))
    ue = jnp.concatenate([halo, u], axis=0)
    tpos = (j * tm + lax.broadcasted_iota(jnp.int32, (tm, POOL_GDIM), 0) + 1).astype(F32)
    ycs = []
    for gi, win in enumerate(POOL_WINDOWS):
        a = ue[:, gi * POOL_GDIM:(gi + 1) * POOL_GDIM]
        span = 1
        while span < win:
            a = a[span:] + a[:-span]
            span *= 2
        lo = MAX_POOL - win + 1
        d = a[lo:lo + tm] / jnp.minimum(tpos, float(win)) - u[:, gi * POOL_GDIM:(gi + 1) * POOL_GDIM]
        ycs.append(_dot(d.astype(BF16), poolw_ref[gi]))
    yc = jnp.concatenate(ycs, axis=1) * pscale_ref[...]
    merged = gate_ref[:, 0:D_MODEL].astype(F32) * _dot(ya_ref[...], wpa_ref[...])
    merged = merged + gate_ref[:, D_MODEL:2 * D_MODEL].astype(F32) * _dot(yb_ref[...], wpb_ref[...])
    merged = merged + gate_ref[:, 2 * D_MODEL:3 * D_MODEL].astype(F32) * _dot(yc.astype(BF16), wpc_ref[...])
    y = _dot(merged.astype(BF16), wo_ref[...])
    o_ref[...] = _deepnorm(x_ref[...], y, gt_ref[0], lng_ref[...], lnb_ref[...])


def _mixer_kernel(sink_ref, x_ref, qkv_ref, ab_ref, z_ref, gpar_ref, nw_ref, qb_ref, kvb_ref, kvp_ref, bias_ref,
                  uc_ref, ucp_ref, gate_ref, gt_ref, poolw_ref, pscale_ref, wpa_ref, wpb_ref, wpc_ref, wo_ref,
                  lng_ref, lnb_ref, o_ref, state_ref, ya_ref, yb_ref, *, tm):
    pending = iter(_swa_units(sink_ref, qb_ref, kvb_ref, kvp_ref, bias_ref, yb_ref, tq=tm))

    def filler():
        unit = next(pending, None)
        if unit is not None:
            unit()

    _gdn_kernel(qkv_ref, ab_ref, z_ref, gpar_ref, nw_ref, ya_ref, state_ref, tb=tm, filler=filler)
    for unit in pending:
        unit()
    _merge_kernel(x_ref, ya_ref, yb_ref, uc_ref, ucp_ref, gate_ref, gt_ref, poolw_ref, pscale_ref,
                  wpa_ref, wpb_ref, wpc_ref, wo_ref, lng_ref, lnb_ref, o_ref, tm=tm)


def _mixer(x, qkv, ab, z, qb, kvb, uc, gates, mod3, gpar, norm_w, sinks, pool_w, pool_scale, w_pa, w_pb, w_pc, w_o,
           ln_g, ln_b, batch, seq):
    t, d = x.shape
    tm = TOKEN_TILE
    per_b = seq // tm

    def tok(n):
        return pl.BlockSpec((tm, n), lambda b, j: (b * per_b + j, 0))

    def prev_rows(rows, n):
        return pl.BlockSpec((rows, n), lambda b, j: (jnp.maximum((b * per_b + j) * (tm // rows) - 1, 0), 0))

    def full(shape):
        return pl.BlockSpec(shape, lambda b, j: (0,) * len(shape))

    return pl.pallas_call(
        functools.partial(_mixer_kernel, tm=tm),
        grid=(batch, per_b),
        in_specs=[
            pl.BlockSpec(memory_space=pltpu.SMEM),
            tok(d), tok(QKV_W), tok(AB_W), tok(GDN_V), full((2, AB_W)), full((1, GDN_DV)),
            tok(SWA_Q), tok(2 * SWA_KV), prev_rows(WINDOW, 2 * SWA_KV),
            full((SWA_HKV, GQA_GROUP * WINDOW, WINDOW)),
            tok(POOL_DIM), prev_rows(MAX_POOL, POOL_DIM),
            tok(N_BRANCH * d),
            pl.BlockSpec((1, 1, d), lambda b, j: (b, 0, 2)),
            full((POOL_GROUPS, POOL_GDIM, POOL_GDIM)), full((1, POOL_DIM)),
            full((GDN_V, d)), full((SWA_Q, d)), full((POOL_DIM, d)), full((d, d)),
            full((1, d)), full((1, d)),
        ],
        out_specs=tok(d),
        out_shape=jax.ShapeDtypeStruct((t, d), F32),
        scratch_shapes=[pltpu.VMEM((GDN_HEADS, GDN_DK, GDN_DV), F32), pltpu.VMEM((tm, GDN_V), BF16),
                        pltpu.VMEM((tm, SWA_Q), BF16)],
        compiler_params=_params("parallel", "arbitrary"),
        name="mixer",
    )(sinks, x, qkv, ab, z, gpar, norm_w, qb, kvb, kvb, _swa_bias(), uc, uc, gates, mod3, pool_w, pool_scale,
      w_pa, w_pb, w_pc, w_o, ln_g, ln_b)


def _router_kernel(x_ref, sh_ref, sc_ref, rwt_ref, rb_ref, tri_ref, hp_ref, topi_ref, topg_ref, rank_ref, cnt_ref,
                   run_ref):
    @pl.when(pl.program_id(0) == 0)
    def _():
        run_ref[...] = jnp.zeros_like(run_ref)

    h = _layer_norm(x_ref[...]) * (1.0 + sc_ref[0]) + sh_ref[0]
    hp_ref[...] = _pack_rows(h)
    logits = lax.dot_general(rwt_ref[...], h, (((1,), (1,)), ((), ())), preferred_element_type=F32,
                             precision=lax.Precision.HIGHEST) + rb_ref[...]
    sub = lax.broadcasted_iota(jnp.int32, logits.shape, 0)
    vals, idxs = [], []
    for _ in range(TOP_K):
        m = jnp.max(logits, 0, keepdims=True)
        idx = jnp.min(jnp.where(logits == m, sub, N_EXPERTS), 0, keepdims=True)
        vals.append(m)
        idxs.append(idx)
        logits = jnp.where(sub == idx, -jnp.inf, logits)
    es = [jnp.exp(v - vals[0]) for v in vals]
    denom = es[0] + es[1] + es[2] + es[3]
    topi_ref[...] = jnp.concatenate(idxs, axis=0)
    topg_ref[...] = jnp.concatenate([e / denom for e in es], axis=0)

    sel = jnp.zeros(logits.shape, F32)
    for idx in idxs:
        sel = sel + (sub == idx).astype(F32)
    before = run_ref[:, 0:1] + _dot(sel.astype(BF16), tri_ref[...])
    ranks = [jnp.sum(jnp.where(sub == idx, before, 0.0), 0, keepdims=True) for idx in idxs]
    rank_ref[...] = jnp.concatenate(ranks, axis=0).astype(jnp.int32)
    run_ref[...] = run_ref[...] + jnp.sum(sel, 1, keepdims=True)
    cnt_ref[...] = run_ref[...].astype(jnp.int32)


def _router(x, mod3, router_wt, router_b, seq):
    t, d = x.shape
    tm = ROUTER_TILE
    per_b = seq // tm
    tri = (jnp.arange(tm)[:, None] < jnp.arange(tm)[None, :]).astype(BF16)
    return pl.pallas_call(
        _router_kernel,
        grid=(t // tm,),
        in_specs=[
            pl.BlockSpec((tm, d), lambda i: (i, 0)),
            pl.BlockSpec((1, 1, d), lambda i: (i // per_b, 0, 3)),
            pl.BlockSpec((1, 1, d), lambda i: (i // per_b, 0, 4)),
            pl.BlockSpec((N_EXPERTS, d), lambda i: (0, 0)),
            pl.BlockSpec((N_EXPERTS, 1), lambda i: (0, 0)),
            pl.BlockSpec((tm, tm), lambda i: (0, 0)),
        ],
        out_specs=[
            pl.BlockSpec((tm, PACKED_D), lambda i: (i, 0)),
            pl.BlockSpec((TOP_K, tm), lambda i: (0, i)),
            pl.BlockSpec((TOP_K, tm), lambda i: (0, i)),
            pl.BlockSpec((TOP_K, tm), lambda i: (0, i)),
            pl.BlockSpec((N_EXPERTS, 128), lambda i: (0, 0)),
        ],
        out_shape=[
            jax.ShapeDtypeStruct((t, PACKED_D), U32),
            jax.ShapeDtypeStruct((TOP_K, t), jnp.int32),
            jax.ShapeDtypeStruct((TOP_K, t), F32),
            jax.ShapeDtypeStruct((TOP_K, t), jnp.int32),
            jax.ShapeDtypeStruct((N_EXPERTS, 128), jnp.int32),
        ],
        scratch_shapes=[pltpu.VMEM((N_EXPERTS, 128), F32)],
        compiler_params=_params("arbitrary"),
        name="router",
    )(x, mod3, mod3, router_wt, router_b, tri)


def _dispatch_plan(topi_t, rank_t, counts, n_blocks):
    padded = (counts + ROUTE_BLOCK - 1) // ROUTE_BLOCK * ROUTE_BLOCK
    ends_p = jnp.cumsum(padded)
    pstart = ends_p - padded
    experts = jnp.arange(N_EXPERTS, dtype=jnp.int32)
    pstart_tok = jnp.sum(jnp.where(topi_t[:, :, None] == experts, pstart, 0), -1)
    dest_t = (pstart_tok + rank_t).astype(jnp.int32)
    blk_start = jnp.arange(n_blocks, dtype=jnp.int32) * ROUTE_BLOCK
    block_e = jnp.minimum(jnp.sum((blk_start[:, None] >= ends_p).astype(jnp.int32), -1), N_EXPERTS - 1)
    blk_end = jnp.sum(jnp.where(block_e[:, None] == experts, pstart + counts, 0), -1)
    n_valid = jnp.clip(blk_end - blk_start, 0, ROUTE_BLOCK).astype(jnp.int32)
    return dest_t, block_e.astype(jnp.int32), n_valid


def _sc_mesh():
    return plsc.VectorSubcoreMesh(core_axis_name="c", subcore_axis_name="s")


def _sc_dispatch(hp, dest_flat, rows):
    t, dp = hp.shape
    per_w = t // SC_WORKERS
    n_chunks = per_w // SC_CHUNK

    assert n_chunks % 2 == 0

    @functools.partial(
        pl.kernel, mesh=_sc_mesh(), out_type=jax.ShapeDtypeStruct((rows, dp), hp.dtype),
        scratch_types=[pltpu.VMEM((TOP_K * n_chunks, SC_CHUNK), jnp.int32),
                       pltpu.VMEM((SC_CHUNK, dp), hp.dtype), pltpu.VMEM((SC_CHUNK, dp), hp.dtype),
                       pltpu.SemaphoreType.DMA, pltpu.SemaphoreType.DMA,
                       pltpu.SemaphoreType.DMA, pltpu.SemaphoreType.DMA])
    def dispatch(h_hbm, d_hbm, xs_hbm, idx_v, rows_a, rows_b, load_a, load_b, scat_a, scat_b):
        worker = lax.axis_index("s") * SC_CORES + lax.axis_index("c")
        base = worker * per_w
        for k in range(TOP_K):
            pltpu.sync_copy(d_hbm.at[pl.ds((k * SC_WORKERS + worker) * n_chunks, n_chunks)],
                            idx_v.at[pl.ds(k * n_chunks, n_chunks)])

        def load(i, buf, sem):
            return pltpu.make_async_copy(h_hbm.at[pl.ds(base + i * SC_CHUNK, SC_CHUNK)], buf, sem)

        def scatter(i, k, buf, sem):
            return pltpu.make_async_copy(buf, xs_hbm.at[idx_v.at[k * n_chunks + i]], sem)

        def scatter_all(i, buf, sem):
            for k in range(TOP_K):
                scatter(i, k, buf, sem).start()
            for k in range(TOP_K):
                scatter(i, k, buf, sem).wait()

        load(0, rows_a, load_a).start()

        @pl.loop(0, n_chunks, step=2)
        def _(i):
            load(i, rows_a, load_a).wait()
            load(i + 1, rows_b, load_b).start()
            scatter_all(i, rows_a, scat_a)
            load(i + 1, rows_b, load_b).wait()

            @pl.when(i + 2 < n_chunks)
            def _():
                load(i + 2, rows_a, load_a).start()

            scatter_all(i + 1, rows_b, scat_b)

    return dispatch(hp, dest_flat.reshape(TOP_K * SC_WORKERS * n_chunks, SC_CHUNK))


def _sc_gather(table, idx):
    n = idx.shape[0]
    dp = table.shape[1]
    per_w = n // SC_WORKERS
    n_chunks = per_w // SC_CHUNK

    assert n_chunks % 2 == 0

    @functools.partial(
        pl.kernel, mesh=_sc_mesh(), out_type=jax.ShapeDtypeStruct((n, dp), table.dtype),
        scratch_types=[pltpu.VMEM((n_chunks, SC_CHUNK), jnp.int32),
                       pltpu.VMEM((SC_CHUNK, dp), table.dtype), pltpu.VMEM((SC_CHUNK, dp), table.dtype),
                       pltpu.SemaphoreType.DMA, pltpu.SemaphoreType.DMA,
                       pltpu.SemaphoreType.DMA, pltpu.SemaphoreType.DMA])
    def gather(t_hbm, i_hbm, o_hbm, idx_v, rows_a, rows_b, fetch_a, fetch_b, store_a, store_b):
        worker = lax.axis_index("s") * SC_CORES + lax.axis_index("c")
        base = worker * per_w
        pltpu.sync_copy(i_hbm.at[pl.ds(worker * n_chunks, n_chunks)], idx_v)

        def fetch(i, buf, sem):
            return pltpu.make_async_copy(t_hbm.at[idx_v.at[i]], buf, sem)

        def store(i, buf, sem):
            return pltpu.make_async_copy(buf, o_hbm.at[pl.ds(base + i * SC_CHUNK, SC_CHUNK)], sem)

        fetch(0, rows_a, fetch_a).start()

        @pl.loop(0, n_chunks, step=2)
        def _(i):
            fetch(i, rows_a, fetch_a).wait()

            @pl.when(i > 0)
            def _():
                store(i - 1, rows_b, store_b).wait()

            fetch(i + 1, rows_b, fetch_b).start()
            store(i, rows_a, store_a).start()
            fetch(i + 1, rows_b, fetch_b).wait()
            store(i, rows_a, store_a).wait()

            @pl.when(i + 2 < n_chunks)
            def _():
                fetch(i + 2, rows_a, fetch_a).start()

            store(i + 1, rows_b, store_b).start()

        store(n_chunks - 1, rows_b, store_b).wait()

    return gather(table, idx.reshape(SC_WORKERS * n_chunks, SC_CHUNK))


def _expert_kernel(be_ref, nv_ref, x_ref, w1_ref, b1_ref, w2_ref, b2_ref, o_ref, w1b_ref, w2b_ref):
    i = pl.program_id(0)
    n_valid = nv_ref[i]
    new_expert = (i == 0) | (be_ref[i] != be_ref[jnp.maximum(i - 1, 0)])

    @pl.when(new_expert & (n_valid > 0))
    def _():
        w1b_ref[...] = w1_ref[0].astype(BF16)
        w2b_ref[...] = w2_ref[0].astype(BF16)

    for piece in range(ROUTE_BLOCK // EXPERT_ROWS):
        r = slice(piece * EXPERT_ROWS, (piece + 1) * EXPERT_ROWS)
        left = n_valid - piece * EXPERT_ROWS

        @pl.when(left > 0)
        def _():
            live = lax.broadcasted_iota(jnp.int32, (EXPERT_ROWS, D_MODEL), 0) < left
            x = jnp.where(live, _unpack_rows(x_ref[r, :]), 0.0).astype(BF16)
            gu = _dot(x, w1b_ref[...]) + b1_ref[0]
            glu = jnp.minimum(gu[:, :D_FF], SWIGLU_LIMIT)
            lin = jnp.clip(gu[:, D_FF:], -SWIGLU_LIMIT, SWIGLU_LIMIT)
            act = glu * _sigmoid(SWIGLU_ALPHA * glu) * (lin + 1.0)
            o_ref[r, :] = _pack_rows(_dot(act.astype(BF16), w2b_ref[...]) + b2_ref[0])

        @pl.when(left <= 0)
        def _():
            o_ref[r, :] = jnp.zeros((EXPERT_ROWS, PACKED_D), U32)


def _experts(xs, block_e, n_valid, w1, b1, w2, b2, layer):
    rows, dp = xs.shape
    d = D_MODEL
    n_blocks = rows // ROUTE_BLOCK
    e0 = layer * N_EXPERTS
    grid_spec = pltpu.PrefetchScalarGridSpec(
        num_scalar_prefetch=2,
        grid=(n_blocks,),
        in_specs=[
            pl.BlockSpec((ROUTE_BLOCK, dp), lambda i, be, nv: (i, 0)),
            pl.BlockSpec((1, d, 2 * D_FF), lambda i, be, nv: (e0 + be[i], 0, 0)),
            pl.BlockSpec((1, 1, 2 * D_FF), lambda i, be, nv: (e0 + be[i], 0, 0)),
            pl.BlockSpec((1, D_FF, d), lambda i, be, nv: (e0 + be[i], 0, 0)),
            pl.BlockSpec((1, 1, d), lambda i, be, nv: (e0 + be[i], 0, 0)),
        ],
        out_specs=pl.BlockSpec((ROUTE_BLOCK, dp), lambda i, be, nv: (i, 0)),
        scratch_shapes=[pltpu.VMEM((d, 2 * D_FF), BF16), pltpu.VMEM((D_FF, d), BF16)],
    )
    return pl.pallas_call(
        _expert_kernel,
        grid_spec=grid_spec,
        out_shape=jax.ShapeDtypeStruct((rows, dp), U32),
        compiler_params=_params("arbitrary"),
        name="experts",
    )(block_e, n_valid, xs, w1, b1, w2, b2)


def _combine_kernel(x_ref, yg_ref, topg_ref, gt_ref, lng_ref, lnb_ref, o_ref):
    o_ref[...] = _moe_combine(x_ref[...], [yg_ref[k] for k in range(TOP_K)], topg_ref[...], gt_ref[0],
                              lng_ref[...], lnb_ref[...])


def _combine(x, yg, topg, mod3, ln_g, ln_b, seq):
    t, d = x.shape
    tm = TOKEN_TILE
    per_b = seq // tm
    return pl.pallas_call(
        _combine_kernel,
        grid=(t // tm,),
        in_specs=[
            pl.BlockSpec((tm, d), lambda i: (i, 0)),
            pl.BlockSpec((TOP_K, tm, PACKED_D), lambda i: (0, i, 0)),
            pl.BlockSpec((TOP_K, tm), lambda i: (0, i)),
            pl.BlockSpec((1, 1, d), lambda i: (i // per_b, 0, 5)),
            pl.BlockSpec((1, d), lambda i: (0, 0)),
            pl.BlockSpec((1, d), lambda i: (0, 0)),
        ],
        out_specs=pl.BlockSpec((tm, d), lambda i: (i, 0)),
        out_shape=jax.ShapeDtypeStruct((t, d), F32),
        compiler_params=_params("parallel"),
        name="combine",
    )(x, yg, topg, mod3, ln_g, ln_b)


def _rearranged_w_in(w_in):
    depth, d, _ = w_in.shape
    ab_end = SEG_AB + 2 * GDN_HEADS
    ab_pad = jnp.zeros((depth, d, AB_W - 2 * GDN_HEADS), w_in.dtype)
    return jnp.concatenate([w_in[:, :, :ab_end], ab_pad, w_in[:, :, ab_end:]], axis=2).astype(BF16)


def kernel(x, c, ada_w, ada_b, w_in, conv_w, a_log, dt_bias, gdn_norm_w, sinks, pool_w, pool_scale, w_pa, w_pb, w_pc, w_o, ln1_g, ln1_b, ln2_g, ln2_b, router_w, router_b, exp_w1, exp_b1, exp_w2, exp_b2):
    batch, seq, d = x.shape
    t = batch * seq
    n_blocks = (t * TOP_K + ROUTE_BLOCK - 1) // ROUTE_BLOCK + N_EXPERTS
    mod = _ada_mod(c, ada_w, ada_b)
    xt = x.reshape(t, d)
    lane_pad = jnp.zeros((AB_W - GDN_HEADS,), F32)
    rows = n_blocks * ROUTE_BLOCK
    w_in_all = _rearranged_w_in(w_in)
    w1_all = exp_w1.reshape(DEPTH * N_EXPERTS, d, 2 * D_FF)
    b1_all = exp_b1.reshape(DEPTH * N_EXPERTS, 1, 2 * D_FF)
    w2_all = exp_w2.reshape(DEPTH * N_EXPERTS, D_FF, d)
    b2_all = exp_b2.reshape(DEPTH * N_EXPERTS, 1, d)
    moe = None
    for l in range(DEPTH):
        mod3 = mod[l].reshape(batch, 1, 6 * d)
        if moe is None:
            qkv, z, ab, qb, kvb, uc, gates = _in_proj(xt, mod3, w_in_all, conv_w, seq, l)
        else:
            xt, qkv, z, ab, qb, kvb, uc, gates = _in_proj(xt, mod3, w_in_all, conv_w, seq, l, moe)
        gpar = jnp.stack([jnp.concatenate([-jnp.exp(a_log[l]), lane_pad]), jnp.concatenate([dt_bias[l], lane_pad])])
        xt = _mixer(xt, qkv, ab, z, qb, kvb, uc, gates, mod3, gpar, gdn_norm_w[l].reshape(1, GDN_DV), sinks[l],
                    pool_w[l].astype(BF16), pool_scale[l].reshape(1, POOL_DIM),
                    w_pa[l].astype(BF16), w_pb[l].astype(BF16), w_pc[l].astype(BF16), w_o[l].astype(BF16),
                    ln1_g[l].reshape(1, d), ln1_b[l].reshape(1, d), batch, seq)
        hp, topi_t, topg_t, rank_t, cnt = _router(xt, mod3, router_w[l].T, router_b[l].reshape(N_EXPERTS, 1), seq)
        dest_t, block_e, n_valid = _dispatch_plan(topi_t, rank_t, cnt[:, 0], n_blocks)
        dest_flat = dest_t.reshape(TOP_K * t)
        xs = _sc_dispatch(hp, dest_flat, rows)
        ys = _experts(xs, block_e, n_valid, w1_all, b1_all, w2_all, b2_all, l)
        yg = _sc_gather(ys, dest_flat).reshape(TOP_K, t, PACKED_D)
        moe = (yg, topg_t, mod3, ln2_g[l].reshape(1, d), ln2_b[l].reshape(1, d))
    xt = _combine(xt, *moe, seq)
    return xt.reshape(batch, seq, d)
```

```python
import functools

import jax
import jax.numpy as jnp
from jax import lax
from jax.experimental import pallas as pl
from jax.experimental.pallas import tpu as pltpu
from jax.experimental.pallas import tpu_sc as plsc

D_MODEL = 1024
DEPTH = 4
GDN_HEADS = 4
GDN_DK = 128
GDN_DV = 128
GDN_CONV = 4
GDN_CHUNK = 64
SWA_HQ = 8
SWA_HKV = 2
SWA_DH = 64
WINDOW = 128
POOL_WINDOWS = (2, 4, 8, 16)
POOL_GROUPS = 4
POOL_GDIM = 128
N_BRANCH = 3
N_EXPERTS = 32
TOP_K = 4
D_FF = 1024
ROUTE_BLOCK = 1024
EXPERT_ROWS = 512
SWIGLU_ALPHA = 1.702
SWIGLU_LIMIT = 7.0
LN_EPS = 1e-5
RMS_EPS = 1e-6
DEEPNORM_ALPHA = (2 * DEPTH) ** 0.25

GDN_QK = GDN_HEADS * GDN_DK
GDN_V = GDN_HEADS * GDN_DV
SWA_Q = SWA_HQ * SWA_DH
SWA_KV = SWA_HKV * SWA_DH
POOL_DIM = POOL_GROUPS * POOL_GDIM
GQA_GROUP = SWA_HQ // SWA_HKV
MAX_POOL = max(POOL_WINDOWS)

QKV_W = 2 * GDN_QK + GDN_V
AB_W = 128
SEG_QKV = 0
SEG_Z = SEG_QKV + QKV_W
SEG_AB = SEG_Z + GDN_V
SEG_QB = SEG_AB + AB_W
SEG_KVB = SEG_QB + SWA_Q
SEG_UC = SEG_KVB + 2 * SWA_KV
SEG_GATE = SEG_UC + POOL_DIM
PROJ_W = SEG_GATE + N_BRANCH * D_MODEL

PROJ_CHUNK = 256
TOKEN_TILE = 512
ROUTER_TILE = 1024
NEG_BIG = -1e30
VMEM_LIMIT = 56 * 1024 * 1024

PACKED_D = D_MODEL // 2
SC_CORES = 2
SC_SUBCORES = 16
SC_WORKERS = SC_CORES * SC_SUBCORES
SC_CHUNK = 64

F32 = jnp.float32
BF16 = jnp.bfloat16
U32 = jnp.uint32
HI_MASK = 0xFFFF0000


def _pack_rows(x):
    bits = pltpu.bitcast(x.astype(BF16).astype(F32), U32)
    return (bits[:, :PACKED_D] >> 16) | (bits[:, PACKED_D:] & jnp.uint32(HI_MASK))


def _unpack_rows(w):
    lo = pltpu.bitcast(w << 16, F32)
    hi = pltpu.bitcast(w & jnp.uint32(HI_MASK), F32)
    return jnp.concatenate([lo, hi], axis=1)


def _params(*sem):
    return pltpu.CompilerParams(dimension_semantics=sem, vmem_limit_bytes=VMEM_LIMIT)


def _sigmoid(x):
    return 0.5 * jnp.tanh(0.5 * x) + 0.5


def _layer_norm(x):
    mu = jnp.mean(x, -1, keepdims=True)
    xc = x - mu
    var = jnp.mean(xc * xc, -1, keepdims=True)
    return xc * lax.rsqrt(var + LN_EPS)


def _dot(a, b):
    return jnp.dot(a, b, preferred_element_type=F32)


def _dot_nt(a, b):
    return lax.dot_general(a, b, (((1,), (1,)), ((), ())), preferred_element_type=F32)


def _bdot(a, b):
    return lax.dot_general(a, b, (((2,), (1,)), ((0,), (0,))), preferred_element_type=F32)


def _bdot_tn(a, b):
    return lax.dot_general(a, b, (((1,), (1,)), ((0,), (0,))), preferred_element_type=F32)


def _bdot_nt(a, b):
    return lax.dot_general(a, b, (((2,), (2,)), ((0,), (0,))), preferred_element_type=F32)


def _ada_kernel(c_ref, w_ref, b_ref, o_ref):
    c = c_ref[...]
    cond = c * _sigmoid(c)
    o_ref[0] = _dot(cond.astype(BF16), w_ref[0].astype(BF16)) + b_ref[0]


def _ada_mod(c, ada_w, ada_b):
    depth, d, n = ada_w.shape
    b = c.shape[0]
    tn = 1024
    return pl.pallas_call(
        _ada_kernel,
        grid=(depth, n // tn),
        in_specs=[
            pl.BlockSpec((b, d), lambda l, j: (0, 0)),
            pl.BlockSpec((1, d, tn), lambda l, j: (l, 0, j)),
            pl.BlockSpec((1, 1, tn), lambda l, j: (l, 0, j)),
        ],
        out_specs=pl.BlockSpec((1, b, tn), lambda l, j: (l, 0, j)),
        out_shape=jax.ShapeDtypeStruct((depth, b, n), F32),
        compiler_params=_params("parallel", "parallel"),
        name="ada_mod",
    )(c, ada_w, ada_b.reshape(depth, 1, n))


HEAD_UNITS = 4


def _moe_combine(x, yg_rows, gates_t, gt, ln_g, ln_b):
    g = jnp.concatenate([gates_t, jnp.zeros((128 - TOP_K, gates_t.shape[1]), F32)], axis=0).T
    y = g[:, 0:1] * _unpack_rows(yg_rows[0])
    for k in range(1, TOP_K):
        y = y + g[:, k:k + 1] * _unpack_rows(yg_rows[k])
    return _deepnorm(x, y, gt, ln_g, ln_b)


def _proj_kernel(*refs, tm, per_b, fused):
    if fused:
        (x_ref, yg_ref, topg_ref, gt_ref, lng_ref, lnb_ref, sh_ref, sc_ref, w_ref, convw_ref,
         xo_ref, qkv_ref, z_ref, ab_ref, qb_ref, kvb_ref, uc_ref, gate_ref, h_ref, raw_ref) = refs
    else:
        (x_ref, sh_ref, sc_ref, w_ref, convw_ref,
         qkv_ref, z_ref, ab_ref, qb_ref, kvb_ref, uc_ref, gate_ref, h_ref, raw_ref) = refs
    i = pl.program_id(0)
    slot = i % 2

    @pl.when(i == 0)
    def _():
        h_ref[1] = jnp.zeros(h_ref.shape[1:], BF16)

    rows_per_unit = tm // HEAD_UNITS

    def head_unit(u):
        r = slice(u * rows_per_unit, (u + 1) * rows_per_unit)
        x = x_ref[r, :]
        if fused:
            x = _moe_combine(x, [yg_ref[k, r, :] for k in range(TOP_K)], topg_ref[:, r], gt_ref[0],
                             lng_ref[...], lnb_ref[...])
            xo_ref[r, :] = x
        h_ref[slot, r, :] = (_layer_norm(x) * (1.0 + sc_ref[0]) + sh_ref[0]).astype(BF16)

    h = h_ref[1 - slot]

    def seg(start, width):
        return _dot(h, w_ref[:, start:start + width])

    seq_start = (i == 0) | ((i + per_b - 1) % per_b == 0)

    @pl.when(seq_start)
    def _():
        raw_ref[0:8, :] = jnp.zeros((8, QKV_W), F32)

    @pl.when(jnp.logical_not(seq_start))
    def _():
        raw_ref[0:8, :] = raw_ref[tm:tm + 8, :]

    for j in range(QKV_W // PROJ_CHUNK):
        cols = slice(j * PROJ_CHUNK, (j + 1) * PROJ_CHUNK)
        raw_ref[8:8 + tm, cols] = seg(SEG_QKV + j * PROJ_CHUNK, PROJ_CHUNK)
        if j < HEAD_UNITS:
            head_unit(j)

    def conv_silu(j):
        cols = slice(j * PROJ_CHUNK, (j + 1) * PROJ_CHUNK)
        conv = raw_ref[8:8 + tm, cols] * convw_ref[GDN_CONV - 1:GDN_CONV, cols]
        for tap in range(GDN_CONV - 1):
            conv = conv + raw_ref[5 + tap:5 + tap + tm, cols] * convw_ref[tap:tap + 1, cols]
        qkv_ref[:, cols] = (conv * _sigmoid(conv)).astype(BF16)
    n_conv = QKV_W // PROJ_CHUNK
    done = 0
    for ref, start, width in ((z_ref, SEG_Z, GDN_V), (qb_ref, SEG_QB, SWA_Q), (kvb_ref, SEG_KVB, 2 * SWA_KV),
                              (uc_ref, SEG_UC, POOL_DIM)):
        for j in range(width // PROJ_CHUNK):
            cols = slice(j * PROJ_CHUNK, (j + 1) * PROJ_CHUNK)
            ref[:, cols] = seg(start + j * PROJ_CHUNK, PROJ_CHUNK).astype(BF16)
            if done < n_conv:
                conv_silu(done)
                done += 1
    assert done == n_conv and HEAD_UNITS <= n_conv
    ab_ref[...] = seg(SEG_AB, AB_W)
    gate_ref[...] = _sigmoid(seg(SEG_GATE, N_BRANCH * D_MODEL)).astype(BF16)


def _in_proj(x, mod3, w, conv_w, seq, layer, moe=None):
    t, d = x.shape
    tm = TOKEN_TILE
    per_b = seq // tm
    n_tiles = t // tm
    widths = (QKV_W, GDN_V, AB_W, SWA_Q, 2 * SWA_KV, POOL_DIM, N_BRANCH * D_MODEL)
    dtypes = (BF16, BF16, F32, BF16, BF16, BF16, BF16)

    def head(i):
        return jnp.minimum(i, n_tiles - 1)

    def proj(i):
        return jnp.maximum(i - 1, 0)

    def mod_spec(k):
        return pl.BlockSpec((1, 1, d), lambda i: (head(i) // per_b, 0, k))

    in_specs = [pl.BlockSpec((tm, d), lambda i: (head(i), 0))]
    operands = [x]
    out_specs = [pl.BlockSpec((tm, n), lambda i: (proj(i), 0)) for n in widths]
    out_shape = [jax.ShapeDtypeStruct((t, n), dt) for n, dt in zip(widths, dtypes)]
    if moe is not None:
        yg, topg_t, mod3_prev, ln_g, ln_b = moe
        in_specs += [pl.BlockSpec((TOP_K, tm, PACKED_D), lambda i: (0, head(i), 0)),
                     pl.BlockSpec((TOP_K, tm), lambda i: (0, head(i))),
                     mod_spec(5),
                     pl.BlockSpec((1, d), lambda i: (0, 0)), pl.BlockSpec((1, d), lambda i: (0, 0))]
        operands += [yg, topg_t, mod3_prev, ln_g, ln_b]
        out_specs = [pl.BlockSpec((tm, d), lambda i: (head(i), 0))] + out_specs
        out_shape = [jax.ShapeDtypeStruct((t, d), F32)] + out_shape
    in_specs += [mod_spec(0), mod_spec(1),
                 pl.BlockSpec((None, d, PROJ_W), lambda i: (layer, 0, 0), pipeline_mode=pl.Buffered(1)),
                 pl.BlockSpec((None, GDN_CONV, QKV_W), lambda i: (layer, 0, 0))]
    operands += [mod3, mod3, w, conv_w]
    return pl.pallas_call(
        functools.partial(_proj_kernel, tm=tm, per_b=per_b, fused=moe is not None),
        grid=(n_tiles + 1,),
        in_specs=in_specs,
        out_specs=out_specs,
        out_shape=out_shape,
        scratch_shapes=[pltpu.VMEM((2, tm, d), BF16), pltpu.VMEM((8 + tm, QKV_W), F32)],
        compiler_params=_params("arbitrary"),
        name="in_proj",
    )(*operands)


def _softplus(x):
    return jnp.maximum(x, 0.0) + jnp.log(1.0 + jnp.exp(-jnp.abs(x)))


def _unit_lower_inverse(lm, row, col, between_levels=lambda: None):
    n = lm.shape[-1]
    eye = (row == col).astype(F32)
    t = None
    s = 1
    while s < n:
        join = ((row // (2 * s)) == (col // (2 * s))) & ((row % (2 * s)) >= s) & ((col % (2 * s)) < s)
        cm = jnp.where(join, lm, 0.0)
        if t is None:
            t = eye - cm
        else:
            p = _bdot(t, cm)
            t = t - _bdot(p, t)
            between_levels()
        s *= 2
    return t


def _gdn_kernel(qkv_ref, ab_ref, z_ref, gpar_ref, nw_ref, o_ref, state_ref, *, tb, filler=lambda: None):
    c_len = GDN_CHUNK
    nc = tb // c_len

    @pl.when(pl.program_id(1) == 0)
    def _():
        state_ref[...] = jnp.zeros_like(state_ref)

    ab = ab_ref[...]
    gpar = gpar_ref[...]
    g_all = gpar[0:1] * _softplus(ab + gpar[1:2])
    beta_all = _sigmoid(ab)
    pos = lax.broadcasted_iota(jnp.int32, (tb, AB_W), 0) % c_len
    gc_all = g_all
    s = 1
    while s < c_len:
        gc_all = gc_all + jnp.where(pos >= s, pltpu.roll(gc_all, s, axis=0), 0.0)
        s *= 2

    nh = GDN_HEADS
    nb = nc * nh
    row = lax.broadcasted_iota(jnp.int32, (1, c_len, c_len), 1)
    col = lax.broadcasted_iota(jnp.int32, (1, c_len, c_len), 2)
    causal = row >= col
    strict = row > col
    diag = row == col

    def chunked(per_head):
        return jnp.stack([per_head(h).reshape(nc, c_len, GDN_DK) for h in range(nh)], axis=1).reshape(nb, c_len, GDN_DK)

    q = chunked(lambda h: qkv_ref[:, h * GDN_DK:(h + 1) * GDN_DK].astype(F32))
    k = chunked(lambda h: qkv_ref[:, GDN_QK + h * GDN_DK:GDN_QK + (h + 1) * GDN_DK].astype(F32))
    v = chunked(lambda h: qkv_ref[:, 2 * GDN_QK + h * GDN_DV:2 * GDN_QK + (h + 1) * GDN_DV].astype(F32))
    gc = chunked(lambda h: jnp.broadcast_to(gc_all[:, h:h + 1], (tb, GDN_DK)))
    beta = chunked(lambda h: jnp.broadcast_to(beta_all[:, nh + h:nh + h + 1], (tb, GDN_DK)))
    q = q * lax.rsqrt(jnp.sum(q * q, -1, keepdims=True) + 1e-6) * (GDN_DK ** -0.5)
    k = k * lax.rsqrt(jnp.sum(k * k, -1, keepdims=True) + 1e-6)

    gc_i = gc[:, :, :c_len]
    gc_j = jnp.sum(jnp.where(diag, gc_i, 0.0), axis=1, keepdims=True)
    decay = jnp.where(causal, jnp.exp(jnp.where(causal, gc_i - gc_j, 0.0)), 0.0)
    eg = jnp.exp(gc)
    gc_last = gc[:, c_len - 1:c_len, :]
    g_last = jnp.exp(gc_last)
    k_beta = k * beta
    kq = _bdot_nt(jnp.concatenate([k_beta, q], axis=1).astype(BF16), k.astype(BF16))
    lower = jnp.where(strict, kq[:, :c_len] * decay, 0.0)
    attn = jnp.where(causal, kq[:, c_len:] * decay, 0.0).astype(BF16)
    tinv = _unit_lower_inverse(lower, row, col, filler)
    uw = _bdot(tinv.astype(BF16), jnp.concatenate([v * beta, k_beta * eg], axis=2).astype(BF16)).astype(BF16)
    q_g = q * eg
    k_g = (k * jnp.exp(gc_last - gc)).astype(BF16)

    k_uw = _bdot_tn(k_g, uw)
    a_uw = _bdot(attn, uw)
    ku = k_uw[:, :, :GDN_DV]
    o_loc = a_uw[:, :, :GDN_DV]
    qk_eff = jnp.concatenate([q_g - a_uw[:, :, GDN_DV:], k_uw[:, :, GDN_DV:]], axis=1).astype(BF16)

    st = state_ref[...]
    outs = []
    for c in range(nc):
        sl = slice(c * nh, (c + 1) * nh)
        prod = _bdot(qk_eff[sl], st.astype(BF16))
        outs.append(prod[:, :c_len] + o_loc[sl])
        st = st * g_last[sl] - prod[:, c_len:] + ku[sl]
        filler()
    state_ref[...] = st

    nw = nw_ref[...]
    for h in range(nh):
        o = jnp.concatenate([outs[c][h] for c in range(nc)], axis=0)
        zh = z_ref[:, h * GDN_DV:(h + 1) * GDN_DV].astype(F32)
        o = o * lax.rsqrt(jnp.mean(o * o, -1, keepdims=True) + RMS_EPS) * nw * (zh * _sigmoid(zh))
        o_ref[:, h * GDN_DV:(h + 1) * GDN_DV] = o.astype(BF16)


def _swa_units(sink_ref, q_ref, kv_ref, kvp_ref, bias_ref, o_ref, *, tq):
    nw = tq // WINDOW
    rows = GQA_GROUP * WINDOW
    scale = SWA_DH ** -0.5

    def unit(hk, w):
        heads = range(hk * GQA_GROUP, (hk + 1) * GQA_GROUP)
        tok = slice(w * WINDOW, (w + 1) * WINDOW)

        def keys(col0):
            cols = slice(col0, col0 + SWA_DH)
            prev = kvp_ref[:, cols] if w == 0 else kv_ref[(w - 1) * WINDOW:w * WINDOW, cols]
            return jnp.concatenate([prev, kv_ref[tok, cols]], axis=0)

        k = keys(hk * SWA_DH)
        v = keys(SWA_KV + hk * SWA_DH)
        q = jnp.concatenate([q_ref[tok, hq * SWA_DH:(hq + 1) * SWA_DH] for hq in heads], axis=0)
        sink = jnp.concatenate([jnp.full((WINDOW, 1), sink_ref[hq], F32) for hq in heads], axis=0)
        qi = lax.broadcasted_iota(jnp.int32, (rows, WINDOW), 0) % WINDOW
        kj = lax.broadcasted_iota(jnp.int32, (rows, WINDOW), 1)
        from_prev = kj > qi
        s2 = _dot_nt(q, k)
        s = jnp.where(from_prev, s2[:, :WINDOW], s2[:, WINDOW:]) * scale + bias_ref[hk]
        if w == 0:
            s = jnp.where(from_prev & (pl.program_id(1) == 0), NEG_BIG, s)
        m = jnp.maximum(jnp.max(s, -1, keepdims=True), sink)
        p = jnp.exp(s - m)
        p2 = jnp.concatenate([jnp.where(from_prev, p, 0.0), jnp.where(from_prev, 0.0, p)], axis=1).astype(BF16)
        v_ext = jnp.concatenate([v, jnp.zeros((2 * WINDOW, 128 - SWA_DH), BF16), jnp.ones((2 * WINDOW, 128), BF16)],
                                axis=1)
        pv = _dot(p2, v_ext)
        o = (pv[:, :128] / (pv[:, 128:] + jnp.exp(sink - m)))[:, :SWA_DH]
        for g, hq in enumerate(heads):
            o_ref[tok, hq * SWA_DH:(hq + 1) * SWA_DH] = o[g * WINDOW:(g + 1) * WINDOW, :].astype(BF16)

    return [functools.partial(unit, hk, w) for hk in range(SWA_HKV) for w in range(nw)]


def _swa_bias():
    qi = jnp.arange(WINDOW)[:, None]
    kj = jnp.arange(WINDOW)[None, :]
    dist = jnp.where(kj > qi, qi - kj + WINDOW, qi - kj).astype(F32)
    slopes = 2.0 ** (-8.0 * jnp.arange(1, SWA_HQ + 1, dtype=F32) / SWA_HQ)
    bias = -slopes[:, None, None] * dist[None]
    return bias.reshape(SWA_HKV, GQA_GROUP * WINDOW, WINDOW)


def _deepnorm(x, y, gt, g, b):
    return _layer_norm(DEEPNORM_ALPHA * x + (1.0 + gt) * y) * g + b


def _merge_kernel(x_ref, ya_ref, yb_ref, uc_ref, ucp_ref, gate_ref, gt_ref, poolw_ref, pscale_ref,
                  wpa_ref, wpb_ref, wpc_ref, wo_ref, lng_ref, lnb_ref, o_ref, *, tm):
    j = pl.program_id(1)
    u = uc_ref[...].astype(F32)
    halo = jnp.where(j == 0, 0.0, ucp_ref[...].astype(F32))
    ue = jnp.concatenate([halo, u], axis=0)
    tpos = (j * tm + lax.broadcasted_iota(jnp.int32, (tm, POOL_GDIM), 0) + 1).astype(F32)
    ycs = []
    for gi, win in enumerate(POOL_WINDOWS):
        a = ue[:, gi * POOL_GDIM:(gi + 1) * POOL_GDIM]
        span = 1
        while span < win:
            a = a[span:] + a[:-span]
            span *= 2
        lo = MAX_POOL - win + 1
        d = a[lo:lo + tm] / jnp.minimum(tpos, float(win)) - u[:, gi * POOL_GDIM:(gi + 1) * POOL_GDIM]
        ycs.append(_dot(d.astype(BF16), poolw_ref[gi]))
    yc = jnp.concatenate(ycs, axis=1) * pscale_ref[...]
    merged = gate_ref[:, 0:D_MODEL].astype(F32) * _dot(ya_ref[...], wpa_ref[...])
    merged = merged + gate_ref[:, D_MODEL:2 * D_MODEL].astype(F32) * _dot(yb_ref[...], wpb_ref[...])
    merged = merged + gate_ref[:, 2 * D_MODEL:3 * D_MODEL].astype(F32) * _dot(yc.astype(BF16), wpc_ref[...])
    y = _dot(merged.astype(BF16), wo_ref[...])
    o_ref[...] = _deepnorm(x_ref[...], y, gt_ref[0], lng_ref[...], lnb_ref[...])


def _mixer_kernel(sink_ref, x_ref, qkv_ref, ab_ref, z_ref, gpar_ref, nw_ref, qb_ref, kvb_ref, kvp_ref, bias_ref,
                  uc_ref, ucp_ref, gate_ref, gt_ref, poolw_ref, pscale_ref, wpa_ref, wpb_ref, wpc_ref, wo_ref,
                  lng_ref, lnb_ref, o_ref, state_ref, ya_ref, yb_ref, *, tm):
    pending = iter(_swa_units(sink_ref, qb_ref, kvb_ref, kvp_ref, bias_ref, yb_ref, tq=tm))

    def filler():
        unit = next(pending, None)
        if unit is not None:
            unit()

    _gdn_kernel(qkv_ref, ab_ref, z_ref, gpar_ref, nw_ref, ya_ref, state_ref, tb=tm, filler=filler)
    for unit in pending:
        unit()
    _merge_kernel(x_ref, ya_ref, yb_ref, uc_ref, ucp_ref, gate_ref, gt_ref, poolw_ref, pscale_ref,
                  wpa_ref, wpb_ref, wpc_ref, wo_ref, lng_ref, lnb_ref, o_ref, tm=tm)


def _mixer(x, qkv, ab, z, qb, kvb, uc, gates, mod3, gpar, norm_w, sinks, pool_w, pool_scale, w_pa, w_pb, w_pc, w_o,
           ln_g, ln_b, batch, seq):
    t, d = x.shape
    tm = TOKEN_TILE
    per_b = seq // tm

    def tok(n):
        return pl.BlockSpec((tm, n), lambda b, j: (b * per_b + j, 0))

    def prev_rows(rows, n):
        return pl.BlockSpec((rows, n), lambda b, j: (jnp.maximum((b * per_b + j) * (tm // rows) - 1, 0), 0))

    def full(shape):
        return pl.BlockSpec(shape, lambda b, j: (0,) * len(shape))

    return pl.pallas_call(
        functools.partial(_mixer_kernel, tm=tm),
        grid=(batch, per_b),
        in_specs=[
            pl.BlockSpec(memory_space=pltpu.SMEM),
            tok(d), tok(QKV_W), tok(AB_W), tok(GDN_V), full((2, AB_W)), full((1, GDN_DV)),
            tok(SWA_Q), tok(2 * SWA_KV), prev_rows(WINDOW, 2 * SWA_KV),
            full((SWA_HKV, GQA_GROUP * WINDOW, WINDOW)),
            tok(POOL_DIM), prev_rows(MAX_POOL, POOL_DIM),
            tok(N_BRANCH * d),
            pl.BlockSpec((1, 1, d), lambda b, j: (b, 0, 2)),
            full((POOL_GROUPS, POOL_GDIM, POOL_GDIM)), full((1, POOL_DIM)),
            full((GDN_V, d)), full((SWA_Q, d)), full((POOL_DIM, d)), full((d, d)),
            full((1, d)), full((1, d)),
        ],
        out_specs=tok(d),
        out_shape=jax.ShapeDtypeStruct((t, d), F32),
        scratch_shapes=[pltpu.VMEM((GDN_HEADS, GDN_DK, GDN_DV), F32), pltpu.VMEM((tm, GDN_V), BF16),
                        pltpu.VMEM((tm, SWA_Q), BF16)],
        compiler_params=_params("parallel", "arbitrary"),
        name="mixer",
    )(sinks, x, qkv, ab, z, gpar, norm_w, qb, kvb, kvb, _swa_bias(), uc, uc, gates, mod3, pool_w, pool_scale,
      w_pa, w_pb, w_pc, w_o, ln_g, ln_b)


def _router_kernel(x_ref, sh_ref, sc_ref, rwt_ref, rb_ref, tri_ref, hp_ref, topi_ref, topg_ref, rank_ref, cnt_ref,
                   run_ref):
    @pl.when(pl.program_id(0) == 0)
    def _():
        run_ref[...] = jnp.zeros_like(run_ref)

    h = _layer_norm(x_ref[...]) * (1.0 + sc_ref[0]) + sh_ref[0]
    hp_ref[...] = _pack_rows(h)
    logits = lax.dot_general(rwt_ref[...], h, (((1,), (1,)), ((), ())), preferred_element_type=F32,
                             precision=lax.Precision.HIGHEST) + rb_ref[...]
    sub = lax.broadcasted_iota(jnp.int32, logits.shape, 0)
    vals, idxs = [], []
    for _ in range(TOP_K):
        m = jnp.max(logits, 0, keepdims=True)
        idx = jnp.min(jnp.where(logits == m, sub, N_EXPERTS), 0, keepdims=True)
        vals.append(m)
        idxs.append(idx)
        logits = jnp.where(sub == idx, -jnp.inf, logits)
    es = [jnp.exp(v - vals[0]) for v in vals]
    denom = es[0] + es[1] + es[2] + es[3]
    topi_ref[...] = jnp.concatenate(idxs, axis=0)
    topg_ref[...] = jnp.concatenate([e / denom for e in es], axis=0)

    sel = jnp.zeros(logits.shape, F32)
    for idx in idxs:
        sel = sel + (sub == idx).astype(F32)
    before = run_ref[:, 0:1] + _dot(sel.astype(BF16), tri_ref[...])
    ranks = [jnp.sum(jnp.where(sub == idx, before, 0.0), 0, keepdims=True) for idx in idxs]
    rank_ref[...] = jnp.concatenate(ranks, axis=0).astype(jnp.int32)
    run_ref[...] = run_ref[...] + jnp.sum(sel, 1, keepdims=True)
    cnt_ref[...] = run_ref[...].astype(jnp.int32)


def _router(x, mod3, router_wt, router_b, seq):
    t, d = x.shape
    tm = ROUTER_TILE
    per_b = seq // tm
    tri = (jnp.arange(tm)[:, None] < jnp.arange(tm)[None, :]).astype(BF16)
    return pl.pallas_call(
        _router_kernel,
        grid=(t // tm,),
        in_specs=[
            pl.BlockSpec((tm, d), lambda i: (i, 0)),
            pl.BlockSpec((1, 1, d), lambda i: (i // per_b, 0, 3)),
            pl.BlockSpec((1, 1, d), lambda i: (i // per_b, 0, 4)),
            pl.BlockSpec((N_EXPERTS, d), lambda i: (0, 0)),
            pl.BlockSpec((N_EXPERTS, 1), lambda i: (0, 0)),
            pl.BlockSpec((tm, tm), lambda i: (0, 0)),
        ],
        out_specs=[
            pl.BlockSpec((tm, PACKED_D), lambda i: (i, 0)),
            pl.BlockSpec((TOP_K, tm), lambda i: (0, i)),
            pl.BlockSpec((TOP_K, tm), lambda i: (0, i)),
            pl.BlockSpec((TOP_K, tm), lambda i: (0, i)),
            pl.BlockSpec((N_EXPERTS, 128), lambda i: (0, 0)),
        ],
        out_shape=[
            jax.ShapeDtypeStruct((t, PACKED_D), U32),
            jax.ShapeDtypeStruct((TOP_K, t), jnp.int32),
            jax.ShapeDtypeStruct((TOP_K, t), F32),
            jax.ShapeDtypeStruct((TOP_K, t), jnp.int32),
            jax.ShapeDtypeStruct((N_EXPERTS, 128), jnp.int32),
        ],
        scratch_shapes=[pltpu.VMEM((N_EXPERTS, 128), F32)],
        compiler_params=_params("arbitrary"),
        name="router",
    )(x, mod3, mod3, router_wt, router_b, tri)


def _dispatch_plan(topi_t, rank_t, counts, n_blocks):
    padded = (counts + ROUTE_BLOCK - 1) // ROUTE_BLOCK * ROUTE_BLOCK
    ends_p = jnp.cumsum(padded)
    pstart = ends_p - padded
    experts = jnp.arange(N_EXPERTS, dtype=jnp.int32)
    pstart_tok = jnp.sum(jnp.where(topi_t[:, :, None] == experts, pstart, 0), -1)
    dest_t = (pstart_tok + rank_t).astype(jnp.int32)
    blk_start = jnp.arange(n_blocks, dtype=jnp.int32) * ROUTE_BLOCK
    block_e = jnp.minimum(jnp.sum((blk_start[:, None] >= ends_p).astype(jnp.int32), -1), N_EXPERTS - 1)
    blk_end = jnp.sum(jnp.where(block_e[:, None] == experts, pstart + counts, 0), -1)
    n_valid = jnp.clip(blk_end - blk_start, 0, ROUTE_BLOCK).astype(jnp.int32)
    return dest_t, block_e.astype(jnp.int32), n_valid


def _sc_mesh():
    return plsc.VectorSubcoreMesh(core_axis_name="c", subcore_axis_name="s")


def _sc_dispatch(hp, dest_flat, rows):
    t, dp = hp.shape
    per_w = t // SC_WORKERS
    n_chunks = per_w // SC_CHUNK

    assert n_chunks % 2 == 0

    @functools.partial(
        pl.kernel, mesh=_sc_mesh(), out_type=jax.ShapeDtypeStruct((rows, dp), hp.dtype),
        scratch_types=[pltpu.VMEM((TOP_K * n_chunks, SC_CHUNK), jnp.int32),
                       pltpu.VMEM((SC_CHUNK, dp), hp.dtype), pltpu.VMEM((SC_CHUNK, dp), hp.dtype),
                       pltpu.SemaphoreType.DMA, pltpu.SemaphoreType.DMA,
                       pltpu.SemaphoreType.DMA, pltpu.SemaphoreType.DMA])
    def dispatch(h_hbm, d_hbm, xs_hbm, idx_v, rows_a, rows_b, load_a, load_b, scat_a, scat_b):
        worker = lax.axis_index("s") * SC_CORES + lax.axis_index("c")
        base = worker * per_w
        for k in range(TOP_K):
            pltpu.sync_copy(d_hbm.at[pl.ds((k * SC_WORKERS + worker) * n_chunks, n_chunks)],
                            idx_v.at[pl.ds(k * n_chunks, n_chunks)])

        def load(i, buf, sem):
            return pltpu.make_async_copy(h_hbm.at[pl.ds(base + i * SC_CHUNK, SC_CHUNK)], buf, sem)

        def scatter(i, k, buf, sem):
            return pltpu.make_async_copy(buf, xs_hbm.at[idx_v.at[k * n_chunks + i]], sem)

        def scatter_all(i, buf, sem):
            for k in range(TOP_K):
                scatter(i, k, buf, sem).start()
            for k in range(TOP_K):
                scatter(i, k, buf, sem).wait()

        load(0, rows_a, load_a).start()

        @pl.loop(0, n_chunks, step=2)
        def _(i):
            load(i, rows_a, load_a).wait()
            load(i + 1, rows_b, load_b).start()
            scatter_all(i, rows_a, scat_a)
            load(i + 1, rows_b, load_b).wait()

            @pl.when(i + 2 < n_chunks)
            def _():
                load(i + 2, rows_a, load_a).start()

            scatter_all(i + 1, rows_b, scat_b)

    return dispatch(hp, dest_flat.reshape(TOP_K * SC_WORKERS * n_chunks, SC_CHUNK))


def _sc_gather(table, idx):
    n = idx.shape[0]
    dp = table.shape[1]
    per_w = n // SC_WORKERS
    n_chunks = per_w // SC_CHUNK

    assert n_chunks % 2 == 0

    @functools.partial(
        pl.kernel, mesh=_sc_mesh(), out_type=jax.ShapeDtypeStruct((n, dp), table.dtype),
        scratch_types=[pltpu.VMEM((n_chunks, SC_CHUNK), jnp.int32),
                       pltpu.VMEM((SC_CHUNK, dp), table.dtype), pltpu.VMEM((SC_CHUNK, dp), table.dtype),
                       pltpu.SemaphoreType.DMA, pltpu.SemaphoreType.DMA,
                       pltpu.SemaphoreType.DMA, pltpu.SemaphoreType.DMA])
    def gather(t_hbm, i_hbm, o_hbm, idx_v, rows_a, rows_b, fetch_a, fetch_b, store_a, store_b):
        worker = lax.axis_index("s") * SC_CORES + lax.axis_index("c")
        base = worker * per_w
        pltpu.sync_copy(i_hbm.at[pl.ds(worker * n_chunks, n_chunks)], idx_v)

        def fetch(i, buf, sem):
            return pltpu.make_async_copy(t_hbm.at[idx_v.at[i]], buf, sem)

        def store(i, buf, sem):
            return pltpu.make_async_copy(buf, o_hbm.at[pl.ds(base + i * SC_CHUNK, SC_CHUNK)], sem)

        fetch(0, rows_a, fetch_a).start()

        @pl.loop(0, n_chunks, step=2)
        def _(i):
            fetch(i, rows_a, fetch_a).wait()

            @pl.when(i > 0)
            def _():
                store(i - 1, rows_b, store_b).wait()

            fetch(i + 1, rows_b, fetch_b).start()
            store(i, rows_a, store_a).start()
            fetch(i + 1, rows_b, fetch_b).wait()
            store(i, rows_a, store_a).wait()

            @pl.when(i + 2 < n_chunks)
            def _():
                fetch(i + 2, rows_a, fetch_a).start()

            store(i + 1, rows_b, store_b).start()

        store(n_chunks - 1, rows_b, store_b).wait()

    return gather(table, idx.reshape(SC_WORKERS * n_chunks, SC_CHUNK))


def _expert_kernel(be_ref, nv_ref, x_ref, w1_ref, b1_ref, w2_ref, b2_ref, o_ref, w1b_ref, w2b_ref):
    i = pl.program_id(0)
    n_valid = nv_ref[i]
    new_expert = (i == 0) | (be_ref[i] != be_ref[jnp.maximum(i - 1, 0)])

    @pl.when(new_expert & (n_valid > 0))
    def _():
        w1b_ref[...] = w1_ref[0].astype(BF16)
        w2b_ref[...] = w2_ref[0].astype(BF16)

    for piece in range(ROUTE_BLOCK // EXPERT_ROWS):
        r = slice(piece * EXPERT_ROWS, (piece + 1) * EXPERT_ROWS)
        left = n_valid - piece * EXPERT_ROWS

        @pl.when(left > 0)
        def _():
            live = lax.broadcasted_iota(jnp.int32, (EXPERT_ROWS, D_MODEL), 0) < left
            x = jnp.where(live, _unpack_rows(x_ref[r, :]), 0.0).astype(BF16)
            gu = _dot(x, w1b_ref[...]) + b1_ref[0]
            glu = jnp.minimum(gu[:, :D_FF], SWIGLU_LIMIT)
            lin = jnp.clip(gu[:, D_FF:], -SWIGLU_LIMIT, SWIGLU_LIMIT)
            act = glu * _sigmoid(SWIGLU_ALPHA * glu) * (lin + 1.0)
            o_ref[r, :] = _pack_rows(_dot(act.astype(BF16), w2b_ref[...]) + b2_ref[0])

        @pl.when(left <= 0)
        def _():
            o_ref[r, :] = jnp.zeros((EXPERT_ROWS, PACKED_D), U32)


def _experts(xs, block_e, n_valid, w1, b1, w2, b2, layer):
    rows, dp = xs.shape
    d = D_MODEL
    n_blocks = rows // ROUTE_BLOCK
    e0 = layer * N_EXPERTS
    grid_spec = pltpu.PrefetchScalarGridSpec(
        num_scalar_prefetch=2,
        grid=(n_blocks,),
        in_specs=[
            pl.BlockSpec((ROUTE_BLOCK, dp), lambda i, be, nv: (i, 0)),
            pl.BlockSpec((1, d, 2 * D_FF), lambda i, be, nv: (e0 + be[i], 0, 0)),
            pl.BlockSpec((1, 1, 2 * D_FF), lambda i, be, nv: (e0 + be[i], 0, 0)),
            pl.BlockSpec((1, D_FF, d), lambda i, be, nv: (e0 + be[i], 0, 0)),
            pl.BlockSpec((1, 1, d), lambda i, be, nv: (e0 + be[i], 0, 0)),
        ],
        out_specs=pl.BlockSpec((ROUTE_BLOCK, dp), lambda i, be, nv: (i, 0)),
        scratch_shapes=[pltpu.VMEM((d, 2 * D_FF), BF16), pltpu.VMEM((D_FF, d), BF16)],
    )
    return pl.pallas_call(
        _expert_kernel,
        grid_spec=grid_spec,
        out_shape=jax.ShapeDtypeStruct((rows, dp), U32),
        compiler_params=_params("arbitrary"),
        name="experts",
    )(block_e, n_valid, xs, w1, b1, w2, b2)


def _combine_kernel(x_ref, yg_ref, topg_ref, gt_ref, lng_ref, lnb_ref, o_ref):
    o_ref[...] = _moe_combine(x_ref[...], [yg_ref[k] for k in range(TOP_K)], topg_ref[...], gt_ref[0],
                              lng_ref[...], lnb_ref[...])


def _combine(x, yg, topg, mod3, ln_g, ln_b, seq):
    t, d = x.shape
    tm = TOKEN_TILE
    per_b = seq // tm
    return pl.pallas_call(
        _combine_kernel,
        grid=(t // tm,),
        in_specs=[
            pl.BlockSpec((tm, d), lambda i: (i, 0)),
            pl.BlockSpec((TOP_K, tm, PACKED_D), lambda i: (0, i, 0)),
            pl.BlockSpec((TOP_K, tm), lambda i: (0, i)),
            pl.BlockSpec((1, 1, d), lambda i: (i // per_b, 0, 5)),
            pl.BlockSpec((1, d), lambda i: (0, 0)),
            pl.BlockSpec((1, d), lambda i: (0, 0)),
        ],
        out_specs=pl.BlockSpec((tm, d), lambda i: (i, 0)),
        out_shape=jax.ShapeDtypeStruct((t, d), F32),
        compiler_params=_params("parallel"),
        name="combine",
    )(x, yg, topg, mod3, ln_g, ln_b)


def _rearranged_w_in(w_in):
    depth, d, _ = w_in.shape
    ab_end = SEG_AB + 2 * GDN_HEADS
    ab_pad = jnp.zeros((depth, d, AB_W - 2 * GDN_HEADS), w_in.dtype)
    return jnp.concatenate([w_in[:, :, :ab_end], ab_pad, w_in[:, :, ab_end:]], axis=2).astype(BF16)


def kernel(x, c, ada_w, ada_b, w_in, conv_w, a_log, dt_bias, gdn_norm_w, sinks, pool_w, pool_scale, w_pa, w_pb, w_pc, w_o, ln1_g, ln1_b, ln2_g, ln2_b, router_w, router_b, exp_w1, exp_b1, exp_w2, exp_b2):
    batch, seq, d = x.shape
    t = batch * seq
    n_blocks = (t * TOP_K + ROUTE_BLOCK - 1) // ROUTE_BLOCK + N_EXPERTS
    mod = _ada_mod(c, ada_w, ada_b)
    xt = x.reshape(t, d)
    lane_pad = jnp.zeros((AB_W - GDN_HEADS,), F32)
    rows = n_blocks * ROUTE_BLOCK
    w_in_all = _rearranged_w_in(w_in)
    w1_all = exp_w1.reshape(DEPTH * N_EXPERTS, d, 2 * D_FF)
    b1_all = exp_b1.reshape(DEPTH * N_EXPERTS, 1, 2 * D_FF)
    w2_all = exp_w2.reshape(DEPTH * N_EXPERTS, D_FF, d)
    b2_all = exp_b2.reshape(DEPTH * N_EXPERTS, 1, d)
    moe = None
    for l in range(DEPTH):
        mod3 = mod[l].reshape(batch, 1, 6 * d)
        if moe is None:
            qkv, z, ab, qb, kvb, uc, gates = _in_proj(xt, mod3, w_in_all, conv_w, seq, l)
        else:
            xt, qkv, z, ab, qb, kvb, uc, gates = _in_proj(xt, mod3, w_in_all, conv_w, seq, l, moe)
        gpar = jnp.stack([jnp.concatenate([-jnp.exp(a_log[l]), lane_pad]), jnp.concatenate([dt_bias[l], lane_pad])])
        xt = _mixer(xt, qkv, ab, z, qb, kvb, uc, gates, mod3, gpar, gdn_norm_w[l].reshape(1, GDN_DV), sinks[l],
                    pool_w[l].astype(BF16), pool_scale[l].reshape(1, POOL_DIM),
                    w_pa[l].astype(BF16), w_pb[l].astype(BF16), w_pc[l].astype(BF16), w_o[l].astype(BF16),
                    ln1_g[l].reshape(1, d), ln1_b[l].reshape(1, d), batch, seq)
        hp, topi_t, topg_t, rank_t, cnt = _router(xt, mod3, router_w[l].T, router_b[l].reshape(N_EXPERTS, 1), seq)
        dest_t, block_e, n_valid = _dispatch_plan(topi_t, rank_t, cnt[:, 0], n_blocks)
        dest_flat = dest_t.reshape(TOP_K * t)
        xs = _sc_dispatch(hp, dest_flat, rows)
        ys = _experts(xs, block_e, n_valid, w1_all, b1_all, w2_all, b2_all, l)
        yg = _sc_gather(ys, dest_flat).reshape(TOP_K, t, PACKED_D)
        moe = (yg, topg_t, mod3, ln2_g[l].reshape(1, d), ln2_b[l].reshape(1, d))
    xt = _combine(xt, *moe, seq)
    return xt.reshape(batch, seq, d)
```

```python
import functools

import jax
import jax.numpy as jnp
from jax import lax
from jax.experimental import pallas as pl
from jax.experimental.pallas import tpu as pltpu
from jax.experimental.pallas import tpu_sc as plsc

D_MODEL = 1024
DEPTH = 4
GDN_HEADS = 4
GDN_DK = 128
GDN_DV = 128
GDN_CONV = 4
GDN_CHUNK = 64
SWA_HQ = 8
SWA_HKV = 2
SWA_DH = 64
WINDOW = 128
POOL_WINDOWS = (2, 4, 8, 16)
POOL_GROUPS = 4
POOL_GDIM = 128
N_BRANCH = 3
N_EXPERTS = 32
TOP_K = 4
D_FF = 1024
ROUTE_BLOCK = 1024
EXPERT_ROWS = 512
SWIGLU_ALPHA = 1.702
SWIGLU_LIMIT = 7.0
LN_EPS = 1e-5
RMS_EPS = 1e-6
DEEPNORM_ALPHA = (2 * DEPTH) ** 0.25

GDN_QK = GDN_HEADS * GDN_DK
GDN_V = GDN_HEADS * GDN_DV
SWA_Q = SWA_HQ * SWA_DH
SWA_KV = SWA_HKV * SWA_DH
POOL_DIM = POOL_GROUPS * POOL_GDIM
GQA_GROUP = SWA_HQ // SWA_HKV
MAX_POOL = max(POOL_WINDOWS)

QKV_W = 2 * GDN_QK + GDN_V
AB_W = 128
SEG_QKV = 0
SEG_Z = SEG_QKV + QKV_W
SEG_AB = SEG_Z + GDN_V
SEG_QB = SEG_AB + AB_W
SEG_KVB = SEG_QB + SWA_Q
SEG_UC = SEG_KVB + 2 * SWA_KV
SEG_GATE = SEG_UC + POOL_DIM
PROJ_W = SEG_GATE + N_BRANCH * D_MODEL

PROJ_CHUNK = 256
TOKEN_TILE = 512
ROUTER_TILE = 1024
NEG_BIG = -1e30
VMEM_LIMIT = 56 * 1024 * 1024

PACKED_D = D_MODEL // 2
SC_CORES = 2
SC_SUBCORES = 16
SC_WORKERS = SC_CORES * SC_SUBCORES
SC_CHUNK = 64

F32 = jnp.float32
BF16 = jnp.bfloat16
U32 = jnp.uint32
HI_MASK = 0xFFFF0000


def _pack_rows(x):
    bits = pltpu.bitcast(x.astype(BF16).astype(F32), U32)
    return (bits[:, :PACKED_D] >> 16) | (bits[:, PACKED_D:] & jnp.uint32(HI_MASK))


def _unpack_rows(w):
    lo = pltpu.bitcast(w << 16, F32)
    hi = pltpu.bitcast(w & jnp.uint32(HI_MASK), F32)
    return jnp.concatenate([lo, hi], axis=1)


def _params(*sem):
    return pltpu.CompilerParams(dimension_semantics=sem, vmem_limit_bytes=VMEM_LIMIT)


def _sigmoid(x):
    return 0.5 * jnp.tanh(0.5 * x) + 0.5


def _layer_norm(x):
    mu = jnp.mean(x, -1, keepdims=True)
    xc = x - mu
    var = jnp.mean(xc * xc, -1, keepdims=True)
    return xc * lax.rsqrt(var + LN_EPS)


def _dot(a, b):
    return jnp.dot(a, b, preferred_element_type=F32)


def _dot_nt(a, b):
    return lax.dot_general(a, b, (((1,), (1,)), ((), ())), preferred_element_type=F32)


def _bdot(a, b):
    return lax.dot_general(a, b, (((2,), (1,)), ((0,), (0,))), preferred_element_type=F32)


def _bdot_tn(a, b):
    return lax.dot_general(a, b, (((1,), (1,)), ((0,), (0,))), preferred_element_type=F32)


def _bdot_nt(a, b):
    return lax.dot_general(a, b, (((2,), (2,)), ((0,), (0,))), preferred_element_type=F32)


def _ada_kernel(c_ref, w_ref, b_ref, o_ref):
    c = c_ref[...]
    cond = c * _sigmoid(c)
    o_ref[0] = _dot(cond.astype(BF16), w_ref[0].astype(BF16)) + b_ref[0]


def _ada_mod(c, ada_w, ada_b):
    depth, d, n = ada_w.shape
    b = c.shape[0]
    tn = 1024
    return pl.pallas_call(
        _ada_kernel,
        grid=(depth, n // tn),
        in_specs=[
            pl.BlockSpec((b, d), lambda l, j: (0, 0)),
            pl.BlockSpec((1, d, tn), lambda l, j: (l, 0, j)),
            pl.BlockSpec((1, 1, tn), lambda l, j: (l, 0, j)),
        ],
        out_specs=pl.BlockSpec((1, b, tn), lambda l, j: (l, 0, j)),
        out_shape=jax.ShapeDtypeStruct((depth, b, n), F32),
        compiler_params=_params("parallel", "parallel"),
        name="ada_mod",
    )(c, ada_w, ada_b.reshape(depth, 1, n))


HEAD_UNITS = 4


def _moe_combine(x, yg_rows, gates_t, gt, ln_g, ln_b):
    g = jnp.concatenate([gates_t, jnp.zeros((128 - TOP_K, gates_t.shape[1]), F32)], axis=0).T
    y = g[:, 0:1] * _unpack_rows(yg_rows[0])
    for k in range(1, TOP_K):
        y = y + g[:, k:k + 1] * _unpack_rows(yg_rows[k])
    return _deepnorm(x, y, gt, ln_g, ln_b)


def _proj_kernel(*refs, tm, per_b, fused):
    if fused:
        (x_ref, yg_ref, topg_ref, gt_ref, lng_ref, lnb_ref, sh_ref, sc_ref, w_ref, convw_ref, gpar_ref,
         xo_ref, qkv_ref, z_ref, ab_ref, qb_ref, kvb_ref, uc_ref, gate_ref, h_ref, raw_ref) = refs
    else:
        (x_ref, sh_ref, sc_ref, w_ref, convw_ref, gpar_ref,
         qkv_ref, z_ref, ab_ref, qb_ref, kvb_ref, uc_ref, gate_ref, h_ref, raw_ref) = refs
    i = pl.program_id(0)
    slot = i % 2

    @pl.when(i == 0)
    def _():
        h_ref[1] = jnp.zeros(h_ref.shape[1:], BF16)

    rows_per_unit = tm // HEAD_UNITS

    def head_unit(u):
        r = slice(u * rows_per_unit, (u + 1) * rows_per_unit)
        x = x_ref[r, :]
        if fused:
            x = _moe_combine(x, [yg_ref[k, r, :] for k in range(TOP_K)], topg_ref[:, r], gt_ref[0],
                             lng_ref[...], lnb_ref[...])
            xo_ref[r, :] = x
        h_ref[slot, r, :] = (_layer_norm(x) * (1.0 + sc_ref[0]) + sh_ref[0]).astype(BF16)

    h = h_ref[1 - slot]

    def seg(start, width):
        return _dot(h, w_ref[:, start:start + width])

    seq_start = (i == 0) | ((i + per_b - 1) % per_b == 0)

    @pl.when(seq_start)
    def _():
        raw_ref[0:8, :] = jnp.zeros((8, QKV_W), F32)

    @pl.when(jnp.logical_not(seq_start))
    def _():
        raw_ref[0:8, :] = raw_ref[tm:tm + 8, :]

    for j in range(QKV_W // PROJ_CHUNK):
        cols = slice(j * PROJ_CHUNK, (j + 1) * PROJ_CHUNK)
        raw_ref[8:8 + tm, cols] = seg(SEG_QKV + j * PROJ_CHUNK, PROJ_CHUNK)
        if j < HEAD_UNITS:
            head_unit(j)

    def conv_silu(j):
        cols = slice(j * PROJ_CHUNK, (j + 1) * PROJ_CHUNK)
        conv = raw_ref[8:8 + tm, cols] * convw_ref[GDN_CONV - 1:GDN_CONV, cols]
        for tap in range(GDN_CONV - 1):
            conv = conv + raw_ref[5 + tap:5 + tap + tm, cols] * convw_ref[tap:tap + 1, cols]
        qkv_ref[:, cols] = (conv * _sigmoid(conv)).astype(BF16)
    n_conv = QKV_W // PROJ_CHUNK
    done = 0
    for ref, start, width in ((z_ref, SEG_Z, GDN_V), (qb_ref, SEG_QB, SWA_Q), (kvb_ref, SEG_KVB, 2 * SWA_KV),
                              (uc_ref, SEG_UC, POOL_DIM)):
        for j in range(width // PROJ_CHUNK):
            cols = slice(j * PROJ_CHUNK, (j + 1) * PROJ_CHUNK)
            ref[:, cols] = seg(start + j * PROJ_CHUNK, PROJ_CHUNK).astype(BF16)
            if done < n_conv:
                conv_silu(done)
                done += 1
    assert done == n_conv and HEAD_UNITS <= n_conv
    ab = seg(SEG_AB, AB_W)
    gpar = gpar_ref[...]
    gc = gpar[0:1] * _softplus(ab + gpar[1:2])
    pos = lax.broadcasted_iota(jnp.int32, (tm, AB_W), 0) % GDN_CHUNK
    span = 1
    while span < GDN_CHUNK:
        gc = gc + jnp.where(pos >= span, pltpu.roll(gc, span, axis=0), 0.0)
        span *= 2
    lane = lax.broadcasted_iota(jnp.int32, (tm, AB_W), 1)
    ab_ref[...] = jnp.where(lane < GDN_HEADS, gc, _sigmoid(ab))
    gate_ref[...] = _sigmoid(seg(SEG_GATE, N_BRANCH * D_MODEL)).astype(BF16)


def _in_proj(x, mod3, w, conv_w, gpar, seq, layer, moe=None):
    t, d = x.shape
    tm = TOKEN_TILE
    per_b = seq // tm
    n_tiles = t // tm
    widths = (QKV_W, GDN_V, AB_W, SWA_Q, 2 * SWA_KV, POOL_DIM, N_BRANCH * D_MODEL)
    dtypes = (BF16, BF16, F32, BF16, BF16, BF16, BF16)

    def head(i):
        return jnp.minimum(i, n_tiles - 1)

    def proj(i):
        return jnp.maximum(i - 1, 0)

    def mod_spec(k):
        return pl.BlockSpec((1, 1, d), lambda i: (head(i) // per_b, 0, k))

    in_specs = [pl.BlockSpec((tm, d), lambda i: (head(i), 0))]
    operands = [x]
    out_specs = [pl.BlockSpec((tm, n), lambda i: (proj(i), 0)) for n in widths]
    out_shape = [jax.ShapeDtypeStruct((t, n), dt) for n, dt in zip(widths, dtypes)]
    if moe is not None:
        yg, topg_t, mod3_prev, ln_g, ln_b = moe
        in_specs += [pl.BlockSpec((TOP_K, tm, PACKED_D), lambda i: (0, head(i), 0)),
                     pl.BlockSpec((TOP_K, tm), lambda i: (0, head(i))),
                     mod_spec(5),
                     pl.BlockSpec((1, d), lambda i: (0, 0)), pl.BlockSpec((1, d), lambda i: (0, 0))]
        operands += [yg, topg_t, mod3_prev, ln_g, ln_b]
        out_specs = [pl.BlockSpec((tm, d), lambda i: (head(i), 0))] + out_specs
        out_shape = [jax.ShapeDtypeStruct((t, d), F32)] + out_shape
    in_specs += [mod_spec(0), mod_spec(1),
                 pl.BlockSpec((None, d, PROJ_W), lambda i: (layer, 0, 0), pipeline_mode=pl.Buffered(1)),
                 pl.BlockSpec((None, GDN_CONV, QKV_W), lambda i: (layer, 0, 0)),
                 pl.BlockSpec((2, AB_W), lambda i: (0, 0))]
    operands += [mod3, mod3, w, conv_w, gpar]
    return pl.pallas_call(
        functools.partial(_proj_kernel, tm=tm, per_b=per_b, fused=moe is not None),
        grid=(n_tiles + 1,),
        in_specs=in_specs,
        out_specs=out_specs,
        out_shape=out_shape,
        scratch_shapes=[pltpu.VMEM((2, tm, d), BF16), pltpu.VMEM((8 + tm, QKV_W), F32)],
        compiler_params=_params("arbitrary"),
        name="in_proj",
    )(*operands)


def _softplus(x):
    return jnp.maximum(x, 0.0) + jnp.log(1.0 + jnp.exp(-jnp.abs(x)))


def _unit_lower_inverse(lm, row, col, between_levels=lambda: None):
    n = lm.shape[-1]
    eye = (row == col).astype(F32)
    t = None
    s = 1
    while s < n:
        join = ((row // (2 * s)) == (col // (2 * s))) & ((row % (2 * s)) >= s) & ((col % (2 * s)) < s)
        cm = jnp.where(join, lm, 0.0)
        if t is None:
            t = eye - cm
        else:
            p = _bdot(t, cm)
            t = t - _bdot(p, t)
            between_levels()
        s *= 2
    return t


def _gdn_kernel(qkv_ref, ab_ref, z_ref, nw_ref, o_ref, state_ref, *, tb, filler=lambda: None):
    c_len = GDN_CHUNK
    nc = tb // c_len

    @pl.when(pl.program_id(1) == 0)
    def _():
        state_ref[...] = jnp.zeros_like(state_ref)

    gc_all = ab_ref[...]
    beta_all = gc_all

    nh = GDN_HEADS
    nb = nc * nh
    row = lax.broadcasted_iota(jnp.int32, (1, c_len, c_len), 1)
    col = lax.broadcasted_iota(jnp.int32, (1, c_len, c_len), 2)
    causal = row >= col
    strict = row > col
    diag = row == col

    def chunked(per_head):
        return jnp.stack([per_head(h).reshape(nc, c_len, GDN_DK) for h in range(nh)], axis=1).reshape(nb, c_len, GDN_DK)

    q = chunked(lambda h: qkv_ref[:, h * GDN_DK:(h + 1) * GDN_DK].astype(F32))
    k = chunked(lambda h: qkv_ref[:, GDN_QK + h * GDN_DK:GDN_QK + (h + 1) * GDN_DK].astype(F32))
    v = chunked(lambda h: qkv_ref[:, 2 * GDN_QK + h * GDN_DV:2 * GDN_QK + (h + 1) * GDN_DV].astype(F32))
    gc = chunked(lambda h: jnp.broadcast_to(gc_all[:, h:h + 1], (tb, GDN_DK)))
    beta = chunked(lambda h: jnp.broadcast_to(beta_all[:, nh + h:nh + h + 1], (tb, GDN_DK)))
    q = q * lax.rsqrt(jnp.sum(q * q, -1, keepdims=True) + 1e-6) * (GDN_DK ** -0.5)
    k = k * lax.rsqrt(jnp.sum(k * k, -1, keepdims=True) + 1e-6)

    gc_i = gc[:, :, :c_len]
    gc_j = jnp.sum(jnp.where(diag, gc_i, 0.0), axis=1, keepdims=True)
    decay = jnp.where(causal, jnp.exp(jnp.where(causal, gc_i - gc_j, 0.0)), 0.0)
    eg = jnp.exp(gc)
    gc_last = gc[:, c_len - 1:c_len, :]
    g_last = jnp.exp(gc_last)
    k_beta = k * beta
    kq = _bdot_nt(jnp.concatenate([k_beta, q], axis=1).astype(BF16), k.astype(BF16))
    lower = jnp.where(strict, kq[:, :c_len] * decay, 0.0)
    attn = jnp.where(causal, kq[:, c_len:] * decay, 0.0).astype(BF16)
    tinv = _unit_lower_inverse(lower, row, col, filler)
    uw = _bdot(tinv.astype(BF16), jnp.concatenate([v * beta, k_beta * eg], axis=2).astype(BF16)).astype(BF16)
    q_g = q * eg
    k_g = (k * jnp.exp(gc_last - gc)).astype(BF16)

    k_uw = _bdot_tn(k_g, uw)
    a_uw = _bdot(attn, uw)
    ku = k_uw[:, :, :GDN_DV]
    o_loc = a_uw[:, :, :GDN_DV]
    qk_eff = jnp.concatenate([q_g - a_uw[:, :, GDN_DV:], k_uw[:, :, GDN_DV:]], axis=1).astype(BF16)

    st = state_ref[...]
    outs = []
    for c in range(nc):
        sl = slice(c * nh, (c + 1) * nh)
        prod = _bdot(qk_eff[sl], st.astype(BF16))
        outs.append(prod[:, :c_len] + o_loc[sl])
        st = st * g_last[sl] - prod[:, c_len:] + ku[sl]
        filler()
    state_ref[...] = st

    nw = nw_ref[...]
    for h in range(nh):
        o = jnp.concatenate([outs[c][h] for c in range(nc)], axis=0)
        zh = z_ref[:, h * GDN_DV:(h + 1) * GDN_DV].astype(F32)
        o = o * lax.rsqrt(jnp.mean(o * o, -1, keepdims=True) + RMS_EPS) * nw * (zh * _sigmoid(zh))
        o_ref[:, h * GDN_DV:(h + 1) * GDN_DV] = o.astype(BF16)


def _swa_units(sink_ref, q_ref, kv_ref, kvp_ref, bias_ref, o_ref, *, tq):
    nw = tq // WINDOW
    rows = GQA_GROUP * WINDOW
    scale = SWA_DH ** -0.5

    def unit(hk, w):
        heads = range(hk * GQA_GROUP, (hk + 1) * GQA_GROUP)
        tok = slice(w * WINDOW, (w + 1) * WINDOW)

        def keys(col0):
            cols = slice(col0, col0 + SWA_DH)
            prev = kvp_ref[:, cols] if w == 0 else kv_ref[(w - 1) * WINDOW:w * WINDOW, cols]
            return jnp.concatenate([prev, kv_ref[tok, cols]], axis=0)

        k = keys(hk * SWA_DH)
        v = keys(SWA_KV + hk * SWA_DH)
        q = jnp.concatenate([q_ref[tok, hq * SWA_DH:(hq + 1) * SWA_DH] for hq in heads], axis=0)
        sink = jnp.concatenate([jnp.full((WINDOW, 1), sink_ref[hq], F32) for hq in heads], axis=0)
        qi = lax.broadcasted_iota(jnp.int32, (rows, WINDOW), 0) % WINDOW
        kj = lax.broadcasted_iota(jnp.int32, (rows, WINDOW), 1)
        from_prev = kj > qi
        s2 = _dot_nt(q, k)
        s = jnp.where(from_prev, s2[:, :WINDOW], s2[:, WINDOW:]) * scale + bias_ref[hk]
        if w == 0:
            s = jnp.where(from_prev & (pl.program_id(1) == 0), NEG_BIG, s)
        m = jnp.maximum(jnp.max(s, -1, keepdims=True), sink)
        p = jnp.exp(s - m)
        p2 = jnp.concatenate([jnp.where(from_prev, p, 0.0), jnp.where(from_prev, 0.0, p)], axis=1).astype(BF16)
        v_ext = jnp.concatenate([v, jnp.zeros((2 * WINDOW, 128 - SWA_DH), BF16), jnp.ones((2 * WINDOW, 128), BF16)],
                                axis=1)
        pv = _dot(p2, v_ext)
        o = (pv[:, :128] / (pv[:, 128:] + jnp.exp(sink - m)))[:, :SWA_DH]
        for g, hq in enumerate(heads):
            o_ref[tok, hq * SWA_DH:(hq + 1) * SWA_DH] = o[g * WINDOW:(g + 1) * WINDOW, :].astype(BF16)

    return [functools.partial(unit, hk, w) for hk in range(SWA_HKV) for w in range(nw)]


def _swa_bias():
    qi = jnp.arange(WINDOW)[:, None]
    kj = jnp.arange(WINDOW)[None, :]
    dist = jnp.where(kj > qi, qi - kj + WINDOW, qi - kj).astype(F32)
    slopes = 2.0 ** (-8.0 * jnp.arange(1, SWA_HQ + 1, dtype=F32) / SWA_HQ)
    bias = -slopes[:, None, None] * dist[None]
    return bias.reshape(SWA_HKV, GQA_GROUP * WINDOW, WINDOW)


def _deepnorm(x, y, gt, g, b):
    return _layer_norm(DEEPNORM_ALPHA * x + (1.0 + gt) * y) * g + b


def _merge_kernel(x_ref, ya_ref, yb_ref, uc_ref, ucp_ref, gate_ref, gt_ref, poolw_ref, pscale_ref,
                  wpa_ref, wpb_ref, wpc_ref, wo_ref, lng_ref, lnb_ref, o_ref, *, tm):
    j = pl.program_id(1)
    u = uc_ref[...].astype(F32)
    halo = jnp.where(j == 0, 0.0, ucp_ref[...].astype(F32))
    ue = jnp.concatenate([halo, u], axis=0)
    tpos = (j * tm + lax.broadcasted_iota(jnp.int32, (tm, POOL_GDIM), 0) + 1).astype(F32)
    ycs = []
    for gi, win in enumerate(POOL_WINDOWS):
        a = ue[:, gi * POOL_GDIM:(gi + 1) * POOL_GDIM]
        span = 1
        while span < win:
            a = a[span:] + a[:-span]
            span *= 2
        lo = MAX_POOL - win + 1
        d = a[lo:lo + tm] / jnp.minimum(tpos, float(win)) - u[:, gi * POOL_GDIM:(gi + 1) * POOL_GDIM]
        ycs.append(_dot(d.astype(BF16), poolw_ref[gi]))
    yc = jnp.concatenate(ycs, axis=1) * pscale_ref[...]
    merged = gate_ref[:, 0:D_MODEL].astype(F32) * _dot(ya_ref[...], wpa_ref[...])
    merged = merged + gate_ref[:, D_MODEL:2 * D_MODEL].astype(F32) * _dot(yb_ref[...], wpb_ref[...])
    merged = merged + gate_ref[:, 2 * D_MODEL:3 * D_MODEL].astype(F32) * _dot(yc.astype(BF16), wpc_ref[...])
    y = _dot(merged.astype(BF16), wo_ref[...])
    o_ref[...] = _deepnorm(x_ref[...], y, gt_ref[0], lng_ref[...], lnb_ref[...])


def _mixer_kernel(sink_ref, x_ref, qkv_ref, ab_ref, z_ref, nw_ref, qb_ref, kvb_ref, kvp_ref, bias_ref,
                  uc_ref, ucp_ref, gate_ref, gt_ref, poolw_ref, pscale_ref, wpa_ref, wpb_ref, wpc_ref, wo_ref,
                  lng_ref, lnb_ref, o_ref, state_ref, ya_ref, yb_ref, *, tm):
    pending = iter(_swa_units(sink_ref, qb_ref, kvb_ref, kvp_ref, bias_ref, yb_ref, tq=tm))

    def filler():
        unit = next(pending, None)
        if unit is not None:
            unit()

    _gdn_kernel(qkv_ref, ab_ref, z_ref, nw_ref, ya_ref, state_ref, tb=tm, filler=filler)
    for unit in pending:
        unit()
    _merge_kernel(x_ref, ya_ref, yb_ref, uc_ref, ucp_ref, gate_ref, gt_ref, poolw_ref, pscale_ref,
                  wpa_ref, wpb_ref, wpc_ref, wo_ref, lng_ref, lnb_ref, o_ref, tm=tm)


def _mixer(x, qkv, ab, z, qb, kvb, uc, gates, mod3, norm_w, sinks, pool_w, pool_scale, w_pa, w_pb, w_pc, w_o,
           ln_g, ln_b, batch, seq):
    t, d = x.shape
    tm = TOKEN_TILE
    per_b = seq // tm

    def tok(n):
        return pl.BlockSpec((tm, n), lambda b, j: (b * per_b + j, 0))

    def prev_rows(rows, n):
        return pl.BlockSpec((rows, n), lambda b, j: (jnp.maximum((b * per_b + j) * (tm // rows) - 1, 0), 0))

    def full(shape):
        return pl.BlockSpec(shape, lambda b, j: (0,) * len(shape))

    return pl.pallas_call(
        functools.partial(_mixer_kernel, tm=tm),
        grid=(batch, per_b),
        in_specs=[
            pl.BlockSpec(memory_space=pltpu.SMEM),
            tok(d), tok(QKV_W), tok(AB_W), tok(GDN_V), full((1, GDN_DV)),
            tok(SWA_Q), tok(2 * SWA_KV), prev_rows(WINDOW, 2 * SWA_KV),
            full((SWA_HKV, GQA_GROUP * WINDOW, WINDOW)),
            tok(POOL_DIM), prev_rows(MAX_POOL, POOL_DIM),
            tok(N_BRANCH * d),
            pl.BlockSpec((1, 1, d), lambda b, j: (b, 0, 2)),
            full((POOL_GROUPS, POOL_GDIM, POOL_GDIM)), full((1, POOL_DIM)),
            full((GDN_V, d)), full((SWA_Q, d)), full((POOL_DIM, d)), full((d, d)),
            full((1, d)), full((1, d)),
        ],
        out_specs=tok(d),
        out_shape=jax.ShapeDtypeStruct((t, d), F32),
        scratch_shapes=[pltpu.VMEM((GDN_HEADS, GDN_DK, GDN_DV), F32), pltpu.VMEM((tm, GDN_V), BF16),
                        pltpu.VMEM((tm, SWA_Q), BF16)],
        compiler_params=_params("parallel", "arbitrary"),
        name="mixer",
    )(sinks, x, qkv, ab, z, norm_w, qb, kvb, kvb, _swa_bias(), uc, uc, gates, mod3, pool_w, pool_scale,
      w_pa, w_pb, w_pc, w_o, ln_g, ln_b)


def _router_kernel(x_ref, sh_ref, sc_ref, rwt_ref, rb_ref, tri_ref, hp_ref, topi_ref, topg_ref, rank_ref, cnt_ref,
                   run_ref):
    @pl.when(pl.program_id(0) == 0)
    def _():
        run_ref[...] = jnp.zeros_like(run_ref)

    h = _layer_norm(x_ref[...]) * (1.0 + sc_ref[0]) + sh_ref[0]
    hp_ref[...] = _pack_rows(h)
    logits = lax.dot_general(rwt_ref[...], h, (((1,), (1,)), ((), ())), preferred_element_type=F32,
                             precision=lax.Precision.HIGHEST) + rb_ref[...]
    sub = lax.broadcasted_iota(jnp.int32, logits.shape, 0)
    vals, idxs = [], []
    for _ in range(TOP_K):
        m = jnp.max(logits, 0, keepdims=True)
        idx = jnp.min(jnp.where(logits == m, sub, N_EXPERTS), 0, keepdims=True)
        vals.append(m)
        idxs.append(idx)
        logits = jnp.where(sub == idx, -jnp.inf, logits)
    es = [jnp.exp(v - vals[0]) for v in vals]
    denom = es[0] + es[1] + es[2] + es[3]
    topi_ref[...] = jnp.concatenate(idxs, axis=0)
    topg_ref[...] = jnp.concatenate([e / denom for e in es], axis=0)

    sel = jnp.zeros(logits.shape, F32)
    for idx in idxs:
        sel = sel + (sub == idx).astype(F32)
    before = run_ref[:, 0:1] + _dot(sel.astype(BF16), tri_ref[...])
    ranks = [jnp.sum(jnp.where(sub == idx, before, 0.0), 0, keepdims=True) for idx in idxs]
    rank_ref[...] = jnp.concatenate(ranks, axis=0).astype(jnp.int32)
    run_ref[...] = run_ref[...] + jnp.sum(sel, 1, keepdims=True)
    cnt_ref[...] = run_ref[...].astype(jnp.int32)


def _router(x, mod3, router_wt, router_b, seq):
    t, d = x.shape
    tm = ROUTER_TILE
    per_b = seq // tm
    tri = (jnp.arange(tm)[:, None] < jnp.arange(tm)[None, :]).astype(BF16)
    return pl.pallas_call(
        _router_kernel,
        grid=(t // tm,),
        in_specs=[
            pl.BlockSpec((tm, d), lambda i: (i, 0)),
            pl.BlockSpec((1, 1, d), lambda i: (i // per_b, 0, 3)),
            pl.BlockSpec((1, 1, d), lambda i: (i // per_b, 0, 4)),
            pl.BlockSpec((N_EXPERTS, d), lambda i: (0, 0)),
            pl.BlockSpec((N_EXPERTS, 1), lambda i: (0, 0)),
            pl.BlockSpec((tm, tm), lambda i: (0, 0)),
        ],
        out_specs=[
            pl.BlockSpec((tm, PACKED_D), lambda i: (i, 0)),
            pl.BlockSpec((TOP_K, tm), lambda i: (0, i)),
            pl.BlockSpec((TOP_K, tm), lambda i: (0, i)),
            pl.BlockSpec((TOP_K, tm), lambda i: (0, i)),
            pl.BlockSpec((N_EXPERTS, 128), lambda i: (0, 0)),
        ],
        out_shape=[
            jax.ShapeDtypeStruct((t, PACKED_D), U32),
            jax.ShapeDtypeStruct((TOP_K, t), jnp.int32),
            jax.ShapeDtypeStruct((TOP_K, t), F32),
            jax.ShapeDtypeStruct((TOP_K, t), jnp.int32),
            jax.ShapeDtypeStruct((N_EXPERTS, 128), jnp.int32),
        ],
        scratch_shapes=[pltpu.VMEM((N_EXPERTS, 128), F32)],
        compiler_params=_params("arbitrary"),
        name="router",
    )(x, mod3, mod3, router_wt, router_b, tri)


def _dispatch_plan(topi_t, rank_t, counts, n_blocks):
    padded = (counts + ROUTE_BLOCK - 1) // ROUTE_BLOCK * ROUTE_BLOCK
    ends_p = jnp.cumsum(padded)
    pstart = ends_p - padded
    experts = jnp.arange(N_EXPERTS, dtype=jnp.int32)
    pstart_tok = jnp.sum(jnp.where(topi_t[:, :, None] == experts, pstart, 0), -1)
    dest_t = (pstart_tok + rank_t).astype(jnp.int32)
    blk_start = jnp.arange(n_blocks, dtype=jnp.int32) * ROUTE_BLOCK
    block_e = jnp.minimum(jnp.sum((blk_start[:, None] >= ends_p).astype(jnp.int32), -1), N_EXPERTS - 1)
    blk_end = jnp.sum(jnp.where(block_e[:, None] == experts, pstart + counts, 0), -1)
    n_valid = jnp.clip(blk_end - blk_start, 0, ROUTE_BLOCK).astype(jnp.int32)
    return dest_t, block_e.astype(jnp.int32), n_valid


def _sc_mesh():
    return plsc.VectorSubcoreMesh(core_axis_name="c", subcore_axis_name="s")


def _sc_dispatch(hp, dest_flat, rows):
    t, dp = hp.shape
    per_w = t // SC_WORKERS
    n_chunks = per_w // SC_CHUNK

    assert n_chunks % 2 == 0

    @functools.partial(
        pl.kernel, mesh=_sc_mesh(), out_type=jax.ShapeDtypeStruct((rows, dp), hp.dtype),
        scratch_types=[pltpu.VMEM((TOP_K * n_chunks, SC_CHUNK), jnp.int32),
                       pltpu.VMEM((SC_CHUNK, dp), hp.dtype), pltpu.VMEM((SC_CHUNK, dp), hp.dtype),
                       pltpu.SemaphoreType.DMA, pltpu.SemaphoreType.DMA,
                       pltpu.SemaphoreType.DMA, pltpu.SemaphoreType.DMA])
    def dispatch(h_hbm, d_hbm, xs_hbm, idx_v, rows_a, rows_b, load_a, load_b, scat_a, scat_b):
        worker = lax.axis_index("s") * SC_CORES + lax.axis_index("c")
        base = worker * per_w
        for k in range(TOP_K):
            pltpu.sync_copy(d_hbm.at[pl.ds((k * SC_WORKERS + worker) * n_chunks, n_chunks)],
                            idx_v.at[pl.ds(k * n_chunks, n_chunks)])

        def load(i, buf, sem):
            return pltpu.make_async_copy(h_hbm.at[pl.ds(base + i * SC_CHUNK, SC_CHUNK)], buf, sem)

        def scatter(i, k, buf, sem):
            return pltpu.make_async_copy(buf, xs_hbm.at[idx_v.at[k * n_chunks + i]], sem)

        def scatter_all(i, buf, sem):
            for k in range(TOP_K):
                scatter(i, k, buf, sem).start()
            for k in range(TOP_K):
                scatter(i, k, buf, sem).wait()

        load(0, rows_a, load_a).start()

        @pl.loop(0, n_chunks, step=2)
        def _(i):
            load(i, rows_a, load_a).wait()
            load(i + 1, rows_b, load_b).start()
            scatter_all(i, rows_a, scat_a)
            load(i + 1, rows_b, load_b).wait()

            @pl.when(i + 2 < n_chunks)
            def _():
                load(i + 2, rows_a, load_a).start()

            scatter_all(i + 1, rows_b, scat_b)

    return dispatch(hp, dest_flat.reshape(TOP_K * SC_WORKERS * n_chunks, SC_CHUNK))


def _sc_gather(table, idx):
    n = idx.shape[0]
    dp = table.shape[1]
    per_w = n // SC_WORKERS
    n_chunks = per_w // SC_CHUNK

    assert n_chunks % 2 == 0

    @functools.partial(
        pl.kernel, mesh=_sc_mesh(), out_type=jax.ShapeDtypeStruct((n, dp), table.dtype),
        scratch_types=[pltpu.VMEM((n_chunks, SC_CHUNK), jnp.int32),
                       pltpu.VMEM((SC_CHUNK, dp), table.dtype), pltpu.VMEM((SC_CHUNK, dp), table.dtype),
                       pltpu.SemaphoreType.DMA, pltpu.SemaphoreType.DMA,
                       pltpu.SemaphoreType.DMA, pltpu.SemaphoreType.DMA])
    def gather(t_hbm, i_hbm, o_hbm, idx_v, rows_a, rows_b, fetch_a, fetch_b, store_a, store_b):
        worker = lax.axis_index("s") * SC_CORES + lax.axis_index("c")
        base = worker * per_w
        pltpu.sync_copy(i_hbm.at[pl.ds(worker * n_chunks, n_chunks)], idx_v)

        def fetch(i, buf, sem):
            return pltpu.make_async_copy(t_hbm.at[idx_v.at[i]], buf, sem)

        def store(i, buf, sem):
            return pltpu.make_async_copy(buf, o_hbm.at[pl.ds(base + i * SC_CHUNK, SC_CHUNK)], sem)

        fetch(0, rows_a, fetch_a).start()

        @pl.loop(0, n_chunks, step=2)
        def _(i):
            fetch(i, rows_a, fetch_a).wait()

            @pl.when(i > 0)
            def _():
                store(i - 1, rows_b, store_b).wait()

            fetch(i + 1, rows_b, fetch_b).start()
            store(i, rows_a, store_a).start()
            fetch(i + 1, rows_b, fetch_b).wait()
            store(i, rows_a, store_a).wait()

            @pl.when(i + 2 < n_chunks)
            def _():
                fetch(i + 2, rows_a, fetch_a).start()

            store(i + 1, rows_b, store_b).start()

        store(n_chunks - 1, rows_b, store_b).wait()

    return gather(table, idx.reshape(SC_WORKERS * n_chunks, SC_CHUNK))


def _expert_kernel(be_ref, nv_ref, x_ref, w1_ref, b1_ref, w2_ref, b2_ref, o_ref, w1b_ref, w2b_ref):
    i = pl.program_id(0)
    n_valid = nv_ref[i]
    new_expert = (i == 0) | (be_ref[i] != be_ref[jnp.maximum(i - 1, 0)])

    @pl.when(new_expert & (n_valid > 0))
    def _():
        w1b_ref[...] = w1_ref[0].astype(BF16)
        w2b_ref[...] = w2_ref[0].astype(BF16)

    for piece in range(ROUTE_BLOCK // EXPERT_ROWS):
        r = slice(piece * EXPERT_ROWS, (piece + 1) * EXPERT_ROWS)
        left = n_valid - piece * EXPERT_ROWS

        @pl.when(left > 0)
        def _():
            live = lax.broadcasted_iota(jnp.int32, (EXPERT_ROWS, D_MODEL), 0) < left
            x = jnp.where(live, _unpack_rows(x_ref[r, :]), 0.0).astype(BF16)
            gu = _dot(x, w1b_ref[...]) + b1_ref[0]
            glu = jnp.minimum(gu[:, :D_FF], SWIGLU_LIMIT)
            lin = jnp.clip(gu[:, D_FF:], -SWIGLU_LIMIT, SWIGLU_LIMIT)
            act = glu * _sigmoid(SWIGLU_ALPHA * glu) * (lin + 1.0)
            o_ref[r, :] = _pack_rows(_dot(act.astype(BF16), w2b_ref[...]) + b2_ref[0])

        @pl.when(left <= 0)
        def _():
            o_ref[r, :] = jnp.zeros((EXPERT_ROWS, PACKED_D), U32)


def _experts(xs, block_e, n_valid, w1, b1, w2, b2, layer):
    rows, dp = xs.shape
    d = D_MODEL
    n_blocks = rows // ROUTE_BLOCK
    e0 = layer * N_EXPERTS
    grid_spec = pltpu.PrefetchScalarGridSpec(
        num_scalar_prefetch=2,
        grid=(n_blocks,),
        in_specs=[
            pl.BlockSpec((ROUTE_BLOCK, dp), lambda i, be, nv: (i, 0)),
            pl.BlockSpec((1, d, 2 * D_FF), lambda i, be, nv: (e0 + be[i], 0, 0)),
            pl.BlockSpec((1, 1, 2 * D_FF), lambda i, be, nv: (e0 + be[i], 0, 0)),
            pl.BlockSpec((1, D_FF, d), lambda i, be, nv: (e0 + be[i], 0, 0)),
            pl.BlockSpec((1, 1, d), lambda i, be, nv: (e0 + be[i], 0, 0)),
        ],
        out_specs=pl.BlockSpec((ROUTE_BLOCK, dp), lambda i, be, nv: (i, 0)),
        scratch_shapes=[pltpu.VMEM((d, 2 * D_FF), BF16), pltpu.VMEM((D_FF, d), BF16)],
    )
    return pl.pallas_call(
        _expert_kernel,
        grid_spec=grid_spec,
        out_shape=jax.ShapeDtypeStruct((rows, dp), U32),
        compiler_params=_params("arbitrary"),
        name="experts",
    )(block_e, n_valid, xs, w1, b1, w2, b2)


def _combine_kernel(x_ref, yg_ref, topg_ref, gt_ref, lng_ref, lnb_ref, o_ref):
    o_ref[...] = _moe_combine(x_ref[...], [yg_ref[k] for k in range(TOP_K)], topg_ref[...], gt_ref[0],
                              lng_ref[...], lnb_ref[...])


def _combine(x, yg, topg, mod3, ln_g, ln_b, seq):
    t, d = x.shape
    tm = TOKEN_TILE
    per_b = seq // tm
    return pl.pallas_call(
        _combine_kernel,
        grid=(t // tm,),
        in_specs=[
            pl.BlockSpec((tm, d), lambda i: (i, 0)),
            pl.BlockSpec((TOP_K, tm, PACKED_D), lambda i: (0, i, 0)),
            pl.BlockSpec((TOP_K, tm), lambda i: (0, i)),
            pl.BlockSpec((1, 1, d), lambda i: (i // per_b, 0, 5)),
            pl.BlockSpec((1, d), lambda i: (0, 0)),
            pl.BlockSpec((1, d), lambda i: (0, 0)),
        ],
        out_specs=pl.BlockSpec((tm, d), lambda i: (i, 0)),
        out_shape=jax.ShapeDtypeStruct((t, d), F32),
        compiler_params=_params("parallel"),
        name="combine",
    )(x, yg, topg, mod3, ln_g, ln_b)


def _rearranged_w_in(w_in):
    depth, d, _ = w_in.shape
    ab_end = SEG_AB + 2 * GDN_HEADS
    ab_pad = jnp.zeros((depth, d, AB_W - 2 * GDN_HEADS), w_in.dtype)
    return jnp.concatenate([w_in[:, :, :ab_end], ab_pad, w_in[:, :, ab_end:]], axis=2).astype(BF16)


def kernel(x, c, ada_w, ada_b, w_in, conv_w, a_log, dt_bias, gdn_norm_w, sinks, pool_w, pool_scale, w_pa, w_pb, w_pc, w_o, ln1_g, ln1_b, ln2_g, ln2_b, router_w, router_b, exp_w1, exp_b1, exp_w2, exp_b2):
    batch, seq, d = x.shape
    t = batch * seq
    n_blocks = (t * TOP_K + ROUTE_BLOCK - 1) // ROUTE_BLOCK + N_EXPERTS
    mod = _ada_mod(c, ada_w, ada_b)
    xt = x.reshape(t, d)
    lane_pad = jnp.zeros((AB_W - GDN_HEADS,), F32)
    rows = n_blocks * ROUTE_BLOCK
    w_in_all = _rearranged_w_in(w_in)
    w1_all = exp_w1.reshape(DEPTH * N_EXPERTS, d, 2 * D_FF)
    b1_all = exp_b1.reshape(DEPTH * N_EXPERTS, 1, 2 * D_FF)
    w2_all = exp_w2.reshape(DEPTH * N_EXPERTS, D_FF, d)
    b2_all = exp_b2.reshape(DEPTH * N_EXPERTS, 1, d)
    moe = None
    for l in range(DEPTH):
        mod3 = mod[l].reshape(batch, 1, 6 * d)
        gpar = jnp.stack([jnp.concatenate([-jnp.exp(a_log[l]), lane_pad]), jnp.concatenate([dt_bias[l], lane_pad])])
        if moe is None:
            qkv, z, ab, qb, kvb, uc, gates = _in_proj(xt, mod3, w_in_all, conv_w, gpar, seq, l)
        else:
            xt, qkv, z, ab, qb, kvb, uc, gates = _in_proj(xt, mod3, w_in_all, conv_w, gpar, seq, l, moe)
        xt = _mixer(xt, qkv, ab, z, qb, kvb, uc, gates, mod3, gdn_norm_w[l].reshape(1, GDN_DV), sinks[l],
                    pool_w[l].astype(BF16), pool_scale[l].reshape(1, POOL_DIM),
                    w_pa[l].astype(BF16), w_pb[l].astype(BF16), w_pc[l].astype(BF16), w_o[l].astype(BF16),
                    ln1_g[l].reshape(1, d), ln1_b[l].reshape(1, d), batch, seq)
        hp, topi_t, topg_t, rank_t, cnt = _router(xt, mod3, router_w[l].T, router_b[l].reshape(N_EXPERTS, 1), seq)
        dest_t, block_e, n_valid = _dispatch_plan(topi_t, rank_t, cnt[:, 0], n_blocks)
        dest_flat = dest_t.reshape(TOP_K * t)
        xs = _sc_dispatch(hp, dest_flat, rows)
        ys = _experts(xs, block_e, n_valid, w1_all, b1_all, w2_all, b2_all, l)
        yg = _sc_gather(ys, dest_flat).reshape(TOP_K, t, PACKED_D)
        moe = (yg, topg_t, mod3, ln2_g[l].reshape(1, d), ln2_b[l].reshape(1, d))
    xt = _combine(xt, *moe, seq)
    return xt.reshape(batch, seq, d)
```

```python
import functools

import jax
import jax.numpy as jnp
from jax import lax
from jax.experimental import pallas as pl
from jax.experimental.pallas import tpu as pltpu
from jax.experimental.pallas import tpu_sc as plsc

D_MODEL = 1024
DEPTH = 4
GDN_HEADS = 4
GDN_DK = 128
GDN_DV = 128
GDN_CONV = 4
GDN_CHUNK = 64
SWA_HQ = 8
SWA_HKV = 2
SWA_DH = 64
WINDOW = 128
POOL_WINDOWS = (2, 4, 8, 16)
POOL_GROUPS = 4
POOL_GDIM = 128
N_BRANCH = 3
N_EXPERTS = 32
TOP_K = 4
D_FF = 1024
ROUTE_BLOCK = 1024
EXPERT_ROWS = 512
SWIGLU_ALPHA = 1.702
SWIGLU_LIMIT = 7.0
LN_EPS = 1e-5
RMS_EPS = 1e-6
DEEPNORM_ALPHA = (2 * DEPTH) ** 0.25

GDN_QK = GDN_HEADS * GDN_DK
GDN_V = GDN_HEADS * GDN_DV
SWA_Q = SWA_HQ * SWA_DH
SWA_KV = SWA_HKV * SWA_DH
POOL_DIM = POOL_GROUPS * POOL_GDIM
GQA_GROUP = SWA_HQ // SWA_HKV
MAX_POOL = max(POOL_WINDOWS)

QKV_W = 2 * GDN_QK + GDN_V
AB_W = 128
SEG_QKV = 0
SEG_Z = SEG_QKV + QKV_W
SEG_AB = SEG_Z + GDN_V
SEG_QB = SEG_AB + AB_W
SEG_KVB = SEG_QB + SWA_Q
SEG_UC = SEG_KVB + 2 * SWA_KV
SEG_GATE = SEG_UC + POOL_DIM
PROJ_W = SEG_GATE + N_BRANCH * D_MODEL

PROJ_CHUNK = 256
TOKEN_TILE = 512
ROUTER_TILE = 1024
NEG_BIG = -1e30
VMEM_LIMIT = 56 * 1024 * 1024

PACKED_D = D_MODEL // 2
SC_CORES = 2
SC_SUBCORES = 16
SC_WORKERS = SC_CORES * SC_SUBCORES
SC_CHUNK = 64

F32 = jnp.float32
BF16 = jnp.bfloat16
U32 = jnp.uint32
HI_MASK = 0xFFFF0000


def _pack_rows(x):
    bits = pltpu.bitcast(x.astype(BF16).astype(F32), U32)
    return (bits[:, :PACKED_D] >> 16) | (bits[:, PACKED_D:] & jnp.uint32(HI_MASK))


def _unpack_rows(w):
    lo = pltpu.bitcast(w << 16, F32)
    hi = pltpu.bitcast(w & jnp.uint32(HI_MASK), F32)
    return jnp.concatenate([lo, hi], axis=1)


def _params(*sem):
    return pltpu.CompilerParams(dimension_semantics=sem, vmem_limit_bytes=VMEM_LIMIT)


def _sigmoid(x):
    return 0.5 * jnp.tanh(0.5 * x) + 0.5


def _layer_norm(x):
    mu = jnp.mean(x, -1, keepdims=True)
    xc = x - mu
    var = jnp.mean(xc * xc, -1, keepdims=True)
    return xc * lax.rsqrt(var + LN_EPS)


def _dot(a, b):
    return jnp.dot(a, b, preferred_element_type=F32)


def _dot_nt(a, b):
    return lax.dot_general(a, b, (((1,), (1,)), ((), ())), preferred_element_type=F32)


def _bdot(a, b):
    return lax.dot_general(a, b, (((2,), (1,)), ((0,), (0,))), preferred_element_type=F32)


def _bdot_tn(a, b):
    return lax.dot_general(a, b, (((1,), (1,)), ((0,), (0,))), preferred_element_type=F32)


def _bdot_nt(a, b):
    return lax.dot_general(a, b, (((2,), (2,)), ((0,), (0,))), preferred_element_type=F32)


def _ada_kernel(c_ref, w_ref, b_ref, o_ref):
    c = c_ref[...]
    cond = c * _sigmoid(c)
    o_ref[0] = _dot(cond.astype(BF16), w_ref[0].astype(BF16)) + b_ref[0]


def _ada_mod(c, ada_w, ada_b):
    depth, d, n = ada_w.shape
    b = c.shape[0]
    tn = 1024
    return pl.pallas_call(
        _ada_kernel,
        grid=(depth, n // tn),
        in_specs=[
            pl.BlockSpec((b, d), lambda l, j: (0, 0)),
            pl.BlockSpec((1, d, tn), lambda l, j: (l, 0, j)),
            pl.BlockSpec((1, 1, tn), lambda l, j: (l, 0, j)),
        ],
        out_specs=pl.BlockSpec((1, b, tn), lambda l, j: (l, 0, j)),
        out_shape=jax.ShapeDtypeStruct((depth, b, n), F32),
        compiler_params=_params("parallel", "parallel"),
        name="ada_mod",
    )(c, ada_w, ada_b.reshape(depth, 1, n))


HEAD_UNITS = 4


def _moe_combine(x, yg_rows, gates_t, gt, ln_g, ln_b):
    g = jnp.concatenate([gates_t, jnp.zeros((128 - TOP_K, gates_t.shape[1]), F32)], axis=0).T
    y = g[:, 0:1] * _unpack_rows(yg_rows[0])
    for k in range(1, TOP_K):
        y = y + g[:, k:k + 1] * _unpack_rows(yg_rows[k])
    return _deepnorm(x, y, gt, ln_g, ln_b)


def _proj_kernel(*refs, tm, per_b, fused):
    if fused:
        (x_ref, yg_ref, topg_ref, gt_ref, lng_ref, lnb_ref, sh_ref, sc_ref, w_ref, convw_ref, gpar_ref,
         xo_ref, qkv_ref, z_ref, ab_ref, qb_ref, kvb_ref, uc_ref, gate_ref, h_ref, raw_ref) = refs
    else:
        (x_ref, sh_ref, sc_ref, w_ref, convw_ref, gpar_ref,
         qkv_ref, z_ref, ab_ref, qb_ref, kvb_ref, uc_ref, gate_ref, h_ref, raw_ref) = refs
    i = pl.program_id(0)
    slot = i % 2

    @pl.when(i == 0)
    def _():
        h_ref[1] = jnp.zeros(h_ref.shape[1:], BF16)

    rows_per_unit = tm // HEAD_UNITS

    def head_unit(u):
        r = slice(u * rows_per_unit, (u + 1) * rows_per_unit)
        x = x_ref[r, :]
        if fused:
            x = _moe_combine(x, [yg_ref[k, r, :] for k in range(TOP_K)], topg_ref[:, r], gt_ref[0],
                             lng_ref[...], lnb_ref[...])
            xo_ref[r, :] = x
        h_ref[slot, r, :] = (_layer_norm(x) * (1.0 + sc_ref[0]) + sh_ref[0]).astype(BF16)

    h = h_ref[1 - slot]

    def seg(start, width):
        return _dot(h, w_ref[:, start:start + width])

    seq_start = (i == 0) | ((i + per_b - 1) % per_b == 0)

    @pl.when(seq_start)
    def _():
        raw_ref[0:8, :] = jnp.zeros((8, QKV_W), F32)

    @pl.when(jnp.logical_not(seq_start))
    def _():
        raw_ref[0:8, :] = raw_ref[tm:tm + 8, :]

    for j in range(QKV_W // PROJ_CHUNK):
        cols = slice(j * PROJ_CHUNK, (j + 1) * PROJ_CHUNK)
        raw_ref[8:8 + tm, cols] = seg(SEG_QKV + j * PROJ_CHUNK, PROJ_CHUNK)
        if j < HEAD_UNITS:
            head_unit(j)

    def conv_silu(j):
        cols = slice(j * PROJ_CHUNK, (j + 1) * PROJ_CHUNK)
        conv = raw_ref[8:8 + tm, cols] * convw_ref[GDN_CONV - 1:GDN_CONV, cols]
        for tap in range(GDN_CONV - 1):
            conv = conv + raw_ref[5 + tap:5 + tap + tm, cols] * convw_ref[tap:tap + 1, cols]
        qkv_ref[:, cols] = (conv * _sigmoid(conv)).astype(BF16)
    n_conv = QKV_W // PROJ_CHUNK
    done = 0
    for ref, start, width in ((z_ref, SEG_Z, GDN_V), (qb_ref, SEG_QB, SWA_Q), (kvb_ref, SEG_KVB, 2 * SWA_KV),
                              (uc_ref, SEG_UC, POOL_DIM)):
        for j in range(width // PROJ_CHUNK):
            cols = slice(j * PROJ_CHUNK, (j + 1) * PROJ_CHUNK)
            ref[:, cols] = seg(start + j * PROJ_CHUNK, PROJ_CHUNK).astype(BF16)
            if done < n_conv:
                conv_silu(done)
                done += 1
    assert done == n_conv and HEAD_UNITS <= n_conv
    ab = seg(SEG_AB, AB_W)
    gpar = gpar_ref[...]
    gc = gpar[0:1] * _softplus(ab + gpar[1:2])
    pos = lax.broadcasted_iota(jnp.int32, (tm, AB_W), 0) % GDN_CHUNK
    span = 1
    while span < GDN_CHUNK:
        gc = gc + jnp.where(pos >= span, pltpu.roll(gc, span, axis=0), 0.0)
        span *= 2
    lane = lax.broadcasted_iota(jnp.int32, (tm, AB_W), 1)
    ab_ref[...] = jnp.where(lane < GDN_HEADS, gc, _sigmoid(ab))
    gate_ref[...] = _sigmoid(seg(SEG_GATE, N_BRANCH * D_MODEL)).astype(BF16)


def _in_proj(x, mod3, w, conv_w, gpar, seq, layer, moe=None):
    t, d = x.shape
    tm = TOKEN_TILE
    per_b = seq // tm
    n_tiles = t // tm
    widths = (QKV_W, GDN_V, AB_W, SWA_Q, 2 * SWA_KV, POOL_DIM, N_BRANCH * D_MODEL)
    dtypes = (BF16, BF16, F32, BF16, BF16, BF16, BF16)

    def head(i):
        return jnp.minimum(i, n_tiles - 1)

    def proj(i):
        return jnp.maximum(i - 1, 0)

    def mod_spec(k):
        return pl.BlockSpec((1, 1, d), lambda i: (head(i) // per_b, 0, k))

    in_specs = [pl.BlockSpec((tm, d), lambda i: (head(i), 0))]
    operands = [x]
    out_specs = [pl.BlockSpec((tm, n), lambda i: (proj(i), 0)) for n in widths]
    out_shape = [jax.ShapeDtypeStruct((t, n), dt) for n, dt in zip(widths, dtypes)]
    if moe is not None:
        yg, topg_t, mod3_prev, ln_g, ln_b = moe
        in_specs += [pl.BlockSpec((TOP_K, tm, PACKED_D), lambda i: (0, head(i), 0)),
                     pl.BlockSpec((TOP_K, tm), lambda i: (0, head(i))),
                     mod_spec(5),
                     pl.BlockSpec((1, d), lambda i: (0, 0)), pl.BlockSpec((1, d), lambda i: (0, 0))]
        operands += [yg, topg_t, mod3_prev, ln_g, ln_b]
        out_specs = [pl.BlockSpec((tm, d), lambda i: (head(i), 0))] + out_specs
        out_shape = [jax.ShapeDtypeStruct((t, d), F32)] + out_shape
    in_specs += [mod_spec(0), mod_spec(1),
                 pl.BlockSpec((None, d, PROJ_W), lambda i: (layer, 0, 0), pipeline_mode=pl.Buffered(1)),
                 pl.BlockSpec((None, GDN_CONV, QKV_W), lambda i: (layer, 0, 0)),
                 pl.BlockSpec((2, AB_W), lambda i: (0, 0))]
    operands += [mod3, mod3, w, conv_w, gpar]
    return pl.pallas_call(
        functools.partial(_proj_kernel, tm=tm, per_b=per_b, fused=moe is not None),
        grid=(n_tiles + 1,),
        in_specs=in_specs,
        out_specs=out_specs,
        out_shape=out_shape,
        scratch_shapes=[pltpu.VMEM((2, tm, d), BF16), pltpu.VMEM((8 + tm, QKV_W), F32)],
        compiler_params=_params("arbitrary"),
        name="in_proj",
    )(*operands)


def _softplus(x):
    return jnp.maximum(x, 0.0) + jnp.log(1.0 + jnp.exp(-jnp.abs(x)))


def _unit_lower_inverse(lm, row, col, between_levels=lambda: None):
    n = lm.shape[-1]
    eye = (row == col).astype(F32)
    t = None
    s = 1
    while s < n:
        join = ((row // (2 * s)) == (col // (2 * s))) & ((row % (2 * s)) >= s) & ((col % (2 * s)) < s)
        cm = jnp.where(join, lm, 0.0)
        if t is None:
            t = eye - cm
        else:
            t16 = t.astype(BF16)
            p = _bdot(t16, cm.astype(BF16))
            t = t - _bdot(p.astype(BF16), t16)
            between_levels()
        s *= 2
    return t


def _gdn_kernel(qkv_ref, ab_ref, z_ref, nw_ref, o_ref, state_ref, *, tb, filler=lambda: None):
    c_len = GDN_CHUNK
    nc = tb // c_len

    @pl.when(pl.program_id(1) == 0)
    def _():
        state_ref[...] = jnp.zeros_like(state_ref)

    gc_all = ab_ref[...]
    beta_all = gc_all

    nh = GDN_HEADS
    nb = nc * nh
    row = lax.broadcasted_iota(jnp.int32, (1, c_len, c_len), 1)
    col = lax.broadcasted_iota(jnp.int32, (1, c_len, c_len), 2)
    causal = row >= col
    strict = row > col
    diag = row == col

    def chunked(per_head):
        return jnp.stack([per_head(h).reshape(nc, c_len, GDN_DK) for h in range(nh)], axis=1).reshape(nb, c_len, GDN_DK)

    q = chunked(lambda h: qkv_ref[:, h * GDN_DK:(h + 1) * GDN_DK].astype(F32))
    k = chunked(lambda h: qkv_ref[:, GDN_QK + h * GDN_DK:GDN_QK + (h + 1) * GDN_DK].astype(F32))
    v = chunked(lambda h: qkv_ref[:, 2 * GDN_QK + h * GDN_DV:2 * GDN_QK + (h + 1) * GDN_DV].astype(F32))
    gc = chunked(lambda h: jnp.broadcast_to(gc_all[:, h:h + 1], (tb, GDN_DK)))
    beta = chunked(lambda h: jnp.broadcast_to(beta_all[:, nh + h:nh + h + 1], (tb, GDN_DK)))
    q = q * lax.rsqrt(jnp.sum(q * q, -1, keepdims=True) + 1e-6) * (GDN_DK ** -0.5)
    k = k * lax.rsqrt(jnp.sum(k * k, -1, keepdims=True) + 1e-6)

    gc_i = gc[:, :, :c_len]
    gc_j = jnp.sum(jnp.where(diag, gc_i, 0.0), axis=1, keepdims=True)
    decay = jnp.where(causal, jnp.exp(jnp.where(causal, gc_i - gc_j, 0.0)), 0.0)
    eg = jnp.exp(gc)
    gc_last = gc[:, c_len - 1:c_len, :]
    g_last = jnp.exp(gc_last)
    k_beta = k * beta
    kq = _bdot_nt(jnp.concatenate([k_beta, q], axis=1).astype(BF16), k.astype(BF16))
    lower = jnp.where(strict, kq[:, :c_len] * decay, 0.0)
    attn = jnp.where(causal, kq[:, c_len:] * decay, 0.0).astype(BF16)
    tinv = _unit_lower_inverse(lower, row, col, filler)
    uw = _bdot(tinv.astype(BF16), jnp.concatenate([v * beta, k_beta * eg], axis=2).astype(BF16)).astype(BF16)
    q_g = q * eg
    k_g = (k * jnp.exp(gc_last - gc)).astype(BF16)

    k_uw = _bdot_tn(k_g, uw)
    a_uw = _bdot(attn, uw)
    ku = k_uw[:, :, :GDN_DV]
    o_loc = a_uw[:, :, :GDN_DV]
    qk_eff = jnp.concatenate([q_g - a_uw[:, :, GDN_DV:], k_uw[:, :, GDN_DV:]], axis=1).astype(BF16)

    st = state_ref[...]
    outs = []
    for c in range(nc):
        sl = slice(c * nh, (c + 1) * nh)
        prod = _bdot(qk_eff[sl], st.astype(BF16))
        outs.append(prod[:, :c_len] + o_loc[sl])
        st = st * g_last[sl] - prod[:, c_len:] + ku[sl]
        filler()
    state_ref[...] = st

    nw = nw_ref[...]
    for h in range(nh):
        o = jnp.concatenate([outs[c][h] for c in range(nc)], axis=0)
        zh = z_ref[:, h * GDN_DV:(h + 1) * GDN_DV].astype(F32)
        o = o * lax.rsqrt(jnp.mean(o * o, -1, keepdims=True) + RMS_EPS) * nw * (zh * _sigmoid(zh))
        o_ref[:, h * GDN_DV:(h + 1) * GDN_DV] = o.astype(BF16)


def _swa_units(sink_ref, q_ref, kv_ref, kvp_ref, bias_ref, o_ref, *, tq):
    nw = tq // WINDOW
    rows = GQA_GROUP * WINDOW
    scale = SWA_DH ** -0.5

    def unit(hk, w):
        heads = range(hk * GQA_GROUP, (hk + 1) * GQA_GROUP)
        tok = slice(w * WINDOW, (w + 1) * WINDOW)

        def keys(col0):
            cols = slice(col0, col0 + SWA_DH)
            prev = kvp_ref[:, cols] if w == 0 else kv_ref[(w - 1) * WINDOW:w * WINDOW, cols]
            return jnp.concatenate([prev, kv_ref[tok, cols]], axis=0)

        k = keys(hk * SWA_DH)
        v = keys(SWA_KV + hk * SWA_DH)
        q = jnp.concatenate([q_ref[tok, hq * SWA_DH:(hq + 1) * SWA_DH] for hq in heads], axis=0)
        sink = jnp.concatenate([jnp.full((WINDOW, 1), sink_ref[hq], F32) for hq in heads], axis=0)
        qi = lax.broadcasted_iota(jnp.int32, (rows, WINDOW), 0) % WINDOW
        kj = lax.broadcasted_iota(jnp.int32, (rows, WINDOW), 1)
        from_prev = kj > qi
        s2 = _dot_nt(q, k)
        s = jnp.where(from_prev, s2[:, :WINDOW], s2[:, WINDOW:]) * scale + bias_ref[hk]
        if w == 0:
            s = jnp.where(from_prev & (pl.program_id(1) == 0), NEG_BIG, s)
        m = jnp.maximum(jnp.max(s, -1, keepdims=True), sink)
        p = jnp.exp(s - m)
        p2 = jnp.concatenate([jnp.where(from_prev, p, 0.0), jnp.where(from_prev, 0.0, p)], axis=1).astype(BF16)
        v_ext = jnp.concatenate([v, jnp.zeros((2 * WINDOW, 128 - SWA_DH), BF16), jnp.ones((2 * WINDOW, 128), BF16)],
                                axis=1)
        pv = _dot(p2, v_ext)
        o = (pv[:, :128] / (pv[:, 128:] + jnp.exp(sink - m)))[:, :SWA_DH]
        for g, hq in enumerate(heads):
            o_ref[tok, hq * SWA_DH:(hq + 1) * SWA_DH] = o[g * WINDOW:(g + 1) * WINDOW, :].astype(BF16)

    return [functools.partial(unit, hk, w) for hk in range(SWA_HKV) for w in range(nw)]


def _swa_bias():
    qi = jnp.arange(WINDOW)[:, None]
    kj = jnp.arange(WINDOW)[None, :]
    dist = jnp.where(kj > qi, qi - kj + WINDOW, qi - kj).astype(F32)
    slopes = 2.0 ** (-8.0 * jnp.arange(1, SWA_HQ + 1, dtype=F32) / SWA_HQ)
    bias = -slopes[:, None, None] * dist[None]
    return bias.reshape(SWA_HKV, GQA_GROUP * WINDOW, WINDOW)


def _deepnorm(x, y, gt, g, b):
    return _layer_norm(DEEPNORM_ALPHA * x + (1.0 + gt) * y) * g + b


def _merge_kernel(x_ref, ya_ref, yb_ref, uc_ref, ucp_ref, gate_ref, gt_ref, poolw_ref, pscale_ref,
                  wpa_ref, wpb_ref, wpc_ref, wo_ref, lng_ref, lnb_ref, o_ref, *, tm):
    j = pl.program_id(1)
    u = uc_ref[...].astype(F32)
    halo = jnp.where(j == 0, 0.0, ucp_ref[...].astype(F32))
    ue = jnp.concatenate([halo, u], axis=0)
    tpos = (j * tm + lax.broadcasted_iota(jnp.int32, (tm, POOL_GDIM), 0) + 1).astype(F32)
    ycs = []
    for gi, win in enumerate(POOL_WINDOWS):
        a = ue[:, gi * POOL_GDIM:(gi + 1) * POOL_GDIM]
        span = 1
        while span < win:
            a = a[span:] + a[:-span]
            span *= 2
        lo = MAX_POOL - win + 1
        d = a[lo:lo + tm] / jnp.minimum(tpos, float(win)) - u[:, gi * POOL_GDIM:(gi + 1) * POOL_GDIM]
        ycs.append(_dot(d.astype(BF16), poolw_ref[gi]))
    yc = jnp.concatenate(ycs, axis=1) * pscale_ref[...]
    merged = gate_ref[:, 0:D_MODEL].astype(F32) * _dot(ya_ref[...], wpa_ref[...])
    merged = merged + gate_ref[:, D_MODEL:2 * D_MODEL].astype(F32) * _dot(yb_ref[...], wpb_ref[...])
    merged = merged + gate_ref[:, 2 * D_MODEL:3 * D_MODEL].astype(F32) * _dot(yc.astype(BF16), wpc_ref[...])
    y = _dot(merged.astype(BF16), wo_ref[...])
    o_ref[...] = _deepnorm(x_ref[...], y, gt_ref[0], lng_ref[...], lnb_ref[...])


def _mixer_kernel(sink_ref, x_ref, qkv_ref, ab_ref, z_ref, nw_ref, qb_ref, kvb_ref, kvp_ref, bias_ref,
                  uc_ref, ucp_ref, gate_ref, gt_ref, poolw_ref, pscale_ref, wpa_ref, wpb_ref, wpc_ref, wo_ref,
                  lng_ref, lnb_ref, o_ref, state_ref, ya_ref, yb_ref, *, tm):
    pending = iter(_swa_units(sink_ref, qb_ref, kvb_ref, kvp_ref, bias_ref, yb_ref, tq=tm))

    def filler():
        unit = next(pending, None)
        if unit is not None:
            unit()

    _gdn_kernel(qkv_ref, ab_ref, z_ref, nw_ref, ya_ref, state_ref, tb=tm, filler=filler)
    for unit in pending:
        unit()
    _merge_kernel(x_ref, ya_ref, yb_ref, uc_ref, ucp_ref, gate_ref, gt_ref, poolw_ref, pscale_ref,
                  wpa_ref, wpb_ref, wpc_ref, wo_ref, lng_ref, lnb_ref, o_ref, tm=tm)


def _mixer(x, qkv, ab, z, qb, kvb, uc, gates, mod3, norm_w, sinks, pool_w, pool_scale, w_pa, w_pb, w_pc, w_o,
           ln_g, ln_b, batch, seq):
    t, d = x.shape
    tm = TOKEN_TILE
    per_b = seq // tm

    def tok(n):
        return pl.BlockSpec((tm, n), lambda b, j: (b * per_b + j, 0))

    def prev_rows(rows, n):
        return pl.BlockSpec((rows, n), lambda b, j: (jnp.maximum((b * per_b + j) * (tm // rows) - 1, 0), 0))

    def full(shape):
        return pl.BlockSpec(shape, lambda b, j: (0,) * len(shape))

    return pl.pallas_call(
        functools.partial(_mixer_kernel, tm=tm),
        grid=(batch, per_b),
        in_specs=[
            pl.BlockSpec(memory_space=pltpu.SMEM),
            tok(d), tok(QKV_W), tok(AB_W), tok(GDN_V), full((1, GDN_DV)),
            tok(SWA_Q), tok(2 * SWA_KV), prev_rows(WINDOW, 2 * SWA_KV),
            full((SWA_HKV, GQA_GROUP * WINDOW, WINDOW)),
            tok(POOL_DIM), prev_rows(MAX_POOL, POOL_DIM),
            tok(N_BRANCH * d),
            pl.BlockSpec((1, 1, d), lambda b, j: (b, 0, 2)),
            full((POOL_GROUPS, POOL_GDIM, POOL_GDIM)), full((1, POOL_DIM)),
            full((GDN_V, d)), full((SWA_Q, d)), full((POOL_DIM, d)), full((d, d)),
            full((1, d)), full((1, d)),
        ],
        out_specs=tok(d),
        out_shape=jax.ShapeDtypeStruct((t, d), F32),
        scratch_shapes=[pltpu.VMEM((GDN_HEADS, GDN_DK, GDN_DV), F32), pltpu.VMEM((tm, GDN_V), BF16),
                        pltpu.VMEM((tm, SWA_Q), BF16)],
        compiler_params=_params("parallel", "arbitrary"),
        name="mixer",
    )(sinks, x, qkv, ab, z, norm_w, qb, kvb, kvb, _swa_bias(), uc, uc, gates, mod3, pool_w, pool_scale,
      w_pa, w_pb, w_pc, w_o, ln_g, ln_b)


def _router_kernel(x_ref, sh_ref, sc_ref, rwt_ref, rb_ref, tri_ref, hp_ref, topi_ref, topg_ref, rank_ref, cnt_ref,
                   run_ref):
    @pl.when(pl.program_id(0) == 0)
    def _():
        run_ref[...] = jnp.zeros_like(run_ref)

    h = _layer_norm(x_ref[...]) * (1.0 + sc_ref[0]) + sh_ref[0]
    hp_ref[...] = _pack_rows(h)
    logits = lax.dot_general(rwt_ref[...], h, (((1,), (1,)), ((), ())), preferred_element_type=F32,
                             precision=lax.Precision.HIGHEST) + rb_ref[...]
    sub = lax.broadcasted_iota(jnp.int32, logits.shape, 0)
    vals, idxs = [], []
    for _ in range(TOP_K):
        m = jnp.max(logits, 0, keepdims=True)
        idx = jnp.min(jnp.where(logits == m, sub, N_EXPERTS), 0, keepdims=True)
        vals.append(m)
        idxs.append(idx)
        logits = jnp.where(sub == idx, -jnp.inf, logits)
    es = [jnp.exp(v - vals[0]) for v in vals]
    denom = es[0] + es[1] + es[2] + es[3]
    topi_ref[...] = jnp.concatenate(idxs, axis=0)
    topg_ref[...] = jnp.concatenate([e / denom for e in es], axis=0)

    sel = jnp.zeros(logits.shape, F32)
    for idx in idxs:
        sel = sel + (sub == idx).astype(F32)
    before = run_ref[:, 0:1] + _dot(sel.astype(BF16), tri_ref[...])
    ranks = [jnp.sum(jnp.where(sub == idx, before, 0.0), 0, keepdims=True) for idx in idxs]
    rank_ref[...] = jnp.concatenate(ranks, axis=0).astype(jnp.int32)
    run_ref[...] = run_ref[...] + jnp.sum(sel, 1, keepdims=True)
    cnt_ref[...] = run_ref[...].astype(jnp.int32)


def _router(x, mod3, router_wt, router_b, seq):
    t, d = x.shape
    tm = ROUTER_TILE
    per_b = seq // tm
    tri = (jnp.arange(tm)[:, None] < jnp.arange(tm)[None, :]).astype(BF16)
    return pl.pallas_call(
        _router_kernel,
        grid=(t // tm,),
        in_specs=[
            pl.BlockSpec((tm, d), lambda i: (i, 0)),
            pl.BlockSpec((1, 1, d), lambda i: (i // per_b, 0, 3)),
            pl.BlockSpec((1, 1, d), lambda i: (i // per_b, 0, 4)),
            pl.BlockSpec((N_EXPERTS, d), lambda i: (0, 0)),
            pl.BlockSpec((N_EXPERTS, 1), lambda i: (0, 0)),
            pl.BlockSpec((tm, tm), lambda i: (0, 0)),
        ],
        out_specs=[
            pl.BlockSpec((tm, PACKED_D), lambda i: (i, 0)),
            pl.BlockSpec((TOP_K, tm), lambda i: (0, i)),
            pl.BlockSpec((TOP_K, tm), lambda i: (0, i)),
            pl.BlockSpec((TOP_K, tm), lambda i: (0, i)),
            pl.BlockSpec((N_EXPERTS, 128), lambda i: (0, 0)),
        ],
        out_shape=[
            jax.ShapeDtypeStruct((t, PACKED_D), U32),
            jax.ShapeDtypeStruct((TOP_K, t), jnp.int32),
            jax.ShapeDtypeStruct((TOP_K, t), F32),
            jax.ShapeDtypeStruct((TOP_K, t), jnp.int32),
            jax.ShapeDtypeStruct((N_EXPERTS, 128), jnp.int32),
        ],
        scratch_shapes=[pltpu.VMEM((N_EXPERTS, 128), F32)],
        compiler_params=_params("arbitrary"),
        name="router",
    )(x, mod3, mod3, router_wt, router_b, tri)


def _dispatch_plan(topi_t, rank_t, counts, n_blocks):
    padded = (counts + ROUTE_BLOCK - 1) // ROUTE_BLOCK * ROUTE_BLOCK
    ends_p = jnp.cumsum(padded)
    pstart = ends_p - padded
    experts = jnp.arange(N_EXPERTS, dtype=jnp.int32)
    pstart_tok = jnp.sum(jnp.where(topi_t[:, :, None] == experts, pstart, 0), -1)
    dest_t = (pstart_tok + rank_t).astype(jnp.int32)
    blk_start = jnp.arange(n_blocks, dtype=jnp.int32) * ROUTE_BLOCK
    block_e = jnp.minimum(jnp.sum((blk_start[:, None] >= ends_p).astype(jnp.int32), -1), N_EXPERTS - 1)
    blk_end = jnp.sum(jnp.where(block_e[:, None] == experts, pstart + counts, 0), -1)
    n_valid = jnp.clip(blk_end - blk_start, 0, ROUTE_BLOCK).astype(jnp.int32)
    return dest_t, block_e.astype(jnp.int32), n_valid


def _sc_mesh():
    return plsc.VectorSubcoreMesh(core_axis_name="c", subcore_axis_name="s")


def _sc_dispatch(hp, dest_flat, rows):
    t, dp = hp.shape
    per_w = t // SC_WORKERS
    n_chunks = per_w // SC_CHUNK

    assert n_chunks % 2 == 0

    @functools.partial(
        pl.kernel, mesh=_sc_mesh(), out_type=jax.ShapeDtypeStruct((rows, dp), hp.dtype),
        scratch_types=[pltpu.VMEM((TOP_K * n_chunks, SC_CHUNK), jnp.int32),
                       pltpu.VMEM((SC_CHUNK, dp), hp.dtype), pltpu.VMEM((SC_CHUNK, dp), hp.dtype),
                       pltpu.SemaphoreType.DMA, pltpu.SemaphoreType.DMA,
                       pltpu.SemaphoreType.DMA, pltpu.SemaphoreType.DMA])
    def dispatch(h_hbm, d_hbm, xs_hbm, idx_v, rows_a, rows_b, load_a, load_b, scat_a, scat_b):
        worker = lax.axis_index("s") * SC_CORES + lax.axis_index("c")
        base = worker * per_w
        for k in range(TOP_K):
            pltpu.sync_copy(d_hbm.at[pl.ds((k * SC_WORKERS + worker) * n_chunks, n_chunks)],
                            idx_v.at[pl.ds(k * n_chunks, n_chunks)])

        def load(i, buf, sem):
            return pltpu.make_async_copy(h_hbm.at[pl.ds(base + i * SC_CHUNK, SC_CHUNK)], buf, sem)

        def scatter(i, k, buf, sem):
            return pltpu.make_async_copy(buf, xs_hbm.at[idx_v.at[k * n_chunks + i]], sem)

        def scatter_all(i, buf, sem):
            for k in range(TOP_K):
                scatter(i, k, buf, sem).start()
            for k in range(TOP_K):
                scatter(i, k, buf, sem).wait()

        load(0, rows_a, load_a).start()

        @pl.loop(0, n_chunks, step=2)
        def _(i):
            load(i, rows_a, load_a).wait()
            load(i + 1, rows_b, load_b).start()
            scatter_all(i, rows_a, scat_a)
            load(i + 1, rows_b, load_b).wait()

            @pl.when(i + 2 < n_chunks)
            def _():
                load(i + 2, rows_a, load_a).start()

            scatter_all(i + 1, rows_b, scat_b)

    return dispatch(hp, dest_flat.reshape(TOP_K * SC_WORKERS * n_chunks, SC_CHUNK))


def _sc_gather(table, idx):
    n = idx.shape[0]
    dp = table.shape[1]
    per_w = n // SC_WORKERS
    n_chunks = per_w // SC_CHUNK

    assert n_chunks % 2 == 0

    @functools.partial(
        pl.kernel, mesh=_sc_mesh(), out_type=jax.ShapeDtypeStruct((n, dp), table.dtype),
        scratch_types=[pltpu.VMEM((n_chunks, SC_CHUNK), jnp.int32),
                       pltpu.VMEM((SC_CHUNK, dp), table.dtype), pltpu.VMEM((SC_CHUNK, dp), table.dtype),
                       pltpu.SemaphoreType.DMA, pltpu.SemaphoreType.DMA,
                       pltpu.SemaphoreType.DMA, pltpu.SemaphoreType.DMA])
    def gather(t_hbm, i_hbm, o_hbm, idx_v, rows_a, rows_b, fetch_a, fetch_b, store_a, store_b):
        worker = lax.axis_index("s") * SC_CORES + lax.axis_index("c")
        base = worker * per_w
        pltpu.sync_copy(i_hbm.at[pl.ds(worker * n_chunks, n_chunks)], idx_v)

        def fetch(i, buf, sem):
            return pltpu.make_async_copy(t_hbm.at[idx_v.at[i]], buf, sem)

        def store(i, buf, sem):
            return pltpu.make_async_copy(buf, o_hbm.at[pl.ds(base + i * SC_CHUNK, SC_CHUNK)], sem)

        fetch(0, rows_a, fetch_a).start()

        @pl.loop(0, n_chunks, step=2)
        def _(i):
            fetch(i, rows_a, fetch_a).wait()

            @pl.when(i > 0)
            def _():
                store(i - 1, rows_b, store_b).wait()

            fetch(i + 1, rows_b, fetch_b).start()
            store(i, rows_a, store_a).start()
            fetch(i + 1, rows_b, fetch_b).wait()
            store(i, rows_a, store_a).wait()

            @pl.when(i + 2 < n_chunks)
            def _():
                fetch(i + 2, rows_a, fetch_a).start()

            store(i + 1, rows_b, store_b).start()

        store(n_chunks - 1, rows_b, store_b).wait()

    return gather(table, idx.reshape(SC_WORKERS * n_chunks, SC_CHUNK))


def _expert_kernel(be_ref, nv_ref, x_ref, w1_ref, b1_ref, w2_ref, b2_ref, o_ref, w1b_ref, w2b_ref):
    i = pl.program_id(0)
    n_valid = nv_ref[i]
    new_expert = (i == 0) | (be_ref[i] != be_ref[jnp.maximum(i - 1, 0)])

    @pl.when(new_expert & (n_valid > 0))
    def _():
        w1b_ref[...] = w1_ref[0].astype(BF16)
        w2b_ref[...] = w2_ref[0].astype(BF16)

    for piece in range(ROUTE_BLOCK // EXPERT_ROWS):
        r = slice(piece * EXPERT_ROWS, (piece + 1) * EXPERT_ROWS)
        left = n_valid - piece * EXPERT_ROWS

        @pl.when(left > 0)
        def _():
            live = lax.broadcasted_iota(jnp.int32, (EXPERT_ROWS, D_MODEL), 0) < left
            x = jnp.where(live, _unpack_rows(x_ref[r, :]), 0.0).astype(BF16)
            gu = _dot(x, w1b_ref[...]) + b1_ref[0]
            glu = jnp.minimum(gu[:, :D_FF], SWIGLU_LIMIT)
            lin = jnp.clip(gu[:, D_FF:], -SWIGLU_LIMIT, SWIGLU_LIMIT)
            act = glu * _sigmoid(SWIGLU_ALPHA * glu) * (lin + 1.0)
            o_ref[r, :] = _pack_rows(_dot(act.astype(BF16), w2b_ref[...]) + b2_ref[0])

        @pl.when(left <= 0)
        def _():
            o_ref[r, :] = jnp.zeros((EXPERT_ROWS, PACKED_D), U32)


def _experts(xs, block_e, n_valid, w1, b1, w2, b2, layer):
    rows, dp = xs.shape
    d = D_MODEL
    n_blocks = rows // ROUTE_BLOCK
    e0 = layer * N_EXPERTS
    grid_spec = pltpu.PrefetchScalarGridSpec(
        num_scalar_prefetch=2,
        grid=(n_blocks,),
        in_specs=[
            pl.BlockSpec((ROUTE_BLOCK, dp), lambda i, be, nv: (i, 0)),
            pl.BlockSpec((1, d, 2 * D_FF), lambda i, be, nv: (e0 + be[i], 0, 0)),
            pl.BlockSpec((1, 1, 2 * D_FF), lambda i, be, nv: (e0 + be[i], 0, 0)),
            pl.BlockSpec((1, D_FF, d), lambda i, be, nv: (e0 + be[i], 0, 0)),
            pl.BlockSpec((1, 1, d), lambda i, be, nv: (e0 + be[i], 0, 0)),
        ],
        out_specs=pl.BlockSpec((ROUTE_BLOCK, dp), lambda i, be, nv: (i, 0)),
        scratch_shapes=[pltpu.VMEM((d, 2 * D_FF), BF16), pltpu.VMEM((D_FF, d), BF16)],
    )
    return pl.pallas_call(
        _expert_kernel,
        grid_spec=grid_spec,
        out_shape=jax.ShapeDtypeStruct((rows, dp), U32),
        compiler_params=_params("arbitrary"),
        name="experts",
    )(block_e, n_valid, xs, w1, b1, w2, b2)


def _combine_kernel(x_ref, yg_ref, topg_ref, gt_ref, lng_ref, lnb_ref, o_ref):
    o_ref[...] = _moe_combine(x_ref[...], [yg_ref[k] for k in range(TOP_K)], topg_ref[...], gt_ref[0],
                              lng_ref[...], lnb_ref[...])


def _combine(x, yg, topg, mod3, ln_g, ln_b, seq):
    t, d = x.shape
    tm = TOKEN_TILE
    per_b = seq // tm
    return pl.pallas_call(
        _combine_kernel,
        grid=(t // tm,),
        in_specs=[
            pl.BlockSpec((tm, d), lambda i: (i, 0)),
            pl.BlockSpec((TOP_K, tm, PACKED_D), lambda i: (0, i, 0)),
            pl.BlockSpec((TOP_K, tm), lambda i: (0, i)),
            pl.BlockSpec((1, 1, d), lambda i: (i // per_b, 0, 5)),
            pl.BlockSpec((1, d), lambda i: (0, 0)),
            pl.BlockSpec((1, d), lambda i: (0, 0)),
        ],
        out_specs=pl.BlockSpec((tm, d), lambda i: (i, 0)),
        out_shape=jax.ShapeDtypeStruct((t, d), F32),
        compiler_params=_params("parallel"),
        name="combine",
    )(x, yg, topg, mod3, ln_g, ln_b)


def _rearranged_w_in(w_in):
    depth, d, _ = w_in.shape
    ab_end = SEG_AB + 2 * GDN_HEADS
    ab_pad = jnp.zeros((depth, d, AB_W - 2 * GDN_HEADS), w_in.dtype)
    return jnp.concatenate([w_in[:, :, :ab_end], ab_pad, w_in[:, :, ab_end:]], axis=2).astype(BF16)


def kernel(x, c, ada_w, ada_b, w_in, conv_w, a_log, dt_bias, gdn_norm_w, sinks, pool_w, pool_scale, w_pa, w_pb, w_pc, w_o, ln1_g, ln1_b, ln2_g, ln2_b, router_w, router_b, exp_w1, exp_b1, exp_w2, exp_b2):
    batch, seq, d = x.shape
    t = batch * seq
    n_blocks = (t * TOP_K + ROUTE_BLOCK - 1) // ROUTE_BLOCK + N_EXPERTS
    mod = _ada_mod(c, ada_w, ada_b)
    xt = x.reshape(t, d)
    lane_pad = jnp.zeros((AB_W - GDN_HEADS,), F32)
    rows = n_blocks * ROUTE_BLOCK
    w_in_all = _rearranged_w_in(w_in)
    w1_all = exp_w1.reshape(DEPTH * N_EXPERTS, d, 2 * D_FF)
    b1_all = exp_b1.reshape(DEPTH * N_EXPERTS, 1, 2 * D_FF)
    w2_all = exp_w2.reshape(DEPTH * N_EXPERTS, D_FF, d)
    b2_all = exp_b2.reshape(DEPTH * N_EXPERTS, 1, d)
    moe = None
    for l in range(DEPTH):
        mod3 = mod[l].reshape(batch, 1, 6 * d)
        gpar = jnp.stack([jnp.concatenate([-jnp.exp(a_log[l]), lane_pad]), jnp.concatenate([dt_bias[l], lane_pad])])
        if moe is None:
            qkv, z, ab, qb, kvb, uc, gates = _in_proj(xt, mod3, w_in_all, conv_w, gpar, seq, l)
        else:
            xt, qkv, z, ab, qb, kvb, uc, gates = _in_proj(xt, mod3, w_in_all, conv_w, gpar, seq, l, moe)
        xt = _mixer(xt, qkv, ab, z, qb, kvb, uc, gates, mod3, gdn_norm_w[l].reshape(1, GDN_DV), sinks[l],
                    pool_w[l].astype(BF16), pool_scale[l].reshape(1, POOL_DIM),
                    w_pa[l].astype(BF16), w_pb[l].astype(BF16), w_pc[l].astype(BF16), w_o[l].astype(BF16),
                    ln1_g[l].reshape(1, d), ln1_b[l].reshape(1, d), batch, seq)
        hp, topi_t, topg_t, rank_t, cnt = _router(xt, mod3, router_w[l].T, router_b[l].reshape(N_EXPERTS, 1), seq)
        dest_t, block_e, n_valid = _dispatch_plan(topi_t, rank_t, cnt[:, 0], n_blocks)
        dest_flat = dest_t.reshape(TOP_K * t)
        xs = _sc_dispatch(hp, dest_flat, rows)
        ys = _experts(xs, block_e, n_valid, w1_all, b1_all, w2_all, b2_all, l)
        yg = _sc_gather(ys, dest_flat).reshape(TOP_K, t, PACKED_D)
        moe = (yg, topg_t, mod3, ln2_g[l].reshape(1, d), ln2_b[l].reshape(1, d))
    xt = _combine(xt, *moe, seq)
    return xt.reshape(batch, seq, d)
```
